```python
import math
import jax, jax.numpy as jnp
from jax import lax
import numpy as np

D_MODEL = 4096
BATCH = 4
SEQ = 4096
DEPTH = 2
DEC_BATCH = 16
DEC_SEQ = 32
PAST_LEN = 4096

CHUNK = 64
Q_BLOCK = 128
ROPE_THETA = 500000.0
DEEPNORM_ALPHA = (2 * DEPTH) ** 0.25
DEEPNORM_BETA = (8 * DEPTH) ** -0.25
NEG_INF = -1e30

SSD_WIDTH = D_MODEL // 2
SSD_HEAD_DIM = 64
SSD_HEADS = SSD_WIDTH // SSD_HEAD_DIM
SSD_GROUPS = 4
SSD_R = SSD_HEADS // SSD_GROUPS
SSD_STATE = 128
SSD_CONV = 4
SSD_CHUNK = CHUNK
SSD_CONV_DIM = SSD_WIDTH + 2 * SSD_GROUPS * SSD_STATE

DIFF_WIDTH = D_MODEL // 4
DIFF_QK_DIM = 64
DIFF_HEADS = DIFF_WIDTH // (2 * DIFF_QK_DIM)
DIFF_V_DIM = 2 * DIFF_QK_DIM
DIFF_ROT = DIFF_QK_DIM // 4
DIFF_SCALE = DIFF_QK_DIM ** -0.5

MLA_WIDTH = D_MODEL // 4
MLA_V_DIM = 128
MLA_HEADS = MLA_WIDTH // MLA_V_DIM
MLA_NOPE_DIM = 128
MLA_ROPE_DIM = 64
MLA_Q_RANK = 768
MLA_KV_RANK = 256
MLA_SCALE = (MLA_NOPE_DIM + MLA_ROPE_DIM) ** -0.5

MIX_WIDTH = SSD_WIDTH + DIFF_WIDTH + MLA_WIDTH
IN_SIZES = (SSD_WIDTH,
            SSD_CONV_DIM,
            SSD_HEADS,
            DIFF_WIDTH,
            DIFF_WIDTH,
            DIFF_WIDTH,
            DIFF_WIDTH,
            MLA_Q_RANK,
            MLA_KV_RANK,
            MLA_ROPE_DIM,
            MLA_WIDTH)
IN_WIDTH = sum(IN_SIZES)

kernel_name = 'hybrid_ssd_diffattn_mla_streaming_step'


def rmsnorm(x, w, eps=1e-6):
    xf = x.astype(jnp.float32)
    y = xf * lax.rsqrt(jnp.mean(xf * xf, axis=-1, keepdims=True) + eps)
    return (y * w.astype(jnp.float32)).astype(x.dtype)


def layernorm(x, g, b, eps=1e-5):
    xf = x.astype(jnp.float32)
    mu = jnp.mean(xf, axis=-1, keepdims=True)
    var = jnp.mean(jnp.square(xf - mu), axis=-1, keepdims=True)
    y = (xf - mu) * lax.rsqrt(var + eps) * g.astype(jnp.float32) + b.astype(jnp.float32)
    return y.astype(x.dtype)


def rope(x, pos, rot_dim):
    half = rot_dim // 2
    inv = ROPE_THETA ** (-jnp.arange(half, dtype=jnp.float32) * (2.0 / rot_dim))
    ang = pos.astype(jnp.float32)[:, None] * inv[None, :]
    cos = jnp.cos(ang)[None, :, None, :]
    sin = jnp.sin(ang)[None, :, None, :]
    xr = x[..., :rot_dim].astype(jnp.float32)
    x1, x2 = xr[..., :half], xr[..., half:]
    rot = jnp.concatenate([x1 * cos - x2 * sin, x2 * cos + x1 * sin], axis=-1).astype(x.dtype)
    return jnp.concatenate([rot, x[..., rot_dim:]], axis=-1)


def chunk_visible(q_pos, k_pos):
    return (k_pos[None, :] // CHUNK) <= (q_pos[:, None] // CHUNK)


def over_query_blocks(fn, q_parts, q_pos):
    L = q_pos.shape[0]
    blk = Q_BLOCK if L % Q_BLOCK == 0 else L
    nb = L // blk
    split = lambda t: jnp.moveaxis(t.reshape((t.shape[0], nb, blk) + t.shape[2:]), 1, 0)
    out = lax.map(lambda a: fn(a[0], a[1]), (tuple(split(t) for t in q_parts), q_pos.reshape(nb, blk)))
    out = jnp.moveaxis(out, 0, 1)
    return out.reshape((out.shape[0], L) + out.shape[3:])


def causal_conv(xbc, conv_state, w, b):
    L = xbc.shape[1]
    xp = jnp.concatenate([conv_state.astype(xbc.dtype), xbc], axis=1)
    y = b
    for j in range(SSD_CONV):
        y = y + xp[:, j:j + L] * w[j]
    return jax.nn.silu(y), xp[:, L:]


def ssd_scan(x, dt, A, B, C, state0):
    b, L = x.shape[0], x.shape[1]
    pad = (-L) % SSD_CHUNK
    padt = lambda t: jnp.pad(t, [(0, 0), (0, pad)] + [(0, 0)] * (t.ndim - 2))
    x, dt, B, C = padt(x), padt(dt), padt(B), padt(C)
    nc = (L + pad) // SSD_CHUNK
    ch = lambda t: t.reshape((b, nc, SSD_CHUNK) + t.shape[2:])
    x, dt, B, C = ch(x), ch(dt), ch(B), ch(C)
    a_cs = jnp.cumsum(dt * A, axis=2)
    xd = x * dt[..., None]
    seg = a_cs[:, :, :, None] - a_cs[:, :, None, :]
    causal = jnp.tril(jnp.ones((SSD_CHUNK, SSD_CHUNK), dtype=bool))[None, None, :, :, None, None]
    decay = jnp.exp(jnp.where(causal, seg, -jnp.inf))
    cb = jnp.einsum('bclgn,bcsgn->bclsg', C, B)
    y_diag = jnp.einsum('bclsgr,bcsgrp->bclgrp', cb[..., None] * decay, xd)
    to_end = jnp.exp(a_cs[:, :, -1:] - a_cs)
    chunk_states = jnp.einsum('bcsgn,bcsgrp->bcgrpn', B, xd * to_end[..., None])
    chunk_decay = jnp.exp(a_cs[:, :, -1])

    def step(s, inp):
        cs, cd = inp
        return s * cd[..., None, None] + cs, s

    final, prev = lax.scan(step, state0, (jnp.moveaxis(chunk_states, 1, 0), jnp.moveaxis(chunk_decay, 1, 0)))
    prev = jnp.moveaxis(prev, 0, 1)
    y_off = jnp.einsum('bclgn,bcgrpn->bclgrp', C, prev) * jnp.exp(a_cs)[..., None]
    y = (y_diag + y_off).reshape((b, nc * SSD_CHUNK) + x.shape[3:])[:, :L]
    return y, final


def trunk_layer(x, c, k_cache, v_cache, lat_cache, kr_cache, ssm_state, conv_state, p, layer_idx):
    f32 = jnp.float32
    b, L, _ = x.shape
    P = k_cache.shape[1]
    q_pos = P + jnp.arange(L, dtype=jnp.int32)
    k_pos = jnp.arange(P + L, dtype=jnp.int32)

    mod = jax.nn.silu(c) @ p['w_mod'] + p['b_mod']
    shift, scale, gate = jnp.split(mod, 3, axis=-1)
    u = x * (1 + scale[:, None]) + shift[:, None]

    h = u @ p['w_in']
    offs = [int(o) for o in np.cumsum(IN_SIZES)[:-1]]
    z, xbc, dt_raw, dq, dk, dv, dgate, cq, ckv, kr, mgate = jnp.split(h, offs, axis=-1)

    xbc, conv_new = causal_conv(xbc, conv_state, p['conv_w'], p['conv_b'])
    xs, Bm, Cm = jnp.split(xbc, [SSD_WIDTH, SSD_WIDTH + SSD_GROUPS * SSD_STATE], axis=-1)
    xs5 = xs.reshape(b, L, SSD_GROUPS, SSD_R, SSD_HEAD_DIM).astype(f32)
    dt = jax.nn.softplus((dt_raw + p['dt_bias']).astype(f32)).reshape(b, L, SSD_GROUPS, SSD_R)
    A = -jnp.exp(p['a_log'].astype(f32)).reshape(SSD_GROUPS, SSD_R)
    y, ssm_new = ssd_scan(xs5, dt, A,
                          Bm.reshape(b, L, SSD_GROUPS, SSD_STATE).astype(f32),
                          Cm.reshape(b, L, SSD_GROUPS, SSD_STATE).astype(f32),
                          ssm_state.astype(f32).reshape(b, SSD_GROUPS, SSD_R, SSD_HEAD_DIM, SSD_STATE))
    y = y + p['d_skip'].astype(f32).reshape(SSD_GROUPS, SSD_R)[..., None] * xs5
    yg = y.reshape(b, L, SSD_WIDTH) * jax.nn.silu(z.astype(f32))
    y_ssd = rmsnorm(yg.reshape(b, L, SSD_GROUPS, SSD_WIDTH // SSD_GROUPS),
                    p['ssd_norm_w'].reshape(SSD_GROUPS, SSD_WIDTH // SSD_GROUPS)).reshape(b, L, SSD_WIDTH).astype(x.dtype)
    ssm_new = ssm_new.reshape(b, SSD_HEADS, SSD_HEAD_DIM, SSD_STATE).astype(ssm_state.dtype)

    lam_init = 0.8 - 0.6 * math.exp(-0.3 * layer_idx)
    lam = (jnp.exp(jnp.sum(p['lambda_q1'].astype(f32) * p['lambda_k1'].astype(f32)))
           - jnp.exp(jnp.sum(p['lambda_q2'].astype(f32) * p['lambda_k2'].astype(f32))) + lam_init)
    q = rope(dq.reshape(b, L, DIFF_HEADS * 2, DIFF_QK_DIM), q_pos, DIFF_ROT).reshape(b, L, DIFF_HEADS, 2, DIFF_QK_DIM)
    k = rope(dk.reshape(b, L, DIFF_HEADS * 2, DIFF_QK_DIM), q_pos, DIFF_ROT).reshape(b, L, DIFF_HEADS, 2, DIFF_QK_DIM)
    v = dv.reshape(b, L, DIFF_HEADS, DIFF_V_DIM)
    k_all = jnp.concatenate([k_cache.astype(k.dtype), k], axis=1)
    v_all = jnp.concatenate([v_cache.astype(v.dtype), v], axis=1)

    def diff_block(qs, pos):
        (qb,) = qs
        s = jnp.einsum('bqhcd,bkhcd->bchqk', qb, k_all, preferred_element_type=f32) * DIFF_SCALE
        s = jnp.where(chunk_visible(pos, k_pos), s, NEG_INF)
        a = jax.nn.softmax(s, axis=-1)
        amap = a[:, 0] - lam * a[:, 1]
        return jnp.einsum('bhqk,bkhe->bqhe', amap, v_all.astype(f32))

    o = over_query_blocks(diff_block, (q,), q_pos)
    o = rmsnorm(o, p['diff_norm_w']) * (1.0 - lam_init)
    y_diff = (o.reshape(b, L, DIFF_WIDTH) * jax.nn.silu(dgate.astype(f32))).astype(x.dtype)

    qm = (rmsnorm(cq, p['mla_q_norm_w']) @ p['w_uq']).reshape(b, L, MLA_HEADS, MLA_NOPE_DIM + MLA_ROPE_DIM)
    q_nope, q_rope = qm[..., :MLA_NOPE_DIM], qm[..., MLA_NOPE_DIM:]
    q_rope = rope(q_rope, q_pos, MLA_ROPE_DIM)
    lat = rmsnorm(ckv, p['mla_kv_norm_w'])
    krot = rope(kr[:, :, None, :], q_pos, MLA_ROPE_DIM)[:, :, 0]
    lat_all = jnp.concatenate([lat_cache.astype(lat.dtype), lat], axis=1)
    kr_all = jnp.concatenate([kr_cache.astype(krot.dtype), krot], axis=1)
    q_lat = jnp.einsum('bqhd,ehd->bqhe', q_nope, p['w_uk'])

    def mla_block(qs, pos):
        ql, qr = qs
        s = (jnp.einsum('bqhe,bke->bhqk', ql, lat_all, preferred_element_type=f32)
             + jnp.einsum('bqhr,bkr->bhqk', qr, kr_all, preferred_element_type=f32)) * MLA_SCALE
        s = jnp.where(chunk_visible(pos, k_pos), s, NEG_INF)
        a = jax.nn.softmax(s, axis=-1)
        return jnp.einsum('bhqk,bke->bqhe', a, lat_all.astype(f32))

    o_lat = over_query_blocks(mla_block, (q_lat, q_rope), q_pos)
    om = jnp.einsum('bqhe,ehd->bqhd', o_lat, p['w_uv'].astype(f32)).reshape(b, L, MLA_WIDTH)
    y_mla = (om * jax.nn.silu(mgate.astype(f32))).astype(x.dtype)

    mix = jnp.concatenate([y_ssd, y_diff, y_mla], axis=-1) @ p['w_out']
    x_new = layernorm(DEEPNORM_ALPHA * x + gate[:, None] * mix, p['ln_g'], p['ln_b'])
    return x_new, (k, v, lat, krot, ssm_new, conv_new)


def setup_inputs(seed: int = 0) -> dict:
    key = jax.random.key(seed)
    ks = iter(jax.random.split(key, 48))
    f32 = jnp.float32
    nrm = lambda shape, s=1.0: jax.random.normal(next(ks), shape, f32) * s
    ones_noise = lambda shape: 1.0 + 0.02 * jax.random.normal(next(ks), shape, f32)
    dt0 = jnp.exp(jax.random.uniform(next(ks), (DEPTH, SSD_HEADS), f32, math.log(1e-3), math.log(1e-1)))
    dt_bias = dt0 + jnp.log(-jnp.expm1(-dt0))
    a_log = jnp.log(jax.random.uniform(next(ks), (DEPTH, SSD_HEADS), f32, 1.0, 16.0))
    return {
        'x_prompt': nrm((BATCH, SEQ, D_MODEL)),
        'x_sample': nrm((DEC_BATCH, DEC_SEQ, D_MODEL)),
        'cache_diff_k': nrm((DEPTH, DEC_BATCH, PAST_LEN, DIFF_HEADS, 2, DIFF_QK_DIM)),
        'cache_diff_v': nrm((DEPTH, DEC_BATCH, PAST_LEN, DIFF_HEADS, DIFF_V_DIM)),
        'cache_mla_latent': nrm((DEPTH, DEC_BATCH, PAST_LEN, MLA_KV_RANK)),
        'cache_mla_krope': nrm((DEPTH, DEC_BATCH, PAST_LEN, MLA_ROPE_DIM)),
        'state_ssm': nrm((DEPTH, DEC_BATCH, SSD_HEADS, SSD_HEAD_DIM, SSD_STATE), 0.5),
        'state_conv': nrm((DEPTH, DEC_BATCH, SSD_CONV - 1, SSD_CONV_DIM)),
        'c_prompt': nrm((BATCH, D_MODEL)),
        'c_sample': nrm((DEC_BATCH, D_MODEL)),
        'w_mod': nrm((DEPTH, D_MODEL, 3 * D_MODEL), 0.5 * D_MODEL ** -0.5),
        'b_mod': nrm((DEPTH, 3 * D_MODEL), 0.01),
        'w_in': nrm((DEPTH, D_MODEL, IN_WIDTH), D_MODEL ** -0.5),
        'conv_w': nrm((DEPTH, SSD_CONV, SSD_CONV_DIM), SSD_CONV ** -0.5),
        'conv_b': nrm((DEPTH, SSD_CONV_DIM), 0.01),
        'dt_bias': dt_bias,
        'a_log': a_log,
        'd_skip': ones_noise((DEPTH, SSD_HEADS)),
        'ssd_norm_w': ones_noise((DEPTH, SSD_WIDTH)),
        'lambda_q1': nrm((DEPTH, DIFF_QK_DIM), 0.1),
        'lambda_k1': nrm((DEPTH, DIFF_QK_DIM), 0.1),
        'lambda_q2': nrm((DEPTH, DIFF_QK_DIM), 0.1),
        'lambda_k2': nrm((DEPTH, DIFF_QK_DIM), 0.1),
        'diff_norm_w': ones_noise((DEPTH, DIFF_V_DIM)),
        'mla_q_norm_w': ones_noise((DEPTH, MLA_Q_RANK)),
        'mla_kv_norm_w': ones_noise((DEPTH, MLA_KV_RANK)),
        'w_uq': nrm((DEPTH, MLA_Q_RANK, MLA_HEADS * (MLA_NOPE_DIM + MLA_ROPE_DIM)), MLA_Q_RANK ** -0.5),
        'w_uk': nrm((DEPTH, MLA_KV_RANK, MLA_HEADS, MLA_NOPE_DIM), MLA_KV_RANK ** -0.5),
        'w_uv': nrm((DEPTH, MLA_KV_RANK, MLA_HEADS, MLA_V_DIM), MLA_KV_RANK ** -0.5),
        'w_out': nrm((DEPTH, MIX_WIDTH, D_MODEL), MIX_WIDTH ** -0.5 * DEEPNORM_BETA),
        'ln_g': ones_noise((DEPTH, D_MODEL)),
        'ln_b': nrm((DEPTH, D_MODEL), 0.01),
    }


def reference(x_prompt, x_sample, cache_diff_k, cache_diff_v, cache_mla_latent, cache_mla_krope,
              state_ssm, state_conv, c_prompt, c_sample, w_mod, b_mod, w_in, conv_w, conv_b,
              dt_bias, a_log, d_skip, ssd_norm_w, lambda_q1, lambda_k1, lambda_q2, lambda_k2,
              diff_norm_w, mla_q_norm_w, mla_kv_norm_w, w_uq, w_uk, w_uv, w_out, ln_g, ln_b):
    bp = x_prompt.shape[0]
    dtp = x_prompt.dtype
    hp, hs = x_prompt, x_sample
    st_prompt, st_sample = [], []
    for l in range(DEPTH):
        p = dict(w_mod=w_mod[l], b_mod=b_mod[l], w_in=w_in[l], conv_w=conv_w[l], conv_b=conv_b[l],
                 dt_bias=dt_bias[l], a_log=a_log[l], d_skip=d_skip[l], ssd_norm_w=ssd_norm_w[l],
                 lambda_q1=lambda_q1[l], lambda_k1=lambda_k1[l], lambda_q2=lambda_q2[l], lambda_k2=lambda_k2[l],
                 diff_norm_w=diff_norm_w[l], mla_q_norm_w=mla_q_norm_w[l], mla_kv_norm_w=mla_kv_norm_w[l],
                 w_uq=w_uq[l], w_uk=w_uk[l], w_uv=w_uv[l], w_out=w_out[l], ln_g=ln_g[l], ln_b=ln_b[l])
        hp, st_p = trunk_layer(
            hp, c_prompt,
            jnp.zeros((bp, 0, DIFF_HEADS, 2, DIFF_QK_DIM), dtp),
            jnp.zeros((bp, 0, DIFF_HEADS, DIFF_V_DIM), dtp),
            jnp.zeros((bp, 0, MLA_KV_RANK), dtp),
            jnp.zeros((bp, 0, MLA_ROPE_DIM), dtp),
            jnp.zeros((bp, SSD_HEADS, SSD_HEAD_DIM, SSD_STATE), dtp),
            jnp.zeros((bp, SSD_CONV - 1, SSD_CONV_DIM), dtp),
            p, l)
        hs, st_s = trunk_layer(
            hs, c_sample, cache_diff_k[l], cache_diff_v[l], cache_mla_latent[l], cache_mla_krope[l],
            state_ssm[l], state_conv[l], p, l)
        st_prompt.append(st_p)
        st_sample.append(st_s)
    diff_k_prompt = jnp.stack([s[0] for s in st_prompt])
    diff_v_prompt = jnp.stack([s[1] for s in st_prompt])
    mla_latent_prompt = jnp.stack([s[2] for s in st_prompt])
    mla_krope_prompt = jnp.stack([s[3] for s in st_prompt])
    ssm_prompt = jnp.stack([s[4] for s in st_prompt])
    conv_prompt = jnp.stack([s[5] for s in st_prompt])
    diff_k_sample = jnp.stack([s[0] for s in st_sample])
    diff_v_sample = jnp.stack([s[1] for s in st_sample])
    mla_latent_sample = jnp.stack([s[2] for s in st_sample])
    mla_krope_sample = jnp.stack([s[3] for s in st_sample])
    ssm_sample = jnp.stack([s[4] for s in st_sample])
    conv_sample = jnp.stack([s[5] for s in st_sample])
    return (hp, hs, diff_k_prompt, diff_v_prompt, mla_latent_prompt, mla_krope_prompt, ssm_prompt, conv_prompt,
            diff_k_sample, diff_v_sample, mla_latent_sample, mla_krope_sample, ssm_sample, conv_sample)
```

```python
import functools
import math

import numpy as np
import jax
import jax.numpy as jnp
from jax import lax
from jax.experimental import pallas as pl
from jax.experimental.pallas import tpu as pltpu

F32 = jnp.float32
BF16 = jnp.bfloat16

CHUNK = 64
ROPE_THETA = 500000.0
NEG_INF = -1e30
SSD_HEAD_DIM = 64
SSD_GROUPS = 4
SSD_STATE = 128
SSD_CONV = 4
DIFF_QK_DIM = 64
DIFF_V_DIM = 128
DIFF_ROT = DIFF_QK_DIM // 4
DIFF_SCALE = DIFF_QK_DIM ** -0.5
MLA_V_DIM = 128
MLA_NOPE_DIM = 128
MLA_ROPE_DIM = 64
MLA_Q_RANK = 768
MLA_KV_RANK = 256
MLA_SCALE = (MLA_NOPE_DIM + MLA_ROPE_DIM) ** -0.5
MLA_QK_PAD = MLA_KV_RANK + 128

LANES = 128
VMEM_LIMIT = 56 * 1024 * 1024


def _cparams(sem):
    return pltpu.CompilerParams(dimension_semantics=sem, vmem_limit_bytes=VMEM_LIMIT)


def _silu(x):
    return x * jax.nn.sigmoid(x)


def _nt_dot(a, b):
    return lax.dot_general(a, b, (((1,), (1,)), ((), ())), preferred_element_type=F32)


def _split2(v):
    hi = v.astype(BF16)
    lo = (v - hi.astype(F32)).astype(BF16)
    return hi, lo


def _split3(v):
    hi = v.astype(BF16)
    r = v - hi.astype(F32)
    mid = r.astype(BF16)
    lo = (r - mid.astype(F32)).astype(BF16)
    return hi, mid, lo


def _mod_kernel(c_ref, w_ref, b_ref, o_ref):
    a = _silu(c_ref[...]).astype(BF16)
    o_ref[0] = jnp.dot(a, w_ref[0].astype(BF16), preferred_element_type=F32) + b_ref[0]


def _mod_call(c_all, w_mod, b_mod):
    depth, d, n3 = w_mod.shape
    rows = c_all.shape[0]
    tn = 512
    return pl.pallas_call(
        _mod_kernel,
        grid=(depth, n3 // tn),
        in_specs=[pl.BlockSpec((rows, d), lambda l, j: (0, 0)),
                  pl.BlockSpec((1, d, tn), lambda l, j: (l, 0, j)),
                  pl.BlockSpec((1, 1, tn), lambda l, j: (l, 0, j))],
        out_specs=pl.BlockSpec((1, rows, tn), lambda l, j: (l, 0, j)),
        out_shape=jax.ShapeDtypeStruct((depth, rows, n3), F32),
        compiler_params=_cparams(("arbitrary", "arbitrary")),
        name="adaln_mod",
    )(c_all, w_mod, b_mod.reshape(depth, 1, n3))


def _modulate_kernel(x_ref, sc_ref, sh_ref, u_ref):
    u_ref[...] = (x_ref[...] * (1.0 + sc_ref[...]) + sh_ref[...]).astype(BF16)


def _seq_tiles(nb, L, rows):
    if L >= rows:
        return 1, rows
    return min(nb, rows // L), L


def _modulate_call(x, scale, shift):
    nb, L, d = x.shape
    bt, lt = _seq_tiles(nb, L, 256)
    return pl.pallas_call(
        _modulate_kernel,
        grid=(nb // bt, L // lt),
        in_specs=[pl.BlockSpec((bt, lt, d), lambda i, j: (i, j, 0)),
                  pl.BlockSpec((bt, 1, d), lambda i, j: (i, 0, 0)),
                  pl.BlockSpec((bt, 1, d), lambda i, j: (i, 0, 0))],
        out_specs=pl.BlockSpec((bt, lt, d), lambda i, j: (i, j, 0)),
        out_shape=jax.ShapeDtypeStruct((nb, L, d), BF16),
        compiler_params=_cparams(("arbitrary", "arbitrary")),
        name="modulate",
    )(x, scale, shift)


class _Cols:
    def __init__(self, d_model):
        self.ssd_w = d_model // 2
        self.diff_w = d_model // 4
        self.mla_w = d_model // 4
        self.heads = self.ssd_w // SSD_HEAD_DIM
        self.conv_dim = self.ssd_w + 2 * SSD_GROUPS * SSD_STATE
        self.in_sizes = (self.ssd_w, self.conv_dim, self.heads, self.diff_w, self.diff_w, self.diff_w,
                         self.diff_w, MLA_Q_RANK, MLA_KV_RANK, MLA_ROPE_DIM, self.mla_w)
        o = 0
        self.xbc = o; o += self.conv_dim
        self.dq = o; o += self.diff_w
        self.z = o; o += self.ssd_w
        self.dk = o; o += self.diff_w
        self.dv = o; o += self.diff_w
        self.dgate = o; o += self.diff_w
        self.mgate = o; o += self.mla_w
        self.cq = o; o += MLA_Q_RANK + MLA_KV_RANK
        self.krdt = o; o += LANES
        self.used = o
        self.tn = 1280
        self.total = -(-o // self.tn) * self.tn


def _permute_w_in(w, cols):
    offs = np.concatenate([[0], np.cumsum(cols.in_sizes)])
    seg = lambda i: w[:, offs[i]:offs[i + 1]]
    pad_dt = LANES - MLA_ROPE_DIM - cols.heads
    parts = [seg(1), seg(3), seg(0), seg(4), seg(5), seg(6), seg(10), seg(7), seg(8), seg(9), seg(2),
             jnp.zeros((w.shape[0], pad_dt + cols.total - cols.used), w.dtype)]
    return jnp.concatenate(parts, axis=1).astype(BF16)


def _matmul_kernel(x_ref, w_ref, o_ref):
    o_ref[...] = jnp.dot(x_ref[...], w_ref[...], preferred_element_type=F32)


def _inproj_call(u2d, wp, tn):
    m, k = u2d.shape
    n = wp.shape[1]
    tm = min(m, 512)
    return pl.pallas_call(
        _matmul_kernel,
        grid=(n // tn, m // tm),
        in_specs=[pl.BlockSpec((tm, k), lambda j, i: (i, 0)),
                  pl.BlockSpec((k, tn), lambda j, i: (0, j))],
        out_specs=pl.BlockSpec((tm, tn), lambda j, i: (i, j)),
        out_shape=jax.ShapeDtypeStruct((m, n), F32),
        compiler_params=_cparams(("arbitrary", "arbitrary")),
        name="in_proj",
    )(u2d, wp)


def _ssd_kernel(xbc_ref, z_ref, dtb_ref, conv0_ref, st0_ref, cw_ref, cb_ref, dtbias_ref, alog_ref,
                dskip_ref, nw_ref, e_ref, et_ref, y_ref, st_ref, convo_ref, ext_ref, *, T, nchunks, heads):
    c = pl.program_id(1)
    ssd_w = heads * SSD_HEAD_DIM
    gw = ssd_w // SSD_GROUPS
    hpg = heads // SSD_GROUPS
    conv_dim = ext_ref.shape[1]
    dt_lo = MLA_ROPE_DIM

    @pl.when(c == 0)
    def _():
        ext_ref[0:8, :] = jnp.zeros((8, conv_dim), F32)
        ext_ref[8 - (SSD_CONV - 1):8, :] = conv0_ref[0]
        st_ref[0] = st0_ref[0]

    ext_ref[8:8 + T, :] = xbc_ref[...]
    acc = cb_ref[...]
    for j in range(SSD_CONV):
        lo = 8 - (SSD_CONV - 1) + j
        acc = acc + ext_ref[lo:lo + T, :] * cw_ref[j:j + 1, :]
    ext_ref[0:8, :] = ext_ref[T:T + 8, :]

    @pl.when(c == nchunks - 1)
    def _():
        convo_ref[0] = ext_ref[8 - (SSD_CONV - 1):8, :]

    xact = _silu(acc)
    xs = xact[:, :ssd_w]
    bm = xact[:, ssd_w:ssd_w + SSD_GROUPS * SSD_STATE]
    cm = xact[:, ssd_w + SSD_GROUPS * SSD_STATE:]

    lane = lax.broadcasted_iota(jnp.int32, (1, LANES), 1)
    is_dt = (lane >= dt_lo) & (lane < dt_lo + heads)
    xdt = dtb_ref[...] + dtbias_ref[...]
    dt = jnp.where(is_dt, jnp.maximum(xdt, 0.0) + jnp.log1p(jnp.exp(-jnp.abs(xdt))), 0.0)
    a_neg = jnp.where(is_dt, -jnp.exp(alog_ref[...]), 0.0)
    da = dt * a_neg

    row = lax.broadcasted_iota(jnp.int32, (T, T), 0)
    col = lax.broadcasted_iota(jnp.int32, (T, T), 1)
    causal = row >= col
    tril = causal.astype(BF16)
    eye = (lax.broadcasted_iota(jnp.int32, (LANES, LANES), 0)
           == lax.broadcasted_iota(jnp.int32, (LANES, LANES), 1)).astype(BF16)

    da3 = _split3(da)
    a_cs = sum(jnp.dot(tril, p, preferred_element_type=F32) for p in da3)
    a3 = _split3(a_cs)
    a_cs_t = sum(_nt_dot(eye, p) for p in a3)
    a_last = a_cs[T - 1:T, :]
    ea = jnp.exp(a_cs)
    te = jnp.exp(a_last - a_cs)

    e_mat = e_ref[...]

    def expand(v):
        hi, lo = _split2(v)
        return jnp.dot(hi, e_mat, preferred_element_type=F32) + jnp.dot(lo, e_mat, preferred_element_type=F32)

    dt_x = expand(dt)
    ea_x = expand(ea)
    te_x = expand(te)
    cd_col = jnp.exp(a_cs_t[:, T - 1:T])
    cd_b = jnp.broadcast_to(cd_col, (LANES, SSD_STATE))
    cdh, cdl = _split2(cd_b)
    et_mat = et_ref[...]
    cd_full = (jnp.dot(et_mat, cdh, preferred_element_type=F32)
               + jnp.dot(et_mat, cdl, preferred_element_type=F32))

    xd = xs * dt_x
    xde = (xd * te_x).astype(BF16)
    lane_p = lax.broadcasted_iota(jnp.int32, (T, LANES), 1)
    lower_half = lane_p < SSD_HEAD_DIM

    for g in range(SSD_GROUPS):
        cg = cm[:, g * SSD_STATE:(g + 1) * SSD_STATE].astype(BF16)
        bg = bm[:, g * SSD_STATE:(g + 1) * SSD_STATE].astype(BF16)
        cbm = _nt_dot(cg, bg)
        st_g = st_ref[0, g * hpg:(g + 1) * hpg].reshape(gw, SSD_STATE)
        y_off = _nt_dot(cg, st_g.astype(BF16)) * ea_x[:, g * gw:(g + 1) * gw]
        pieces = []
        for q in range(hpg // 2):
            c0 = g * gw + q * LANES
            xd_pair = xd[:, c0:c0 + LANES].astype(BF16)
            ys = []
            for h in (g * hpg + 2 * q, g * hpg + 2 * q + 1):
                seg = a_cs[:, dt_lo + h:dt_lo + h + 1] - a_cs_t[dt_lo + h:dt_lo + h + 1, :]
                decay = jnp.exp(jnp.where(causal, seg, -jnp.inf))
                ys.append(jnp.dot((cbm * decay).astype(BF16), xd_pair, preferred_element_type=F32))
            pieces.append(jnp.where(lower_half, ys[0], ys[1]))
        y_g = jnp.concatenate(pieces, axis=1) + y_off
        upd = lax.dot_general(xde[:, g * gw:(g + 1) * gw], bg, (((0,), (0,)), ((), ())),
                              preferred_element_type=F32)
        st_new = st_g * cd_full[g * gw:(g + 1) * gw, :] + upd
        st_ref[0, g * hpg:(g + 1) * hpg] = st_new.reshape(hpg, SSD_HEAD_DIM, SSD_STATE)
        y_g = y_g + dskip_ref[:, g * gw:(g + 1) * gw] * xs[:, g * gw:(g + 1) * gw]
        yg = y_g * _silu(z_ref[:, g * gw:(g + 1) * gw])
        ms = jnp.mean(yg * yg, axis=-1, keepdims=True)
        y_ref[:, g * gw:(g + 1) * gw] = (yg * lax.rsqrt(ms + 1e-6) * nw_ref[:, g * gw:(g + 1) * gw]).astype(BF16)


def _ssd_call(h2d, conv0, st0, conv_w, conv_b, dtbias_p, alog_p, dskip_x, norm_w, e_mat, et_mat, cols, nb, L, T):
    nchunks = L // T
    heads = cols.heads
    ssd_w = cols.ssd_w
    cd = cols.conv_dim
    kern = functools.partial(_ssd_kernel, T=T, nchunks=nchunks, heads=heads)
    row = lambda b, c: b * nchunks + c
    const2 = lambda b, c: (0, 0)
    return pl.pallas_call(
        kern,
        grid=(nb, nchunks),
        in_specs=[pl.BlockSpec((T, cd), lambda b, c: (row(b, c), cols.xbc // cd)),
                  pl.BlockSpec((T, ssd_w), lambda b, c: (row(b, c), cols.z // ssd_w)),
                  pl.BlockSpec((T, LANES), lambda b, c: (row(b, c), cols.krdt // LANES)),
                  pl.BlockSpec((1, SSD_CONV - 1, cd), lambda b, c: (b, 0, 0)),
                  pl.BlockSpec((1, heads, SSD_HEAD_DIM, SSD_STATE), lambda b, c: (b, 0, 0, 0)),
                  pl.BlockSpec((SSD_CONV, cd), const2),
                  pl.BlockSpec((1, cd), const2),
                  pl.BlockSpec((1, LANES), const2),
                  pl.BlockSpec((1, LANES), const2),
                  pl.BlockSpec((1, ssd_w), const2),
                  pl.BlockSpec((1, ssd_w), const2),
                  pl.BlockSpec((LANES, ssd_w), const2),
                  pl.BlockSpec((ssd_w, LANES), const2)],
        out_specs=[pl.BlockSpec((T, ssd_w), lambda b, c: (row(b, c), 0)),
                   pl.BlockSpec((1, heads, SSD_HEAD_DIM, SSD_STATE), lambda b, c: (b, 0, 0, 0)),
                   pl.BlockSpec((1, SSD_CONV - 1, cd), lambda b, c: (b, 0, 0))],
        out_shape=[jax.ShapeDtypeStruct((nb * L, ssd_w), BF16),
                   jax.ShapeDtypeStruct((nb, heads, SSD_HEAD_DIM, SSD_STATE), F32),
                   jax.ShapeDtypeStruct((nb, SSD_CONV - 1, cd), F32)],
        scratch_shapes=[pltpu.VMEM((T + 8, cd), F32)],
        compiler_params=_cparams(("arbitrary", "arbitrary")),
        name="ssd_scan",
    )(h2d, h2d, h2d, conv0, st0, conv_w, conv_b, dtbias_p, alog_p, dskip_x, norm_w, e_mat, et_mat)


def _rope_tables(pos, rot_dim, period, width):
    half = rot_dim // 2
    inv = ROPE_THETA ** (-jnp.arange(half, dtype=F32) * (2.0 / rot_dim))
    ang = pos.astype(F32)[:, None] * inv[None, :]
    cos, sin = jnp.cos(ang), jnp.sin(ang)
    lane = np.arange(LANES)
    inner = lane % period
    idx = jnp.asarray(inner % half)
    first = jnp.asarray((inner < half) & (lane < width))
    second = jnp.asarray((inner >= half) & (inner < rot_dim) & (lane < width))
    keep = jnp.asarray((inner >= rot_dim) & (lane < width))
    cos_l, sin_l = cos[:, idx], sin[:, idx]
    cos_t = jnp.where(first | second, cos_l, jnp.where(keep, 1.0, 0.0))
    sin_a = jnp.where(first, -sin_l, 0.0)
    sin_b = jnp.where(second, sin_l, 0.0)
    return cos_t.astype(F32), sin_a.astype(F32), sin_b.astype(F32)


def _rope_tile(x, cos_t, sin_a, sin_b, half):
    return (x * cos_t + pltpu.roll(x, LANES - half, 1) * sin_a + pltpu.roll(x, half, 1) * sin_b)


def _rmsnorm(x, w):
    return x * lax.rsqrt(jnp.mean(x * x, axis=-1, keepdims=True) + 1e-6) * w


def _prep_kernel(dq_ref, dk_ref, dv_ref, cq_ref, kr_ref, cd_ref, sad_ref, sbd_ref, cm_ref, sam_ref, sbm_ref,
                 kvw_ref, qd_ref, k32_ref, kb_ref, v32_ref, vb_ref, lat_ref, kro_ref, kcat_ref):
    cos_d, sa_d, sb_d = cd_ref[...], sad_ref[...], sbd_ref[...]
    width = dq_ref.shape[1]
    for c in range(width // LANES):
        sl = slice(c * LANES, (c + 1) * LANES)
        q = _rope_tile(dq_ref[:, sl], cos_d, sa_d, sb_d, DIFF_ROT // 2)
        qd_ref[0, :, sl] = (q * DIFF_SCALE).astype(BF16)
        k = _rope_tile(dk_ref[:, sl], cos_d, sa_d, sb_d, DIFF_ROT // 2)
        k32_ref[0, :, sl] = k
        kb_ref[0, :, sl] = k.astype(BF16)
    v = dv_ref[...]
    v32_ref[0] = v
    vb_ref[0] = v.astype(BF16)
    lat = _rmsnorm(cq_ref[:, MLA_Q_RANK:MLA_Q_RANK + MLA_KV_RANK], kvw_ref[...])
    lat_ref[0] = lat
    kr = _rope_tile(kr_ref[...], cm_ref[...], sam_ref[...], sbm_ref[...], MLA_ROPE_DIM // 2)
    kro_ref[0] = kr[:, :MLA_ROPE_DIM]
    kcat_ref[0, :, :MLA_KV_RANK] = lat.astype(BF16)
    kcat_ref[0, :, MLA_KV_RANK:] = kr.astype(BF16)


def _prep_call(h2d, tabs_d, tabs_m, kv_w, cols, nb, L):
    tm = min(L, 256)
    nt = L // tm
    dw = cols.diff_w
    row = lambda b, i: b * nt + i
    hspec = lambda width, off: pl.BlockSpec((tm, width), lambda b, i: (row(b, i), off // width))
    tspec = pl.BlockSpec((tm, LANES), lambda b, i: (i, 0))
    ospec = lambda width: pl.BlockSpec((1, tm, width), lambda b, i: (b, i, 0))
    return pl.pallas_call(
        _prep_kernel,
        grid=(nb, nt),
        in_specs=[hspec(dw, cols.dq), hspec(dw, cols.dk), hspec(dw, cols.dv),
                  hspec(MLA_Q_RANK + MLA_KV_RANK, cols.cq), hspec(LANES, cols.krdt),
                  tspec, tspec, tspec, tspec, tspec, tspec,
                  pl.BlockSpec((1, MLA_KV_RANK), lambda b, i: (0, 0))],
        out_specs=[ospec(dw), ospec(dw), ospec(dw), ospec(dw), ospec(dw),
                   ospec(MLA_KV_RANK), ospec(MLA_ROPE_DIM), ospec(MLA_QK_PAD)],
        out_shape=[jax.ShapeDtypeStruct((nb, L, dw), BF16),
                   jax.ShapeDtypeStruct((nb, L, dw), F32),
                   jax.ShapeDtypeStruct((nb, L, dw), BF16),
                   jax.ShapeDtypeStruct((nb, L, dw), F32),
                   jax.ShapeDtypeStruct((nb, L, dw), BF16),
                   jax.ShapeDtypeStruct((nb, L, MLA_KV_RANK), F32),
                   jax.ShapeDtypeStruct((nb, L, MLA_ROPE_DIM), F32),
                   jax.ShapeDtypeStruct((nb, L, MLA_QK_PAD), BF16)],
        compiler_params=_cparams(("arbitrary", "arbitrary")),
        name="attn_prep",
    )(h2d, h2d, h2d, h2d, h2d, *tabs_d, *tabs_m, kv_w)


def _mlaq_kernel(cq_ref, qw_ref, wuq_ref, wuk_ref, cm_ref, sam_ref, sbm_ref, o_ref, *, heads):
    cqn = _rmsnorm(cq_ref[:, :MLA_Q_RANK], qw_ref[...]).astype(BF16)
    qm = jnp.dot(cqn, wuq_ref[...], preferred_element_type=F32)
    cos_m, sa_m, sb_m = cm_ref[...], sam_ref[...], sbm_ref[...]
    nope_w = heads * MLA_NOPE_DIM
    for h in range(heads):
        nope = qm[:, h * MLA_NOPE_DIM:(h + 1) * MLA_NOPE_DIM].astype(BF16)
        ql = jnp.dot(nope, wuk_ref[h], preferred_element_type=F32) * MLA_SCALE
        qr = _rope_tile(qm[:, nope_w + h * LANES:nope_w + (h + 1) * LANES], cos_m, sa_m, sb_m,
                        MLA_ROPE_DIM // 2) * MLA_SCALE
        o_ref[0, h, :, :MLA_KV_RANK] = ql.astype(BF16)
        o_ref[0, h, :, MLA_KV_RANK:] = qr.astype(BF16)


def _mlaq_call(h2d, q_w, wuq_p, wuk_t, tabs_m, cols, nb, L, heads):
    tm = min(L, 256)
    nt = L // tm
    width = MLA_Q_RANK + MLA_KV_RANK
    tspec = pl.BlockSpec((tm, LANES), lambda b, i: (i, 0))
    return pl.pallas_call(
        functools.partial(_mlaq_kernel, heads=heads),
        grid=(nb, nt),
        in_specs=[pl.BlockSpec((tm, width), lambda b, i: (b * nt + i, cols.cq // width)),
                  pl.BlockSpec((1, MLA_Q_RANK), lambda b, i: (0, 0)),
                  pl.BlockSpec(wuq_p.shape, lambda b, i: (0, 0)),
                  pl.BlockSpec(wuk_t.shape, lambda b, i: (0, 0, 0)),
                  tspec, tspec, tspec],
        out_specs=pl.BlockSpec((1, heads, tm, MLA_QK_PAD), lambda b, i: (b, 0, i, 0)),
        out_shape=jax.ShapeDtypeStruct((nb, heads, L, MLA_QK_PAD), BF16),
        compiler_params=_cparams(("arbitrary", "arbitrary")),
        name="mla_q",
    )(h2d, q_w, wuq_p, wuk_t, *tabs_m)


def _softmax_update(s, v, m_ref, l_ref, acc_ref):
    m_prev = m_ref[...]
    m_new = jnp.maximum(m_prev, jnp.max(s, axis=-1, keepdims=True))
    alpha = jnp.exp(m_prev - m_new)
    p = jnp.exp(s - m_new)
    l_ref[...] = alpha * l_ref[...] + jnp.sum(p, axis=-1, keepdims=True)
    acc_ref[...] = alpha * acc_ref[...] + jnp.dot(p.astype(BF16), v, preferred_element_type=F32)
    m_ref[...] = m_new


def _chunk_mask(rows, tq, tk, q0, k0):
    r = lax.broadcasted_iota(jnp.int32, (rows, tk), 0)
    q_tok = q0 + (r & (tq - 1))
    k_tok = k0 + lax.broadcasted_iota(jnp.int32, (rows, tk), 1)
    return (k_tok // CHUNK) <= (q_tok // CHUNK)


def _tri_pairs(nq, tq, tk):
    qi, kj = [], []
    for i in range(nq):
        last = ((i + 1) * tq - 1) // tk
        for j in range(last + 1):
            qi.append(i)
            kj.append(j)
    return jnp.asarray(qi, jnp.int32), jnp.asarray(kj, jnp.int32)


def _diff_lambda(lam_ref, lam_init):
    s1 = jnp.sum(lam_ref[0:1, :] * lam_ref[1:2, :], axis=-1, keepdims=True)
    s2 = jnp.sum(lam_ref[2:3, :] * lam_ref[3:4, :], axis=-1, keepdims=True)
    return jnp.exp(s1) - jnp.exp(s2) + lam_init


def _diff_finish(o1, o2, lam, nw, gate, lam_init):
    o = o1 - lam * o2
    o = _rmsnorm(o, nw) * (1.0 - lam_init)
    return (o * _silu(gate)).astype(BF16)


def _stack_streams(q):
    lane = lax.broadcasted_iota(jnp.int32, q.shape, 1)
    zero = jnp.zeros_like(q)
    return jnp.concatenate([jnp.where(lane < DIFF_QK_DIM, q, zero), jnp.where(lane >= DIFF_QK_DIM, q, zero)], axis=0)


def _diff_prompt_kernel(qi_ref, kj_ref, q_ref, k_ref, v_ref, lam_ref, nw_ref, g_ref, o_ref,
                        qz_ref, m_ref, l_ref, acc_ref, *, tq, tk, lam_init):
    p = pl.program_id(2)
    qi, kj = qi_ref[p], kj_ref[p]

    @pl.when(kj == 0)
    def _():
        qz_ref[...] = _stack_streams(q_ref[0])
        m_ref[...] = jnp.full(m_ref.shape, -jnp.inf, F32)
        l_ref[...] = jnp.zeros(l_ref.shape, F32)
        acc_ref[...] = jnp.zeros(acc_ref.shape, F32)

    s = _nt_dot(qz_ref[...], k_ref[0])
    on_diag = (kj + 1) * tk > qi * tq

    @pl.when(on_diag)
    def _():
        mask = _chunk_mask(2 * tq, tq, tk, qi * tq, kj * tk)
        _softmax_update(jnp.where(mask, s, NEG_INF), v_ref[0], m_ref, l_ref, acc_ref)

    @pl.when(jnp.logical_not(on_diag))
    def _():
        _softmax_update(s, v_ref[0], m_ref, l_ref, acc_ref)

    @pl.when(kj == ((qi + 1) * tq - 1) // tk)
    def _():
        o = acc_ref[...] / l_ref[...]
        lam = _diff_lambda(lam_ref, lam_init)
        o_ref[0] = _diff_finish(o[:tq], o[tq:], lam, nw_ref[...], g_ref[...], lam_init)


def _diff_prompt_call(qd, kb, vb, lam_p, norm_w, h2d, cols, lam_init, tq, tk):
    nb, L, dw = qd.shape
    heads = dw // DIFF_V_DIM
    nq = L // tq
    qi, kj = _tri_pairs(nq, tq, tk)
    kern = functools.partial(_diff_prompt_kernel, tq=tq, tk=tk, lam_init=lam_init)
    grid_spec = pltpu.PrefetchScalarGridSpec(
        num_scalar_prefetch=2,
        grid=(nb, heads, int(qi.shape[0])),
        in_specs=[pl.BlockSpec((1, tq, LANES), lambda b, h, p, qi, kj: (b, qi[p], h)),
                  pl.BlockSpec((1, tk, LANES), lambda b, h, p, qi, kj: (b, kj[p], h)),
                  pl.BlockSpec((1, tk, LANES), lambda b, h, p, qi, kj: (b, kj[p], h)),
                  pl.BlockSpec((4, DIFF_QK_DIM), lambda b, h, p, qi, kj: (0, 0)),
                  pl.BlockSpec((1, DIFF_V_DIM), lambda b, h, p, qi, kj: (0, 0)),
                  pl.BlockSpec((tq, LANES), lambda b, h, p, qi, kj: (b * nq + qi[p], cols.dgate // LANES + h))],
        out_specs=pl.BlockSpec((1, tq, LANES), lambda b, h, p, qi, kj: (b, qi[p], h)),
        scratch_shapes=[pltpu.VMEM((2 * tq, LANES), BF16),
                        pltpu.VMEM((2 * tq, 1), F32),
                        pltpu.VMEM((2 * tq, 1), F32),
                        pltpu.VMEM((2 * tq, DIFF_V_DIM), F32)])
    return pl.pallas_call(
        kern, grid_spec=grid_spec,
        out_shape=jax.ShapeDtypeStruct((nb, L, dw), BF16),
        compiler_params=_cparams(("arbitrary", "arbitrary", "arbitrary")),
        name="diff_attn_prompt",
    )(qi, kj, qd, kb, vb, lam_p, norm_w, h2d)


def _diff_sample_kernel(q_ref, kc_ref, vc_ref, kn_ref, vn_ref, lam_ref, nw_ref, g_ref, o_ref,
                        qz_ref, m_ref, l_ref, acc_ref, *, nkc, heads, lam_init):
    j = pl.program_id(1)
    L = q_ref.shape[1]

    @pl.when(j == 0)
    def _():
        for h in range(heads):
            qz_ref[h] = _stack_streams(q_ref[0, :, h * LANES:(h + 1) * LANES])
        m_ref[...] = jnp.full(m_ref.shape, -jnp.inf, F32)
        l_ref[...] = jnp.zeros(l_ref.shape, F32)
        acc_ref[...] = jnp.zeros(acc_ref.shape, F32)

    def step(k_all, v_all):
        for h in range(heads):
            sl = slice(h * LANES, (h + 1) * LANES)
            s = _nt_dot(qz_ref[h], k_all[:, sl].astype(BF16))
            _softmax_update(s, v_all[:, sl].astype(BF16), m_ref.at[h], l_ref.at[h], acc_ref.at[h])

    @pl.when(j < nkc)
    def _():
        step(kc_ref[0, 0], vc_ref[0, 0])

    @pl.when(j == nkc)
    def _():
        step(kn_ref[0], vn_ref[0])
        lam = _diff_lambda(lam_ref, lam_init)
        for h in range(heads):
            sl = slice(h * LANES, (h + 1) * LANES)
            o = acc_ref[h] / l_ref[h]
            o_ref[0, :, sl] = _diff_finish(o[:L], o[L:], lam, nw_ref[...], g_ref[:, sl], lam_init)


def _diff_sample_call(qd, kb, vb, k_cache, v_cache, layer, lam_p, norm_w, h2d, cols, lam_init, tk):
    nb, L, dw = qd.shape
    heads = dw // DIFF_V_DIM
    P = k_cache.shape[2]
    nkc = P // tk
    last = nkc - 1
    kern = functools.partial(_diff_sample_kernel, nkc=nkc, heads=heads, lam_init=lam_init)
    cspec = pl.BlockSpec((1, 1, tk, dw), lambda b, j: (layer, b, jnp.minimum(j, last), 0))
    nspec = pl.BlockSpec((1, L, dw), lambda b, j: (b, 0, 0))
    return pl.pallas_call(
        kern,
        grid=(nb, nkc + 1),
        in_specs=[nspec, cspec, cspec, nspec, nspec,
                  pl.BlockSpec((4, DIFF_QK_DIM), lambda b, j: (0, 0)),
                  pl.BlockSpec((1, DIFF_V_DIM), lambda b, j: (0, 0)),
                  pl.BlockSpec((L, dw), lambda b, j: (b, cols.dgate // dw))],
        out_specs=nspec,
        out_shape=jax.ShapeDtypeStruct((nb, L, dw), BF16),
        scratch_shapes=[pltpu.VMEM((heads, 2 * L, LANES), BF16),
                        pltpu.VMEM((heads, 2 * L, 1), F32),
                        pltpu.VMEM((heads, 2 * L, 1), F32),
                        pltpu.VMEM((heads, 2 * L, DIFF_V_DIM), F32)],
        compiler_params=_cparams(("arbitrary", "arbitrary")),
        name="diff_attn_sample",
    )(qd, k_cache, v_cache, kb, vb, lam_p, norm_w, h2d)


def _mla_finish(acc_ref, l_ref, wuv_ref, g_ref, o_ref, heads, t):
    o = (acc_ref[...] / l_ref[...]).astype(BF16)
    for h in range(heads):
        sl = slice(h * MLA_V_DIM, (h + 1) * MLA_V_DIM)
        om = jnp.dot(o[h * t:(h + 1) * t], wuv_ref[:, sl], preferred_element_type=F32)
        o_ref[0, :, sl] = (om * _silu(g_ref[:, sl])).astype(BF16)


def _mla_prompt_kernel(qi_ref, kj_ref, q_ref, k_ref, wuv_ref, g_ref, o_ref, m_ref, l_ref, acc_ref,
                       *, tq, tk, heads):
    p = pl.program_id(1)
    qi, kj = qi_ref[p], kj_ref[p]

    @pl.when(kj == 0)
    def _():
        m_ref[...] = jnp.full(m_ref.shape, -jnp.inf, F32)
        l_ref[...] = jnp.zeros(l_ref.shape, F32)
        acc_ref[...] = jnp.zeros(acc_ref.shape, F32)

    q = q_ref[0].reshape(heads * tq, MLA_QK_PAD)
    kc = k_ref[0]
    s = _nt_dot(q, kc)
    v = kc[:, :MLA_KV_RANK]
    on_diag = (kj + 1) * tk > qi * tq

    @pl.when(on_diag)
    def _():
        mask = _chunk_mask(heads * tq, tq, tk, qi * tq, kj * tk)
        _softmax_update(jnp.where(mask, s, NEG_INF), v, m_ref, l_ref, acc_ref)

    @pl.when(jnp.logical_not(on_diag))
    def _():
        _softmax_update(s, v, m_ref, l_ref, acc_ref)

    @pl.when(kj == ((qi + 1) * tq - 1) // tk)
    def _():
        _mla_finish(acc_ref, l_ref, wuv_ref, g_ref, o_ref, heads, tq)


def _mla_prompt_call(qcat, kcat, wuv, h2d, cols, tq, tk):
    nb, heads, L, _ = qcat.shape
    mw = heads * MLA_V_DIM
    nq = L // tq
    qi, kj = _tri_pairs(nq, tq, tk)
    kern = functools.partial(_mla_prompt_kernel, tq=tq, tk=tk, heads=heads)
    grid_spec = pltpu.PrefetchScalarGridSpec(
        num_scalar_prefetch=2,
        grid=(nb, int(qi.shape[0])),
        in_specs=[pl.BlockSpec((1, heads, tq, MLA_QK_PAD), lambda b, p, qi, kj: (b, 0, qi[p], 0)),
                  pl.BlockSpec((1, tk, MLA_QK_PAD), lambda b, p, qi, kj: (b, kj[p], 0)),
                  pl.BlockSpec(wuv.shape, lambda b, p, qi, kj: (0, 0)),
                  pl.BlockSpec((tq, mw), lambda b, p, qi, kj: (b * nq + qi[p], cols.mgate // mw))],
        out_specs=pl.BlockSpec((1, tq, mw), lambda b, p, qi, kj: (b, qi[p], 0)),
        scratch_shapes=[pltpu.VMEM((heads * tq, 1), F32),
                        pltpu.VMEM((heads * tq, 1), F32),
                        pltpu.VMEM((heads * tq, MLA_KV_RANK), F32)])
    return pl.pallas_call(
        kern, grid_spec=grid_spec,
        out_shape=jax.ShapeDtypeStruct((nb, L, mw), BF16),
        compiler_params=_cparams(("arbitrary", "arbitrary")),
        name="mla_attn_prompt",
    )(qi, kj, qcat, kcat, wuv, h2d)


def _mla_sample_kernel(q_ref, lc_ref, rc_ref, kn_ref, wuv_ref, g_ref, o_ref, m_ref, l_ref, acc_ref,
                       *, nkc, heads):
    j = pl.program_id(1)
    L = q_ref.shape[2]

    @pl.when(j == 0)
    def _():
        m_ref[...] = jnp.full(m_ref.shape, -jnp.inf, F32)
        l_ref[...] = jnp.zeros(l_ref.shape, F32)
        acc_ref[...] = jnp.zeros(acc_ref.shape, F32)

    q = q_ref[0].reshape(heads * L, MLA_QK_PAD)

    @pl.when(j < nkc)
    def _():
        lat = lc_ref[0, 0].astype(BF16)
        kr = rc_ref[0, 0].astype(BF16)
        s = _nt_dot(q[:, :MLA_KV_RANK], lat) + _nt_dot(q[:, MLA_KV_RANK:MLA_KV_RANK + MLA_ROPE_DIM], kr)
        _softmax_update(s, lat, m_ref, l_ref, acc_ref)

    @pl.when(j == nkc)
    def _():
        kn = kn_ref[0]
        _softmax_update(_nt_dot(q, kn), kn[:, :MLA_KV_RANK], m_ref, l_ref, acc_ref)
        _mla_finish(acc_ref, l_ref, wuv_ref, g_ref, o_ref, heads, L)


def _mla_sample_call(qcat, kcat, lat_cache, kr_cache, layer, wuv, h2d, cols, tk):
    nb, heads, L, _ = qcat.shape
    mw = heads * MLA_V_DIM
    P = lat_cache.shape[2]
    nkc = P // tk
    last = nkc - 1
    kern = functools.partial(_mla_sample_kernel, nkc=nkc, heads=heads)
    return pl.pallas_call(
        kern,
        grid=(nb, nkc + 1),
        in_specs=[pl.BlockSpec((1, heads, L, MLA_QK_PAD), lambda b, j: (b, 0, 0, 0)),
                  pl.BlockSpec((1, 1, tk, MLA_KV_RANK), lambda b, j: (layer, b, jnp.minimum(j, last), 0)),
                  pl.BlockSpec((1, 1, tk, MLA_ROPE_DIM), lambda b, j: (layer, b, jnp.minimum(j, last), 0)),
                  pl.BlockSpec((1, L, MLA_QK_PAD), lambda b, j: (b, 0, 0)),
                  pl.BlockSpec(wuv.shape, lambda b, j: (0, 0)),
                  pl.BlockSpec((L, mw), lambda b, j: (b, cols.mgate // mw))],
        out_specs=pl.BlockSpec((1, L, mw), lambda b, j: (b, 0, 0)),
        out_shape=jax.ShapeDtypeStruct((nb, L, mw), BF16),
        scratch_shapes=[pltpu.VMEM((heads * L, 1), F32),
                        pltpu.VMEM((heads * L, 1), F32),
                        pltpu.VMEM((heads * L, MLA_KV_RANK), F32)],
        compiler_params=_cparams(("arbitrary", "arbitrary")),
        name="mla_attn_sample",
    )(qcat, lat_cache, kr_cache, kcat, wuv, h2d)


def _outproj_kernel(ys_ref, yd_ref, ym_ref, w_ref, x_ref, g_ref, r_ref, *, alpha):
    bt, lt, _ = ys_ref.shape
    flat = lambda ref: ref[...].reshape(bt * lt, ref.shape[2])
    mix = jnp.concatenate([flat(ys_ref), flat(yd_ref), flat(ym_ref)], axis=1)
    acc = jnp.dot(mix, w_ref[...], preferred_element_type=F32)
    r_ref[...] = alpha * x_ref[...] + g_ref[...] * acc.reshape(bt, lt, acc.shape[1])


def _outproj_call(y_ssd, y_diff, y_mla, w_out, x, gate, alpha):
    nb, L, d = x.shape
    bt, lt = _seq_tiles(nb, L, 512)
    tn = 1024
    yspec = lambda width: pl.BlockSpec((bt, lt, width), lambda j, b, i: (b, i, 0))
    return pl.pallas_call(
        functools.partial(_outproj_kernel, alpha=alpha),
        grid=(d // tn, nb // bt, L // lt),
        in_specs=[yspec(y_ssd.shape[2]), yspec(y_diff.shape[2]), yspec(y_mla.shape[2]),
                  pl.BlockSpec((w_out.shape[0], tn), lambda j, b, i: (0, j)),
                  pl.BlockSpec((bt, lt, tn), lambda j, b, i: (b, i, j)),
                  pl.BlockSpec((bt, 1, tn), lambda j, b, i: (b, 0, j))],
        out_specs=pl.BlockSpec((bt, lt, tn), lambda j, b, i: (b, i, j)),
        out_shape=jax.ShapeDtypeStruct((nb, L, d), F32),
        compiler_params=_cparams(("arbitrary", "arbitrary", "arbitrary")),
        name="out_proj",
    )(y_ssd, y_diff, y_mla, w_out, x, gate)


def _layernorm(r, g, b):
    mu = jnp.mean(r, axis=-1, keepdims=True)
    var = jnp.mean(jnp.square(r - mu), axis=-1, keepdims=True)
    return (r - mu) * lax.rsqrt(var + 1e-5) * g + b


def _ln_kernel(r_ref, g_ref, b_ref, x_ref):
    x_ref[...] = _layernorm(r_ref[...], g_ref[...], b_ref[...])


def _ln_mod_kernel(r_ref, g_ref, b_ref, sc_ref, sh_ref, x_ref, u_ref):
    x = _layernorm(r_ref[...], g_ref[...], b_ref[...])
    x_ref[...] = x
    u_ref[...] = (x * (1.0 + sc_ref[...]) + sh_ref[...]).astype(BF16)


def _ln_call(r, g, b, scale=None, shift=None):
    nb, L, d = r.shape
    bt, lt = _seq_tiles(nb, L, 256)
    xspec = pl.BlockSpec((bt, lt, d), lambda i, j: (i, j, 0))
    wspec = pl.BlockSpec((1, 1, d), lambda i, j: (0, 0, 0))
    sspec = pl.BlockSpec((bt, 1, d), lambda i, j: (i, 0, 0))
    g3, b3 = g.reshape(1, 1, d), b.reshape(1, 1, d)
    if scale is None:
        return pl.pallas_call(
            _ln_kernel, grid=(nb // bt, L // lt),
            in_specs=[xspec, wspec, wspec], out_specs=xspec,
            out_shape=jax.ShapeDtypeStruct((nb, L, d), F32),
            compiler_params=_cparams(("arbitrary", "arbitrary")), name="layernorm",
        )(r, g3, b3), None
    return pl.pallas_call(
        _ln_mod_kernel, grid=(nb // bt, L // lt),
        in_specs=[xspec, wspec, wspec, sspec, sspec], out_specs=[xspec, xspec],
        out_shape=[jax.ShapeDtypeStruct((nb, L, d), F32), jax.ShapeDtypeStruct((nb, L, d), BF16)],
        compiler_params=_cparams(("arbitrary", "arbitrary")), name="layernorm_modulate",
    )(r, g3, b3, scale, shift)


def _ssd_chunk(L):
    for t in (128, 64, 32, 16, 8):
        if L % t == 0:
            return t
    raise ValueError(f"sequence length {L} is not a multiple of 8")


def _attn_tile(L, want):
    t = want
    while L % t:
        t //= 2
    return t


def _layer(x, u, mod_l, next_mod, wl, cols, layer_idx, depth, caches):
    nb, L, d = x.shape
    heads = cols.heads
    mla_heads = cols.mla_w // MLA_V_DIM
    _, _, gate = mod_l
    k_cache, v_cache, lat_cache, kr_cache, st0, conv0 = caches
    P = 0 if k_cache is None else k_cache.shape[2]
    pos = P + jnp.arange(L, dtype=jnp.int32)

    h2d = _inproj_call(u.reshape(nb * L, d), wl["w_in"], cols.tn)

    T = _ssd_chunk(L)
    y_ssd, ssm_new, conv_new = _ssd_call(h2d, conv0, st0, wl["conv_w"], wl["conv_b"], wl["dt_bias"], wl["a_log"],
                                         wl["d_skip"], wl["ssd_norm_w"], wl["e_mat"], wl["et_mat"], cols, nb, L, T)

    tabs_d = _rope_tables(pos, DIFF_ROT, DIFF_QK_DIM, LANES)
    tabs_m = _rope_tables(pos, MLA_ROPE_DIM, MLA_ROPE_DIM, MLA_ROPE_DIM)
    qd, k32, kb, v32, vb, lat32, kr32, kcat = _prep_call(h2d, tabs_d, tabs_m, wl["kv_norm_w"], cols, nb, L)
    qcat = _mlaq_call(h2d, wl["q_norm_w"], wl["w_uq"], wl["w_uk"], tabs_m, cols, nb, L, mla_heads)

    lam_init = 0.8 - 0.6 * math.exp(-0.3 * layer_idx)
    if k_cache is None:
        tq = _attn_tile(L, 512)
        y_diff = _diff_prompt_call(qd, kb, vb, wl["lam"], wl["diff_norm_w"], h2d, cols, lam_init, tq, tq)
        tqm = _attn_tile(L, 256)
        y_mla = _mla_prompt_call(qcat, kcat, wl["w_uv"], h2d, cols, tqm, _attn_tile(L, 512))
    else:
        tk = _attn_tile(P, 512)
        y_diff = _diff_sample_call(qd, kb, vb, k_cache, v_cache, layer_idx, wl["lam"], wl["diff_norm_w"], h2d,
                                   cols, lam_init, tk)
        y_mla = _mla_sample_call(qcat, kcat, lat_cache, kr_cache, layer_idx, wl["w_uv"], h2d, cols, tk)

    alpha = (2 * depth) ** 0.25
    r = _outproj_call(y_ssd.reshape(nb, L, cols.ssd_w), y_diff, y_mla, wl["w_out"], x, gate, alpha)
    if next_mod is None:
        x_new, u_new = _ln_call(r, wl["ln_g"], wl["ln_b"])
    else:
        x_new, u_new = _ln_call(r, wl["ln_g"], wl["ln_b"], next_mod[1], next_mod[0])
    k_out = k32.reshape(nb, L, cols.diff_w // (2 * DIFF_QK_DIM), 2, DIFF_QK_DIM)
    v_out = v32.reshape(nb, L, cols.diff_w // DIFF_V_DIM, DIFF_V_DIM)
    return x_new, u_new, (k_out, v_out, lat32, kr32, ssm_new, conv_new)


def _layer_weights(l, cols, w_in, conv_w, conv_b, dt_bias, a_log, d_skip, ssd_norm_w, lambda_q1, lambda_k1,
                   lambda_q2, lambda_k2, diff_norm_w, mla_q_norm_w, mla_kv_norm_w, w_uq, w_uk, w_uv, w_out,
                   ln_g, ln_b):
    heads = cols.heads
    mla_heads = cols.mla_w // MLA_V_DIM
    lane_pad = lambda v: jnp.pad(v, (MLA_ROPE_DIM, LANES - MLA_ROPE_DIM - heads)).reshape(1, LANES)
    qk = MLA_NOPE_DIM + MLA_ROPE_DIM
    wq = w_uq[l].reshape(MLA_Q_RANK, mla_heads, qk)
    wq_nope = wq[:, :, :MLA_NOPE_DIM].reshape(MLA_Q_RANK, mla_heads * MLA_NOPE_DIM)
    wq_rope = jnp.pad(wq[:, :, MLA_NOPE_DIM:], ((0, 0), (0, 0), (0, LANES - MLA_ROPE_DIM)))
    wq_p = jnp.concatenate([wq_nope, wq_rope.reshape(MLA_Q_RANK, mla_heads * LANES)], axis=1).astype(BF16)
    eh = np.zeros((LANES, cols.ssd_w), np.float32)
    for h in range(heads):
        eh[MLA_ROPE_DIM + h, h * SSD_HEAD_DIM:(h + 1) * SSD_HEAD_DIM] = 1.0
    return dict(
        w_in=_permute_w_in(w_in[l], cols),
        conv_w=conv_w[l], conv_b=conv_b[l].reshape(1, -1),
        dt_bias=lane_pad(dt_bias[l]), a_log=lane_pad(a_log[l]),
        d_skip=jnp.repeat(d_skip[l], SSD_HEAD_DIM).reshape(1, -1),
        ssd_norm_w=ssd_norm_w[l].reshape(1, -1),
        e_mat=jnp.asarray(eh, BF16), et_mat=jnp.asarray(eh.T, BF16),
        lam=jnp.stack([lambda_q1[l], lambda_k1[l], lambda_q2[l], lambda_k2[l]]),
        diff_norm_w=diff_norm_w[l].reshape(1, -1),
        q_norm_w=mla_q_norm_w[l].reshape(1, -1), kv_norm_w=mla_kv_norm_w[l].reshape(1, -1),
        w_uq=wq_p,
        w_uk=jnp.transpose(w_uk[l], (1, 2, 0)).astype(BF16),
        w_uv=w_uv[l].reshape(MLA_KV_RANK, mla_heads * MLA_V_DIM).astype(BF16),
        w_out=w_out[l].astype(BF16),
        ln_g=ln_g[l], ln_b=ln_b[l],
    )


def kernel(x_prompt, x_sample, cache_diff_k, cache_diff_v, cache_mla_latent, cache_mla_krope, state_ssm, state_conv,
           c_prompt, c_sample, w_mod, b_mod, w_in, conv_w, conv_b, dt_bias, a_log, d_skip, ssd_norm_w, lambda_q1,
           lambda_k1, lambda_q2, lambda_k2, diff_norm_w, mla_q_norm_w, mla_kv_norm_w, w_uq, w_uk, w_uv, w_out,
           ln_g, ln_b):
    depth = w_in.shape[0]
    bp, _, d = x_prompt.shape
    bs = x_sample.shape[0]
    cols = _Cols(d)
    heads = cols.heads

    rows = -(-(bp + bs) // 8) * 8
    c_all = jnp.concatenate([c_prompt, c_sample, jnp.zeros((rows - bp - bs, d), F32)], axis=0)
    mod = _mod_call(c_all, w_mod, b_mod)

    def mods(l, lo, n):
        part = lambda k: mod[l, lo:lo + n, k * d:(k + 1) * d].reshape(n, 1, d)
        return part(0), part(1), part(2)

    pdiff = cache_diff_k.shape[2]
    kc = cache_diff_k.reshape(depth, bs, pdiff, cols.diff_w)
    vc = cache_diff_v.reshape(depth, bs, pdiff, cols.diff_w)
    zero_state = jnp.zeros((bp, heads, SSD_HEAD_DIM, SSD_STATE), F32)
    zero_conv = jnp.zeros((bp, SSD_CONV - 1, cols.conv_dim), F32)

    hp, hs = x_prompt, x_sample
    mp, ms = mods(0, 0, bp), mods(0, bp, bs)
    up = _modulate_call(hp, mp[1], mp[0])
    us = _modulate_call(hs, ms[1], ms[0])
    st_p, st_s = [], []
    for l in range(depth):
        wl = _layer_weights(l, cols, w_in, conv_w, conv_b, dt_bias, a_log, d_skip, ssd_norm_w, lambda_q1, lambda_k1,
                            lambda_q2, lambda_k2, diff_norm_w, mla_q_norm_w, mla_kv_norm_w, w_uq, w_uk, w_uv, w_out,
                            ln_g, ln_b)
        nmp = mods(l + 1, 0, bp) if l + 1 < depth else None
        nms = mods(l + 1, bp, bs) if l + 1 < depth else None
        hp, up, sp = _layer(hp, up, mp, nmp, wl, cols, l, depth, (None, None, None, None, zero_state, zero_conv))
        hs, us, ss = _layer(hs, us, ms, nms, wl, cols, l, depth,
                            (kc, vc, cache_mla_latent, cache_mla_krope, state_ssm[l], state_conv[l]))
        st_p.append(sp)
        st_s.append(ss)
        mp, ms = nmp, nms
    stack = lambda sts, i: jnp.stack([s[i] for s in sts])
    return ((hp, hs) + tuple(stack(st_p, i) for i in range(6)) + tuple(stack(st_s, i) for i in range(6)))
```

```python
import functools
import math

import numpy as np
import jax
import jax.numpy as jnp
from jax import lax
from jax.experimental import pallas as pl
from jax.experimental.pallas import tpu as pltpu

F32 = jnp.float32
BF16 = jnp.bfloat16

CHUNK = 64
ROPE_THETA = 500000.0
NEG_INF = -1e30
SSD_HEAD_DIM = 64
SSD_GROUPS = 4
SSD_STATE = 128
SSD_CONV = 4
DIFF_QK_DIM = 64
DIFF_V_DIM = 128
DIFF_ROT = DIFF_QK_DIM // 4
DIFF_SCALE = DIFF_QK_DIM ** -0.5
MLA_V_DIM = 128
MLA_NOPE_DIM = 128
MLA_ROPE_DIM = 64
MLA_Q_RANK = 768
MLA_KV_RANK = 256
MLA_SCALE = (MLA_NOPE_DIM + MLA_ROPE_DIM) ** -0.5
MLA_QK_PAD = MLA_KV_RANK + 128
LOG2E = math.log2(math.e)

LANES = 128
VMEM_LIMIT = 56 * 1024 * 1024


def _cparams(sem):
    return pltpu.CompilerParams(dimension_semantics=sem, vmem_limit_bytes=VMEM_LIMIT)


def _silu(x):
    return x * jax.nn.sigmoid(x)


def _nt_dot(a, b):
    return lax.dot_general(a, b, (((1,), (1,)), ((), ())), preferred_element_type=F32)


def _split2(v):
    hi = v.astype(BF16)
    lo = (v - hi.astype(F32)).astype(BF16)
    return hi, lo


def _split3(v):
    hi = v.astype(BF16)
    r = v - hi.astype(F32)
    mid = r.astype(BF16)
    lo = (r - mid.astype(F32)).astype(BF16)
    return hi, mid, lo


def _mod_kernel(c_ref, w_ref, b_ref, o_ref):
    a = _silu(c_ref[...]).astype(BF16)
    o_ref[0] = jnp.dot(a, w_ref[0].astype(BF16), preferred_element_type=F32) + b_ref[0]


def _mod_call(c_all, w_mod, b_mod):
    depth, d, n3 = w_mod.shape
    rows = c_all.shape[0]
    tn = 512
    return pl.pallas_call(
        _mod_kernel,
        grid=(depth, n3 // tn),
        in_specs=[pl.BlockSpec((rows, d), lambda l, j: (0, 0)),
                  pl.BlockSpec((1, d, tn), lambda l, j: (l, 0, j)),
                  pl.BlockSpec((1, 1, tn), lambda l, j: (l, 0, j))],
        out_specs=pl.BlockSpec((1, rows, tn), lambda l, j: (l, 0, j)),
        out_shape=jax.ShapeDtypeStruct((depth, rows, n3), F32),
        compiler_params=_cparams(("arbitrary", "arbitrary")),
        name="adaln_mod",
    )(c_all, w_mod, b_mod.reshape(depth, 1, n3))


def _modulate_kernel(x_ref, sc_ref, sh_ref, u_ref):
    u_ref[...] = (x_ref[...] * (1.0 + sc_ref[...]) + sh_ref[...]).astype(BF16)


def _seq_tiles(nb, L, rows):
    if L >= rows:
        return 1, rows
    return min(nb, rows // L), L


def _modulate_call(x, scale, shift):
    nb, L, d = x.shape
    bt, lt = _seq_tiles(nb, L, 256)
    return pl.pallas_call(
        _modulate_kernel,
        grid=(nb // bt, L // lt),
        in_specs=[pl.BlockSpec((bt, lt, d), lambda i, j: (i, j, 0)),
                  pl.BlockSpec((bt, 1, d), lambda i, j: (i, 0, 0)),
                  pl.BlockSpec((bt, 1, d), lambda i, j: (i, 0, 0))],
        out_specs=pl.BlockSpec((bt, lt, d), lambda i, j: (i, j, 0)),
        out_shape=jax.ShapeDtypeStruct((nb, L, d), BF16),
        compiler_params=_cparams(("arbitrary", "arbitrary")),
        name="modulate",
    )(x, scale, shift)


class _Cols:
    def __init__(self, d_model):
        self.ssd_w = d_model // 2
        self.diff_w = d_model // 4
        self.mla_w = d_model // 4
        self.heads = self.ssd_w // SSD_HEAD_DIM
        self.conv_dim = self.ssd_w + 2 * SSD_GROUPS * SSD_STATE
        self.in_sizes = (self.ssd_w, self.conv_dim, self.heads, self.diff_w, self.diff_w, self.diff_w,
                         self.diff_w, MLA_Q_RANK, MLA_KV_RANK, MLA_ROPE_DIM, self.mla_w)
        o = 0
        self.xbc = o; o += self.conv_dim
        self.dq = o; o += self.diff_w
        self.z = o; o += self.ssd_w
        self.dk = o; o += self.diff_w
        self.dv = o; o += self.diff_w
        self.dgate = o; o += self.diff_w
        self.mgate = o; o += self.mla_w
        self.cq = o; o += MLA_Q_RANK + MLA_KV_RANK
        self.krdt = o; o += LANES
        self.used = o
        self.tn = 1280
        self.total = -(-o // self.tn) * self.tn


def _permute_w_in(w, cols):
    wt = w.T
    offs = np.concatenate([[0], np.cumsum(cols.in_sizes)])
    seg = lambda i: wt[offs[i]:offs[i + 1]]
    pad_dt = LANES - MLA_ROPE_DIM - cols.heads
    parts = [seg(1), seg(3), seg(0), seg(4), seg(5), seg(6), seg(10), seg(7), seg(8), seg(9), seg(2),
             jnp.zeros((pad_dt + cols.total - cols.used, wt.shape[1]), w.dtype)]
    return jnp.concatenate(parts, axis=0).astype(BF16)


def _matmul_nt_kernel(x_ref, w_ref, o_ref):
    o_ref[...] = _nt_dot(x_ref[...], w_ref[...])


def _inproj_call(u2d, wp, tn):
    m, k = u2d.shape
    n = wp.shape[0]
    tm = min(m, 512)
    return pl.pallas_call(
        _matmul_nt_kernel,
        grid=(n // tn, m // tm),
        in_specs=[pl.BlockSpec((tm, k), lambda j, i: (i, 0)),
                  pl.BlockSpec((tn, k), lambda j, i: (j, 0))],
        out_specs=pl.BlockSpec((tm, tn), lambda j, i: (i, j)),
        out_shape=jax.ShapeDtypeStruct((m, n), F32),
        compiler_params=_cparams(("arbitrary", "arbitrary")),
        name="in_proj",
    )(u2d, wp)


def _ssd_kernel(xbc_ref, z_ref, dtb_ref, conv0_ref, st0_ref, cw_ref, cb_ref, dtbias_ref, alog_ref,
                dskip_ref, nw_ref, e_ref, et_ref, y_ref, st_ref, convo_ref, ext_ref, *, T, nchunks, heads):
    c = pl.program_id(1)
    ssd_w = heads * SSD_HEAD_DIM
    gw = ssd_w // SSD_GROUPS
    hpg = heads // SSD_GROUPS
    conv_dim = ext_ref.shape[1]
    dt_lo = MLA_ROPE_DIM

    @pl.when(c == 0)
    def _():
        ext_ref[0:8, :] = jnp.zeros((8, conv_dim), F32)
        ext_ref[8 - (SSD_CONV - 1):8, :] = conv0_ref[0]
        st_ref[0] = st0_ref[0]

    ext_ref[8:8 + T, :] = xbc_ref[...]
    acc = cb_ref[...]
    for j in range(SSD_CONV):
        lo = 8 - (SSD_CONV - 1) + j
        acc = acc + ext_ref[lo:lo + T, :] * cw_ref[j:j + 1, :]
    ext_ref[0:8, :] = ext_ref[T:T + 8, :]

    @pl.when(c == nchunks - 1)
    def _():
        convo_ref[0] = ext_ref[8 - (SSD_CONV - 1):8, :]

    xact = _silu(acc)
    xs = xact[:, :ssd_w]
    bm = xact[:, ssd_w:ssd_w + SSD_GROUPS * SSD_STATE]
    cm = xact[:, ssd_w + SSD_GROUPS * SSD_STATE:]

    lane = lax.broadcasted_iota(jnp.int32, (1, LANES), 1)
    is_dt = (lane >= dt_lo) & (lane < dt_lo + heads)
    xdt = dtb_ref[...] + dtbias_ref[...]
    dt = jnp.where(is_dt, jnp.maximum(xdt, 0.0) + jnp.log1p(jnp.exp(-jnp.abs(xdt))), 0.0)
    a_neg = jnp.where(is_dt, -jnp.exp(alog_ref[...]), 0.0)
    da = dt * a_neg

    row = lax.broadcasted_iota(jnp.int32, (T, T), 0)
    col = lax.broadcasted_iota(jnp.int32, (T, T), 1)
    causal = row >= col
    tril = causal.astype(BF16)
    eye = (lax.broadcasted_iota(jnp.int32, (LANES, LANES), 0)
           == lax.broadcasted_iota(jnp.int32, (LANES, LANES), 1)).astype(BF16)

    da3 = _split3(da)
    a_cs = sum(jnp.dot(tril, p, preferred_element_type=F32) for p in da3)
    a3 = _split3(a_cs)
    a_cs_t = sum(_nt_dot(eye, p) for p in a3)
    a_last = a_cs[T - 1:T, :]
    ea = jnp.exp(a_cs)
    te = jnp.exp(a_last - a_cs)

    e_mat = e_ref[...]

    def expand(v):
        hi, lo = _split2(v)
        return jnp.dot(hi, e_mat, preferred_element_type=F32) + jnp.dot(lo, e_mat, preferred_element_type=F32)

    dt_x = expand(dt)
    ea_x = expand(ea)
    te_x = expand(te)
    cd_col = jnp.exp(a_cs_t[:, T - 1:T])
    cd_b = jnp.broadcast_to(cd_col, (LANES, SSD_STATE))
    cdh, cdl = _split2(cd_b)
    et_mat = et_ref[...]
    cd_full = (jnp.dot(et_mat, cdh, preferred_element_type=F32)
               + jnp.dot(et_mat, cdl, preferred_element_type=F32))

    xd = xs * dt_x
    xde = (xd * te_x).astype(BF16)
    lane_p = lax.broadcasted_iota(jnp.int32, (T, LANES), 1)
    lower_half = lane_p < SSD_HEAD_DIM

    for g in range(SSD_GROUPS):
        cg = cm[:, g * SSD_STATE:(g + 1) * SSD_STATE].astype(BF16)
        bg = bm[:, g * SSD_STATE:(g + 1) * SSD_STATE].astype(BF16)
        cbm = _nt_dot(cg, bg)
        st_g = st_ref[0, g * hpg:(g + 1) * hpg].reshape(gw, SSD_STATE)
        y_off = _nt_dot(cg, st_g.astype(BF16)) * ea_x[:, g * gw:(g + 1) * gw]
        pieces = []
        for q in range(hpg // 2):
            c0 = g * gw + q * LANES
            xd_pair = xd[:, c0:c0 + LANES].astype(BF16)
            ys = []
            for h in (g * hpg + 2 * q, g * hpg + 2 * q + 1):
                seg = a_cs[:, dt_lo + h:dt_lo + h + 1] - a_cs_t[dt_lo + h:dt_lo + h + 1, :]
                decay = jnp.exp(jnp.where(causal, seg, -jnp.inf))
                ys.append(jnp.dot((cbm * decay).astype(BF16), xd_pair, preferred_element_type=F32))
            pieces.append(jnp.where(lower_half, ys[0], ys[1]))
        y_g = jnp.concatenate(pieces, axis=1) + y_off
        upd = lax.dot_general(xde[:, g * gw:(g + 1) * gw], bg, (((0,), (0,)), ((), ())),
                              preferred_element_type=F32)
        st_new = st_g * cd_full[g * gw:(g + 1) * gw, :] + upd
        st_ref[0, g * hpg:(g + 1) * hpg] = st_new.reshape(hpg, SSD_HEAD_DIM, SSD_STATE)
        y_g = y_g + dskip_ref[:, g * gw:(g + 1) * gw] * xs[:, g * gw:(g + 1) * gw]
        yg = y_g * _silu(z_ref[:, g * gw:(g + 1) * gw])
        ms = jnp.mean(yg * yg, axis=-1, keepdims=True)
        y_ref[:, g * gw:(g + 1) * gw] = (yg * lax.rsqrt(ms + 1e-6) * nw_ref[:, g * gw:(g + 1) * gw]).astype(BF16)


def _ssd_call(h2d, conv0, st0, conv_w, conv_b, dtbias_p, alog_p, dskip_x, norm_w, e_mat, et_mat, cols, nb, L, T):
    nchunks = L // T
    heads = cols.heads
    ssd_w = cols.ssd_w
    cd = cols.conv_dim
    kern = functools.partial(_ssd_kernel, T=T, nchunks=nchunks, heads=heads)
    row = lambda b, c: b * nchunks + c
    const2 = lambda b, c: (0, 0)
    return pl.pallas_call(
        kern,
        grid=(nb, nchunks),
        in_specs=[pl.BlockSpec((T, cd), lambda b, c: (row(b, c), cols.xbc // cd)),
                  pl.BlockSpec((T, ssd_w), lambda b, c: (row(b, c), cols.z // ssd_w)),
                  pl.BlockSpec((T, LANES), lambda b, c: (row(b, c), cols.krdt // LANES)),
                  pl.BlockSpec((1, SSD_CONV - 1, cd), lambda b, c: (b, 0, 0)),
                  pl.BlockSpec((1, heads, SSD_HEAD_DIM, SSD_STATE), lambda b, c: (b, 0, 0, 0)),
                  pl.BlockSpec((SSD_CONV, cd), const2),
                  pl.BlockSpec((1, cd), const2),
                  pl.BlockSpec((1, LANES), const2),
                  pl.BlockSpec((1, LANES), const2),
                  pl.BlockSpec((1, ssd_w), const2),
                  pl.BlockSpec((1, ssd_w), const2),
                  pl.BlockSpec((LANES, ssd_w), const2),
                  pl.BlockSpec((ssd_w, LANES), const2)],
        out_specs=[pl.BlockSpec((T, ssd_w), lambda b, c: (row(b, c), 0)),
                   pl.BlockSpec((1, heads, SSD_HEAD_DIM, SSD_STATE), lambda b, c: (b, 0, 0, 0)),
                   pl.BlockSpec((1, SSD_CONV - 1, cd), lambda b, c: (b, 0, 0))],
        out_shape=[jax.ShapeDtypeStruct((nb * L, ssd_w), BF16),
                   jax.ShapeDtypeStruct((nb, heads, SSD_HEAD_DIM, SSD_STATE), F32),
                   jax.ShapeDtypeStruct((nb, SSD_CONV - 1, cd), F32)],
        scratch_shapes=[pltpu.VMEM((T + 8, cd), F32)],
        compiler_params=_cparams(("arbitrary", "arbitrary")),
        name="ssd_scan",
    )(h2d, h2d, h2d, conv0, st0, conv_w, conv_b, dtbias_p, alog_p, dskip_x, norm_w, e_mat, et_mat)


def _rope_tables(pos, rot_dim, period, width):
    half = rot_dim // 2
    inv = ROPE_THETA ** (-jnp.arange(half, dtype=F32) * (2.0 / rot_dim))
    ang = pos.astype(F32)[:, None] * inv[None, :]
    cos, sin = jnp.cos(ang), jnp.sin(ang)
    lane = np.arange(LANES)
    inner = lane % period
    idx = jnp.asarray(inner % half)
    first = jnp.asarray((inner < half) & (lane < width))
    second = jnp.asarray((inner >= half) & (inner < rot_dim) & (lane < width))
    keep = jnp.asarray((inner >= rot_dim) & (lane < width))
    cos_l, sin_l = cos[:, idx], sin[:, idx]
    cos_t = jnp.where(first | second, cos_l, jnp.where(keep, 1.0, 0.0))
    sin_a = jnp.where(first, -sin_l, 0.0)
    sin_b = jnp.where(second, sin_l, 0.0)
    return cos_t.astype(F32), sin_a.astype(F32), sin_b.astype(F32)


def _rope_tile(x, cos_t, sin_a, sin_b, half):
    return (x * cos_t + pltpu.roll(x, LANES - half, 1) * sin_a + pltpu.roll(x, half, 1) * sin_b)


def _rmsnorm(x, w):
    return x * lax.rsqrt(jnp.mean(x * x, axis=-1, keepdims=True) + 1e-6) * w


def _prep_kernel(dq_ref, dk_ref, dv_ref, cq_ref, kr_ref, cd_ref, sad_ref, sbd_ref, cm_ref, sam_ref, sbm_ref,
                 kvw_ref, qd_ref, k32_ref, kb_ref, v32_ref, vb_ref, lat_ref, kro_ref, kcat_ref):
    cos_d, sa_d, sb_d = cd_ref[...], sad_ref[...], sbd_ref[...]
    width = dq_ref.shape[1]
    for c in range(width // LANES):
        sl = slice(c * LANES, (c + 1) * LANES)
        q = _rope_tile(dq_ref[:, sl], cos_d, sa_d, sb_d, DIFF_ROT // 2)
        qd_ref[0, :, sl] = (q * (DIFF_SCALE * LOG2E)).astype(BF16)
        k = _rope_tile(dk_ref[:, sl], cos_d, sa_d, sb_d, DIFF_ROT // 2)
        k32_ref[0, :, sl] = k
        kb_ref[0, :, sl] = k.astype(BF16)
    v = dv_ref[...]
    v32_ref[0] = v
    vb_ref[0] = v.astype(BF16)
    lat = _rmsnorm(cq_ref[:, MLA_Q_RANK:MLA_Q_RANK + MLA_KV_RANK], kvw_ref[...])
    lat_ref[0] = lat
    kr = _rope_tile(kr_ref[...], cm_ref[...], sam_ref[...], sbm_ref[...], MLA_ROPE_DIM // 2)
    kro_ref[0] = kr[:, :MLA_ROPE_DIM]
    kcat_ref[0, :, :MLA_KV_RANK] = lat.astype(BF16)
    kcat_ref[0, :, MLA_KV_RANK:] = kr.astype(BF16)


def _prep_call(h2d, tabs_d, tabs_m, kv_w, cols, nb, L):
    tm = min(L, 256)
    nt = L // tm
    dw = cols.diff_w
    row = lambda b, i: b * nt + i
    hspec = lambda width, off: pl.BlockSpec((tm, width), lambda b, i: (row(b, i), off // width))
    tspec = pl.BlockSpec((tm, LANES), lambda b, i: (i, 0))
    ospec = lambda width: pl.BlockSpec((1, tm, width), lambda b, i: (b, i, 0))
    return pl.pallas_call(
        _prep_kernel,
        grid=(nb, nt),
        in_specs=[hspec(dw, cols.dq), hspec(dw, cols.dk), hspec(dw, cols.dv),
                  hspec(MLA_Q_RANK + MLA_KV_RANK, cols.cq), hspec(LANES, cols.krdt),
                  tspec, tspec, tspec, tspec, tspec, tspec,
                  pl.BlockSpec((1, MLA_KV_RANK), lambda b, i: (0, 0))],
        out_specs=[ospec(dw), ospec(dw), ospec(dw), ospec(dw), ospec(dw),
                   ospec(MLA_KV_RANK), ospec(MLA_ROPE_DIM), ospec(MLA_QK_PAD)],
        out_shape=[jax.ShapeDtypeStruct((nb, L, dw), BF16),
                   jax.ShapeDtypeStruct((nb, L, dw), F32),
                   jax.ShapeDtypeStruct((nb, L, dw), BF16),
                   jax.ShapeDtypeStruct((nb, L, dw), F32),
                   jax.ShapeDtypeStruct((nb, L, dw), BF16),
                   jax.ShapeDtypeStruct((nb, L, MLA_KV_RANK), F32),
                   jax.ShapeDtypeStruct((nb, L, MLA_ROPE_DIM), F32),
                   jax.ShapeDtypeStruct((nb, L, MLA_QK_PAD), BF16)],
        compiler_params=_cparams(("arbitrary", "arbitrary")),
        name="attn_prep",
    )(h2d, h2d, h2d, h2d, h2d, *tabs_d, *tabs_m, kv_w)


def _mlaq_kernel(cq_ref, qw_ref, wuq_ref, wuk_ref, cm_ref, sam_ref, sbm_ref, o_ref, *, heads):
    cqn = _rmsnorm(cq_ref[:, :MLA_Q_RANK], qw_ref[...]).astype(BF16)
    qm = jnp.dot(cqn, wuq_ref[...], preferred_element_type=F32)
    cos_m, sa_m, sb_m = cm_ref[...], sam_ref[...], sbm_ref[...]
    nope_w = heads * MLA_NOPE_DIM
    for h in range(heads):
        nope = qm[:, h * MLA_NOPE_DIM:(h + 1) * MLA_NOPE_DIM].astype(BF16)
        ql = jnp.dot(nope, wuk_ref[h], preferred_element_type=F32) * (MLA_SCALE * LOG2E)
        qr = _rope_tile(qm[:, nope_w + h * LANES:nope_w + (h + 1) * LANES], cos_m, sa_m, sb_m,
                        MLA_ROPE_DIM // 2) * (MLA_SCALE * LOG2E)
        o_ref[0, h, :, :MLA_KV_RANK] = ql.astype(BF16)
        o_ref[0, h, :, MLA_KV_RANK:] = qr.astype(BF16)


def _mlaq_call(h2d, q_w, wuq_p, wuk_t, tabs_m, cols, nb, L, heads):
    tm = min(L, 256)
    nt = L // tm
    width = MLA_Q_RANK + MLA_KV_RANK
    tspec = pl.BlockSpec((tm, LANES), lambda b, i: (i, 0))
    return pl.pallas_call(
        functools.partial(_mlaq_kernel, heads=heads),
        grid=(nb, nt),
        in_specs=[pl.BlockSpec((tm, width), lambda b, i: (b * nt + i, cols.cq // width)),
                  pl.BlockSpec((1, MLA_Q_RANK), lambda b, i: (0, 0)),
                  pl.BlockSpec(wuq_p.shape, lambda b, i: (0, 0)),
                  pl.BlockSpec(wuk_t.shape, lambda b, i: (0, 0, 0)),
                  tspec, tspec, tspec],
        out_specs=pl.BlockSpec((1, heads, tm, MLA_QK_PAD), lambda b, i: (b, 0, i, 0)),
        out_shape=jax.ShapeDtypeStruct((nb, heads, L, MLA_QK_PAD), BF16),
        compiler_params=_cparams(("arbitrary", "arbitrary")),
        name="mla_q",
    )(h2d, q_w, wuq_p, wuk_t, *tabs_m)


def _softmax_init(m_ref, l_ref, acc_ref):
    m_ref[...] = jnp.full(m_ref.shape, -jnp.inf, F32)
    l_ref[...] = jnp.zeros(l_ref.shape, F32)
    acc_ref[...] = jnp.zeros(acc_ref.shape, F32)


def _softmax_update(s, v, m_ref, l_ref, acc_ref):
    m_prev = m_ref[...]
    m_new = jnp.maximum(m_prev, jnp.max(s, axis=-1, keepdims=True))
    alpha = jnp.exp2(m_prev - m_new)
    p = jnp.exp2(s - m_new)
    l_ref[...] = alpha * l_ref[...] + jnp.sum(p, axis=-1, keepdims=True)
    acc_ref[...] = alpha * acc_ref[...] + jnp.dot(p.astype(BF16), v, preferred_element_type=F32)
    m_ref[...] = m_new


def _diag_mask(rows, tq, tk, q0):
    q_tok = q0 + (lax.broadcasted_iota(jnp.int32, (rows, tk), 0) & (tq - 1))
    k_tok = lax.broadcasted_iota(jnp.int32, (rows, tk), 1)
    return (k_tok // CHUNK) <= (q_tok // CHUNK)


def _causal_sweep(n_full, qk, upd, finish, sa_ref, sb_ref):
    qk(0, sa_ref)

    def pair(jj, carry):
        j = 2 * jj
        qk(j + 1, sb_ref)
        upd(j, sa_ref)
        qk(j + 2, sa_ref)
        upd(j + 1, sb_ref)
        return carry

    lax.fori_loop(0, n_full // 2, pair, 0)

    @pl.when(n_full % 2 == 1)
    def _():
        qk(n_full, sb_ref)
        upd(n_full - 1, sa_ref)
        finish(sb_ref)

    @pl.when(n_full % 2 == 0)
    def _():
        finish(sa_ref)


def _softmax_init_wide(m_ref, acc_ref):
    m_ref[...] = jnp.full(m_ref.shape, -jnp.inf, F32)
    acc_ref[...] = jnp.zeros(acc_ref.shape, F32)


def _softmax_update_wide(s, v, m_ref, acc_ref):
    rows, tk = s.shape
    m_prev = m_ref[...]
    m_new = jnp.maximum(m_prev, jnp.max(s, axis=-1, keepdims=True))
    alpha = jnp.exp2(m_prev - m_new)
    p = jnp.concatenate([jnp.exp2(s[:, c:c + LANES] - m_new).astype(BF16) for c in range(0, tk, LANES)], axis=1)
    v_ext = jnp.concatenate([v, jnp.ones((tk, LANES), BF16)], axis=1)
    pv = jnp.dot(p, v_ext, preferred_element_type=F32)
    acc_ref[...] = jnp.concatenate([alpha] * (v_ext.shape[1] // LANES), axis=1) * acc_ref[...] + pv
    m_ref[...] = m_new


def _diff_lambda(lam_ref, lam_init):
    s1 = jnp.sum(lam_ref[0:1, :] * lam_ref[1:2, :], axis=-1, keepdims=True)
    s2 = jnp.sum(lam_ref[2:3, :] * lam_ref[3:4, :], axis=-1, keepdims=True)
    return jnp.exp(s1) - jnp.exp(s2) + lam_init


def _diff_finish(o1, o2, lam, nw, gate, lam_init):
    o = o1 - lam * o2
    o = _rmsnorm(o, nw) * (1.0 - lam_init)
    return (o * _silu(gate)).astype(BF16)


def _stack_streams(q):
    lane = lax.broadcasted_iota(jnp.int32, q.shape, 1)
    zero = jnp.zeros_like(q)
    return jnp.concatenate([jnp.where(lane < DIFF_QK_DIM, q, zero), jnp.where(lane >= DIFF_QK_DIM, q, zero)], axis=0)


def _diff_prompt_kernel(q_ref, k_ref, v_ref, lam_ref, nw_ref, g_ref, o_ref,
                        qz_ref, sa_ref, sb_ref, m_ref, acc_ref, *, t, lam_init):
    qi = pl.program_id(2)
    qz_ref[...] = _stack_streams(q_ref[0])
    _softmax_init_wide(m_ref, acc_ref)

    def blk(ref, j):
        return ref[0, pl.ds(pl.multiple_of(j * t, t), t), :]

    def qk(j, s_ref):
        s_ref[...] = _nt_dot(qz_ref[...], blk(k_ref, j))

    def upd(j, s_ref):
        _softmax_update_wide(s_ref[...], blk(v_ref, j), m_ref, acc_ref)

    def finish(s_ref):
        s = jnp.where(_diag_mask(2 * t, t, t, 0), s_ref[...], NEG_INF)
        _softmax_update_wide(s, blk(v_ref, qi), m_ref, acc_ref)
        o = acc_ref[:, :DIFF_V_DIM] / acc_ref[:, DIFF_V_DIM:]
        lam = _diff_lambda(lam_ref, lam_init)
        o_ref[0] = _diff_finish(o[:t], o[t:], lam, nw_ref[...], g_ref[...], lam_init)

    _causal_sweep(qi, qk, upd, finish, sa_ref, sb_ref)


def _diff_prompt_call(qd, kb, vb, lam_p, norm_w, h2d, cols, lam_init, t):
    nb, L, dw = qd.shape
    heads = dw // DIFF_V_DIM
    nq = L // t
    kern = functools.partial(_diff_prompt_kernel, t=t, lam_init=lam_init)
    seq = pl.BlockSpec((1, L, LANES), lambda b, h, i: (b, 0, h))
    tile = pl.BlockSpec((1, t, LANES), lambda b, h, i: (b, i, h))
    return pl.pallas_call(
        kern,
        grid=(nb, heads, nq),
        in_specs=[tile, seq, seq,
                  pl.BlockSpec((4, DIFF_QK_DIM), lambda b, h, i: (0, 0)),
                  pl.BlockSpec((1, DIFF_V_DIM), lambda b, h, i: (0, 0)),
                  pl.BlockSpec((t, LANES), lambda b, h, i: (b * nq + i, cols.dgate // LANES + h))],
        out_specs=tile,
        out_shape=jax.ShapeDtypeStruct((nb, L, dw), BF16),
        scratch_shapes=[pltpu.VMEM((2 * t, LANES), BF16),
                        pltpu.VMEM((2 * t, t), F32),
                        pltpu.VMEM((2 * t, t), F32),
                        pltpu.VMEM((2 * t, LANES), F32),
                        pltpu.VMEM((2 * t, DIFF_V_DIM + LANES), F32)],
        compiler_params=_cparams(("arbitrary", "arbitrary", "arbitrary")),
        name="diff_attn_prompt",
    )(qd, kb, vb, lam_p, norm_w, h2d)


def _diff_sample_kernel(q_ref, kc_ref, vc_ref, kn_ref, vn_ref, lam_ref, nw_ref, g_ref, o_ref,
                        qz_ref, m_ref, l_ref, acc_ref, *, nkc, heads, lam_init):
    j = pl.program_id(1)
    L = q_ref.shape[1]

    @pl.when(j == 0)
    def _():
        for h in range(heads):
            qz_ref[h] = _stack_streams(q_ref[0, :, h * LANES:(h + 1) * LANES])
        _softmax_init(m_ref, l_ref, acc_ref)

    @pl.when(j < nkc)
    def _():
        tk = kc_ref.shape[5]
        for h in range(heads):
            k_t = kc_ref[0, 0, h].reshape(2 * DIFF_QK_DIM, tk).astype(BF16)
            s = jnp.dot(qz_ref[h], k_t, preferred_element_type=F32)
            v_h = vc_ref[0, 0, pl.ds(h, tk, stride=heads), :].astype(BF16)
            _softmax_update(s, v_h, m_ref.at[h], l_ref.at[h], acc_ref.at[h])

    @pl.when(j == nkc)
    def _():
        for h in range(heads):
            sl = slice(h * LANES, (h + 1) * LANES)
            s = _nt_dot(qz_ref[h], kn_ref[0, :, sl])
            _softmax_update(s, vn_ref[0, :, sl], m_ref.at[h], l_ref.at[h], acc_ref.at[h])
        lam = _diff_lambda(lam_ref, lam_init)
        for h in range(heads):
            sl = slice(h * LANES, (h + 1) * LANES)
            o = acc_ref[h] / l_ref[h]
            o_ref[0, :, sl] = _diff_finish(o[:L], o[L:], lam, nw_ref[...], g_ref[:, sl], lam_init)


def _diff_sample_call(qd, kb, vb, k_cache, v_cache, layer, lam_p, norm_w, h2d, cols, lam_init, tk):
    nb, L, dw = qd.shape
    heads = dw // DIFF_V_DIM
    P = k_cache.shape[5]
    nkc = P // tk
    last = nkc - 1
    kern = functools.partial(_diff_sample_kernel, nkc=nkc, heads=heads, lam_init=lam_init)
    kspec = pl.BlockSpec((1, 1, heads, 2, DIFF_QK_DIM, tk),
                         lambda b, j: (layer, b, 0, 0, 0, jnp.minimum(j, last)))
    vspec = pl.BlockSpec((1, 1, tk * heads, DIFF_V_DIM), lambda b, j: (layer, b, jnp.minimum(j, last), 0))
    nspec = pl.BlockSpec((1, L, dw), lambda b, j: (b, 0, 0))
    return pl.pallas_call(
        kern,
        grid=(nb, nkc + 1),
        in_specs=[nspec, kspec, vspec, nspec, nspec,
                  pl.BlockSpec((4, DIFF_QK_DIM), lambda b, j: (0, 0)),
                  pl.BlockSpec((1, DIFF_V_DIM), lambda b, j: (0, 0)),
                  pl.BlockSpec((L, dw), lambda b, j: (b, cols.dgate // dw))],
        out_specs=nspec,
        out_shape=jax.ShapeDtypeStruct((nb, L, dw), BF16),
        scratch_shapes=[pltpu.VMEM((heads, 2 * L, LANES), BF16),
                        pltpu.VMEM((heads, 2 * L, 1), F32),
                        pltpu.VMEM((heads, 2 * L, 1), F32),
                        pltpu.VMEM((heads, 2 * L, DIFF_V_DIM), F32)],
        compiler_params=_cparams(("arbitrary", "arbitrary")),
        name="diff_attn_sample",
    )(qd, k_cache, v_cache, kb, vb, lam_p, norm_w, h2d)


def _mla_finish(acc_ref, l_ref, wuv_ref, g_ref, o_ref, heads, t):
    o = (acc_ref[...] / l_ref[...]).astype(BF16)
    for h in range(heads):
        sl = slice(h * MLA_V_DIM, (h + 1) * MLA_V_DIM)
        om = jnp.dot(o[h * t:(h + 1) * t], wuv_ref[:, sl], preferred_element_type=F32)
        o_ref[0, :, sl] = (om * _silu(g_ref[:, sl])).astype(BF16)


def _mla_prompt_kernel(q_ref, k_ref, wuv_ref, g_ref, o_ref, sa_ref, sb_ref, m_ref, l_ref, acc_ref,
                       *, tq, tk, heads):
    qi = pl.program_id(1)
    per = tk // tq
    n_full = qi // per
    _softmax_init(m_ref, l_ref, acc_ref)

    def kblk(j):
        return k_ref[0, pl.ds(pl.multiple_of(j * tk, tk), tk), :]

    def qk(j, s_ref):
        s_ref[...] = _nt_dot(q_ref[0].reshape(heads * tq, MLA_QK_PAD), kblk(j))

    def upd(j, s_ref):
        _softmax_update(s_ref[...], kblk(j)[:, :MLA_KV_RANK], m_ref, l_ref, acc_ref)

    def finish(s_ref):
        mask = _diag_mask(heads * tq, tq, tk, (qi % per) * tq)
        _softmax_update(jnp.where(mask, s_ref[...], NEG_INF), kblk(n_full)[:, :MLA_KV_RANK], m_ref, l_ref, acc_ref)
        _mla_finish(acc_ref, l_ref, wuv_ref, g_ref, o_ref, heads, tq)

    _causal_sweep(n_full, qk, upd, finish, sa_ref, sb_ref)


def _mla_prompt_call(qcat, kcat, wuv, h2d, cols, tq, tk):
    nb, heads, L, _ = qcat.shape
    mw = heads * MLA_V_DIM
    nq = L // tq
    kern = functools.partial(_mla_prompt_kernel, tq=tq, tk=tk, heads=heads)
    return pl.pallas_call(
        kern,
        grid=(nb, nq),
        in_specs=[pl.BlockSpec((1, heads, tq, MLA_QK_PAD), lambda b, i: (b, 0, i, 0)),
                  pl.BlockSpec((1, L, MLA_QK_PAD), lambda b, i: (b, 0, 0)),
                  pl.BlockSpec(wuv.shape, lambda b, i: (0, 0)),
                  pl.BlockSpec((tq, mw), lambda b, i: (b * nq + i, cols.mgate // mw))],
        out_specs=pl.BlockSpec((1, tq, mw), lambda b, i: (b, i, 0)),
        out_shape=jax.ShapeDtypeStruct((nb, L, mw), BF16),
        scratch_shapes=[pltpu.VMEM((heads * tq, tk), F32),
                        pltpu.VMEM((heads * tq, tk), F32),
                        pltpu.VMEM((heads * tq, 1), F32),
                        pltpu.VMEM((heads * tq, 1), F32),
                        pltpu.VMEM((heads * tq, MLA_KV_RANK), F32)],
        compiler_params=_cparams(("arbitrary", "arbitrary")),
        name="mla_attn_prompt",
    )(qcat, kcat, wuv, h2d)


def _mla_sample_kernel(q_ref, lc_ref, rc_ref, kn_ref, wuv_ref, g_ref, o_ref, m_ref, l_ref, acc_ref,
                       *, nkc, heads):
    j = pl.program_id(1)
    L = q_ref.shape[2]

    @pl.when(j == 0)
    def _():
        _softmax_init(m_ref, l_ref, acc_ref)

    q = q_ref[0].reshape(heads * L, MLA_QK_PAD)

    @pl.when(j < nkc)
    def _():
        lat = lc_ref[0, 0].astype(BF16)
        kr_t = rc_ref[0, 0].astype(BF16)
        s = _nt_dot(q[:, :MLA_KV_RANK], lat) + jnp.dot(q[:, MLA_KV_RANK:MLA_KV_RANK + MLA_ROPE_DIM], kr_t,
                                                       preferred_element_type=F32)
        _softmax_update(s, lat, m_ref, l_ref, acc_ref)

    @pl.when(j == nkc)
    def _():
        kn = kn_ref[0]
        _softmax_update(_nt_dot(q, kn), kn[:, :MLA_KV_RANK], m_ref, l_ref, acc_ref)
        _mla_finish(acc_ref, l_ref, wuv_ref, g_ref, o_ref, heads, L)


def _mla_sample_call(qcat, kcat, lat_cache, kr_cache, layer, wuv, h2d, cols, tk):
    nb, heads, L, _ = qcat.shape
    mw = heads * MLA_V_DIM
    P = lat_cache.shape[2]
    nkc = P // tk
    last = nkc - 1
    kern = functools.partial(_mla_sample_kernel, nkc=nkc, heads=heads)
    return pl.pallas_call(
        kern,
        grid=(nb, nkc + 1),
        in_specs=[pl.BlockSpec((1, heads, L, MLA_QK_PAD), lambda b, j: (b, 0, 0, 0)),
                  pl.BlockSpec((1, 1, tk, MLA_KV_RANK), lambda b, j: (layer, b, jnp.minimum(j, last), 0)),
                  pl.BlockSpec((1, 1, MLA_ROPE_DIM, tk), lambda b, j: (layer, b, 0, jnp.minimum(j, last))),
                  pl.BlockSpec((1, L, MLA_QK_PAD), lambda b, j: (b, 0, 0)),
                  pl.BlockSpec(wuv.shape, lambda b, j: (0, 0)),
                  pl.BlockSpec((L, mw), lambda b, j: (b, cols.mgate // mw))],
        out_specs=pl.BlockSpec((1, L, mw), lambda b, j: (b, 0, 0)),
        out_shape=jax.ShapeDtypeStruct((nb, L, mw), BF16),
        scratch_shapes=[pltpu.VMEM((heads * L, 1), F32),
                        pltpu.VMEM((heads * L, 1), F32),
                        pltpu.VMEM((heads * L, MLA_KV_RANK), F32)],
        compiler_params=_cparams(("arbitrary", "arbitrary")),
        name="mla_attn_sample",
    )(qcat, lat_cache, kr_cache, kcat, wuv, h2d)


def _outproj_kernel(ys_ref, yd_ref, ym_ref, w_ref, x_ref, g_ref, r_ref, *, alpha):
    bt, lt, _ = ys_ref.shape
    flat = lambda ref: ref[...].reshape(bt * lt, ref.shape[2])
    mix = jnp.concatenate([flat(ys_ref), flat(yd_ref), flat(ym_ref)], axis=1)
    acc = jnp.dot(mix, w_ref[...], preferred_element_type=F32)
    r_ref[...] = alpha * x_ref[...] + g_ref[...] * acc.reshape(bt, lt, acc.shape[1])


def _outproj_call(y_ssd, y_diff, y_mla, w_out, x, gate, alpha):
    nb, L, d = x.shape
    bt, lt = _seq_tiles(nb, L, 512)
    tn = 1024
    yspec = lambda width: pl.BlockSpec((bt, lt, width), lambda j, b, i: (b, i, 0))
    return pl.pallas_call(
        functools.partial(_outproj_kernel, alpha=alpha),
        grid=(d // tn, nb // bt, L // lt),
        in_specs=[yspec(y_ssd.shape[2]), yspec(y_diff.shape[2]), yspec(y_mla.shape[2]),
                  pl.BlockSpec((w_out.shape[0], tn), lambda j, b, i: (0, j)),
                  pl.BlockSpec((bt, lt, tn), lambda j, b, i: (b, i, j)),
                  pl.BlockSpec((bt, 1, tn), lambda j, b, i: (b, 0, j))],
        out_specs=pl.BlockSpec((bt, lt, tn), lambda j, b, i: (b, i, j)),
        out_shape=jax.ShapeDtypeStruct((nb, L, d), F32),
        compiler_params=_cparams(("arbitrary", "arbitrary", "arbitrary")),
        name="out_proj",
    )(y_ssd, y_diff, y_mla, w_out, x, gate)


def _layernorm(r, g, b):
    mu = jnp.mean(r, axis=-1, keepdims=True)
    var = jnp.mean(jnp.square(r - mu), axis=-1, keepdims=True)
    return (r - mu) * lax.rsqrt(var + 1e-5) * g + b


def _ln_kernel(r_ref, g_ref, b_ref, x_ref):
    x_ref[...] = _layernorm(r_ref[...], g_ref[...], b_ref[...])


def _ln_mod_kernel(r_ref, g_ref, b_ref, sc_ref, sh_ref, x_ref, u_ref):
    x = _layernorm(r_ref[...], g_ref[...], b_ref[...])
    x_ref[...] = x
    u_ref[...] = (x * (1.0 + sc_ref[...]) + sh_ref[...]).astype(BF16)


def _ln_call(r, g, b, scale=None, shift=None):
    nb, L, d = r.shape
    bt, lt = _seq_tiles(nb, L, 256)
    xspec = pl.BlockSpec((bt, lt, d), lambda i, j: (i, j, 0))
    wspec = pl.BlockSpec((1, 1, d), lambda i, j: (0, 0, 0))
    sspec = pl.BlockSpec((bt, 1, d), lambda i, j: (i, 0, 0))
    g3, b3 = g.reshape(1, 1, d), b.reshape(1, 1, d)
    if scale is None:
        return pl.pallas_call(
            _ln_kernel, grid=(nb // bt, L // lt),
            in_specs=[xspec, wspec, wspec], out_specs=xspec,
            out_shape=jax.ShapeDtypeStruct((nb, L, d), F32),
            compiler_params=_cparams(("arbitrary", "arbitrary")), name="layernorm",
        )(r, g3, b3), None
    return pl.pallas_call(
        _ln_mod_kernel, grid=(nb // bt, L // lt),
        in_specs=[xspec, wspec, wspec, sspec, sspec], out_specs=[xspec, xspec],
        out_shape=[jax.ShapeDtypeStruct((nb, L, d), F32), jax.ShapeDtypeStruct((nb, L, d), BF16)],
        compiler_params=_cparams(("arbitrary", "arbitrary")), name="layernorm_modulate",
    )(r, g3, b3, scale, shift)


def _ssd_chunk(L):
    for t in (128, 64, 32, 16, 8):
        if L % t == 0:
            return t
    raise ValueError(f"sequence length {L} is not a multiple of 8")


def _attn_tile(L, want):
    t = want
    while L % t:
        t //= 2
    return t


def _layer(x, u, mod_l, next_mod, wl, cols, layer_idx, depth, caches):
    nb, L, d = x.shape
    heads = cols.heads
    mla_heads = cols.mla_w // MLA_V_DIM
    _, _, gate = mod_l
    k_cache, v_cache, lat_cache, kr_cache, st0, conv0 = caches
    P = 0 if k_cache is None else lat_cache.shape[2]
    pos = P + jnp.arange(L, dtype=jnp.int32)

    h2d = _inproj_call(u.reshape(nb * L, d), wl["w_in"], cols.tn)

    T = _ssd_chunk(L)
    y_ssd, ssm_new, conv_new = _ssd_call(h2d, conv0, st0, wl["conv_w"], wl["conv_b"], wl["dt_bias"], wl["a_log"],
                                         wl["d_skip"], wl["ssd_norm_w"], wl["e_mat"], wl["et_mat"], cols, nb, L, T)

    tabs_d = _rope_tables(pos, DIFF_ROT, DIFF_QK_DIM, LANES)
    tabs_m = _rope_tables(pos, MLA_ROPE_DIM, MLA_ROPE_DIM, MLA_ROPE_DIM)
    qd, k32, kb, v32, vb, lat32, kr32, kcat = _prep_call(h2d, tabs_d, tabs_m, wl["kv_norm_w"], cols, nb, L)
    qcat = _mlaq_call(h2d, wl["q_norm_w"], wl["w_uq"], wl["w_uk"], tabs_m, cols, nb, L, mla_heads)

    lam_init = 0.8 - 0.6 * math.exp(-0.3 * layer_idx)
    if k_cache is None:
        y_diff = _diff_prompt_call(qd, kb, vb, wl["lam"], wl["diff_norm_w"], h2d, cols, lam_init, _attn_tile(L, 512))
        y_mla = _mla_prompt_call(qcat, kcat, wl["w_uv"], h2d, cols, _attn_tile(L, 256), _attn_tile(L, 512))
    else:
        tk = _attn_tile(P, 512)
        y_diff = _diff_sample_call(qd, kb, vb, k_cache, v_cache, layer_idx, wl["lam"], wl["diff_norm_w"], h2d,
                                   cols, lam_init, tk)
        y_mla = _mla_sample_call(qcat, kcat, lat_cache, kr_cache, layer_idx, wl["w_uv"], h2d, cols, tk)

    alpha = (2 * depth) ** 0.25
    r = _outproj_call(y_ssd.reshape(nb, L, cols.ssd_w), y_diff, y_mla, wl["w_out"], x, gate, alpha)
    if next_mod is None:
        x_new, u_new = _ln_call(r, wl["ln_g"], wl["ln_b"])
    else:
        x_new, u_new = _ln_call(r, wl["ln_g"], wl["ln_b"], next_mod[1], next_mod[0])
    k_out = k32.reshape(nb, L, cols.diff_w // (2 * DIFF_QK_DIM), 2, DIFF_QK_DIM)
    v_out = v32.reshape(nb, L, cols.diff_w // DIFF_V_DIM, DIFF_V_DIM)
    return x_new, u_new, (k_out, v_out, lat32, kr32, ssm_new, conv_new)


def _layer_weights(l, cols, w_in, conv_w, conv_b, dt_bias, a_log, d_skip, ssd_norm_w, lambda_q1, lambda_k1,
                   lambda_q2, lambda_k2, diff_norm_w, mla_q_norm_w, mla_kv_norm_w, w_uq, w_uk, w_uv, w_out,
                   ln_g, ln_b):
    heads = cols.heads
    mla_heads = cols.mla_w // MLA_V_DIM
    lane_pad = lambda v: jnp.pad(v, (MLA_ROPE_DIM, LANES - MLA_ROPE_DIM - heads)).reshape(1, LANES)
    qk = MLA_NOPE_DIM + MLA_ROPE_DIM
    wq = w_uq[l].reshape(MLA_Q_RANK, mla_heads, qk)
    wq_nope = wq[:, :, :MLA_NOPE_DIM].reshape(MLA_Q_RANK, mla_heads * MLA_NOPE_DIM)
    wq_rope = jnp.pad(wq[:, :, MLA_NOPE_DIM:], ((0, 0), (0, 0), (0, LANES - MLA_ROPE_DIM)))
    wq_p = jnp.concatenate([wq_nope, wq_rope.reshape(MLA_Q_RANK, mla_heads * LANES)], axis=1).astype(BF16)
    eh = np.zeros((LANES, cols.ssd_w), np.float32)
    for h in range(heads):
        eh[MLA_ROPE_DIM + h, h * SSD_HEAD_DIM:(h + 1) * SSD_HEAD_DIM] = 1.0
    return dict(
        w_in=_permute_w_in(w_in[l], cols),
        conv_w=conv_w[l], conv_b=conv_b[l].reshape(1, -1),
        dt_bias=lane_pad(dt_bias[l]), a_log=lane_pad(a_log[l]),
        d_skip=jnp.repeat(d_skip[l], SSD_HEAD_DIM).reshape(1, -1),
        ssd_norm_w=ssd_norm_w[l].reshape(1, -1),
        e_mat=jnp.asarray(eh, BF16), et_mat=jnp.asarray(eh.T, BF16),
        lam=jnp.stack([lambda_q1[l], lambda_k1[l], lambda_q2[l], lambda_k2[l]]),
        diff_norm_w=diff_norm_w[l].reshape(1, -1),
        q_norm_w=mla_q_norm_w[l].reshape(1, -1), kv_norm_w=mla_kv_norm_w[l].reshape(1, -1),
        w_uq=wq_p,
        w_uk=jnp.transpose(w_uk[l], (1, 2, 0)).astype(BF16),
        w_uv=w_uv[l].reshape(MLA_KV_RANK, mla_heads * MLA_V_DIM).astype(BF16),
        w_out=w_out[l].astype(BF16),
        ln_g=ln_g[l], ln_b=ln_b[l],
    )


def kernel(x_prompt, x_sample, cache_diff_k, cache_diff_v, cache_mla_latent, cache_mla_krope, state_ssm, state_conv,
           c_prompt, c_sample, w_mod, b_mod, w_in, conv_w, conv_b, dt_bias, a_log, d_skip, ssd_norm_w, lambda_q1,
           lambda_k1, lambda_q2, lambda_k2, diff_norm_w, mla_q_norm_w, mla_kv_norm_w, w_uq, w_uk, w_uv, w_out,
           ln_g, ln_b):
    depth = w_in.shape[0]
    bp, _, d = x_prompt.shape
    bs = x_sample.shape[0]
    cols = _Cols(d)
    heads = cols.heads

    rows = -(-(bp + bs) // 8) * 8
    c_all = jnp.concatenate([c_prompt, c_sample, jnp.zeros((rows - bp - bs, d), F32)], axis=0)
    mod = _mod_call(c_all, w_mod, b_mod)

    def mods(l, lo, n):
        part = lambda k: mod[l, lo:lo + n, k * d:(k + 1) * d].reshape(n, 1, d)
        return part(0), part(1), part(2)

    pdiff = cache_diff_k.shape[2]
    kc = jnp.transpose(cache_diff_k, (0, 1, 3, 4, 5, 2))
    vc = cache_diff_v.reshape(depth, bs, pdiff * cache_diff_v.shape[3], DIFF_V_DIM)
    krc = jnp.transpose(cache_mla_krope, (0, 1, 3, 2))
    zero_state = jnp.zeros((bp, heads, SSD_HEAD_DIM, SSD_STATE), F32)
    zero_conv = jnp.zeros((bp, SSD_CONV - 1, cols.conv_dim), F32)

    hp, hs = x_prompt, x_sample
    mp, ms = mods(0, 0, bp), mods(0, bp, bs)
    up = _modulate_call(hp, mp[1], mp[0])
    us = _modulate_call(hs, ms[1], ms[0])
    st_p, st_s = [], []
    for l in range(depth):
        wl = _layer_weights(l, cols, w_in, conv_w, conv_b, dt_bias, a_log, d_skip, ssd_norm_w, lambda_q1, lambda_k1,
                            lambda_q2, lambda_k2, diff_norm_w, mla_q_norm_w, mla_kv_norm_w, w_uq, w_uk, w_uv, w_out,
                            ln_g, ln_b)
        nmp = mods(l + 1, 0, bp) if l + 1 < depth else None
        nms = mods(l + 1, bp, bs) if l + 1 < depth else None
        hp, up, sp = _layer(hp, up, mp, nmp, wl, cols, l, depth, (None, None, None, None, zero_state, zero_conv))
        hs, us, ss = _layer(hs, us, ms, nms, wl, cols, l, depth,
                            (kc, vc, cache_mla_latent, krc, state_ssm[l], state_conv[l]))
        st_p.append(sp)
        st_s.append(ss)
        mp, ms = nmp, nms
    stack = lambda sts, i: jnp.stack([s[i] for s in sts])
    return ((hp, hs) + tuple(stack(st_p, i) for i in range(6)) + tuple(stack(st_s, i) for i in range(6)))
```

```python
import functools
import math

import numpy as np
import jax
import jax.numpy as jnp
from jax import lax
from jax.experimental import pallas as pl
from jax.experimental.pallas import tpu as pltpu

F32 = jnp.float32
BF16 = jnp.bfloat16

CHUNK = 64
ROPE_THETA = 500000.0
NEG_INF = -1e30
SSD_HEAD_DIM = 64
SSD_GROUPS = 4
SSD_STATE = 128
SSD_CONV = 4
DIFF_QK_DIM = 64
DIFF_V_DIM = 128
DIFF_ROT = DIFF_QK_DIM // 4
DIFF_SCALE = DIFF_QK_DIM ** -0.5
MLA_V_DIM = 128
MLA_NOPE_DIM = 128
MLA_ROPE_DIM = 64
MLA_Q_RANK = 768
MLA_KV_RANK = 256
MLA_SCALE = (MLA_NOPE_DIM + MLA_ROPE_DIM) ** -0.5
MLA_QK_PAD = MLA_KV_RANK + 128
LOG2E = math.log2(math.e)

LANES = 128
VMEM_LIMIT = 56 * 1024 * 1024


def _cparams(sem):
    return pltpu.CompilerParams(dimension_semantics=sem, vmem_limit_bytes=VMEM_LIMIT)


def _silu(x):
    hx = 0.5 * x
    return hx + hx * jnp.tanh(hx)


def _nt_dot(a, b):
    return lax.dot_general(a, b, (((1,), (1,)), ((), ())), preferred_element_type=F32)


def _split2(v):
    hi = v.astype(BF16)
    lo = (v - hi.astype(F32)).astype(BF16)
    return hi, lo


def _split3(v):
    hi = v.astype(BF16)
    r = v - hi.astype(F32)
    mid = r.astype(BF16)
    lo = (r - mid.astype(F32)).astype(BF16)
    return hi, mid, lo


def _mod_kernel(c_ref, w_ref, b_ref, o_ref):
    a = _silu(c_ref[...]).astype(BF16)
    o_ref[0] = jnp.dot(a, w_ref[0].astype(BF16), preferred_element_type=F32) + b_ref[0]


def _mod_call(c_all, w_mod, b_mod):
    depth, d, n3 = w_mod.shape
    rows = c_all.shape[0]
    tn = 512
    return pl.pallas_call(
        _mod_kernel,
        grid=(depth, n3 // tn),
        in_specs=[pl.BlockSpec((rows, d), lambda l, j: (0, 0)),
                  pl.BlockSpec((1, d, tn), lambda l, j: (l, 0, j)),
                  pl.BlockSpec((1, 1, tn), lambda l, j: (l, 0, j))],
        out_specs=pl.BlockSpec((1, rows, tn), lambda l, j: (l, 0, j)),
        out_shape=jax.ShapeDtypeStruct((depth, rows, n3), F32),
        compiler_params=_cparams(("arbitrary", "arbitrary")),
        name="adaln_mod",
    )(c_all, w_mod, b_mod.reshape(depth, 1, n3))


def _modulate_kernel(x_ref, sc_ref, sh_ref, u_ref):
    u_ref[...] = (x_ref[...] * (1.0 + sc_ref[...]) + sh_ref[...]).astype(BF16)


def _seq_tiles(nb, L, rows):
    if L >= rows:
        return 1, rows
    return min(nb, rows // L), L


def _modulate_call(x, scale, shift):
    nb, L, d = x.shape
    bt, lt = _seq_tiles(nb, L, 256)
    return pl.pallas_call(
        _modulate_kernel,
        grid=(nb // bt, L // lt),
        in_specs=[pl.BlockSpec((bt, lt, d), lambda i, j: (i, j, 0)),
                  pl.BlockSpec((bt, 1, d), lambda i, j: (i, 0, 0)),
                  pl.BlockSpec((bt, 1, d), lambda i, j: (i, 0, 0))],
        out_specs=pl.BlockSpec((bt, lt, d), lambda i, j: (i, j, 0)),
        out_shape=jax.ShapeDtypeStruct((nb, L, d), BF16),
        compiler_params=_cparams(("arbitrary", "arbitrary")),
        name="modulate",
    )(x, scale, shift)


class _Cols:
    def __init__(self, d_model):
        self.ssd_w = d_model // 2
        self.diff_w = d_model // 4
        self.mla_w = d_model // 4
        self.heads = self.ssd_w // SSD_HEAD_DIM
        self.conv_dim = self.ssd_w + 2 * SSD_GROUPS * SSD_STATE
        self.in_sizes = (self.ssd_w, self.conv_dim, self.heads, self.diff_w, self.diff_w, self.diff_w,
                         self.diff_w, MLA_Q_RANK, MLA_KV_RANK, MLA_ROPE_DIM, self.mla_w)
        o = 0
        self.xbc = o; o += self.conv_dim
        self.dq = o; o += self.diff_w
        self.z = o; o += self.ssd_w
        self.dk = o; o += self.diff_w
        self.dv = o; o += self.diff_w
        self.dgate = o; o += self.diff_w
        self.mgate = o; o += self.mla_w
        self.cq = o; o += MLA_Q_RANK + MLA_KV_RANK
        self.krdt = o; o += LANES
        self.used = o
        self.tn = 1280
        self.total = -(-o // self.tn) * self.tn


def _permute_w_in(w, cols):
    wt = w.T
    offs = np.concatenate([[0], np.cumsum(cols.in_sizes)])
    seg = lambda i: wt[offs[i]:offs[i + 1]]
    pad_dt = LANES - MLA_ROPE_DIM - cols.heads
    parts = [seg(1), seg(3), seg(0), seg(4), seg(5), seg(6), seg(10), seg(7), seg(8), seg(9), seg(2),
             jnp.zeros((pad_dt + cols.total - cols.used, wt.shape[1]), w.dtype)]
    return jnp.concatenate(parts, axis=0).astype(BF16)


def _matmul_nt_kernel(x_ref, w_ref, o_ref):
    o_ref[...] = _nt_dot(x_ref[...], w_ref[...])


def _inproj_call(u2d, wp, tn):
    m, k = u2d.shape
    n = wp.shape[0]
    tm = min(m, 512)
    return pl.pallas_call(
        _matmul_nt_kernel,
        grid=(n // tn, m // tm),
        in_specs=[pl.BlockSpec((tm, k), lambda j, i: (i, 0)),
                  pl.BlockSpec((tn, k), lambda j, i: (j, 0))],
        out_specs=pl.BlockSpec((tm, tn), lambda j, i: (i, j)),
        out_shape=jax.ShapeDtypeStruct((m, n), F32),
        compiler_params=_cparams(("arbitrary", "arbitrary")),
        name="in_proj",
    )(u2d, wp)


def _ssd_kernel(xbc_ref, z_ref, dtb_ref, conv0_ref, st0_ref, cw_ref, cb_ref, dtbias_ref, alog_ref,
                dskip_ref, nw_ref, e_ref, et_ref, y_ref, st_ref, convo_ref, ext_ref, *, T, nchunks, heads):
    c = pl.program_id(1)
    ssd_w = heads * SSD_HEAD_DIM
    gw = ssd_w // SSD_GROUPS
    hpg = heads // SSD_GROUPS
    conv_dim = ext_ref.shape[1]
    dt_lo = MLA_ROPE_DIM

    @pl.when(c == 0)
    def _():
        ext_ref[0:8, :] = jnp.zeros((8, conv_dim), F32)
        ext_ref[8 - (SSD_CONV - 1):8, :] = conv0_ref[0]
        st_ref[0] = st0_ref[0]

    ext_ref[8:8 + T, :] = xbc_ref[...]
    acc = cb_ref[...]
    for j in range(SSD_CONV):
        lo = 8 - (SSD_CONV - 1) + j
        acc = acc + ext_ref[lo:lo + T, :] * cw_ref[j:j + 1, :]
    ext_ref[0:8, :] = ext_ref[T:T + 8, :]

    @pl.when(c == nchunks - 1)
    def _():
        convo_ref[0] = ext_ref[8 - (SSD_CONV - 1):8, :]

    xact = _silu(acc)
    xs = xact[:, :ssd_w]
    bm = xact[:, ssd_w:ssd_w + SSD_GROUPS * SSD_STATE]
    cm = xact[:, ssd_w + SSD_GROUPS * SSD_STATE:]

    lane = lax.broadcasted_iota(jnp.int32, (1, LANES), 1)
    is_dt = (lane >= dt_lo) & (lane < dt_lo + heads)
    xdt = dtb_ref[...] + dtbias_ref[...]
    dt = jnp.where(is_dt, jnp.maximum(xdt, 0.0) + jnp.log1p(jnp.exp(-jnp.abs(xdt))), 0.0)
    a_neg = jnp.where(is_dt, -jnp.exp(alog_ref[...]), 0.0)
    da = dt * a_neg

    row = lax.broadcasted_iota(jnp.int32, (T, T), 0)
    col = lax.broadcasted_iota(jnp.int32, (T, T), 1)
    causal = row >= col
    tril = causal.astype(BF16)
    eye = (lax.broadcasted_iota(jnp.int32, (LANES, LANES), 0)
           == lax.broadcasted_iota(jnp.int32, (LANES, LANES), 1)).astype(BF16)

    da3 = _split3(da)
    a_cs = sum(jnp.dot(tril, p, preferred_element_type=F32) for p in da3)
    a3 = _split3(a_cs)
    a_cs_t = sum(_nt_dot(eye, p) for p in a3)
    a_last = a_cs[T - 1:T, :]
    ea = jnp.exp(a_cs)
    te = jnp.exp(a_last - a_cs)

    e_mat = e_ref[...]

    def expand(v):
        hi, lo = _split2(v)
        return jnp.dot(hi, e_mat, preferred_element_type=F32) + jnp.dot(lo, e_mat, preferred_element_type=F32)

    dt_x = expand(dt)
    ea_x = expand(ea)
    te_x = expand(te)
    cd_col = jnp.exp(a_cs_t[:, T - 1:T])
    cd_b = jnp.broadcast_to(cd_col, (LANES, SSD_STATE))
    cdh, cdl = _split2(cd_b)
    et_mat = et_ref[...]
    cd_full = (jnp.dot(et_mat, cdh, preferred_element_type=F32)
               + jnp.dot(et_mat, cdl, preferred_element_type=F32))

    xd = xs * dt_x
    xde = (xd * te_x).astype(BF16)
    lane_p = lax.broadcasted_iota(jnp.int32, (T, LANES), 1)
    lower_half = lane_p < SSD_HEAD_DIM

    for g in range(SSD_GROUPS):
        cg = cm[:, g * SSD_STATE:(g + 1) * SSD_STATE].astype(BF16)
        bg = bm[:, g * SSD_STATE:(g + 1) * SSD_STATE].astype(BF16)
        cbm = _nt_dot(cg, bg)
        st_g = st_ref[0, g * hpg:(g + 1) * hpg].reshape(gw, SSD_STATE)
        y_off = _nt_dot(cg, st_g.astype(BF16)) * ea_x[:, g * gw:(g + 1) * gw]
        pieces = []
        for q in range(hpg // 2):
            c0 = g * gw + q * LANES
            xd_pair = xd[:, c0:c0 + LANES].astype(BF16)
            ys = []
            for h in (g * hpg + 2 * q, g * hpg + 2 * q + 1):
                seg = a_cs[:, dt_lo + h:dt_lo + h + 1] - a_cs_t[dt_lo + h:dt_lo + h + 1, :]
                decay = jnp.exp(jnp.where(causal, seg, -jnp.inf))
                ys.append(jnp.dot((cbm * decay).astype(BF16), xd_pair, preferred_element_type=F32))
            pieces.append(jnp.where(lower_half, ys[0], ys[1]))
        y_g = jnp.concatenate(pieces, axis=1) + y_off
        upd = lax.dot_general(xde[:, g * gw:(g + 1) * gw], bg, (((0,), (0,)), ((), ())),
                              preferred_element_type=F32)
        st_new = st_g * cd_full[g * gw:(g + 1) * gw, :] + upd
        st_ref[0, g * hpg:(g + 1) * hpg] = st_new.reshape(hpg, SSD_HEAD_DIM, SSD_STATE)
        y_g = y_g + dskip_ref[:, g * gw:(g + 1) * gw] * xs[:, g * gw:(g + 1) * gw]
        yg = y_g * _silu(z_ref[:, g * gw:(g + 1) * gw])
        ms = jnp.mean(yg * yg, axis=-1, keepdims=True)
        y_ref[:, g * gw:(g + 1) * gw] = (yg * lax.rsqrt(ms + 1e-6) * nw_ref[:, g * gw:(g + 1) * gw]).astype(BF16)


def _ssd_call(h2d, conv0, st0, conv_w, conv_b, dtbias_p, alog_p, dskip_x, norm_w, e_mat, et_mat, cols, nb, L, T):
    nchunks = L // T
    heads = cols.heads
    ssd_w = cols.ssd_w
    cd = cols.conv_dim
    kern = functools.partial(_ssd_kernel, T=T, nchunks=nchunks, heads=heads)
    row = lambda b, c: b * nchunks + c
    const2 = lambda b, c: (0, 0)
    return pl.pallas_call(
        kern,
        grid=(nb, nchunks),
        in_specs=[pl.BlockSpec((T, cd), lambda b, c: (row(b, c), cols.xbc // cd)),
                  pl.BlockSpec((T, ssd_w), lambda b, c: (row(b, c), cols.z // ssd_w)),
                  pl.BlockSpec((T, LANES), lambda b, c: (row(b, c), cols.krdt // LANES)),
                  pl.BlockSpec((1, SSD_CONV - 1, cd), lambda b, c: (b, 0, 0)),
                  pl.BlockSpec((1, heads, SSD_HEAD_DIM, SSD_STATE), lambda b, c: (b, 0, 0, 0)),
                  pl.BlockSpec((SSD_CONV, cd), const2),
                  pl.BlockSpec((1, cd), const2),
                  pl.BlockSpec((1, LANES), const2),
                  pl.BlockSpec((1, LANES), const2),
                  pl.BlockSpec((1, ssd_w), const2),
                  pl.BlockSpec((1, ssd_w), const2),
                  pl.BlockSpec((LANES, ssd_w), const2),
                  pl.BlockSpec((ssd_w, LANES), const2)],
        out_specs=[pl.BlockSpec((T, ssd_w), lambda b, c: (row(b, c), 0)),
                   pl.BlockSpec((1, heads, SSD_HEAD_DIM, SSD_STATE), lambda b, c: (b, 0, 0, 0)),
                   pl.BlockSpec((1, SSD_CONV - 1, cd), lambda b, c: (b, 0, 0))],
        out_shape=[jax.ShapeDtypeStruct((nb * L, ssd_w), BF16),
                   jax.ShapeDtypeStruct((nb, heads, SSD_HEAD_DIM, SSD_STATE), F32),
                   jax.ShapeDtypeStruct((nb, SSD_CONV - 1, cd), F32)],
        scratch_shapes=[pltpu.VMEM((T + 8, cd), F32)],
        compiler_params=_cparams(("arbitrary", "arbitrary")),
        name="ssd_scan",
    )(h2d, h2d, h2d, conv0, st0, conv_w, conv_b, dtbias_p, alog_p, dskip_x, norm_w, e_mat, et_mat)


def _rope_tables(pos, rot_dim, period, width):
    half = rot_dim // 2
    inv = ROPE_THETA ** (-jnp.arange(half, dtype=F32) * (2.0 / rot_dim))
    ang = pos.astype(F32)[:, None] * inv[None, :]
    cos, sin = jnp.cos(ang), jnp.sin(ang)
    lane = np.arange(LANES)
    inner = lane % period
    idx = jnp.asarray(inner % half)
    first = jnp.asarray((inner < half) & (lane < width))
    second = jnp.asarray((inner >= half) & (inner < rot_dim) & (lane < width))
    keep = jnp.asarray((inner >= rot_dim) & (lane < width))
    cos_l, sin_l = cos[:, idx], sin[:, idx]
    cos_t = jnp.where(first | second, cos_l, jnp.where(keep, 1.0, 0.0))
    sin_a = jnp.where(first, -sin_l, 0.0)
    sin_b = jnp.where(second, sin_l, 0.0)
    return cos_t.astype(F32), sin_a.astype(F32), sin_b.astype(F32)


def _rope_tile(x, cos_t, sin_a, sin_b, half):
    return (x * cos_t + pltpu.roll(x, LANES - half, 1) * sin_a + pltpu.roll(x, half, 1) * sin_b)


def _rmsnorm(x, w):
    return x * lax.rsqrt(jnp.mean(x * x, axis=-1, keepdims=True) + 1e-6) * w


def _prep_kernel(dq_ref, dk_ref, dv_ref, cq_ref, kr_ref, cd_ref, sad_ref, sbd_ref, cm_ref, sam_ref, sbm_ref,
                 kvw_ref, qd_ref, k32_ref, kb_ref, v32_ref, vb_ref, lat_ref, kro_ref, kcat_ref):
    cos_d, sa_d, sb_d = cd_ref[...], sad_ref[...], sbd_ref[...]
    width = dq_ref.shape[1]
    for c in range(width // LANES):
        sl = slice(c * LANES, (c + 1) * LANES)
        q = _rope_tile(dq_ref[:, sl], cos_d, sa_d, sb_d, DIFF_ROT // 2)
        qd_ref[0, :, sl] = (q * (DIFF_SCALE * LOG2E)).astype(BF16)
        k = _rope_tile(dk_ref[:, sl], cos_d, sa_d, sb_d, DIFF_ROT // 2)
        k32_ref[0, :, sl] = k
        kb_ref[0, :, sl] = k.astype(BF16)
    v = dv_ref[...]
    v32_ref[0] = v
    vb_ref[0] = v.astype(BF16)
    lat = _rmsnorm(cq_ref[:, MLA_Q_RANK:MLA_Q_RANK + MLA_KV_RANK], kvw_ref[...])
    lat_ref[0] = lat
    kr = _rope_tile(kr_ref[...], cm_ref[...], sam_ref[...], sbm_ref[...], MLA_ROPE_DIM // 2)
    kro_ref[0] = kr[:, :MLA_ROPE_DIM]
    kcat_ref[0, :, :MLA_KV_RANK] = lat.astype(BF16)
    kcat_ref[0, :, MLA_KV_RANK:] = kr.astype(BF16)


def _prep_call(h2d, tabs_d, tabs_m, kv_w, cols, nb, L):
    tm = min(L, 256)
    nt = L // tm
    dw = cols.diff_w
    row = lambda b, i: b * nt + i
    hspec = lambda width, off: pl.BlockSpec((tm, width), lambda b, i: (row(b, i), off // width))
    tspec = pl.BlockSpec((tm, LANES), lambda b, i: (i, 0))
    ospec = lambda width: pl.BlockSpec((1, tm, width), lambda b, i: (b, i, 0))
    return pl.pallas_call(
        _prep_kernel,
        grid=(nb, nt),
        in_specs=[hspec(dw, cols.dq), hspec(dw, cols.dk), hspec(dw, cols.dv),
                  hspec(MLA_Q_RANK + MLA_KV_RANK, cols.cq), hspec(LANES, cols.krdt),
                  tspec, tspec, tspec, tspec, tspec, tspec,
                  pl.BlockSpec((1, MLA_KV_RANK), lambda b, i: (0, 0))],
        out_specs=[ospec(dw), ospec(dw), ospec(dw), ospec(dw), ospec(dw),
                   ospec(MLA_KV_RANK), ospec(MLA_ROPE_DIM), ospec(MLA_QK_PAD)],
        out_shape=[jax.ShapeDtypeStruct((nb, L, dw), BF16),
                   jax.ShapeDtypeStruct((nb, L, dw), F32),
                   jax.ShapeDtypeStruct((nb, L, dw), BF16),
                   jax.ShapeDtypeStruct((nb, L, dw), F32),
                   jax.ShapeDtypeStruct((nb, L, dw), BF16),
                   jax.ShapeDtypeStruct((nb, L, MLA_KV_RANK), F32),
                   jax.ShapeDtypeStruct((nb, L, MLA_ROPE_DIM), F32),
                   jax.ShapeDtypeStruct((nb, L, MLA_QK_PAD), BF16)],
        compiler_params=_cparams(("arbitrary", "arbitrary")),
        name="attn_prep",
    )(h2d, h2d, h2d, h2d, h2d, *tabs_d, *tabs_m, kv_w)


def _mlaq_kernel(cq_ref, qw_ref, wuq_ref, wuk_ref, cm_ref, sam_ref, sbm_ref, o_ref, *, heads):
    cqn = _rmsnorm(cq_ref[:, :MLA_Q_RANK], qw_ref[...]).astype(BF16)
    qm = jnp.dot(cqn, wuq_ref[...], preferred_element_type=F32)
    cos_m, sa_m, sb_m = cm_ref[...], sam_ref[...], sbm_ref[...]
    nope_w = heads * MLA_NOPE_DIM
    for h in range(heads):
        nope = qm[:, h * MLA_NOPE_DIM:(h + 1) * MLA_NOPE_DIM].astype(BF16)
        ql = jnp.dot(nope, wuk_ref[h], preferred_element_type=F32) * (MLA_SCALE * LOG2E)
        qr = _rope_tile(qm[:, nope_w + h * LANES:nope_w + (h + 1) * LANES], cos_m, sa_m, sb_m,
                        MLA_ROPE_DIM // 2) * (MLA_SCALE * LOG2E)
        o_ref[0, h, :, :MLA_KV_RANK] = ql.astype(BF16)
        o_ref[0, h, :, MLA_KV_RANK:] = qr.astype(BF16)


def _mlaq_call(h2d, q_w, wuq_p, wuk_t, tabs_m, cols, nb, L, heads):
    tm = min(L, 256)
    nt = L // tm
    width = MLA_Q_RANK + MLA_KV_RANK
    tspec = pl.BlockSpec((tm, LANES), lambda b, i: (i, 0))
    return pl.pallas_call(
        functools.partial(_mlaq_kernel, heads=heads),
        grid=(nb, nt),
        in_specs=[pl.BlockSpec((tm, width), lambda b, i: (b * nt + i, cols.cq // width)),
                  pl.BlockSpec((1, MLA_Q_RANK), lambda b, i: (0, 0)),
                  pl.BlockSpec(wuq_p.shape, lambda b, i: (0, 0)),
                  pl.BlockSpec(wuk_t.shape, lambda b, i: (0, 0, 0)),
                  tspec, tspec, tspec],
        out_specs=pl.BlockSpec((1, heads, tm, MLA_QK_PAD), lambda b, i: (b, 0, i, 0)),
        out_shape=jax.ShapeDtypeStruct((nb, heads, L, MLA_QK_PAD), BF16),
        compiler_params=_cparams(("arbitrary", "arbitrary")),
        name="mla_q",
    )(h2d, q_w, wuq_p, wuk_t, *tabs_m)


def _softmax_init(m_ref, l_ref, acc_ref):
    m_ref[...] = jnp.full(m_ref.shape, -jnp.inf, F32)
    l_ref[...] = jnp.zeros(l_ref.shape, F32)
    acc_ref[...] = jnp.zeros(acc_ref.shape, F32)


def _softmax_update(s, v, m_ref, l_ref, acc_ref):
    m_prev = m_ref[...]
    m_new = jnp.maximum(m_prev, jnp.max(s, axis=-1, keepdims=True))
    alpha = jnp.exp2(m_prev - m_new)
    p = jnp.exp2(s - m_new)
    l_ref[...] = alpha * l_ref[...] + jnp.sum(p, axis=-1, keepdims=True)
    acc_ref[...] = alpha * acc_ref[...] + jnp.dot(p.astype(BF16), v, preferred_element_type=F32)
    m_ref[...] = m_new


def _diag_mask(rows, tq, tk, q0):
    q_tok = q0 + (lax.broadcasted_iota(jnp.int32, (rows, tk), 0) & (tq - 1))
    k_tok = lax.broadcasted_iota(jnp.int32, (rows, tk), 1)
    return (k_tok // CHUNK) <= (q_tok // CHUNK)


def _causal_sweep(n_full, qk, upd, finish, sa_ref, sb_ref):
    qk(0, sa_ref)

    def pair(jj, carry):
        j = 2 * jj
        qk(j + 1, sb_ref)
        upd(j, sa_ref)
        qk(j + 2, sa_ref)
        upd(j + 1, sb_ref)
        return carry

    lax.fori_loop(0, n_full // 2, pair, 0)

    @pl.when(n_full % 2 == 1)
    def _():
        qk(n_full, sb_ref)
        upd(n_full - 1, sa_ref)
        finish(sb_ref)

    @pl.when(n_full % 2 == 0)
    def _():
        finish(sa_ref)


def _softmax_init_wide(m_ref, acc_ref):
    m_ref[...] = jnp.full(m_ref.shape, -jnp.inf, F32)
    acc_ref[...] = jnp.zeros(acc_ref.shape, F32)


def _with_ones(v):
    return jnp.concatenate([v, jnp.ones((v.shape[0], LANES), BF16)], axis=1)


def _softmax_update_wide(s, pv_fn, m_ref, acc_ref):
    tk = s.shape[1]
    w = min(tk, LANES)
    m_prev = m_ref[...]
    m_new = jnp.maximum(m_prev, jnp.max(s, axis=-1, keepdims=True))
    alpha = jnp.exp2(m_prev - m_new)
    p = jnp.concatenate([jnp.exp2(s[:, c:c + w] - m_new[:, :w]).astype(BF16) for c in range(0, tk, w)], axis=1)
    acc = acc_ref[...]
    acc_ref[...] = jnp.concatenate([alpha] * (acc.shape[1] // LANES), axis=1) * acc + pv_fn(p)
    m_ref[...] = m_new


def _diff_lambda(lam_ref, lam_init):
    s1 = jnp.sum(lam_ref[0:1, :] * lam_ref[1:2, :], axis=-1, keepdims=True)
    s2 = jnp.sum(lam_ref[2:3, :] * lam_ref[3:4, :], axis=-1, keepdims=True)
    return jnp.exp(s1) - jnp.exp(s2) + lam_init


def _diff_finish(o1, o2, lam, nw, gate, lam_init):
    o = o1 - lam * o2
    o = _rmsnorm(o, nw) * (1.0 - lam_init)
    return (o * _silu(gate)).astype(BF16)


def _stack_streams(q):
    lane = lax.broadcasted_iota(jnp.int32, q.shape, 1)
    zero = jnp.zeros_like(q)
    return jnp.concatenate([jnp.where(lane < DIFF_QK_DIM, q, zero), jnp.where(lane >= DIFF_QK_DIM, q, zero)], axis=0)


def _diff_prompt_kernel(q_ref, k_ref, v_ref, lam_ref, nw_ref, g_ref, o_ref,
                        qz_ref, sa_ref, sb_ref, m_ref, acc_ref, *, t, lam_init):
    qi = pl.program_id(2)
    qz_ref[...] = _stack_streams(q_ref[0])
    _softmax_init_wide(m_ref, acc_ref)

    def blk(ref, j):
        return ref[0, pl.ds(pl.multiple_of(j * t, t), t), :]

    def qk(j, s_ref):
        s_ref[...] = _nt_dot(qz_ref[...], blk(k_ref, j))

    def pv_fn(j):
        return lambda p: jnp.dot(p, _with_ones(blk(v_ref, j)), preferred_element_type=F32)

    def upd(j, s_ref):
        _softmax_update_wide(s_ref[...], pv_fn(j), m_ref, acc_ref)

    def finish(s_ref):
        s = jnp.where(_diag_mask(2 * t, t, t, 0), s_ref[...], NEG_INF)
        _softmax_update_wide(s, pv_fn(qi), m_ref, acc_ref)
        o = acc_ref[:, :DIFF_V_DIM] / acc_ref[:, DIFF_V_DIM:]
        lam = _diff_lambda(lam_ref, lam_init)
        o_ref[0] = _diff_finish(o[:t], o[t:], lam, nw_ref[...], g_ref[...], lam_init)

    _causal_sweep(qi, qk, upd, finish, sa_ref, sb_ref)


def _diff_prompt_call(qd, kb, vb, lam_p, norm_w, h2d, cols, lam_init, t):
    nb, L, dw = qd.shape
    heads = dw // DIFF_V_DIM
    nq = L // t
    kern = functools.partial(_diff_prompt_kernel, t=t, lam_init=lam_init)
    seq = pl.BlockSpec((1, L, LANES), lambda b, h, i: (b, 0, h))
    tile = pl.BlockSpec((1, t, LANES), lambda b, h, i: (b, i, h))
    return pl.pallas_call(
        kern,
        grid=(nb, heads, nq),
        in_specs=[tile, seq, seq,
                  pl.BlockSpec((4, DIFF_QK_DIM), lambda b, h, i: (0, 0)),
                  pl.BlockSpec((1, DIFF_V_DIM), lambda b, h, i: (0, 0)),
                  pl.BlockSpec((t, LANES), lambda b, h, i: (b * nq + i, cols.dgate // LANES + h))],
        out_specs=tile,
        out_shape=jax.ShapeDtypeStruct((nb, L, dw), BF16),
        scratch_shapes=[pltpu.VMEM((2 * t, LANES), BF16),
                        pltpu.VMEM((2 * t, t), F32),
                        pltpu.VMEM((2 * t, t), F32),
                        pltpu.VMEM((2 * t, LANES), F32),
                        pltpu.VMEM((2 * t, DIFF_V_DIM + LANES), F32)],
        compiler_params=_cparams(("arbitrary", "arbitrary", "arbitrary")),
        name="diff_attn_prompt",
    )(qd, kb, vb, lam_p, norm_w, h2d)


def _diff_sample_kernel(q_ref, kc_ref, vc_ref, kn_ref, vn_ref, lam_ref, nw_ref, g_ref, o_ref,
                        qz_ref, m_ref, acc_ref, *, nkc, heads, lam_init):
    j = pl.program_id(1)
    L = q_ref.shape[1]
    rows = 2 * L

    @pl.when(j == 0)
    def _():
        for h in range(heads):
            qz_ref[h] = _stack_streams(q_ref[0, :, h * LANES:(h + 1) * LANES])
        _softmax_init_wide(m_ref, acc_ref)

    def step(scores, values):
        def pv_fn(p):
            return jnp.concatenate([jnp.dot(p[h * rows:(h + 1) * rows], _with_ones(values(h)),
                                            preferred_element_type=F32) for h in range(heads)], axis=0)
        s = jnp.concatenate([scores(h) for h in range(heads)], axis=0)
        _softmax_update_wide(s, pv_fn, m_ref, acc_ref)

    @pl.when(j < nkc)
    def _():
        tk = kc_ref.shape[5]

        def scores(h):
            return jnp.dot(qz_ref[h], kc_ref[0, 0, h].reshape(2 * DIFF_QK_DIM, tk).astype(BF16),
                           preferred_element_type=F32)

        def values(h):
            return vc_ref[0, 0, pl.ds(h, tk, stride=heads), :].astype(BF16)

        step(scores, values)

    @pl.when(j == nkc)
    def _():
        step(lambda h: _nt_dot(qz_ref[h], kn_ref[0, :, h * LANES:(h + 1) * LANES]),
             lambda h: vn_ref[0, :, h * LANES:(h + 1) * LANES])
        lam = _diff_lambda(lam_ref, lam_init)
        o = acc_ref[:, :DIFF_V_DIM] / acc_ref[:, DIFF_V_DIM:]
        for h in range(heads):
            sl = slice(h * LANES, (h + 1) * LANES)
            oh = o[h * rows:(h + 1) * rows]
            o_ref[0, :, sl] = _diff_finish(oh[:L], oh[L:], lam, nw_ref[...], g_ref[:, sl], lam_init)


def _diff_sample_call(qd, kb, vb, k_cache, v_cache, layer, lam_p, norm_w, h2d, cols, lam_init, tk):
    nb, L, dw = qd.shape
    heads = dw // DIFF_V_DIM
    P = k_cache.shape[5]
    nkc = P // tk
    last = nkc - 1
    kern = functools.partial(_diff_sample_kernel, nkc=nkc, heads=heads, lam_init=lam_init)
    kspec = pl.BlockSpec((1, 1, heads, 2, DIFF_QK_DIM, tk),
                         lambda b, j: (layer, b, 0, 0, 0, jnp.minimum(j, last)))
    vspec = pl.BlockSpec((1, 1, tk * heads, DIFF_V_DIM), lambda b, j: (layer, b, jnp.minimum(j, last), 0))
    nspec = pl.BlockSpec((1, L, dw), lambda b, j: (b, 0, 0))
    return pl.pallas_call(
        kern,
        grid=(nb, nkc + 1),
        in_specs=[nspec, kspec, vspec, nspec, nspec,
                  pl.BlockSpec((4, DIFF_QK_DIM), lambda b, j: (0, 0)),
                  pl.BlockSpec((1, DIFF_V_DIM), lambda b, j: (0, 0)),
                  pl.BlockSpec((L, dw), lambda b, j: (b, cols.dgate // dw))],
        out_specs=nspec,
        out_shape=jax.ShapeDtypeStruct((nb, L, dw), BF16),
        scratch_shapes=[pltpu.VMEM((heads, 2 * L, LANES), BF16),
                        pltpu.VMEM((heads * 2 * L, LANES), F32),
                        pltpu.VMEM((heads * 2 * L, DIFF_V_DIM + LANES), F32)],
        compiler_params=_cparams(("arbitrary", "arbitrary")),
        name="diff_attn_sample",
    )(qd, k_cache, v_cache, kb, vb, lam_p, norm_w, h2d)


def _mla_finish(acc_ref, l_ref, wuv_ref, g_ref, o_ref, heads, t):
    o = (acc_ref[...] / l_ref[...]).astype(BF16)
    for h in range(heads):
        sl = slice(h * MLA_V_DIM, (h + 1) * MLA_V_DIM)
        om = jnp.dot(o[h * t:(h + 1) * t], wuv_ref[:, sl], preferred_element_type=F32)
        o_ref[0, :, sl] = (om * _silu(g_ref[:, sl])).astype(BF16)


def _mla_prompt_kernel(q_ref, k_ref, wuv_ref, g_ref, o_ref, sa_ref, sb_ref, m_ref, l_ref, acc_ref,
                       *, tq, tk, heads):
    qi = pl.program_id(1)
    per = tk // tq
    n_full = qi // per
    _softmax_init(m_ref, l_ref, acc_ref)

    def kblk(j):
        return k_ref[0, pl.ds(pl.multiple_of(j * tk, tk), tk), :]

    def qk(j, s_ref):
        s_ref[...] = _nt_dot(q_ref[0].reshape(heads * tq, MLA_QK_PAD), kblk(j))

    def upd(j, s_ref):
        _softmax_update(s_ref[...], kblk(j)[:, :MLA_KV_RANK], m_ref, l_ref, acc_ref)

    def finish(s_ref):
        mask = _diag_mask(heads * tq, tq, tk, (qi % per) * tq)
        _softmax_update(jnp.where(mask, s_ref[...], NEG_INF), kblk(n_full)[:, :MLA_KV_RANK], m_ref, l_ref, acc_ref)
        _mla_finish(acc_ref, l_ref, wuv_ref, g_ref, o_ref, heads, tq)

    _causal_sweep(n_full, qk, upd, finish, sa_ref, sb_ref)


def _mla_prompt_call(qcat, kcat, wuv, h2d, cols, tq, tk):
    nb, heads, L, _ = qcat.shape
    mw = heads * MLA_V_DIM
    nq = L // tq
    kern = functools.partial(_mla_prompt_kernel, tq=tq, tk=tk, heads=heads)
    return pl.pallas_call(
        kern,
        grid=(nb, nq),
        in_specs=[pl.BlockSpec((1, heads, tq, MLA_QK_PAD), lambda b, i: (b, 0, i, 0)),
                  pl.BlockSpec((1, L, MLA_QK_PAD), lambda b, i: (b, 0, 0)),
                  pl.BlockSpec(wuv.shape, lambda b, i: (0, 0)),
                  pl.BlockSpec((tq, mw), lambda b, i: (b * nq + i, cols.mgate // mw))],
        out_specs=pl.BlockSpec((1, tq, mw), lambda b, i: (b, i, 0)),
        out_shape=jax.ShapeDtypeStruct((nb, L, mw), BF16),
        scratch_shapes=[pltpu.VMEM((heads * tq, tk), F32),
                        pltpu.VMEM((heads * tq, tk), F32),
                        pltpu.VMEM((heads * tq, 1), F32),
                        pltpu.VMEM((heads * tq, 1), F32),
                        pltpu.VMEM((heads * tq, MLA_KV_RANK), F32)],
        compiler_params=_cparams(("arbitrary", "arbitrary")),
        name="mla_attn_prompt",
    )(qcat, kcat, wuv, h2d)


def _mla_sample_kernel(q_ref, lc_ref, rc_ref, kn_ref, wuv_ref, g_ref, o_ref, m_ref, l_ref, acc_ref,
                       *, nkc, heads):
    j = pl.program_id(1)
    L = q_ref.shape[2]

    @pl.when(j == 0)
    def _():
        _softmax_init(m_ref, l_ref, acc_ref)

    q = q_ref[0].reshape(heads * L, MLA_QK_PAD)

    @pl.when(j < nkc)
    def _():
        lat = lc_ref[0, 0].astype(BF16)
        kr_t = rc_ref[0, 0].astype(BF16)
        s = _nt_dot(q[:, :MLA_KV_RANK], lat) + jnp.dot(q[:, MLA_KV_RANK:MLA_KV_RANK + MLA_ROPE_DIM], kr_t,
                                                       preferred_element_type=F32)
        _softmax_update(s, lat, m_ref, l_ref, acc_ref)

    @pl.when(j == nkc)
    def _():
        kn = kn_ref[0]
        _softmax_update(_nt_dot(q, kn), kn[:, :MLA_KV_RANK], m_ref, l_ref, acc_ref)
        _mla_finish(acc_ref, l_ref, wuv_ref, g_ref, o_ref, heads, L)


def _mla_sample_call(qcat, kcat, lat_cache, kr_cache, layer, wuv, h2d, cols, tk):
    nb, heads, L, _ = qcat.shape
    mw = heads * MLA_V_DIM
    P = lat_cache.shape[2]
    nkc = P // tk
    last = nkc - 1
    kern = functools.partial(_mla_sample_kernel, nkc=nkc, heads=heads)
    return pl.pallas_call(
        kern,
        grid=(nb, nkc + 1),
        in_specs=[pl.BlockSpec((1, heads, L, MLA_QK_PAD), lambda b, j: (b, 0, 0, 0)),
                  pl.BlockSpec((1, 1, tk, MLA_KV_RANK), lambda b, j: (layer, b, jnp.minimum(j, last), 0)),
                  pl.BlockSpec((1, 1, MLA_ROPE_DIM, tk), lambda b, j: (layer, b, 0, jnp.minimum(j, last))),
                  pl.BlockSpec((1, L, MLA_QK_PAD), lambda b, j: (b, 0, 0)),
                  pl.BlockSpec(wuv.shape, lambda b, j: (0, 0)),
                  pl.BlockSpec((L, mw), lambda b, j: (b, cols.mgate // mw))],
        out_specs=pl.BlockSpec((1, L, mw), lambda b, j: (b, 0, 0)),
        out_shape=jax.ShapeDtypeStruct((nb, L, mw), BF16),
        scratch_shapes=[pltpu.VMEM((heads * L, 1), F32),
                        pltpu.VMEM((heads * L, 1), F32),
                        pltpu.VMEM((heads * L, MLA_KV_RANK), F32)],
        compiler_params=_cparams(("arbitrary", "arbitrary")),
        name="mla_attn_sample",
    )(qcat, lat_cache, kr_cache, kcat, wuv, h2d)


def _outproj_kernel(ys_ref, yd_ref, ym_ref, w_ref, x_ref, g_ref, r_ref, *, alpha):
    bt, lt, _ = ys_ref.shape
    flat = lambda ref: ref[...].reshape(bt * lt, ref.shape[2])
    mix = jnp.concatenate([flat(ys_ref), flat(yd_ref), flat(ym_ref)], axis=1)
    acc = jnp.dot(mix, w_ref[...], preferred_element_type=F32)
    r_ref[...] = alpha * x_ref[...] + g_ref[...] * acc.reshape(bt, lt, acc.shape[1])


def _outproj_call(y_ssd, y_diff, y_mla, w_out, x, gate, alpha):
    nb, L, d = x.shape
    bt, lt = _seq_tiles(nb, L, 512)
    tn = 1024
    yspec = lambda width: pl.BlockSpec((bt, lt, width), lambda j, b, i: (b, i, 0))
    return pl.pallas_call(
        functools.partial(_outproj_kernel, alpha=alpha),
        grid=(d // tn, nb // bt, L // lt),
        in_specs=[yspec(y_ssd.shape[2]), yspec(y_diff.shape[2]), yspec(y_mla.shape[2]),
                  pl.BlockSpec((w_out.shape[0], tn), lambda j, b, i: (0, j)),
                  pl.BlockSpec((bt, lt, tn), lambda j, b, i: (b, i, j)),
                  pl.BlockSpec((bt, 1, tn), lambda j, b, i: (b, 0, j))],
        out_specs=pl.BlockSpec((bt, lt, tn), lambda j, b, i: (b, i, j)),
        out_shape=jax.ShapeDtypeStruct((nb, L, d), F32),
        compiler_params=_cparams(("arbitrary", "arbitrary", "arbitrary")),
        name="out_proj",
    )(y_ssd, y_diff, y_mla, w_out, x, gate)


def _layernorm(r, g, b):
    mu = jnp.mean(r, axis=-1, keepdims=True)
    var = jnp.mean(jnp.square(r - mu), axis=-1, keepdims=True)
    return (r - mu) * lax.rsqrt(var + 1e-5) * g + b


def _ln_kernel(r_ref, g_ref, b_ref, x_ref):
    x_ref[...] = _layernorm(r_ref[...], g_ref[...], b_ref[...])


def _ln_mod_kernel(r_ref, g_ref, b_ref, sc_ref, sh_ref, x_ref, u_ref):
    x = _layernorm(r_ref[...], g_ref[...], b_ref[...])
    x_ref[...] = x
    u_ref[...] = (x * (1.0 + sc_ref[...]) + sh_ref[...]).astype(BF16)


def _ln_call(r, g, b, scale=None, shift=None):
    nb, L, d = r.shape
    bt, lt = _seq_tiles(nb, L, 256)
    xspec = pl.BlockSpec((bt, lt, d), lambda i, j: (i, j, 0))
    wspec = pl.BlockSpec((1, 1, d), lambda i, j: (0, 0, 0))
    sspec = pl.BlockSpec((bt, 1, d), lambda i, j: (i, 0, 0))
    g3, b3 = g.reshape(1, 1, d), b.reshape(1, 1, d)
    if scale is None:
        return pl.pallas_call(
            _ln_kernel, grid=(nb // bt, L // lt),
            in_specs=[xspec, wspec, wspec], out_specs=xspec,
            out_shape=jax.ShapeDtypeStruct((nb, L, d), F32),
            compiler_params=_cparams(("arbitrary", "arbitrary")), name="layernorm",
        )(r, g3, b3), None
    return pl.pallas_call(
        _ln_mod_kernel, grid=(nb // bt, L // lt),
        in_specs=[xspec, wspec, wspec, sspec, sspec], out_specs=[xspec, xspec],
        out_shape=[jax.ShapeDtypeStruct((nb, L, d), F32), jax.ShapeDtypeStruct((nb, L, d), BF16)],
        compiler_params=_cparams(("arbitrary", "arbitrary")), name="layernorm_modulate",
    )(r, g3, b3, scale, shift)


def _ssd_chunk(L):
    for t in (128, 64, 32, 16, 8):
        if L % t == 0:
            return t
    raise ValueError(f"sequence length {L} is not a multiple of 8")


def _attn_tile(L, want):
    t = want
    while L % t:
        t //= 2
    return t


def _layer(x, u, mod_l, next_mod, wl, cols, layer_idx, depth, caches):
    nb, L, d = x.shape
    heads = cols.heads
    mla_heads = cols.mla_w // MLA_V_DIM
    _, _, gate = mod_l
    k_cache, v_cache, lat_cache, kr_cache, st0, conv0 = caches
    P = 0 if k_cache is None else lat_cache.shape[2]
    pos = P + jnp.arange(L, dtype=jnp.int32)

    h2d = _inproj_call(u.reshape(nb * L, d), wl["w_in"], cols.tn)

    T = _ssd_chunk(L)
    y_ssd, ssm_new, conv_new = _ssd_call(h2d, conv0, st0, wl["conv_w"], wl["conv_b"], wl["dt_bias"], wl["a_log"],
                                         wl["d_skip"], wl["ssd_norm_w"], wl["e_mat"], wl["et_mat"], cols, nb, L, T)

    tabs_d = _rope_tables(pos, DIFF_ROT, DIFF_QK_DIM, LANES)
    tabs_m = _rope_tables(pos, MLA_ROPE_DIM, MLA_ROPE_DIM, MLA_ROPE_DIM)
    qd, k32, kb, v32, vb, lat32, kr32, kcat = _prep_call(h2d, tabs_d, tabs_m, wl["kv_norm_w"], cols, nb, L)
    qcat = _mlaq_call(h2d, wl["q_norm_w"], wl["w_uq"], wl["w_uk"], tabs_m, cols, nb, L, mla_heads)

    lam_init = 0.8 - 0.6 * math.exp(-0.3 * layer_idx)
    if k_cache is None:
        y_diff = _diff_prompt_call(qd, kb, vb, wl["lam"], wl["diff_norm_w"], h2d, cols, lam_init, _attn_tile(L, 512))
        y_mla = _mla_prompt_call(qcat, kcat, wl["w_uv"], h2d, cols, _attn_tile(L, 256), _attn_tile(L, 512))
    else:
        y_diff = _diff_sample_call(qd, kb, vb, k_cache, v_cache, layer_idx, wl["lam"], wl["diff_norm_w"], h2d,
                                   cols, lam_init, _attn_tile(P, 1024))
        y_mla = _mla_sample_call(qcat, kcat, lat_cache, kr_cache, layer_idx, wl["w_uv"], h2d, cols,
                                 _attn_tile(P, 2048))

    alpha = (2 * depth) ** 0.25
    r = _outproj_call(y_ssd.reshape(nb, L, cols.ssd_w), y_diff, y_mla, wl["w_out"], x, gate, alpha)
    if next_mod is None:
        x_new, u_new = _ln_call(r, wl["ln_g"], wl["ln_b"])
    else:
        x_new, u_new = _ln_call(r, wl["ln_g"], wl["ln_b"], next_mod[1], next_mod[0])
    k_out = k32.reshape(nb, L, cols.diff_w // (2 * DIFF_QK_DIM), 2, DIFF_QK_DIM)
    v_out = v32.reshape(nb, L, cols.diff_w // DIFF_V_DIM, DIFF_V_DIM)
    return x_new, u_new, (k_out, v_out, lat32, kr32, ssm_new, conv_new)


def _layer_weights(l, cols, w_in, conv_w, conv_b, dt_bias, a_log, d_skip, ssd_norm_w, lambda_q1, lambda_k1,
                   lambda_q2, lambda_k2, diff_norm_w, mla_q_norm_w, mla_kv_norm_w, w_uq, w_uk, w_uv, w_out,
                   ln_g, ln_b):
    heads = cols.heads
    mla_heads = cols.mla_w // MLA_V_DIM
    lane_pad = lambda v: jnp.pad(v, (MLA_ROPE_DIM, LANES - MLA_ROPE_DIM - heads)).reshape(1, LANES)
    qk = MLA_NOPE_DIM + MLA_ROPE_DIM
    wq = w_uq[l].reshape(MLA_Q_RANK, mla_heads, qk)
    wq_nope = wq[:, :, :MLA_NOPE_DIM].reshape(MLA_Q_RANK, mla_heads * MLA_NOPE_DIM)
    wq_rope = jnp.pad(wq[:, :, MLA_NOPE_DIM:], ((0, 0), (0, 0), (0, LANES - MLA_ROPE_DIM)))
    wq_p = jnp.concatenate([wq_nope, wq_rope.reshape(MLA_Q_RANK, mla_heads * LANES)], axis=1).astype(BF16)
    eh = np.zeros((LANES, cols.ssd_w), np.float32)
    for h in range(heads):
        eh[MLA_ROPE_DIM + h, h * SSD_HEAD_DIM:(h + 1) * SSD_HEAD_DIM] = 1.0
    return dict(
        w_in=_permute_w_in(w_in[l], cols),
        conv_w=conv_w[l], conv_b=conv_b[l].reshape(1, -1),
        dt_bias=lane_pad(dt_bias[l]), a_log=lane_pad(a_log[l]),
        d_skip=jnp.repeat(d_skip[l], SSD_HEAD_DIM).reshape(1, -1),
        ssd_norm_w=ssd_norm_w[l].reshape(1, -1),
        e_mat=jnp.asarray(eh, BF16), et_mat=jnp.asarray(eh.T, BF16),
        lam=jnp.stack([lambda_q1[l], lambda_k1[l], lambda_q2[l], lambda_k2[l]]),
        diff_norm_w=diff_norm_w[l].reshape(1, -1),
        q_norm_w=mla_q_norm_w[l].reshape(1, -1), kv_norm_w=mla_kv_norm_w[l].reshape(1, -1),
        w_uq=wq_p,
        w_uk=jnp.transpose(w_uk[l], (1, 2, 0)).astype(BF16),
        w_uv=w_uv[l].reshape(MLA_KV_RANK, mla_heads * MLA_V_DIM).astype(BF16),
        w_out=w_out[l].astype(BF16),
        ln_g=ln_g[l], ln_b=ln_b[l],
    )


def kernel(x_prompt, x_sample, cache_diff_k, cache_diff_v, cache_mla_latent, cache_mla_krope, state_ssm, state_conv,
           c_prompt, c_sample, w_mod, b_mod, w_in, conv_w, conv_b, dt_bias, a_log, d_skip, ssd_norm_w, lambda_q1,
           lambda_k1, lambda_q2, lambda_k2, diff_norm_w, mla_q_norm_w, mla_kv_norm_w, w_uq, w_uk, w_uv, w_out,
           ln_g, ln_b):
    depth = w_in.shape[0]
    bp, _, d = x_prompt.shape
    bs = x_sample.shape[0]
    cols = _Cols(d)
    heads = cols.heads

    rows = -(-(bp + bs) // 8) * 8
    c_all = jnp.concatenate([c_prompt, c_sample, jnp.zeros((rows - bp - bs, d), F32)], axis=0)
    mod = _mod_call(c_all, w_mod, b_mod)

    def mods(l, lo, n):
        part = lambda k: mod[l, lo:lo + n, k * d:(k + 1) * d].reshape(n, 1, d)
        return part(0), part(1), part(2)

    pdiff = cache_diff_k.shape[2]
    kc = jnp.transpose(cache_diff_k, (0, 1, 3, 4, 5, 2))
    vc = cache_diff_v.reshape(depth, bs, pdiff * cache_diff_v.shape[3], DIFF_V_DIM)
    krc = jnp.transpose(cache_mla_krope, (0, 1, 3, 2))
    zero_state = jnp.zeros((bp, heads, SSD_HEAD_DIM, SSD_STATE), F32)
    zero_conv = jnp.zeros((bp, SSD_CONV - 1, cols.conv_dim), F32)

    hp, hs = x_prompt, x_sample
    mp, ms = mods(0, 0, bp), mods(0, bp, bs)
    up = _modulate_call(hp, mp[1], mp[0])
    us = _modulate_call(hs, ms[1], ms[0])
    st_p, st_s = [], []
    for l in range(depth):
        wl = _layer_weights(l, cols, w_in, conv_w, conv_b, dt_bias, a_log, d_skip, ssd_norm_w, lambda_q1, lambda_k1,
                            lambda_q2, lambda_k2, diff_norm_w, mla_q_norm_w, mla_kv_norm_w, w_uq, w_uk, w_uv, w_out,
                            ln_g, ln_b)
        nmp = mods(l + 1, 0, bp) if l + 1 < depth else None
        nms = mods(l + 1, bp, bs) if l + 1 < depth else None
        hp, up, sp = _layer(hp, up, mp, nmp, wl, cols, l, depth, (None, None, None, None, zero_state, zero_conv))
        hs, us, ss = _layer(hs, us, ms, nms, wl, cols, l, depth,
                            (kc, vc, cache_mla_latent, krc, state_ssm[l], state_conv[l]))
        st_p.append(sp)
        st_s.append(ss)
        mp, ms = nmp, nms
    stack = lambda sts, i: jnp.stack([s[i] for s in sts])
    return ((hp, hs) + tuple(stack(st_p, i) for i in range(6)) + tuple(stack(st_s, i) for i in range(6)))
```

```python
import functools
import math

import numpy as np
import jax
import jax.numpy as jnp
from jax import lax
from jax.experimental import pallas as pl
from jax.experimental.pallas import tpu as pltpu

F32 = jnp.float32
BF16 = jnp.bfloat16

CHUNK = 64
ROPE_THETA = 500000.0
NEG_INF = -1e30
SSD_HEAD_DIM = 64
SSD_GROUPS = 4
SSD_STATE = 128
SSD_CONV = 4
DIFF_QK_DIM = 64
DIFF_V_DIM = 128
DIFF_ROT = DIFF_QK_DIM // 4
DIFF_SCALE = DIFF_QK_DIM ** -0.5
MLA_V_DIM = 128
MLA_NOPE_DIM = 128
MLA_ROPE_DIM = 64
MLA_Q_RANK = 768
MLA_KV_RANK = 256
MLA_SCALE = (MLA_NOPE_DIM + MLA_ROPE_DIM) ** -0.5
MLA_QK_PAD = MLA_KV_RANK + 128
MLA_QH_DIM = MLA_NOPE_DIM + 128
MLA_HEAD_GROUP = 2
LOG2E = math.log2(math.e)

LANES = 128
VMEM_LIMIT = 56 * 1024 * 1024


def _cparams(sem):
    return pltpu.CompilerParams(dimension_semantics=sem, vmem_limit_bytes=VMEM_LIMIT)


def _silu(x):
    hx = 0.5 * x
    return hx + hx * jnp.tanh(hx)


def _nt_dot(a, b):
    return lax.dot_general(a, b, (((1,), (1,)), ((), ())), preferred_element_type=F32)


def _split2(v):
    hi = v.astype(BF16)
    lo = (v - hi.astype(F32)).astype(BF16)
    return hi, lo


def _split3(v):
    hi = v.astype(BF16)
    r = v - hi.astype(F32)
    mid = r.astype(BF16)
    lo = (r - mid.astype(F32)).astype(BF16)
    return hi, mid, lo


def _mod_kernel(c_ref, w_ref, b_ref, o_ref):
    a = _silu(c_ref[...]).astype(BF16)
    o_ref[0] = jnp.dot(a, w_ref[0].astype(BF16), preferred_element_type=F32) + b_ref[0]


def _mod_call(c_all, w_mod, b_mod):
    depth, d, n3 = w_mod.shape
    rows = c_all.shape[0]
    tn = 512
    return pl.pallas_call(
        _mod_kernel,
        grid=(depth, n3 // tn),
        in_specs=[pl.BlockSpec((rows, d), lambda l, j: (0, 0)),
                  pl.BlockSpec((1, d, tn), lambda l, j: (l, 0, j)),
                  pl.BlockSpec((1, 1, tn), lambda l, j: (l, 0, j))],
        out_specs=pl.BlockSpec((1, rows, tn), lambda l, j: (l, 0, j)),
        out_shape=jax.ShapeDtypeStruct((depth, rows, n3), F32),
        compiler_params=_cparams(("arbitrary", "arbitrary")),
        name="adaln_mod",
    )(c_all, w_mod, b_mod.reshape(depth, 1, n3))


def _modulate_kernel(x_ref, sc_ref, sh_ref, u_ref):
    u_ref[...] = (x_ref[...] * (1.0 + sc_ref[...]) + sh_ref[...]).astype(BF16)


def _seq_tiles(nb, L, rows):
    if L >= rows:
        return 1, rows
    return min(nb, rows // L), L


def _modulate_call(x, scale, shift):
    nb, L, d = x.shape
    bt, lt = _seq_tiles(nb, L, 256)
    return pl.pallas_call(
        _modulate_kernel,
        grid=(nb // bt, L // lt),
        in_specs=[pl.BlockSpec((bt, lt, d), lambda i, j: (i, j, 0)),
                  pl.BlockSpec((bt, 1, d), lambda i, j: (i, 0, 0)),
                  pl.BlockSpec((bt, 1, d), lambda i, j: (i, 0, 0))],
        out_specs=pl.BlockSpec((bt, lt, d), lambda i, j: (i, j, 0)),
        out_shape=jax.ShapeDtypeStruct((nb, L, d), BF16),
        compiler_params=_cparams(("arbitrary", "arbitrary")),
        name="modulate",
    )(x, scale, shift)


class _Cols:
    def __init__(self, d_model):
        self.ssd_w = d_model // 2
        self.diff_w = d_model // 4
        self.mla_w = d_model // 4
        self.heads = self.ssd_w // SSD_HEAD_DIM
        self.conv_dim = self.ssd_w + 2 * SSD_GROUPS * SSD_STATE
        self.in_sizes = (self.ssd_w, self.conv_dim, self.heads, self.diff_w, self.diff_w, self.diff_w,
                         self.diff_w, MLA_Q_RANK, MLA_KV_RANK, MLA_ROPE_DIM, self.mla_w)
        o = 0
        self.xbc = o; o += self.conv_dim
        self.dq = o; o += self.diff_w
        self.z = o; o += self.ssd_w
        self.dk = o; o += self.diff_w
        self.dv = o; o += self.diff_w
        self.dgate = o; o += self.diff_w
        self.mgate = o; o += self.mla_w
        self.cq = o; o += MLA_Q_RANK + MLA_KV_RANK
        self.krdt = o; o += LANES
        self.used = o
        self.tn = 1280
        self.total = -(-o // self.tn) * self.tn


def _w_in_blocks(cols):
    offs = np.concatenate([[0], np.cumsum(cols.in_sizes)])
    order = (1, 3, 0, 4, 5, 6, 10)
    rows = []
    for i in order:
        rows += list(range(int(offs[i]), int(offs[i + 1]), LANES))
    rows += list(range(int(offs[7]), int(offs[9]), LANES))
    rows.append(-1)
    rows += [-2] * ((cols.total - cols.used) // LANES)
    return np.asarray(rows, np.int32)


def _wprep_kernel(tab_ref, w_ref, sp_ref, o_ref):
    src = tab_ref[pl.program_id(1)]

    @pl.when(src >= 0)
    def _():
        o_ref[0] = w_ref[0].astype(BF16)

    @pl.when(src == -1)
    def _():
        o_ref[0] = sp_ref[0].astype(BF16)

    @pl.when(src == -2)
    def _():
        o_ref[0] = jnp.zeros(o_ref.shape[1:], BF16)


def _permute_w_in(w_in, cols):
    depth, d, _ = w_in.shape
    wt = jnp.transpose(w_in, (0, 2, 1))
    offs = np.concatenate([[0], np.cumsum(cols.in_sizes)])
    pad = LANES - MLA_ROPE_DIM - cols.heads
    special = jnp.concatenate([wt[:, offs[9]:offs[10]], wt[:, offs[2]:offs[3]], jnp.zeros((depth, pad, d), F32)], axis=1)
    tab = jnp.asarray(_w_in_blocks(cols))
    grid_spec = pltpu.PrefetchScalarGridSpec(
        num_scalar_prefetch=1,
        grid=(depth, int(tab.shape[0])),
        in_specs=[pl.BlockSpec((pl.Element(1), pl.Element(LANES), pl.Element(d)),
                               lambda l, i, tab: (l, pl.multiple_of(jnp.maximum(tab[i], 0), 8), 0)),
                  pl.BlockSpec((1, LANES, d), lambda l, i, tab: (l, 0, 0))],
        out_specs=pl.BlockSpec((1, LANES, d), lambda l, i, tab: (l, i, 0)))
    return pl.pallas_call(
        _wprep_kernel, grid_spec=grid_spec,
        out_shape=jax.ShapeDtypeStruct((depth, cols.total, d), BF16),
        compiler_params=_cparams(("arbitrary", "arbitrary")),
        name="w_in_prep",
    )(tab, wt, special)


def _matmul_nt_kernel(x_ref, w_ref, o_ref):
    o_ref[...] = _nt_dot(x_ref[...], w_ref[...])


def _inproj_call(u2d, wp, layer, tn):
    m, k = u2d.shape
    n = wp.shape[1]
    tm = min(m, 512)
    return pl.pallas_call(
        _matmul_nt_kernel,
        grid=(n // tn, m // tm),
        in_specs=[pl.BlockSpec((tm, k), lambda j, i: (i, 0)),
                  pl.BlockSpec((None, tn, k), lambda j, i: (layer, j, 0))],
        out_specs=pl.BlockSpec((tm, tn), lambda j, i: (i, j)),
        out_shape=jax.ShapeDtypeStruct((m, n), F32),
        compiler_params=_cparams(("arbitrary", "arbitrary")),
        name="in_proj",
    )(u2d, wp)


def _ssd_kernel(xbc_ref, z_ref, dtb_ref, conv0_ref, st0_ref, cw_ref, cb_ref, dtbias_ref, alog_ref,
                dskip_ref, nw_ref, e_ref, et_ref, y_ref, st_ref, convo_ref, ext_ref, *, T, nchunks, heads):
    c = pl.program_id(1)
    ssd_w = heads * SSD_HEAD_DIM
    gw = ssd_w // SSD_GROUPS
    hpg = heads // SSD_GROUPS
    conv_dim = ext_ref.shape[1]
    dt_lo = MLA_ROPE_DIM

    @pl.when(c == 0)
    def _():
        ext_ref[0:8, :] = jnp.zeros((8, conv_dim), F32)
        ext_ref[8 - (SSD_CONV - 1):8, :] = conv0_ref[0]
        st_ref[0] = st0_ref[0]

    ext_ref[8:8 + T, :] = xbc_ref[...]
    acc = cb_ref[...]
    for j in range(SSD_CONV):
        lo = 8 - (SSD_CONV - 1) + j
        acc = acc + ext_ref[lo:lo + T, :] * cw_ref[j:j + 1, :]
    ext_ref[0:8, :] = ext_ref[T:T + 8, :]

    @pl.when(c == nchunks - 1)
    def _():
        convo_ref[0] = ext_ref[8 - (SSD_CONV - 1):8, :]

    xact = _silu(acc)
    xs = xact[:, :ssd_w]
    bm = xact[:, ssd_w:ssd_w + SSD_GROUPS * SSD_STATE]
    cm = xact[:, ssd_w + SSD_GROUPS * SSD_STATE:]

    lane = lax.broadcasted_iota(jnp.int32, (1, LANES), 1)
    is_dt = (lane >= dt_lo) & (lane < dt_lo + heads)
    xdt = dtb_ref[...] + dtbias_ref[...]
    dt = jnp.where(is_dt, jnp.maximum(xdt, 0.0) + jnp.log1p(jnp.exp(-jnp.abs(xdt))), 0.0)
    a_neg = jnp.where(is_dt, -jnp.exp(alog_ref[...]), 0.0)
    da = dt * a_neg

    row = lax.broadcasted_iota(jnp.int32, (T, T), 0)
    col = lax.broadcasted_iota(jnp.int32, (T, T), 1)
    causal = row >= col
    tril = causal.astype(BF16)
    eye = (lax.broadcasted_iota(jnp.int32, (LANES, LANES), 0)
           == lax.broadcasted_iota(jnp.int32, (LANES, LANES), 1)).astype(BF16)

    da3 = _split3(da)
    a_cs = sum(jnp.dot(tril, p, preferred_element_type=F32) for p in da3)
    a3 = _split3(a_cs)
    a_cs_t = sum(_nt_dot(eye, p) for p in a3)
    a_last = a_cs[T - 1:T, :]
    ea = jnp.exp(a_cs)
    te = jnp.exp(a_last - a_cs)

    e_mat = e_ref[...]

    def expand(v):
        hi, lo = _split2(v)
        return jnp.dot(hi, e_mat, preferred_element_type=F32) + jnp.dot(lo, e_mat, preferred_element_type=F32)

    dt_x = expand(dt)
    ea_x = expand(ea)
    te_x = expand(te)
    cd_col = jnp.exp(a_cs_t[:, T - 1:T])
    cd_b = jnp.broadcast_to(cd_col, (LANES, SSD_STATE))
    cdh, cdl = _split2(cd_b)
    et_mat = et_ref[...]
    cd_full = (jnp.dot(et_mat, cdh, preferred_element_type=F32)
               + jnp.dot(et_mat, cdl, preferred_element_type=F32))

    xd = xs * dt_x
    xde = (xd * te_x).astype(BF16)
    lane_p = lax.broadcasted_iota(jnp.int32, (T, LANES), 1)
    lower_half = lane_p < SSD_HEAD_DIM

    for g in range(SSD_GROUPS):
        cg = cm[:, g * SSD_STATE:(g + 1) * SSD_STATE].astype(BF16)
        bg = bm[:, g * SSD_STATE:(g + 1) * SSD_STATE].astype(BF16)
        cbm = _nt_dot(cg, bg)
        st_g = st_ref[0, g * hpg:(g + 1) * hpg].reshape(gw, SSD_STATE)
        y_off = _nt_dot(cg, st_g.astype(BF16)) * ea_x[:, g * gw:(g + 1) * gw]
        pieces = []
        for q in range(hpg // 2):
            c0 = g * gw + q * LANES
            xd_pair = xd[:, c0:c0 + LANES].astype(BF16)
            ys = []
            for h in (g * hpg + 2 * q, g * hpg + 2 * q + 1):
                seg = a_cs[:, dt_lo + h:dt_lo + h + 1] - a_cs_t[dt_lo + h:dt_lo + h + 1, :]
                decay = jnp.exp(jnp.where(causal, seg, -jnp.inf))
                ys.append(jnp.dot((cbm * decay).astype(BF16), xd_pair, preferred_element_type=F32))
            pieces.append(jnp.where(lower_half, ys[0], ys[1]))
        y_g = jnp.concatenate(pieces, axis=1) + y_off
        upd = lax.dot_general(xde[:, g * gw:(g + 1) * gw], bg, (((0,), (0,)), ((), ())),
                              preferred_element_type=F32)
        st_new = st_g * cd_full[g * gw:(g + 1) * gw, :] + upd
        st_ref[0, g * hpg:(g + 1) * hpg] = st_new.reshape(hpg, SSD_HEAD_DIM, SSD_STATE)
        y_g = y_g + dskip_ref[:, g * gw:(g + 1) * gw] * xs[:, g * gw:(g + 1) * gw]
        yg = y_g * _silu(z_ref[:, g * gw:(g + 1) * gw])
        ms = jnp.mean(yg * yg, axis=-1, keepdims=True)
        y_ref[:, g * gw:(g + 1) * gw] = (yg * lax.rsqrt(ms + 1e-6) * nw_ref[:, g * gw:(g + 1) * gw]).astype(BF16)


def _ssd_call(h2d, conv0, st0, conv_w, conv_b, dtbias_p, alog_p, dskip_x, norm_w, e_mat, et_mat, cols, nb, L, T):
    nchunks = L // T
    heads = cols.heads
    ssd_w = cols.ssd_w
    cd = cols.conv_dim
    kern = functools.partial(_ssd_kernel, T=T, nchunks=nchunks, heads=heads)
    row = lambda b, c: b * nchunks + c
    const2 = lambda b, c: (0, 0)
    return pl.pallas_call(
        kern,
        grid=(nb, nchunks),
        in_specs=[pl.BlockSpec((T, cd), lambda b, c: (row(b, c), cols.xbc // cd)),
                  pl.BlockSpec((T, ssd_w), lambda b, c: (row(b, c), cols.z // ssd_w)),
                  pl.BlockSpec((T, LANES), lambda b, c: (row(b, c), cols.krdt // LANES)),
                  pl.BlockSpec((1, SSD_CONV - 1, cd), lambda b, c: (b, 0, 0)),
                  pl.BlockSpec((1, heads, SSD_HEAD_DIM, SSD_STATE), lambda b, c: (b, 0, 0, 0)),
                  pl.BlockSpec((SSD_CONV, cd), const2),
                  pl.BlockSpec((1, cd), const2),
                  pl.BlockSpec((1, LANES), const2),
                  pl.BlockSpec((1, LANES), const2),
                  pl.BlockSpec((1, ssd_w), const2),
                  pl.BlockSpec((1, ssd_w), const2),
                  pl.BlockSpec((LANES, ssd_w), const2),
                  pl.BlockSpec((ssd_w, LANES), const2)],
        out_specs=[pl.BlockSpec((T, ssd_w), lambda b, c: (row(b, c), 0)),
                   pl.BlockSpec((1, heads, SSD_HEAD_DIM, SSD_STATE), lambda b, c: (b, 0, 0, 0)),
                   pl.BlockSpec((1, SSD_CONV - 1, cd), lambda b, c: (b, 0, 0))],
        out_shape=[jax.ShapeDtypeStruct((nb * L, ssd_w), BF16),
                   jax.ShapeDtypeStruct((nb, heads, SSD_HEAD_DIM, SSD_STATE), F32),
                   jax.ShapeDtypeStruct((nb, SSD_CONV - 1, cd), F32)],
        scratch_shapes=[pltpu.VMEM((T + 8, cd), F32)],
        compiler_params=_cparams(("arbitrary", "arbitrary")),
        name="ssd_scan",
    )(h2d, h2d, h2d, conv0, st0, conv_w, conv_b, dtbias_p, alog_p, dskip_x, norm_w, e_mat, et_mat)


def _rope_tables(pos, rot_dim, period, width):
    half = rot_dim // 2
    inv = ROPE_THETA ** (-jnp.arange(half, dtype=F32) * (2.0 / rot_dim))
    ang = pos.astype(F32)[:, None] * inv[None, :]
    cos, sin = jnp.cos(ang), jnp.sin(ang)
    lane = np.arange(LANES)
    inner = lane % period
    idx = jnp.asarray(inner % half)
    first = jnp.asarray((inner < half) & (lane < width))
    second = jnp.asarray((inner >= half) & (inner < rot_dim) & (lane < width))
    keep = jnp.asarray((inner >= rot_dim) & (lane < width))
    cos_l, sin_l = cos[:, idx], sin[:, idx]
    cos_t = jnp.where(first | second, cos_l, jnp.where(keep, 1.0, 0.0))
    sin_a = jnp.where(first, -sin_l, 0.0)
    sin_b = jnp.where(second, sin_l, 0.0)
    return cos_t.astype(F32), sin_a.astype(F32), sin_b.astype(F32)


def _rope_tile(x, cos_t, sin_a, sin_b, half):
    return (x * cos_t + pltpu.roll(x, LANES - half, 1) * sin_a + pltpu.roll(x, half, 1) * sin_b)


def _rmsnorm(x, w):
    return x * lax.rsqrt(jnp.mean(x * x, axis=-1, keepdims=True) + 1e-6) * w


def _prep_kernel(dq_ref, dk_ref, dv_ref, cq_ref, kr_ref, cd_ref, sad_ref, sbd_ref, cm_ref, sam_ref, sbm_ref,
                 kvw_ref, *rest, per_head):
    if per_head:
        wuk_ref, qd_ref, k32_ref, kb_ref, v32_ref, vb_ref, lat_ref, kro_ref, latb_ref, kt_ref = rest
    else:
        qd_ref, k32_ref, kb_ref, v32_ref, vb_ref, lat_ref, kro_ref, kcat_ref = rest
    cos_d, sa_d, sb_d = cd_ref[...], sad_ref[...], sbd_ref[...]
    width = dq_ref.shape[1]
    for c in range(width // LANES):
        sl = slice(c * LANES, (c + 1) * LANES)
        q = _rope_tile(dq_ref[:, sl], cos_d, sa_d, sb_d, DIFF_ROT // 2)
        qd_ref[0, :, sl] = (q * (DIFF_SCALE * LOG2E)).astype(BF16)
        k = _rope_tile(dk_ref[:, sl], cos_d, sa_d, sb_d, DIFF_ROT // 2)
        k32_ref[0, :, sl] = k
        kb_ref[0, :, sl] = k.astype(BF16)
    v = dv_ref[...]
    v32_ref[0] = v
    vb_ref[0] = v.astype(BF16)
    lat = _rmsnorm(cq_ref[:, MLA_Q_RANK:MLA_Q_RANK + MLA_KV_RANK], kvw_ref[...])
    lat_ref[0] = lat
    kr = _rope_tile(kr_ref[...], cm_ref[...], sam_ref[...], sbm_ref[...], MLA_ROPE_DIM // 2)
    kro_ref[0] = kr[:, :MLA_ROPE_DIM]
    lat_b = lat.astype(BF16)
    if per_head:
        latb_ref[0] = lat_b
        k_nope = jnp.dot(lat_b, wuk_ref[...], preferred_element_type=F32)
        for h in range(kt_ref.shape[1]):
            k_h = jnp.concatenate([k_nope[:, h * MLA_NOPE_DIM:(h + 1) * MLA_NOPE_DIM], kr], axis=1)
            kt_ref[0, h, 0] = k_h.T.astype(BF16)
    else:
        kcat_ref[0, :, :MLA_KV_RANK] = lat_b
        kcat_ref[0, :, MLA_KV_RANK:] = kr.astype(BF16)


def _prep_call(h2d, tabs_d, tabs_m, kv_w, cols, nb, L, tm, wuk_flat=None):
    nt = L // tm
    dw = cols.diff_w
    per_head = wuk_flat is not None
    row = lambda b, i: b * nt + i
    hspec = lambda width, off: pl.BlockSpec((tm, width), lambda b, i: (row(b, i), off // width))
    tspec = pl.BlockSpec((tm, LANES), lambda b, i: (i, 0))
    ospec = lambda width: pl.BlockSpec((1, tm, width), lambda b, i: (b, i, 0))
    in_specs = [hspec(dw, cols.dq), hspec(dw, cols.dk), hspec(dw, cols.dv),
                hspec(MLA_Q_RANK + MLA_KV_RANK, cols.cq), hspec(LANES, cols.krdt),
                tspec, tspec, tspec, tspec, tspec, tspec,
                pl.BlockSpec((1, MLA_KV_RANK), lambda b, i: (0, 0))]
    out_specs = [ospec(dw), ospec(dw), ospec(dw), ospec(dw), ospec(dw), ospec(MLA_KV_RANK), ospec(MLA_ROPE_DIM)]
    out_shape = [jax.ShapeDtypeStruct((nb, L, dw), BF16),
                 jax.ShapeDtypeStruct((nb, L, dw), F32),
                 jax.ShapeDtypeStruct((nb, L, dw), BF16),
                 jax.ShapeDtypeStruct((nb, L, dw), F32),
                 jax.ShapeDtypeStruct((nb, L, dw), BF16),
                 jax.ShapeDtypeStruct((nb, L, MLA_KV_RANK), F32),
                 jax.ShapeDtypeStruct((nb, L, MLA_ROPE_DIM), F32)]
    args = [h2d, h2d, h2d, h2d, h2d, *tabs_d, *tabs_m, kv_w]
    if per_head:
        heads = wuk_flat.shape[1] // MLA_NOPE_DIM
        in_specs.append(pl.BlockSpec(wuk_flat.shape, lambda b, i: (0, 0)))
        args.append(wuk_flat)
        out_specs += [ospec(MLA_KV_RANK),
                      pl.BlockSpec((1, heads, 1, MLA_QH_DIM, tm), lambda b, i: (b, 0, i, 0, 0))]
        out_shape += [jax.ShapeDtypeStruct((nb, L, MLA_KV_RANK), BF16),
                      jax.ShapeDtypeStruct((nb, heads, nt, MLA_QH_DIM, tm), BF16)]
    else:
        out_specs.append(ospec(MLA_QK_PAD))
        out_shape.append(jax.ShapeDtypeStruct((nb, L, MLA_QK_PAD), BF16))
    return pl.pallas_call(
        functools.partial(_prep_kernel, per_head=per_head),
        grid=(nb, nt),
        in_specs=in_specs, out_specs=out_specs, out_shape=out_shape,
        compiler_params=_cparams(("arbitrary", "arbitrary")),
        name="attn_prep",
    )(*args)


def _mlaq_kernel(cq_ref, qw_ref, wuq_ref, wuk_ref, cm_ref, sam_ref, sbm_ref, o_ref, *, heads, absorb):
    cqn = _rmsnorm(cq_ref[:, :MLA_Q_RANK], qw_ref[...]).astype(BF16)
    qm = jnp.dot(cqn, wuq_ref[...], preferred_element_type=F32)
    cos_m, sa_m, sb_m = cm_ref[...], sam_ref[...], sbm_ref[...]
    nope_w = heads * MLA_NOPE_DIM
    scale = MLA_SCALE * LOG2E
    for h in range(heads):
        nope = qm[:, h * MLA_NOPE_DIM:(h + 1) * MLA_NOPE_DIM]
        qr = _rope_tile(qm[:, nope_w + h * LANES:nope_w + (h + 1) * LANES], cos_m, sa_m, sb_m, MLA_ROPE_DIM // 2)
        if absorb:
            ql = jnp.dot(nope.astype(BF16), wuk_ref[h], preferred_element_type=F32)
        else:
            ql = nope
        width = ql.shape[1]
        o_ref[0, h, :, :width] = (ql * scale).astype(BF16)
        o_ref[0, h, :, width:] = (qr * scale).astype(BF16)


def _mlaq_call(h2d, q_w, wuq_p, wuk_t, tabs_m, cols, nb, L, heads, absorb):
    tm = min(L, 256)
    nt = L // tm
    width = MLA_Q_RANK + MLA_KV_RANK
    qdim = MLA_QK_PAD if absorb else MLA_QH_DIM
    tspec = pl.BlockSpec((tm, LANES), lambda b, i: (i, 0))
    return pl.pallas_call(
        functools.partial(_mlaq_kernel, heads=heads, absorb=absorb),
        grid=(nb, nt),
        in_specs=[pl.BlockSpec((tm, width), lambda b, i: (b * nt + i, cols.cq // width)),
                  pl.BlockSpec((1, MLA_Q_RANK), lambda b, i: (0, 0)),
                  pl.BlockSpec(wuq_p.shape, lambda b, i: (0, 0)),
                  pl.BlockSpec(wuk_t.shape, lambda b, i: (0, 0, 0)),
                  tspec, tspec, tspec],
        out_specs=pl.BlockSpec((1, heads, tm, qdim), lambda b, i: (b, 0, i, 0)),
        out_shape=jax.ShapeDtypeStruct((nb, heads, L, qdim), BF16),
        compiler_params=_cparams(("arbitrary", "arbitrary")),
        name="mla_q",
    )(h2d, q_w, wuq_p, wuk_t, *tabs_m)


def _softmax_init(m_ref, l_ref, acc_ref):
    m_ref[...] = jnp.full(m_ref.shape, -jnp.inf, F32)
    l_ref[...] = jnp.zeros(l_ref.shape, F32)
    acc_ref[...] = jnp.zeros(acc_ref.shape, F32)


def _softmax_update(s, v, m_ref, l_ref, acc_ref):
    m_prev = m_ref[...]
    m_new = jnp.maximum(m_prev, jnp.max(s, axis=-1, keepdims=True))
    alpha = jnp.exp2(m_prev - m_new)
    p = jnp.exp2(s - m_new)
    l_ref[...] = alpha * l_ref[...] + jnp.sum(p, axis=-1, keepdims=True)
    acc_ref[...] = alpha * acc_ref[...] + jnp.dot(p.astype(BF16), v, preferred_element_type=F32)
    m_ref[...] = m_new


def _diag_mask(rows, tq, tk, q0):
    q_tok = q0 + (lax.broadcasted_iota(jnp.int32, (rows, tk), 0) & (tq - 1))
    k_tok = lax.broadcasted_iota(jnp.int32, (rows, tk), 1)
    return (k_tok // CHUNK) <= (q_tok // CHUNK)


def _causal_sweep(n_full, qk, upd, finish, sa_ref, sb_ref):
    qk(0, sa_ref)

    def pair(jj, carry):
        j = 2 * jj
        qk(j + 1, sb_ref)
        upd(j, sa_ref)
        qk(j + 2, sa_ref)
        upd(j + 1, sb_ref)
        return carry

    lax.fori_loop(0, n_full // 2, pair, 0)

    @pl.when(n_full % 2 == 1)
    def _():
        qk(n_full, sb_ref)
        upd(n_full - 1, sa_ref)
        finish(sb_ref)

    @pl.when(n_full % 2 == 0)
    def _():
        finish(sa_ref)


def _softmax_init_wide(m_ref, acc_ref):
    m_ref[...] = jnp.full(m_ref.shape, -jnp.inf, F32)
    acc_ref[...] = jnp.zeros(acc_ref.shape, F32)


def _with_ones(v):
    return jnp.concatenate([v, jnp.ones((v.shape[0], LANES), BF16)], axis=1)


def _softmax_update_wide(s, pv_fn, m_ref, acc_ref):
    tk = s.shape[1]
    w = min(tk, LANES)
    m_prev = m_ref[...]
    m_new = jnp.maximum(m_prev, jnp.max(s, axis=-1, keepdims=True))
    alpha = jnp.exp2(m_prev - m_new)
    p = jnp.concatenate([jnp.exp2(s[:, c:c + w] - m_new[:, :w]).astype(BF16) for c in range(0, tk, w)], axis=1)
    acc = acc_ref[...]
    acc_ref[...] = jnp.concatenate([alpha] * (acc.shape[1] // LANES), axis=1) * acc + pv_fn(p)
    m_ref[...] = m_new


def _diff_lambda(lam_ref, lam_init):
    s1 = jnp.sum(lam_ref[0:1, :] * lam_ref[1:2, :], axis=-1, keepdims=True)
    s2 = jnp.sum(lam_ref[2:3, :] * lam_ref[3:4, :], axis=-1, keepdims=True)
    return jnp.exp(s1) - jnp.exp(s2) + lam_init


def _diff_finish(o1, o2, lam, nw, gate, lam_init):
    o = o1 - lam * o2
    o = _rmsnorm(o, nw) * (1.0 - lam_init)
    return (o * _silu(gate)).astype(BF16)


def _stack_streams(q):
    lane = lax.broadcasted_iota(jnp.int32, q.shape, 1)
    zero = jnp.zeros_like(q)
    return jnp.concatenate([jnp.where(lane < DIFF_QK_DIM, q, zero), jnp.where(lane >= DIFF_QK_DIM, q, zero)], axis=0)


def _diff_prompt_kernel(q_ref, k_ref, v_ref, lam_ref, nw_ref, g_ref, o_ref,
                        qz_ref, sa_ref, sb_ref, m_ref, acc_ref, *, t, lam_init):
    qi = pl.program_id(2)
    qz_ref[...] = _stack_streams(q_ref[0])
    _softmax_init_wide(m_ref, acc_ref)

    def blk(ref, j):
        return ref[0, pl.ds(pl.multiple_of(j * t, t), t), :]

    def qk(j, s_ref):
        s_ref[...] = _nt_dot(qz_ref[...], blk(k_ref, j))

    def pv_fn(j):
        return lambda p: jnp.dot(p, _with_ones(blk(v_ref, j)), preferred_element_type=F32)

    def upd(j, s_ref):
        _softmax_update_wide(s_ref[...], pv_fn(j), m_ref, acc_ref)

    def finish(s_ref):
        s = jnp.where(_diag_mask(2 * t, t, t, 0), s_ref[...], NEG_INF)
        _softmax_update_wide(s, pv_fn(qi), m_ref, acc_ref)
        o = acc_ref[:, :DIFF_V_DIM] / acc_ref[:, DIFF_V_DIM:]
        lam = _diff_lambda(lam_ref, lam_init)
        o_ref[0] = _diff_finish(o[:t], o[t:], lam, nw_ref[...], g_ref[...], lam_init)

    _causal_sweep(qi, qk, upd, finish, sa_ref, sb_ref)


def _diff_prompt_call(qd, kb, vb, lam_p, norm_w, h2d, cols, lam_init, t):
    nb, L, dw = qd.shape
    heads = dw // DIFF_V_DIM
    nq = L // t
    kern = functools.partial(_diff_prompt_kernel, t=t, lam_init=lam_init)
    seq = pl.BlockSpec((1, L, LANES), lambda b, h, i: (b, 0, h))
    tile = pl.BlockSpec((1, t, LANES), lambda b, h, i: (b, i, h))
    return pl.pallas_call(
        kern,
        grid=(nb, heads, nq),
        in_specs=[tile, seq, seq,
                  pl.BlockSpec((4, DIFF_QK_DIM), lambda b, h, i: (0, 0)),
                  pl.BlockSpec((1, DIFF_V_DIM), lambda b, h, i: (0, 0)),
                  pl.BlockSpec((t, LANES), lambda b, h, i: (b * nq + i, cols.dgate // LANES + h))],
        out_specs=tile,
        out_shape=jax.ShapeDtypeStruct((nb, L, dw), BF16),
        scratch_shapes=[pltpu.VMEM((2 * t, LANES), BF16),
                        pltpu.VMEM((2 * t, t), F32),
                        pltpu.VMEM((2 * t, t), F32),
                        pltpu.VMEM((2 * t, LANES), F32),
                        pltpu.VMEM((2 * t, DIFF_V_DIM + LANES), F32)],
        compiler_params=_cparams(("arbitrary", "arbitrary", "arbitrary")),
        name="diff_attn_prompt",
    )(qd, kb, vb, lam_p, norm_w, h2d)


def _diff_sample_kernel(q_ref, kc_ref, vc_ref, kn_ref, vn_ref, lam_ref, nw_ref, g_ref, o_ref,
                        qz_ref, m_ref, acc_ref, *, nkc, heads, lam_init):
    j = pl.program_id(1)
    L = q_ref.shape[1]
    rows = 2 * L

    @pl.when(j == 0)
    def _():
        for h in range(heads):
            qz_ref[h] = _stack_streams(q_ref[0, :, h * LANES:(h + 1) * LANES])
        _softmax_init_wide(m_ref, acc_ref)

    def step(scores, values):
        def pv_fn(p):
            return jnp.concatenate([jnp.dot(p[h * rows:(h + 1) * rows], _with_ones(values(h)),
                                            preferred_element_type=F32) for h in range(heads)], axis=0)
        s = jnp.concatenate([scores(h) for h in range(heads)], axis=0)
        _softmax_update_wide(s, pv_fn, m_ref, acc_ref)

    @pl.when(j < nkc)
    def _():
        tk = kc_ref.shape[5]

        def scores(h):
            return jnp.dot(qz_ref[h], kc_ref[0, 0, h].reshape(2 * DIFF_QK_DIM, tk).astype(BF16),
                           preferred_element_type=F32)

        def values(h):
            return vc_ref[0, 0, pl.ds(h, tk, stride=heads), :].astype(BF16)

        step(scores, values)

    @pl.when(j == nkc)
    def _():
        step(lambda h: _nt_dot(qz_ref[h], kn_ref[0, :, h * LANES:(h + 1) * LANES]),
             lambda h: vn_ref[0, :, h * LANES:(h + 1) * LANES])
        lam = _diff_lambda(lam_ref, lam_init)
        o = acc_ref[:, :DIFF_V_DIM] / acc_ref[:, DIFF_V_DIM:]
        for h in range(heads):
            sl = slice(h * LANES, (h + 1) * LANES)
            oh = o[h * rows:(h + 1) * rows]
            o_ref[0, :, sl] = _diff_finish(oh[:L], oh[L:], lam, nw_ref[...], g_ref[:, sl], lam_init)


def _diff_sample_call(qd, kb, vb, k_cache, v_cache, layer, lam_p, norm_w, h2d, cols, lam_init, tk):
    nb, L, dw = qd.shape
    heads = dw // DIFF_V_DIM
    P = k_cache.shape[5]
    nkc = P // tk
    last = nkc - 1
    kern = functools.partial(_diff_sample_kernel, nkc=nkc, heads=heads, lam_init=lam_init)
    kspec = pl.BlockSpec((1, 1, heads, 2, DIFF_QK_DIM, tk),
                         lambda b, j: (layer, b, 0, 0, 0, jnp.minimum(j, last)))
    vspec = pl.BlockSpec((1, 1, tk * heads, DIFF_V_DIM), lambda b, j: (layer, b, jnp.minimum(j, last), 0))
    nspec = pl.BlockSpec((1, L, dw), lambda b, j: (b, 0, 0))
    return pl.pallas_call(
        kern,
        grid=(nb, nkc + 1),
        in_specs=[nspec, kspec, vspec, nspec, nspec,
                  pl.BlockSpec((4, DIFF_QK_DIM), lambda b, j: (0, 0)),
                  pl.BlockSpec((1, DIFF_V_DIM), lambda b, j: (0, 0)),
                  pl.BlockSpec((L, dw), lambda b, j: (b, cols.dgate // dw))],
        out_specs=nspec,
        out_shape=jax.ShapeDtypeStruct((nb, L, dw), BF16),
        scratch_shapes=[pltpu.VMEM((heads, 2 * L, LANES), BF16),
                        pltpu.VMEM((heads * 2 * L, LANES), F32),
                        pltpu.VMEM((heads * 2 * L, DIFF_V_DIM + LANES), F32)],
        compiler_params=_cparams(("arbitrary", "arbitrary")),
        name="diff_attn_sample",
    )(qd, k_cache, v_cache, kb, vb, lam_p, norm_w, h2d)


def _mla_finish(acc_ref, l_ref, wuv_ref, g_ref, o_ref, heads, t):
    o = (acc_ref[...] / l_ref[...]).astype(BF16)
    for h in range(heads):
        sl = slice(h * MLA_V_DIM, (h + 1) * MLA_V_DIM)
        om = jnp.dot(o[h * t:(h + 1) * t], wuv_ref[:, sl], preferred_element_type=F32)
        o_ref[0, :, sl] = (om * _silu(g_ref[:, sl])).astype(BF16)


def _mla_prompt_kernel(q_ref, kt_ref, v_ref, wuv_ref, g_ref, o_ref, sa_ref, sb_ref, m_ref, l_ref, acc_ref, *, t, hg):
    qi = pl.program_id(2)
    _softmax_init(m_ref, l_ref, acc_ref)

    def vblk(j):
        return v_ref[0, pl.ds(pl.multiple_of(j * t, t), t), :]

    def qk(j, s_ref):
        for g in range(hg):
            s_ref[g * t:(g + 1) * t, :] = jnp.dot(q_ref[0, g], kt_ref[0, g, j], preferred_element_type=F32)

    def upd(j, s_ref):
        _softmax_update(s_ref[...], vblk(j), m_ref, l_ref, acc_ref)

    def finish(s_ref):
        s = jnp.where(_diag_mask(hg * t, t, t, 0), s_ref[...], NEG_INF)
        _softmax_update(s, vblk(qi), m_ref, l_ref, acc_ref)
        _mla_finish(acc_ref, l_ref, wuv_ref, g_ref, o_ref, hg, t)

    _causal_sweep(qi, qk, upd, finish, sa_ref, sb_ref)


def _mla_prompt_call(qh, kt, latb, wuv, h2d, cols, t, hg):
    nb, heads, L, qdim = qh.shape
    mw = heads * MLA_V_DIM
    gw = hg * MLA_V_DIM
    nq = L // t
    return pl.pallas_call(
        functools.partial(_mla_prompt_kernel, t=t, hg=hg),
        grid=(nb, heads // hg, nq),
        in_specs=[pl.BlockSpec((1, hg, t, qdim), lambda b, h, i: (b, h, i, 0)),
                  pl.BlockSpec((1, hg, nq, qdim, t), lambda b, h, i: (b, h, 0, 0, 0)),
                  pl.BlockSpec((1, L, MLA_KV_RANK), lambda b, h, i: (b, 0, 0)),
                  pl.BlockSpec((MLA_KV_RANK, gw), lambda b, h, i: (0, h)),
                  pl.BlockSpec((t, gw), lambda b, h, i: (b * nq + i, cols.mgate // gw + h))],
        out_specs=pl.BlockSpec((1, t, gw), lambda b, h, i: (b, i, h)),
        out_shape=jax.ShapeDtypeStruct((nb, L, mw), BF16),
        scratch_shapes=[pltpu.VMEM((hg * t, t), F32),
                        pltpu.VMEM((hg * t, t), F32),
                        pltpu.VMEM((hg * t, 1), F32),
                        pltpu.VMEM((hg * t, 1), F32),
                        pltpu.VMEM((hg * t, MLA_KV_RANK), F32)],
        compiler_params=_cparams(("arbitrary", "arbitrary", "arbitrary")),
        name="mla_attn_prompt",
    )(qh, kt, latb, wuv, h2d)


def _mla_sample_kernel(q_ref, lc_ref, rc_ref, kn_ref, wuv_ref, g_ref, o_ref, m_ref, l_ref, acc_ref,
                       *, nkc, heads):
    j = pl.program_id(1)
    L = q_ref.shape[2]

    @pl.when(j == 0)
    def _():
        _softmax_init(m_ref, l_ref, acc_ref)

    q = q_ref[0].reshape(heads * L, MLA_QK_PAD)

    @pl.when(j < nkc)
    def _():
        lat = lc_ref[0, 0].astype(BF16)
        kr_t = rc_ref[0, 0].astype(BF16)
        s = _nt_dot(q[:, :MLA_KV_RANK], lat) + jnp.dot(q[:, MLA_KV_RANK:MLA_KV_RANK + MLA_ROPE_DIM], kr_t,
                                                       preferred_element_type=F32)
        _softmax_update(s, lat, m_ref, l_ref, acc_ref)

    @pl.when(j == nkc)
    def _():
        kn = kn_ref[0]
        _softmax_update(_nt_dot(q, kn), kn[:, :MLA_KV_RANK], m_ref, l_ref, acc_ref)
        _mla_finish(acc_ref, l_ref, wuv_ref, g_ref, o_ref, heads, L)


def _mla_sample_call(qcat, kcat, lat_cache, kr_cache, layer, wuv, h2d, cols, tk):
    nb, heads, L, _ = qcat.shape
    mw = heads * MLA_V_DIM
    P = lat_cache.shape[2]
    nkc = P // tk
    last = nkc - 1
    kern = functools.partial(_mla_sample_kernel, nkc=nkc, heads=heads)
    return pl.pallas_call(
        kern,
        grid=(nb, nkc + 1),
        in_specs=[pl.BlockSpec((1, heads, L, MLA_QK_PAD), lambda b, j: (b, 0, 0, 0)),
                  pl.BlockSpec((1, 1, tk, MLA_KV_RANK), lambda b, j: (layer, b, jnp.minimum(j, last), 0)),
                  pl.BlockSpec((1, 1, MLA_ROPE_DIM, tk), lambda b, j: (layer, b, 0, jnp.minimum(j, last))),
                  pl.BlockSpec((1, L, MLA_QK_PAD), lambda b, j: (b, 0, 0)),
                  pl.BlockSpec(wuv.shape, lambda b, j: (0, 0)),
                  pl.BlockSpec((L, mw), lambda b, j: (b, cols.mgate // mw))],
        out_specs=pl.BlockSpec((1, L, mw), lambda b, j: (b, 0, 0)),
        out_shape=jax.ShapeDtypeStruct((nb, L, mw), BF16),
        scratch_shapes=[pltpu.VMEM((heads * L, 1), F32),
                        pltpu.VMEM((heads * L, 1), F32),
                        pltpu.VMEM((heads * L, MLA_KV_RANK), F32)],
        compiler_params=_cparams(("arbitrary", "arbitrary")),
        name="mla_attn_sample",
    )(qcat, lat_cache, kr_cache, kcat, wuv, h2d)


def _outproj_kernel(ys_ref, yd_ref, ym_ref, w_ref, x_ref, g_ref, r_ref, *, alpha):
    bt, lt, _ = ys_ref.shape
    flat = lambda ref: ref[...].reshape(bt * lt, ref.shape[2])
    mix = jnp.concatenate([flat(ys_ref), flat(yd_ref), flat(ym_ref)], axis=1)
    acc = jnp.dot(mix, w_ref[...], preferred_element_type=F32)
    r_ref[...] = alpha * x_ref[...] + g_ref[...] * acc.reshape(bt, lt, acc.shape[1])


def _outproj_call(y_ssd, y_diff, y_mla, w_out, x, gate, alpha):
    nb, L, d = x.shape
    bt, lt = _seq_tiles(nb, L, 512)
    tn = 1024
    yspec = lambda width: pl.BlockSpec((bt, lt, width), lambda j, b, i: (b, i, 0))
    return pl.pallas_call(
        functools.partial(_outproj_kernel, alpha=alpha),
        grid=(d // tn, nb // bt, L // lt),
        in_specs=[yspec(y_ssd.shape[2]), yspec(y_diff.shape[2]), yspec(y_mla.shape[2]),
                  pl.BlockSpec((w_out.shape[0], tn), lambda j, b, i: (0, j)),
                  pl.BlockSpec((bt, lt, tn), lambda j, b, i: (b, i, j)),
                  pl.BlockSpec((bt, 1, tn), lambda j, b, i: (b, 0, j))],
        out_specs=pl.BlockSpec((bt, lt, tn), lambda j, b, i: (b, i, j)),
        out_shape=jax.ShapeDtypeStruct((nb, L, d), F32),
        compiler_params=_cparams(("arbitrary", "arbitrary", "arbitrary")),
        name="out_proj",
    )(y_ssd, y_diff, y_mla, w_out, x, gate)


def _layernorm(r, g, b):
    mu = jnp.mean(r, axis=-1, keepdims=True)
    var = jnp.mean(jnp.square(r - mu), axis=-1, keepdims=True)
    return (r - mu) * lax.rsqrt(var + 1e-5) * g + b


def _ln_kernel(r_ref, g_ref, b_ref, x_ref):
    x_ref[...] = _layernorm(r_ref[...], g_ref[...], b_ref[...])


def _ln_mod_kernel(r_ref, g_ref, b_ref, sc_ref, sh_ref, x_ref, u_ref):
    x = _layernorm(r_ref[...], g_ref[...], b_ref[...])
    x_ref[...] = x
    u_ref[...] = (x * (1.0 + sc_ref[...]) + sh_ref[...]).astype(BF16)


def _ln_call(r, g, b, scale=None, shift=None):
    nb, L, d = r.shape
    bt, lt = _seq_tiles(nb, L, 256)
    xspec = pl.BlockSpec((bt, lt, d), lambda i, j: (i, j, 0))
    wspec = pl.BlockSpec((1, 1, d), lambda i, j: (0, 0, 0))
    sspec = pl.BlockSpec((bt, 1, d), lambda i, j: (i, 0, 0))
    g3, b3 = g.reshape(1, 1, d), b.reshape(1, 1, d)
    if scale is None:
        return pl.pallas_call(
            _ln_kernel, grid=(nb // bt, L // lt),
            in_specs=[xspec, wspec, wspec], out_specs=xspec,
            out_shape=jax.ShapeDtypeStruct((nb, L, d), F32),
            compiler_params=_cparams(("arbitrary", "arbitrary")), name="layernorm",
        )(r, g3, b3), None
    return pl.pallas_call(
        _ln_mod_kernel, grid=(nb // bt, L // lt),
        in_specs=[xspec, wspec, wspec, sspec, sspec], out_specs=[xspec, xspec],
        out_shape=[jax.ShapeDtypeStruct((nb, L, d), F32), jax.ShapeDtypeStruct((nb, L, d), BF16)],
        compiler_params=_cparams(("arbitrary", "arbitrary")), name="layernorm_modulate",
    )(r, g3, b3, scale, shift)


def _ssd_chunk(L):
    for t in (128, 64, 32, 16, 8):
        if L % t == 0:
            return t
    raise ValueError(f"sequence length {L} is not a multiple of 8")


def _attn_tile(L, want):
    t = want
    while L % t:
        t //= 2
    return t


def _layer(x, u, mod_l, next_mod, wl, cols, layer_idx, depth, caches):
    nb, L, d = x.shape
    heads = cols.heads
    mla_heads = cols.mla_w // MLA_V_DIM
    _, _, gate = mod_l
    k_cache, v_cache, lat_cache, kr_cache, st0, conv0 = caches
    P = 0 if k_cache is None else lat_cache.shape[2]
    pos = P + jnp.arange(L, dtype=jnp.int32)

    h2d = _inproj_call(u.reshape(nb * L, d), wl["w_in"], layer_idx, cols.tn)

    T = _ssd_chunk(L)
    y_ssd, ssm_new, conv_new = _ssd_call(h2d, conv0, st0, wl["conv_w"], wl["conv_b"], wl["dt_bias"], wl["a_log"],
                                         wl["d_skip"], wl["ssd_norm_w"], wl["e_mat"], wl["et_mat"], cols, nb, L, T)

    tabs_d = _rope_tables(pos, DIFF_ROT, DIFF_QK_DIM, LANES)
    tabs_m = _rope_tables(pos, MLA_ROPE_DIM, MLA_ROPE_DIM, MLA_ROPE_DIM)
    lam_init = 0.8 - 0.6 * math.exp(-0.3 * layer_idx)
    if k_cache is None:
        t = _attn_tile(L, 512)
        qd, k32, kb, v32, vb, lat32, kr32, latb, kt = _prep_call(h2d, tabs_d, tabs_m, wl["kv_norm_w"], cols, nb, L, t,
                                                                 wl["w_uk_flat"])
        qh = _mlaq_call(h2d, wl["q_norm_w"], wl["w_uq"], wl["w_uk"], tabs_m, cols, nb, L, mla_heads, False)
        y_diff = _diff_prompt_call(qd, kb, vb, wl["lam"], wl["diff_norm_w"], h2d, cols, lam_init, t)
        y_mla = _mla_prompt_call(qh, kt, latb, wl["w_uv"], h2d, cols, t, MLA_HEAD_GROUP)
    else:
        qd, k32, kb, v32, vb, lat32, kr32, kcat = _prep_call(h2d, tabs_d, tabs_m, wl["kv_norm_w"], cols, nb, L,
                                                             min(L, 256))
        qcat = _mlaq_call(h2d, wl["q_norm_w"], wl["w_uq"], wl["w_uk"], tabs_m, cols, nb, L, mla_heads, True)
        y_diff = _diff_sample_call(qd, kb, vb, k_cache, v_cache, layer_idx, wl["lam"], wl["diff_norm_w"], h2d,
                                   cols, lam_init, _attn_tile(P, 1024))
        y_mla = _mla_sample_call(qcat, kcat, lat_cache, kr_cache, layer_idx, wl["w_uv"], h2d, cols,
                                 _attn_tile(P, 2048))

    alpha = (2 * depth) ** 0.25
    r = _outproj_call(y_ssd.reshape(nb, L, cols.ssd_w), y_diff, y_mla, wl["w_out"], x, gate, alpha)
    if next_mod is None:
        x_new, u_new = _ln_call(r, wl["ln_g"], wl["ln_b"])
    else:
        x_new, u_new = _ln_call(r, wl["ln_g"], wl["ln_b"], next_mod[1], next_mod[0])
    k_out = k32.reshape(nb, L, cols.diff_w // (2 * DIFF_QK_DIM), 2, DIFF_QK_DIM)
    v_out = v32.reshape(nb, L, cols.diff_w // DIFF_V_DIM, DIFF_V_DIM)
    return x_new, u_new, (k_out, v_out, lat32, kr32, ssm_new, conv_new)


def _layer_weights(l, cols, w_in, conv_w, conv_b, dt_bias, a_log, d_skip, ssd_norm_w, lambda_q1, lambda_k1,
                   lambda_q2, lambda_k2, diff_norm_w, mla_q_norm_w, mla_kv_norm_w, w_uq, w_uk, w_uv, w_out,
                   ln_g, ln_b):
    heads = cols.heads
    mla_heads = cols.mla_w // MLA_V_DIM
    lane_pad = lambda v: jnp.pad(v, (MLA_ROPE_DIM, LANES - MLA_ROPE_DIM - heads)).reshape(1, LANES)
    qk = MLA_NOPE_DIM + MLA_ROPE_DIM
    wq = w_uq[l].reshape(MLA_Q_RANK, mla_heads, qk)
    wq_nope = wq[:, :, :MLA_NOPE_DIM].reshape(MLA_Q_RANK, mla_heads * MLA_NOPE_DIM)
    wq_rope = jnp.pad(wq[:, :, MLA_NOPE_DIM:], ((0, 0), (0, 0), (0, LANES - MLA_ROPE_DIM)))
    wq_p = jnp.concatenate([wq_nope, wq_rope.reshape(MLA_Q_RANK, mla_heads * LANES)], axis=1).astype(BF16)
    eh = np.zeros((LANES, cols.ssd_w), np.float32)
    for h in range(heads):
        eh[MLA_ROPE_DIM + h, h * SSD_HEAD_DIM:(h + 1) * SSD_HEAD_DIM] = 1.0
    return dict(
        w_in=w_in,
        conv_w=conv_w[l], conv_b=conv_b[l].reshape(1, -1),
        dt_bias=lane_pad(dt_bias[l]), a_log=lane_pad(a_log[l]),
        d_skip=jnp.repeat(d_skip[l], SSD_HEAD_DIM).reshape(1, -1),
        ssd_norm_w=ssd_norm_w[l].reshape(1, -1),
        e_mat=jnp.asarray(eh, BF16), et_mat=jnp.asarray(eh.T, BF16),
        lam=jnp.stack([lambda_q1[l], lambda_k1[l], lambda_q2[l], lambda_k2[l]]),
        diff_norm_w=diff_norm_w[l].reshape(1, -1),
        q_norm_w=mla_q_norm_w[l].reshape(1, -1), kv_norm_w=mla_kv_norm_w[l].reshape(1, -1),
        w_uq=wq_p,
        w_uk=jnp.transpose(w_uk[l], (1, 2, 0)).astype(BF16),
        w_uk_flat=w_uk[l].reshape(MLA_KV_RANK, mla_heads * MLA_NOPE_DIM).astype(BF16),
        w_uv=w_uv[l].reshape(MLA_KV_RANK, mla_heads * MLA_V_DIM).astype(BF16),
        w_out=w_out[l].astype(BF16),
        ln_g=ln_g[l], ln_b=ln_b[l],
    )


def kernel(x_prompt, x_sample, cache_diff_k, cache_diff_v, cache_mla_latent, cache_mla_krope, state_ssm, state_conv,
           c_prompt, c_sample, w_mod, b_mod, w_in, conv_w, conv_b, dt_bias, a_log, d_skip, ssd_norm_w, lambda_q1,
           lambda_k1, lambda_q2, lambda_k2, diff_norm_w, mla_q_norm_w, mla_kv_norm_w, w_uq, w_uk, w_uv, w_out,
           ln_g, ln_b):
    depth = w_in.shape[0]
    bp, _, d = x_prompt.shape
    bs = x_sample.shape[0]
    cols = _Cols(d)
    heads = cols.heads

    rows = -(-(bp + bs) // 8) * 8
    c_all = jnp.concatenate([c_prompt, c_sample, jnp.zeros((rows - bp - bs, d), F32)], axis=0)
    mod = _mod_call(c_all, w_mod, b_mod)

    def mods(l, lo, n):
        part = lambda k: mod[l, lo:lo + n, k * d:(k + 1) * d].reshape(n, 1, d)
        return part(0), part(1), part(2)

    pdiff = cache_diff_k.shape[2]
    kc = jnp.transpose(cache_diff_k, (0, 1, 3, 4, 5, 2))
    vc = cache_diff_v.reshape(depth, bs, pdiff * cache_diff_v.shape[3], DIFF_V_DIM)
    krc = jnp.transpose(cache_mla_krope, (0, 1, 3, 2))
    zero_state = jnp.zeros((bp, heads, SSD_HEAD_DIM, SSD_STATE), F32)
    zero_conv = jnp.zeros((bp, SSD_CONV - 1, cols.conv_dim), F32)

    w_in_p = _permute_w_in(w_in, cols)
    hp, hs = x_prompt, x_sample
    mp, ms = mods(0, 0, bp), mods(0, bp, bs)
    up = _modulate_call(hp, mp[1], mp[0])
    us = _modulate_call(hs, ms[1], ms[0])
    st_p, st_s = [], []
    for l in range(depth):
        wl = _layer_weights(l, cols, w_in_p, conv_w, conv_b, dt_bias, a_log, d_skip, ssd_norm_w, lambda_q1, lambda_k1,
                            lambda_q2, lambda_k2, diff_norm_w, mla_q_norm_w, mla_kv_norm_w, w_uq, w_uk, w_uv, w_out,
                            ln_g, ln_b)
        nmp = mods(l + 1, 0, bp) if l + 1 < depth else None
        nms = mods(l + 1, bp, bs) if l + 1 < depth else None
        hp, up, sp = _layer(hp, up, mp, nmp, wl, cols, l, depth, (None, None, None, None, zero_state, zero_conv))
        hs, us, ss = _layer(hs, us, ms, nms, wl, cols, l, depth,
                            (kc, vc, cache_mla_latent, krc, state_ssm[l], state_conv[l]))
        st_p.append(sp)
        st_s.append(ss)
        mp, ms = nmp, nms
    stack = lambda sts, i: jnp.stack([s[i] for s in sts])
    return ((hp, hs) + tuple(stack(st_p, i) for i in range(6)) + tuple(stack(st_s, i) for i in range(6)))
```

```python
import functools
import math

import numpy as np
import jax
import jax.numpy as jnp
from jax import lax
from jax.experimental import pallas as pl
from jax.experimental.pallas import tpu as pltpu

F32 = jnp.float32
BF16 = jnp.bfloat16

CHUNK = 64
ROPE_THETA = 500000.0
NEG_INF = -1e30
SSD_HEAD_DIM = 64
SSD_GROUPS = 4
SSD_STATE = 128
SSD_CONV = 4
DIFF_QK_DIM = 64
DIFF_V_DIM = 128
DIFF_ROT = DIFF_QK_DIM // 4
DIFF_SCALE = DIFF_QK_DIM ** -0.5
MLA_V_DIM = 128
MLA_NOPE_DIM = 128
MLA_ROPE_DIM = 64
MLA_Q_RANK = 768
MLA_KV_RANK = 256
MLA_SCALE = (MLA_NOPE_DIM + MLA_ROPE_DIM) ** -0.5
MLA_QK_PAD = MLA_KV_RANK + 128
MLA_QH_DIM = MLA_NOPE_DIM + 128
MLA_HEAD_GROUP = 2
LOG2E = math.log2(math.e)

LANES = 128
VMEM_LIMIT = 56 * 1024 * 1024


def _cparams(sem):
    return pltpu.CompilerParams(dimension_semantics=sem, vmem_limit_bytes=VMEM_LIMIT)


def _silu(x):
    hx = 0.5 * x
    return hx + hx * jnp.tanh(hx)


def _nt_dot(a, b):
    return lax.dot_general(a, b, (((1,), (1,)), ((), ())), preferred_element_type=F32)


def _split2(v):
    hi = v.astype(BF16)
    lo = (v - hi.astype(F32)).astype(BF16)
    return hi, lo


def _split3(v):
    hi = v.astype(BF16)
    r = v - hi.astype(F32)
    mid = r.astype(BF16)
    lo = (r - mid.astype(F32)).astype(BF16)
    return hi, mid, lo


def _mod_kernel(c_ref, w_ref, b_ref, o_ref):
    a = _silu(c_ref[...]).astype(BF16)
    o_ref[0] = jnp.dot(a, w_ref[0].astype(BF16), preferred_element_type=F32) + b_ref[0]


def _mod_call(c_all, w_mod, b_mod):
    depth, d, n3 = w_mod.shape
    rows = c_all.shape[0]
    tn = 512
    return pl.pallas_call(
        _mod_kernel,
        grid=(depth, n3 // tn),
        in_specs=[pl.BlockSpec((rows, d), lambda l, j: (0, 0)),
                  pl.BlockSpec((1, d, tn), lambda l, j: (l, 0, j)),
                  pl.BlockSpec((1, 1, tn), lambda l, j: (l, 0, j))],
        out_specs=pl.BlockSpec((1, rows, tn), lambda l, j: (l, 0, j)),
        out_shape=jax.ShapeDtypeStruct((depth, rows, n3), F32),
        compiler_params=_cparams(("arbitrary", "arbitrary")),
        name="adaln_mod",
    )(c_all, w_mod, b_mod.reshape(depth, 1, n3))


def _modulate_kernel(x_ref, sc_ref, sh_ref, u_ref):
    u_ref[...] = (x_ref[...] * (1.0 + sc_ref[...]) + sh_ref[...]).astype(BF16)


def _seq_tiles(nb, L, rows):
    if L >= rows:
        return 1, rows
    return min(nb, rows // L), L


def _modulate_call(x, scale, shift):
    nb, L, d = x.shape
    bt, lt = _seq_tiles(nb, L, 256)
    return pl.pallas_call(
        _modulate_kernel,
        grid=(nb // bt, L // lt),
        in_specs=[pl.BlockSpec((bt, lt, d), lambda i, j: (i, j, 0)),
                  pl.BlockSpec((bt, 1, d), lambda i, j: (i, 0, 0)),
                  pl.BlockSpec((bt, 1, d), lambda i, j: (i, 0, 0))],
        out_specs=pl.BlockSpec((bt, lt, d), lambda i, j: (i, j, 0)),
        out_shape=jax.ShapeDtypeStruct((nb, L, d), BF16),
        compiler_params=_cparams(("arbitrary", "arbitrary")),
        name="modulate",
    )(x, scale, shift)


class _Cols:
    def __init__(self, d_model):
        self.ssd_w = d_model // 2
        self.diff_w = d_model // 4
        self.mla_w = d_model // 4
        self.heads = self.ssd_w // SSD_HEAD_DIM
        self.conv_dim = self.ssd_w + 2 * SSD_GROUPS * SSD_STATE
        self.in_sizes = (self.ssd_w, self.conv_dim, self.heads, self.diff_w, self.diff_w, self.diff_w,
                         self.diff_w, MLA_Q_RANK, MLA_KV_RANK, MLA_ROPE_DIM, self.mla_w)
        o = 0
        self.xbc = o; o += self.conv_dim
        self.dq = o; o += self.diff_w
        self.z = o; o += self.ssd_w
        self.dk = o; o += self.diff_w
        self.dv = o; o += self.diff_w
        self.dgate = o; o += self.diff_w
        self.mgate = o; o += self.mla_w
        self.cq = o; o += MLA_Q_RANK + MLA_KV_RANK
        self.krdt = o; o += LANES
        self.used = o
        self.tn = 1280
        self.total = -(-o // self.tn) * self.tn


def _w_in_blocks(cols):
    offs = np.concatenate([[0], np.cumsum(cols.in_sizes)])
    order = (1, 3, 0, 4, 5, 6, 10)
    rows = []
    for i in order:
        rows += list(range(int(offs[i]), int(offs[i + 1]), LANES))
    rows += list(range(int(offs[7]), int(offs[9]), LANES))
    rows.append(-1)
    rows += [-2] * ((cols.total - cols.used) // LANES)
    return np.asarray(rows, np.int32)


def _wprep_kernel(tab_ref, w_ref, sp_ref, o_ref):
    src = tab_ref[pl.program_id(1)]

    @pl.when(src >= 0)
    def _():
        o_ref[0] = w_ref[0].astype(BF16)

    @pl.when(src == -1)
    def _():
        o_ref[0] = sp_ref[0].astype(BF16)

    @pl.when(src == -2)
    def _():
        o_ref[0] = jnp.zeros(o_ref.shape[1:], BF16)


def _permute_w_in(w_in, cols):
    depth, d, _ = w_in.shape
    wt = jnp.transpose(w_in, (0, 2, 1))
    offs = np.concatenate([[0], np.cumsum(cols.in_sizes)])
    pad = LANES - MLA_ROPE_DIM - cols.heads
    special = jnp.concatenate([wt[:, offs[9]:offs[10]], wt[:, offs[2]:offs[3]], jnp.zeros((depth, pad, d), F32)], axis=1)
    tab = jnp.asarray(_w_in_blocks(cols))
    grid_spec = pltpu.PrefetchScalarGridSpec(
        num_scalar_prefetch=1,
        grid=(depth, int(tab.shape[0])),
        in_specs=[pl.BlockSpec((pl.Element(1), pl.Element(LANES), pl.Element(d)),
                               lambda l, i, tab: (l, pl.multiple_of(jnp.maximum(tab[i], 0), 8), 0)),
                  pl.BlockSpec((1, LANES, d), lambda l, i, tab: (l, 0, 0))],
        out_specs=pl.BlockSpec((1, LANES, d), lambda l, i, tab: (l, i, 0)))
    return pl.pallas_call(
        _wprep_kernel, grid_spec=grid_spec,
        out_shape=jax.ShapeDtypeStruct((depth, cols.total, d), BF16),
        compiler_params=_cparams(("arbitrary", "arbitrary")),
        name="w_in_prep",
    )(tab, wt, special)


def _matmul_nt_kernel(x_ref, w_ref, o_ref):
    o_ref[...] = _nt_dot(x_ref[...], w_ref[...])


def _inproj_call(u2d, wp, layer, tn):
    m, k = u2d.shape
    n = wp.shape[1]
    tm = min(m, 512)
    return pl.pallas_call(
        _matmul_nt_kernel,
        grid=(n // tn, m // tm),
        in_specs=[pl.BlockSpec((tm, k), lambda j, i: (i, 0)),
                  pl.BlockSpec((None, tn, k), lambda j, i: (layer, j, 0))],
        out_specs=pl.BlockSpec((tm, tn), lambda j, i: (i, j)),
        out_shape=jax.ShapeDtypeStruct((m, n), F32),
        compiler_params=_cparams(("arbitrary", "arbitrary")),
        name="in_proj",
    )(u2d, wp)


def _ssd_kernel(xbc_ref, z_ref, dtb_ref, conv0_ref, st0_ref, cw_ref, cb_ref, dtbias_ref, alog_ref,
                dskip_ref, nw_ref, e_ref, et_ref, y_ref, st_ref, convo_ref, ext_ref, *, T, nchunks, heads):
    c = pl.program_id(1)
    ssd_w = heads * SSD_HEAD_DIM
    gw = ssd_w // SSD_GROUPS
    hpg = heads // SSD_GROUPS
    conv_dim = ext_ref.shape[1]
    dt_lo = MLA_ROPE_DIM

    @pl.when(c == 0)
    def _():
        ext_ref[0:8, :] = jnp.zeros((8, conv_dim), F32)
        ext_ref[8 - (SSD_CONV - 1):8, :] = conv0_ref[0]
        st_ref[0] = st0_ref[0]

    ext_ref[8:8 + T, :] = xbc_ref[...]
    acc = cb_ref[...]
    for j in range(SSD_CONV):
        lo = 8 - (SSD_CONV - 1) + j
        acc = acc + ext_ref[lo:lo + T, :] * cw_ref[j:j + 1, :]
    ext_ref[0:8, :] = ext_ref[T:T + 8, :]

    @pl.when(c == nchunks - 1)
    def _():
        convo_ref[0] = ext_ref[8 - (SSD_CONV - 1):8, :]

    xact = _silu(acc)
    xs = xact[:, :ssd_w]
    bm = xact[:, ssd_w:ssd_w + SSD_GROUPS * SSD_STATE]
    cm = xact[:, ssd_w + SSD_GROUPS * SSD_STATE:]

    lane = lax.broadcasted_iota(jnp.int32, (1, LANES), 1)
    is_dt = (lane >= dt_lo) & (lane < dt_lo + heads)
    xdt = dtb_ref[...] + dtbias_ref[...]
    dt = jnp.where(is_dt, jnp.maximum(xdt, 0.0) + jnp.log1p(jnp.exp(-jnp.abs(xdt))), 0.0)
    a_neg = jnp.where(is_dt, -jnp.exp(alog_ref[...]), 0.0)
    da = dt * a_neg

    row = lax.broadcasted_iota(jnp.int32, (T, T), 0)
    col = lax.broadcasted_iota(jnp.int32, (T, T), 1)
    causal = row >= col
    tril = causal.astype(BF16)
    eye = (lax.broadcasted_iota(jnp.int32, (LANES, LANES), 0)
           == lax.broadcasted_iota(jnp.int32, (LANES, LANES), 1)).astype(BF16)

    da3 = _split3(da)
    a_cs = sum(jnp.dot(tril, p, preferred_element_type=F32) for p in da3)
    a3 = _split3(a_cs)
    a_cs_t = sum(_nt_dot(eye, p) for p in a3)
    a_last = a_cs[T - 1:T, :]
    ea = jnp.exp(a_cs)
    te = jnp.exp(a_last - a_cs)

    e_mat = e_ref[...]

    def expand(v):
        hi, lo = _split2(v)
        return jnp.dot(hi, e_mat, preferred_element_type=F32) + jnp.dot(lo, e_mat, preferred_element_type=F32)

    dt_x = expand(dt)
    ea_x = expand(ea)
    te_x = expand(te)
    cd_col = jnp.exp(a_cs_t[:, T - 1:T])
    cd_b = jnp.broadcast_to(cd_col, (LANES, SSD_STATE))
    cdh, cdl = _split2(cd_b)
    et_mat = et_ref[...]
    cd_full = (jnp.dot(et_mat, cdh, preferred_element_type=F32)
               + jnp.dot(et_mat, cdl, preferred_element_type=F32))

    xd = xs * dt_x
    xde = (xd * te_x).astype(BF16)
    lane_p = lax.broadcasted_iota(jnp.int32, (T, LANES), 1)
    lower_half = lane_p < SSD_HEAD_DIM

    for g in range(SSD_GROUPS):
        cg = cm[:, g * SSD_STATE:(g + 1) * SSD_STATE].astype(BF16)
        bg = bm[:, g * SSD_STATE:(g + 1) * SSD_STATE].astype(BF16)
        cbm = _nt_dot(cg, bg)
        st_g = st_ref[0, g * hpg:(g + 1) * hpg].reshape(gw, SSD_STATE)
        y_off = _nt_dot(cg, st_g.astype(BF16)) * ea_x[:, g * gw:(g + 1) * gw]
        pieces = []
        for q in range(hpg // 2):
            c0 = g * gw + q * LANES
            xd_pair = xd[:, c0:c0 + LANES].astype(BF16)
            ys = []
            for h in (g * hpg + 2 * q, g * hpg + 2 * q + 1):
                seg = a_cs[:, dt_lo + h:dt_lo + h + 1] - a_cs_t[dt_lo + h:dt_lo + h + 1, :]
                decay = jnp.exp(jnp.where(causal, seg, -jnp.inf))
                ys.append(jnp.dot((cbm * decay).astype(BF16), xd_pair, preferred_element_type=F32))
            pieces.append(jnp.where(lower_half, ys[0], ys[1]))
        y_g = jnp.concatenate(pieces, axis=1) + y_off
        upd = lax.dot_general(xde[:, g * gw:(g + 1) * gw], bg, (((0,), (0,)), ((), ())),
                              preferred_element_type=F32)
        st_new = st_g * cd_full[g * gw:(g + 1) * gw, :] + upd
        st_ref[0, g * hpg:(g + 1) * hpg] = st_new.reshape(hpg, SSD_HEAD_DIM, SSD_STATE)
        y_g = y_g + dskip_ref[:, g * gw:(g + 1) * gw] * xs[:, g * gw:(g + 1) * gw]
        yg = y_g * _silu(z_ref[:, g * gw:(g + 1) * gw])
        ms = jnp.mean(yg * yg, axis=-1, keepdims=True)
        y_ref[:, g * gw:(g + 1) * gw] = (yg * lax.rsqrt(ms + 1e-6) * nw_ref[:, g * gw:(g + 1) * gw]).astype(BF16)


def _ssd_call(h2d, conv0, st0, conv_w, conv_b, dtbias_p, alog_p, dskip_x, norm_w, e_mat, et_mat, cols, nb, L, T):
    nchunks = L // T
    heads = cols.heads
    ssd_w = cols.ssd_w
    cd = cols.conv_dim
    kern = functools.partial(_ssd_kernel, T=T, nchunks=nchunks, heads=heads)
    row = lambda b, c: b * nchunks + c
    const2 = lambda b, c: (0, 0)
    return pl.pallas_call(
        kern,
        grid=(nb, nchunks),
        in_specs=[pl.BlockSpec((T, cd), lambda b, c: (row(b, c), cols.xbc // cd)),
                  pl.BlockSpec((T, ssd_w), lambda b, c: (row(b, c), cols.z // ssd_w)),
                  pl.BlockSpec((T, LANES), lambda b, c: (row(b, c), cols.krdt // LANES)),
                  pl.BlockSpec((1, SSD_CONV - 1, cd), lambda b, c: (b, 0, 0)),
                  pl.BlockSpec((1, heads, SSD_HEAD_DIM, SSD_STATE), lambda b, c: (b, 0, 0, 0)),
                  pl.BlockSpec((SSD_CONV, cd), const2),
                  pl.BlockSpec((1, cd), const2),
                  pl.BlockSpec((1, LANES), const2),
                  pl.BlockSpec((1, LANES), const2),
                  pl.BlockSpec((1, ssd_w), const2),
                  pl.BlockSpec((1, ssd_w), const2),
                  pl.BlockSpec((LANES, ssd_w), const2),
                  pl.BlockSpec((ssd_w, LANES), const2)],
        out_specs=[pl.BlockSpec((T, ssd_w), lambda b, c: (row(b, c), 0)),
                   pl.BlockSpec((1, heads, SSD_HEAD_DIM, SSD_STATE), lambda b, c: (b, 0, 0, 0)),
                   pl.BlockSpec((1, SSD_CONV - 1, cd), lambda b, c: (b, 0, 0))],
        out_shape=[jax.ShapeDtypeStruct((nb * L, ssd_w), BF16),
                   jax.ShapeDtypeStruct((nb, heads, SSD_HEAD_DIM, SSD_STATE), F32),
                   jax.ShapeDtypeStruct((nb, SSD_CONV - 1, cd), F32)],
        scratch_shapes=[pltpu.VMEM((T + 8, cd), F32)],
        compiler_params=_cparams(("arbitrary", "arbitrary")),
        name="ssd_scan",
    )(h2d, h2d, h2d, conv0, st0, conv_w, conv_b, dtbias_p, alog_p, dskip_x, norm_w, e_mat, et_mat)


def _rope_tables(pos, rot_dim, period, width):
    half = rot_dim // 2
    inv = ROPE_THETA ** (-jnp.arange(half, dtype=F32) * (2.0 / rot_dim))
    ang = pos.astype(F32)[:, None] * inv[None, :]
    cos, sin = jnp.cos(ang), jnp.sin(ang)
    lane = np.arange(LANES)
    inner = lane % period
    idx = jnp.asarray(inner % half)
    first = jnp.asarray((inner < half) & (lane < width))
    second = jnp.asarray((inner >= half) & (inner < rot_dim) & (lane < width))
    keep = jnp.asarray((inner >= rot_dim) & (lane < width))
    cos_l, sin_l = cos[:, idx], sin[:, idx]
    cos_t = jnp.where(first | second, cos_l, jnp.where(keep, 1.0, 0.0))
    sin_a = jnp.where(first, -sin_l, 0.0)
    sin_b = jnp.where(second, sin_l, 0.0)
    return cos_t.astype(F32), sin_a.astype(F32), sin_b.astype(F32)


def _rope_tile(x, cos_t, sin_a, sin_b, half):
    return (x * cos_t + pltpu.roll(x, LANES - half, 1) * sin_a + pltpu.roll(x, half, 1) * sin_b)


def _rmsnorm(x, w):
    return x * lax.rsqrt(jnp.mean(x * x, axis=-1, keepdims=True) + 1e-6) * w


def _prep_kernel(dq_ref, dk_ref, dv_ref, cq_ref, kr_ref, cd_ref, sad_ref, sbd_ref, cm_ref, sam_ref, sbm_ref,
                 kvw_ref, *rest, per_head, n_alias):
    if per_head:
        wuk_ref, rest = rest[0], rest[1:]
    rest = rest[n_alias:]
    if per_head:
        k32_ref, v32_ref, lat_ref, kro_ref, qd_ref, kb_ref, vb_ref, latb_ref, kt_ref = rest
    else:
        k32_ref, v32_ref, lat_ref, kro_ref, qd_ref, kb_ref, vb_ref, kcat_ref = rest
    cos_d, sa_d, sb_d = cd_ref[...], sad_ref[...], sbd_ref[...]
    width = dq_ref.shape[1]
    for c in range(width // LANES):
        sl = slice(c * LANES, (c + 1) * LANES)
        q = _rope_tile(dq_ref[:, sl], cos_d, sa_d, sb_d, DIFF_ROT // 2)
        qd_ref[0, :, sl] = (q * (DIFF_SCALE * LOG2E)).astype(BF16)
        k = _rope_tile(dk_ref[:, sl], cos_d, sa_d, sb_d, DIFF_ROT // 2)
        k32_ref[0, :, sl] = k
        kb_ref[0, :, sl] = k.astype(BF16)
    v = dv_ref[...]
    v32_ref[0] = v
    vb_ref[0] = v.astype(BF16)
    lat = _rmsnorm(cq_ref[:, MLA_Q_RANK:MLA_Q_RANK + MLA_KV_RANK], kvw_ref[...])
    lat_ref[0] = lat
    kr = _rope_tile(kr_ref[...], cm_ref[...], sam_ref[...], sbm_ref[...], MLA_ROPE_DIM // 2)
    kro_ref[0] = kr[:, :MLA_ROPE_DIM]
    lat_b = lat.astype(BF16)
    if per_head:
        latb_ref[0] = lat_b
        k_nope = jnp.dot(lat_b, wuk_ref[...], preferred_element_type=F32)
        for h in range(kt_ref.shape[1]):
            k_h = jnp.concatenate([k_nope[:, h * MLA_NOPE_DIM:(h + 1) * MLA_NOPE_DIM], kr], axis=1)
            kt_ref[0, h, 0] = k_h.T.astype(BF16)
    else:
        kcat_ref[0, :, :MLA_KV_RANK] = lat_b
        kcat_ref[0, :, MLA_KV_RANK:] = kr.astype(BF16)


def _prep_call(h2d, tabs_d, tabs_m, kv_w, cols, nb, L, tm, layer, state_bufs, wuk_flat=None):
    nt = L // tm
    dw = cols.diff_w
    per_head = wuk_flat is not None
    row = lambda b, i: b * nt + i
    hspec = lambda width, off: pl.BlockSpec((tm, width), lambda b, i: (row(b, i), off // width))
    tspec = pl.BlockSpec((tm, LANES), lambda b, i: (i, 0))
    ospec = lambda width: pl.BlockSpec((1, tm, width), lambda b, i: (b, i, 0))
    sspec = lambda width: pl.BlockSpec((None, 1, tm, width), lambda b, i: (layer, b, i, 0))
    in_specs = [hspec(dw, cols.dq), hspec(dw, cols.dk), hspec(dw, cols.dv),
                hspec(MLA_Q_RANK + MLA_KV_RANK, cols.cq), hspec(LANES, cols.krdt),
                tspec, tspec, tspec, tspec, tspec, tspec,
                pl.BlockSpec((1, MLA_KV_RANK), lambda b, i: (0, 0))]
    args = [h2d, h2d, h2d, h2d, h2d, *tabs_d, *tabs_m, kv_w]
    if per_head:
        in_specs.append(pl.BlockSpec(wuk_flat.shape, lambda b, i: (0, 0)))
        args.append(wuk_flat)
    aliases = {len(args) + n: n for n in range(len(state_bufs))}
    in_specs += [pl.BlockSpec(memory_space=pl.ANY)] * len(state_bufs)
    args += list(state_bufs)
    out_specs = [sspec(dw), sspec(dw), sspec(MLA_KV_RANK), sspec(MLA_ROPE_DIM), ospec(dw), ospec(dw), ospec(dw)]
    out_shape = [jax.ShapeDtypeStruct(b.shape, b.dtype) for b in state_bufs]
    out_shape += [jax.ShapeDtypeStruct((nb, L, dw), BF16)] * 3
    if per_head:
        heads = wuk_flat.shape[1] // MLA_NOPE_DIM
        out_specs += [ospec(MLA_KV_RANK),
                      pl.BlockSpec((1, heads, 1, MLA_QH_DIM, tm), lambda b, i: (b, 0, i, 0, 0))]
        out_shape += [jax.ShapeDtypeStruct((nb, L, MLA_KV_RANK), BF16),
                      jax.ShapeDtypeStruct((nb, heads, nt, MLA_QH_DIM, tm), BF16)]
    else:
        out_specs.append(ospec(MLA_QK_PAD))
        out_shape.append(jax.ShapeDtypeStruct((nb, L, MLA_QK_PAD), BF16))
    return pl.pallas_call(
        functools.partial(_prep_kernel, per_head=per_head, n_alias=len(state_bufs)),
        grid=(nb, nt),
        in_specs=in_specs, out_specs=out_specs, out_shape=out_shape,
        input_output_aliases=aliases,
        compiler_params=_cparams(("arbitrary", "arbitrary")),
        name="attn_prep",
    )(*args)


def _mlaq_kernel(cq_ref, qw_ref, wuq_ref, wuk_ref, cm_ref, sam_ref, sbm_ref, o_ref, *, heads, absorb):
    cqn = _rmsnorm(cq_ref[:, :MLA_Q_RANK], qw_ref[...]).astype(BF16)
    qm = jnp.dot(cqn, wuq_ref[...], preferred_element_type=F32)
    cos_m, sa_m, sb_m = cm_ref[...], sam_ref[...], sbm_ref[...]
    nope_w = heads * MLA_NOPE_DIM
    scale = MLA_SCALE * LOG2E
    for h in range(heads):
        nope = qm[:, h * MLA_NOPE_DIM:(h + 1) * MLA_NOPE_DIM]
        qr = _rope_tile(qm[:, nope_w + h * LANES:nope_w + (h + 1) * LANES], cos_m, sa_m, sb_m, MLA_ROPE_DIM // 2)
        if absorb:
            ql = jnp.dot(nope.astype(BF16), wuk_ref[h], preferred_element_type=F32)
        else:
            ql = nope
        width = ql.shape[1]
        o_ref[0, h, :, :width] = (ql * scale).astype(BF16)
        o_ref[0, h, :, width:] = (qr * scale).astype(BF16)


def _mlaq_call(h2d, q_w, wuq_p, wuk_t, tabs_m, cols, nb, L, heads, absorb):
    tm = min(L, 256)
    nt = L // tm
    width = MLA_Q_RANK + MLA_KV_RANK
    qdim = MLA_QK_PAD if absorb else MLA_QH_DIM
    tspec = pl.BlockSpec((tm, LANES), lambda b, i: (i, 0))
    return pl.pallas_call(
        functools.partial(_mlaq_kernel, heads=heads, absorb=absorb),
        grid=(nb, nt),
        in_specs=[pl.BlockSpec((tm, width), lambda b, i: (b * nt + i, cols.cq // width)),
                  pl.BlockSpec((1, MLA_Q_RANK), lambda b, i: (0, 0)),
                  pl.BlockSpec(wuq_p.shape, lambda b, i: (0, 0)),
                  pl.BlockSpec(wuk_t.shape, lambda b, i: (0, 0, 0)),
                  tspec, tspec, tspec],
        out_specs=pl.BlockSpec((1, heads, tm, qdim), lambda b, i: (b, 0, i, 0)),
        out_shape=jax.ShapeDtypeStruct((nb, heads, L, qdim), BF16),
        compiler_params=_cparams(("arbitrary", "arbitrary")),
        name="mla_q",
    )(h2d, q_w, wuq_p, wuk_t, *tabs_m)


def _softmax_init(m_ref, l_ref, acc_ref):
    m_ref[...] = jnp.full(m_ref.shape, -jnp.inf, F32)
    l_ref[...] = jnp.zeros(l_ref.shape, F32)
    acc_ref[...] = jnp.zeros(acc_ref.shape, F32)


def _softmax_update(s, v, m_ref, l_ref, acc_ref):
    m_prev = m_ref[...]
    m_new = jnp.maximum(m_prev, jnp.max(s, axis=-1, keepdims=True))
    alpha = jnp.exp2(m_prev - m_new)
    p = jnp.exp2(s - m_new)
    l_ref[...] = alpha * l_ref[...] + jnp.sum(p, axis=-1, keepdims=True)
    acc_ref[...] = alpha * acc_ref[...] + jnp.dot(p.astype(BF16), v, preferred_element_type=F32)
    m_ref[...] = m_new


def _diag_mask(rows, tq, tk, q0):
    q_tok = q0 + (lax.broadcasted_iota(jnp.int32, (rows, tk), 0) & (tq - 1))
    k_tok = lax.broadcasted_iota(jnp.int32, (rows, tk), 1)
    return (k_tok // CHUNK) <= (q_tok // CHUNK)


def _causal_sweep(n_full, qk, upd, finish, sa_ref, sb_ref):
    qk(0, sa_ref)

    def pair(jj, carry):
        j = 2 * jj
        qk(j + 1, sb_ref)
        upd(j, sa_ref)
        qk(j + 2, sa_ref)
        upd(j + 1, sb_ref)
        return carry

    lax.fori_loop(0, n_full // 2, pair, 0)

    @pl.when(n_full % 2 == 1)
    def _():
        qk(n_full, sb_ref)
        upd(n_full - 1, sa_ref)
        finish(sb_ref)

    @pl.when(n_full % 2 == 0)
    def _():
        finish(sa_ref)


def _softmax_init_wide(m_ref, acc_ref):
    m_ref[...] = jnp.full(m_ref.shape, -jnp.inf, F32)
    acc_ref[...] = jnp.zeros(acc_ref.shape, F32)


def _with_ones(v):
    return jnp.concatenate([v, jnp.ones((v.shape[0], LANES), BF16)], axis=1)


def _softmax_update_wide(s, pv_fn, m_ref, acc_ref):
    tk = s.shape[1]
    w = min(tk, LANES)
    m_prev = m_ref[...]
    m_new = jnp.maximum(m_prev, jnp.max(s, axis=-1, keepdims=True))
    alpha = jnp.exp2(m_prev - m_new)
    p = jnp.concatenate([jnp.exp2(s[:, c:c + w] - m_new[:, :w]).astype(BF16) for c in range(0, tk, w)], axis=1)
    acc = acc_ref[...]
    acc_ref[...] = jnp.concatenate([alpha] * (acc.shape[1] // LANES), axis=1) * acc + pv_fn(p)
    m_ref[...] = m_new


def _diff_lambda(lam_ref, lam_init):
    s1 = jnp.sum(lam_ref[0:1, :] * lam_ref[1:2, :], axis=-1, keepdims=True)
    s2 = jnp.sum(lam_ref[2:3, :] * lam_ref[3:4, :], axis=-1, keepdims=True)
    return jnp.exp(s1) - jnp.exp(s2) + lam_init


def _diff_finish(o1, o2, lam, nw, gate, lam_init):
    o = o1 - lam * o2
    o = _rmsnorm(o, nw) * (1.0 - lam_init)
    return (o * _silu(gate)).astype(BF16)


def _stack_streams(q):
    lane = lax.broadcasted_iota(jnp.int32, q.shape, 1)
    zero = jnp.zeros_like(q)
    return jnp.concatenate([jnp.where(lane < DIFF_QK_DIM, q, zero), jnp.where(lane >= DIFF_QK_DIM, q, zero)], axis=0)


def _diff_prompt_kernel(q_ref, k_ref, v_ref, lam_ref, nw_ref, g_ref, o_ref,
                        qz_ref, sa_ref, sb_ref, m_ref, acc_ref, *, t, lam_init):
    qi = pl.program_id(2)
    qz_ref[...] = _stack_streams(q_ref[0])
    _softmax_init_wide(m_ref, acc_ref)

    def blk(ref, j):
        return ref[0, pl.ds(pl.multiple_of(j * t, t), t), :]

    def qk(j, s_ref):
        s_ref[...] = _nt_dot(qz_ref[...], blk(k_ref, j))

    def pv_fn(j):
        return lambda p: jnp.dot(p, _with_ones(blk(v_ref, j)), preferred_element_type=F32)

    def upd(j, s_ref):
        _softmax_update_wide(s_ref[...], pv_fn(j), m_ref, acc_ref)

    def finish(s_ref):
        s = jnp.where(_diag_mask(2 * t, t, t, 0), s_ref[...], NEG_INF)
        _softmax_update_wide(s, pv_fn(qi), m_ref, acc_ref)
        o = acc_ref[:, :DIFF_V_DIM] / acc_ref[:, DIFF_V_DIM:]
        lam = _diff_lambda(lam_ref, lam_init)
        o_ref[0] = _diff_finish(o[:t], o[t:], lam, nw_ref[...], g_ref[...], lam_init)

    _causal_sweep(qi, qk, upd, finish, sa_ref, sb_ref)


def _diff_prompt_call(qd, kb, vb, lam_p, norm_w, h2d, cols, lam_init, t):
    nb, L, dw = qd.shape
    heads = dw // DIFF_V_DIM
    nq = L // t
    kern = functools.partial(_diff_prompt_kernel, t=t, lam_init=lam_init)
    seq = pl.BlockSpec((1, L, LANES), lambda b, h, i: (b, 0, h))
    tile = pl.BlockSpec((1, t, LANES), lambda b, h, i: (b, i, h))
    return pl.pallas_call(
        kern,
        grid=(nb, heads, nq),
        in_specs=[tile, seq, seq,
                  pl.BlockSpec((4, DIFF_QK_DIM), lambda b, h, i: (0, 0)),
                  pl.BlockSpec((1, DIFF_V_DIM), lambda b, h, i: (0, 0)),
                  pl.BlockSpec((t, LANES), lambda b, h, i: (b * nq + i, cols.dgate // LANES + h))],
        out_specs=tile,
        out_shape=jax.ShapeDtypeStruct((nb, L, dw), BF16),
        scratch_shapes=[pltpu.VMEM((2 * t, LANES), BF16),
                        pltpu.VMEM((2 * t, t), F32),
                        pltpu.VMEM((2 * t, t), F32),
                        pltpu.VMEM((2 * t, LANES), F32),
                        pltpu.VMEM((2 * t, DIFF_V_DIM + LANES), F32)],
        compiler_params=_cparams(("arbitrary", "arbitrary", "arbitrary")),
        name="diff_attn_prompt",
    )(qd, kb, vb, lam_p, norm_w, h2d)


def _diff_sample_kernel(q_ref, kc_ref, vc_ref, kn_ref, vn_ref, lam_ref, nw_ref, g_ref, o_ref,
                        qz_ref, m_ref, acc_ref, *, nkc, heads, lam_init):
    j = pl.program_id(1)
    L = q_ref.shape[1]
    rows = 2 * L

    @pl.when(j == 0)
    def _():
        for h in range(heads):
            qz_ref[h] = _stack_streams(q_ref[0, :, h * LANES:(h + 1) * LANES])
        _softmax_init_wide(m_ref, acc_ref)

    def step(scores, values):
        def pv_fn(p):
            return jnp.concatenate([jnp.dot(p[h * rows:(h + 1) * rows], _with_ones(values(h)),
                                            preferred_element_type=F32) for h in range(heads)], axis=0)
        s = jnp.concatenate([scores(h) for h in range(heads)], axis=0)
        _softmax_update_wide(s, pv_fn, m_ref, acc_ref)

    @pl.when(j < nkc)
    def _():
        tk = kc_ref.shape[5]

        def scores(h):
            return jnp.dot(qz_ref[h], kc_ref[0, 0, h].reshape(2 * DIFF_QK_DIM, tk).astype(BF16),
                           preferred_element_type=F32)

        def values(h):
            return vc_ref[0, 0, pl.ds(h, tk, stride=heads), :].astype(BF16)

        step(scores, values)

    @pl.when(j == nkc)
    def _():
        step(lambda h: _nt_dot(qz_ref[h], kn_ref[0, :, h * LANES:(h + 1) * LANES]),
             lambda h: vn_ref[0, :, h * LANES:(h + 1) * LANES])
        lam = _diff_lambda(lam_ref, lam_init)
        o = acc_ref[:, :DIFF_V_DIM] / acc_ref[:, DIFF_V_DIM:]
        for h in range(heads):
            sl = slice(h * LANES, (h + 1) * LANES)
            oh = o[h * rows:(h + 1) * rows]
            o_ref[0, :, sl] = _diff_finish(oh[:L], oh[L:], lam, nw_ref[...], g_ref[:, sl], lam_init)


def _diff_sample_call(qd, kb, vb, k_cache, v_cache, layer, lam_p, norm_w, h2d, cols, lam_init, tk):
    nb, L, dw = qd.shape
    heads = dw // DIFF_V_DIM
    P = k_cache.shape[5]
    nkc = P // tk
    last = nkc - 1
    kern = functools.partial(_diff_sample_kernel, nkc=nkc, heads=heads, lam_init=lam_init)
    kspec = pl.BlockSpec((1, 1, heads, 2, DIFF_QK_DIM, tk),
                         lambda b, j: (layer, b, 0, 0, 0, jnp.minimum(j, last)))
    vspec = pl.BlockSpec((1, 1, tk * heads, DIFF_V_DIM), lambda b, j: (layer, b, jnp.minimum(j, last), 0))
    nspec = pl.BlockSpec((1, L, dw), lambda b, j: (b, 0, 0))
    return pl.pallas_call(
        kern,
        grid=(nb, nkc + 1),
        in_specs=[nspec, kspec, vspec, nspec, nspec,
                  pl.BlockSpec((4, DIFF_QK_DIM), lambda b, j: (0, 0)),
                  pl.BlockSpec((1, DIFF_V_DIM), lambda b, j: (0, 0)),
                  pl.BlockSpec((L, dw), lambda b, j: (b, cols.dgate // dw))],
        out_specs=nspec,
        out_shape=jax.ShapeDtypeStruct((nb, L, dw), BF16),
        scratch_shapes=[pltpu.VMEM((heads, 2 * L, LANES), BF16),
                        pltpu.VMEM((heads * 2 * L, LANES), F32),
                        pltpu.VMEM((heads * 2 * L, DIFF_V_DIM + LANES), F32)],
        compiler_params=_cparams(("arbitrary", "arbitrary")),
        name="diff_attn_sample",
    )(qd, k_cache, v_cache, kb, vb, lam_p, norm_w, h2d)


def _mla_finish(acc_ref, l_ref, wuv_ref, g_ref, o_ref, heads, t):
    o = (acc_ref[...] / l_ref[...]).astype(BF16)
    for h in range(heads):
        sl = slice(h * MLA_V_DIM, (h + 1) * MLA_V_DIM)
        om = jnp.dot(o[h * t:(h + 1) * t], wuv_ref[:, sl], preferred_element_type=F32)
        o_ref[0, :, sl] = (om * _silu(g_ref[:, sl])).astype(BF16)


def _mla_prompt_kernel(q_ref, kt_ref, v_ref, wuv_ref, g_ref, o_ref, sa_ref, sb_ref, m_ref, l_ref, acc_ref, *, t, hg):
    qi = pl.program_id(2)
    _softmax_init(m_ref, l_ref, acc_ref)

    def vblk(j):
        return v_ref[0, pl.ds(pl.multiple_of(j * t, t), t), :]

    def qk(j, s_ref):
        for g in range(hg):
            s_ref[g * t:(g + 1) * t, :] = jnp.dot(q_ref[0, g], kt_ref[0, g, j], preferred_element_type=F32)

    def upd(j, s_ref):
        _softmax_update(s_ref[...], vblk(j), m_ref, l_ref, acc_ref)

    def finish(s_ref):
        s = jnp.where(_diag_mask(hg * t, t, t, 0), s_ref[...], NEG_INF)
        _softmax_update(s, vblk(qi), m_ref, l_ref, acc_ref)
        _mla_finish(acc_ref, l_ref, wuv_ref, g_ref, o_ref, hg, t)

    _causal_sweep(qi, qk, upd, finish, sa_ref, sb_ref)


def _mla_prompt_call(qh, kt, latb, wuv, h2d, cols, t, hg):
    nb, heads, L, qdim = qh.shape
    mw = heads * MLA_V_DIM
    gw = hg * MLA_V_DIM
    nq = L // t
    return pl.pallas_call(
        functools.partial(_mla_prompt_kernel, t=t, hg=hg),
        grid=(nb, heads // hg, nq),
        in_specs=[pl.BlockSpec((1, hg, t, qdim), lambda b, h, i: (b, h, i, 0)),
                  pl.BlockSpec((1, hg, nq, qdim, t), lambda b, h, i: (b, h, 0, 0, 0)),
                  pl.BlockSpec((1, L, MLA_KV_RANK), lambda b, h, i: (b, 0, 0)),
                  pl.BlockSpec((MLA_KV_RANK, gw), lambda b, h, i: (0, h)),
                  pl.BlockSpec((t, gw), lambda b, h, i: (b * nq + i, cols.mgate // gw + h))],
        out_specs=pl.BlockSpec((1, t, gw), lambda b, h, i: (b, i, h)),
        out_shape=jax.ShapeDtypeStruct((nb, L, mw), BF16),
        scratch_shapes=[pltpu.VMEM((hg * t, t), F32),
                        pltpu.VMEM((hg * t, t), F32),
                        pltpu.VMEM((hg * t, 1), F32),
                        pltpu.VMEM((hg * t, 1), F32),
                        pltpu.VMEM((hg * t, MLA_KV_RANK), F32)],
        compiler_params=_cparams(("arbitrary", "arbitrary", "arbitrary")),
        name="mla_attn_prompt",
    )(qh, kt, latb, wuv, h2d)


def _mla_sample_kernel(q_ref, lc_ref, rc_ref, kn_ref, wuv_ref, g_ref, o_ref, m_ref, l_ref, acc_ref,
                       *, nkc, heads):
    j = pl.program_id(1)
    L = q_ref.shape[2]

    @pl.when(j == 0)
    def _():
        _softmax_init(m_ref, l_ref, acc_ref)

    q = q_ref[0].reshape(heads * L, MLA_QK_PAD)

    @pl.when(j < nkc)
    def _():
        lat = lc_ref[0, 0].astype(BF16)
        kr_t = rc_ref[0, 0].astype(BF16)
        s = _nt_dot(q[:, :MLA_KV_RANK], lat) + jnp.dot(q[:, MLA_KV_RANK:MLA_KV_RANK + MLA_ROPE_DIM], kr_t,
                                                       preferred_element_type=F32)
        _softmax_update(s, lat, m_ref, l_ref, acc_ref)

    @pl.when(j == nkc)
    def _():
        kn = kn_ref[0]
        _softmax_update(_nt_dot(q, kn), kn[:, :MLA_KV_RANK], m_ref, l_ref, acc_ref)
        _mla_finish(acc_ref, l_ref, wuv_ref, g_ref, o_ref, heads, L)


def _mla_sample_call(qcat, kcat, lat_cache, kr_cache, layer, wuv, h2d, cols, tk):
    nb, heads, L, _ = qcat.shape
    mw = heads * MLA_V_DIM
    P = lat_cache.shape[2]
    nkc = P // tk
    last = nkc - 1
    kern = functools.partial(_mla_sample_kernel, nkc=nkc, heads=heads)
    return pl.pallas_call(
        kern,
        grid=(nb, nkc + 1),
        in_specs=[pl.BlockSpec((1, heads, L, MLA_QK_PAD), lambda b, j: (b, 0, 0, 0)),
                  pl.BlockSpec((1, 1, tk, MLA_KV_RANK), lambda b, j: (layer, b, jnp.minimum(j, last), 0)),
                  pl.BlockSpec((1, 1, MLA_ROPE_DIM, tk), lambda b, j: (layer, b, 0, jnp.minimum(j, last))),
                  pl.BlockSpec((1, L, MLA_QK_PAD), lambda b, j: (b, 0, 0)),
                  pl.BlockSpec(wuv.shape, lambda b, j: (0, 0)),
                  pl.BlockSpec((L, mw), lambda b, j: (b, cols.mgate // mw))],
        out_specs=pl.BlockSpec((1, L, mw), lambda b, j: (b, 0, 0)),
        out_shape=jax.ShapeDtypeStruct((nb, L, mw), BF16),
        scratch_shapes=[pltpu.VMEM((heads * L, 1), F32),
                        pltpu.VMEM((heads * L, 1), F32),
                        pltpu.VMEM((heads * L, MLA_KV_RANK), F32)],
        compiler_params=_cparams(("arbitrary", "arbitrary")),
        name="mla_attn_sample",
    )(qcat, lat_cache, kr_cache, kcat, wuv, h2d)


def _outproj_kernel(ys_ref, yd_ref, ym_ref, w_ref, x_ref, g_ref, r_ref, wb_ref, *, alpha):
    bt, lt, _ = ys_ref.shape

    @pl.when((pl.program_id(1) == 0) & (pl.program_id(2) == 0))
    def _():
        wb_ref[...] = w_ref[...].astype(BF16)

    flat = lambda ref: ref[...].reshape(bt * lt, ref.shape[2])
    mix = jnp.concatenate([flat(ys_ref), flat(yd_ref), flat(ym_ref)], axis=1)
    acc = jnp.dot(mix, wb_ref[...], preferred_element_type=F32)
    r_ref[...] = alpha * x_ref[...] + g_ref[...] * acc.reshape(bt, lt, acc.shape[1])


def _outproj_call(y_ssd, y_diff, y_mla, w_out, layer, x, gate, alpha):
    nb, L, d = x.shape
    bt, lt = _seq_tiles(nb, L, 512)
    tn = 512
    k = w_out.shape[1]
    yspec = lambda width: pl.BlockSpec((bt, lt, width), lambda j, b, i: (b, i, 0))
    return pl.pallas_call(
        functools.partial(_outproj_kernel, alpha=alpha),
        grid=(d // tn, nb // bt, L // lt),
        in_specs=[yspec(y_ssd.shape[2]), yspec(y_diff.shape[2]), yspec(y_mla.shape[2]),
                  pl.BlockSpec((None, k, tn), lambda j, b, i: (layer, 0, j)),
                  pl.BlockSpec((bt, lt, tn), lambda j, b, i: (b, i, j)),
                  pl.BlockSpec((bt, 1, tn), lambda j, b, i: (b, 0, j))],
        out_specs=pl.BlockSpec((bt, lt, tn), lambda j, b, i: (b, i, j)),
        out_shape=jax.ShapeDtypeStruct((nb, L, d), F32),
        scratch_shapes=[pltpu.VMEM((k, tn), BF16)],
        compiler_params=_cparams(("arbitrary", "arbitrary", "arbitrary")),
        name="out_proj",
    )(y_ssd, y_diff, y_mla, w_out, x, gate)


def _layernorm(r, g, b):
    mu = jnp.mean(r, axis=-1, keepdims=True)
    var = jnp.mean(jnp.square(r - mu), axis=-1, keepdims=True)
    return (r - mu) * lax.rsqrt(var + 1e-5) * g + b


def _ln_kernel(r_ref, g_ref, b_ref, x_ref):
    x_ref[...] = _layernorm(r_ref[...], g_ref[...], b_ref[...])


def _ln_mod_kernel(r_ref, g_ref, b_ref, sc_ref, sh_ref, x_ref, u_ref):
    x = _layernorm(r_ref[...], g_ref[...], b_ref[...])
    x_ref[...] = x
    u_ref[...] = (x * (1.0 + sc_ref[...]) + sh_ref[...]).astype(BF16)


def _ln_call(r, g, b, scale=None, shift=None):
    nb, L, d = r.shape
    bt, lt = _seq_tiles(nb, L, 256)
    xspec = pl.BlockSpec((bt, lt, d), lambda i, j: (i, j, 0))
    wspec = pl.BlockSpec((1, 1, d), lambda i, j: (0, 0, 0))
    sspec = pl.BlockSpec((bt, 1, d), lambda i, j: (i, 0, 0))
    g3, b3 = g.reshape(1, 1, d), b.reshape(1, 1, d)
    if scale is None:
        return pl.pallas_call(
            _ln_kernel, grid=(nb // bt, L // lt),
            in_specs=[xspec, wspec, wspec], out_specs=xspec,
            out_shape=jax.ShapeDtypeStruct((nb, L, d), F32),
            compiler_params=_cparams(("arbitrary", "arbitrary")), name="layernorm",
        )(r, g3, b3), None
    return pl.pallas_call(
        _ln_mod_kernel, grid=(nb // bt, L // lt),
        in_specs=[xspec, wspec, wspec, sspec, sspec], out_specs=[xspec, xspec],
        out_shape=[jax.ShapeDtypeStruct((nb, L, d), F32), jax.ShapeDtypeStruct((nb, L, d), BF16)],
        compiler_params=_cparams(("arbitrary", "arbitrary")), name="layernorm_modulate",
    )(r, g3, b3, scale, shift)


def _ssd_chunk(L):
    for t in (128, 64, 32, 16, 8):
        if L % t == 0:
            return t
    raise ValueError(f"sequence length {L} is not a multiple of 8")


def _attn_tile(L, want):
    t = want
    while L % t:
        t //= 2
    return t


def _layer(x, u, mod_l, next_mod, wl, cols, layer_idx, depth, caches, state_bufs):
    nb, L, d = x.shape
    heads = cols.heads
    mla_heads = cols.mla_w // MLA_V_DIM
    _, _, gate = mod_l
    k_cache, v_cache, lat_cache, kr_cache, st0, conv0 = caches
    P = 0 if k_cache is None else lat_cache.shape[2]
    pos = P + jnp.arange(L, dtype=jnp.int32)

    h2d = _inproj_call(u.reshape(nb * L, d), wl["w_in"], layer_idx, cols.tn)

    T = _ssd_chunk(L)
    y_ssd, ssm_new, conv_new = _ssd_call(h2d, conv0, st0, wl["conv_w"], wl["conv_b"], wl["dt_bias"], wl["a_log"],
                                         wl["d_skip"], wl["ssd_norm_w"], wl["e_mat"], wl["et_mat"], cols, nb, L, T)

    tabs_d = _rope_tables(pos, DIFF_ROT, DIFF_QK_DIM, LANES)
    tabs_m = _rope_tables(pos, MLA_ROPE_DIM, MLA_ROPE_DIM, MLA_ROPE_DIM)
    lam_init = 0.8 - 0.6 * math.exp(-0.3 * layer_idx)
    if k_cache is None:
        t = _attn_tile(L, 512)
        k32, v32, lat32, kr32, qd, kb, vb, latb, kt = _prep_call(h2d, tabs_d, tabs_m, wl["kv_norm_w"], cols, nb, L, t,
                                                                 layer_idx, state_bufs, wl["w_uk_flat"])
        qh = _mlaq_call(h2d, wl["q_norm_w"], wl["w_uq"], wl["w_uk"], tabs_m, cols, nb, L, mla_heads, False)
        y_diff = _diff_prompt_call(qd, kb, vb, wl["lam"], wl["diff_norm_w"], h2d, cols, lam_init, t)
        y_mla = _mla_prompt_call(qh, kt, latb, wl["w_uv"], h2d, cols, t, MLA_HEAD_GROUP)
    else:
        k32, v32, lat32, kr32, qd, kb, vb, kcat = _prep_call(h2d, tabs_d, tabs_m, wl["kv_norm_w"], cols, nb, L,
                                                             min(L, 256), layer_idx, state_bufs)
        qcat = _mlaq_call(h2d, wl["q_norm_w"], wl["w_uq"], wl["w_uk"], tabs_m, cols, nb, L, mla_heads, True)
        y_diff = _diff_sample_call(qd, kb, vb, k_cache, v_cache, layer_idx, wl["lam"], wl["diff_norm_w"], h2d,
                                   cols, lam_init, _attn_tile(P, 1024))
        y_mla = _mla_sample_call(qcat, kcat, lat_cache, kr_cache, layer_idx, wl["w_uv"], h2d, cols,
                                 _attn_tile(P, 2048))

    alpha = (2 * depth) ** 0.25
    r = _outproj_call(y_ssd.reshape(nb, L, cols.ssd_w), y_diff, y_mla, wl["w_out"], layer_idx, x, gate, alpha)
    if next_mod is None:
        x_new, u_new = _ln_call(r, wl["ln_g"], wl["ln_b"])
    else:
        x_new, u_new = _ln_call(r, wl["ln_g"], wl["ln_b"], next_mod[1], next_mod[0])
    return x_new, u_new, (k32, v32, lat32, kr32), (ssm_new, conv_new)


def _layer_weights(l, cols, w_in, conv_w, conv_b, dt_bias, a_log, d_skip, ssd_norm_w, lambda_q1, lambda_k1,
                   lambda_q2, lambda_k2, diff_norm_w, mla_q_norm_w, mla_kv_norm_w, w_uq, w_uk, w_uv, w_out,
                   ln_g, ln_b):
    heads = cols.heads
    mla_heads = cols.mla_w // MLA_V_DIM
    lane_pad = lambda v: jnp.pad(v, (MLA_ROPE_DIM, LANES - MLA_ROPE_DIM - heads)).reshape(1, LANES)
    qk = MLA_NOPE_DIM + MLA_ROPE_DIM
    wq = w_uq[l].reshape(MLA_Q_RANK, mla_heads, qk)
    wq_nope = wq[:, :, :MLA_NOPE_DIM].reshape(MLA_Q_RANK, mla_heads * MLA_NOPE_DIM)
    wq_rope = jnp.pad(wq[:, :, MLA_NOPE_DIM:], ((0, 0), (0, 0), (0, LANES - MLA_ROPE_DIM)))
    wq_p = jnp.concatenate([wq_nope, wq_rope.reshape(MLA_Q_RANK, mla_heads * LANES)], axis=1).astype(BF16)
    eh = np.zeros((LANES, cols.ssd_w), np.float32)
    for h in range(heads):
        eh[MLA_ROPE_DIM + h, h * SSD_HEAD_DIM:(h + 1) * SSD_HEAD_DIM] = 1.0
    return dict(
        w_in=w_in,
        conv_w=conv_w[l], conv_b=conv_b[l].reshape(1, -1),
        dt_bias=lane_pad(dt_bias[l]), a_log=lane_pad(a_log[l]),
        d_skip=jnp.repeat(d_skip[l], SSD_HEAD_DIM).reshape(1, -1),
        ssd_norm_w=ssd_norm_w[l].reshape(1, -1),
        e_mat=jnp.asarray(eh, BF16), et_mat=jnp.asarray(eh.T, BF16),
        lam=jnp.stack([lambda_q1[l], lambda_k1[l], lambda_q2[l], lambda_k2[l]]),
        diff_norm_w=diff_norm_w[l].reshape(1, -1),
        q_norm_w=mla_q_norm_w[l].reshape(1, -1), kv_norm_w=mla_kv_norm_w[l].reshape(1, -1),
        w_uq=wq_p,
        w_uk=jnp.transpose(w_uk[l], (1, 2, 0)).astype(BF16),
        w_uk_flat=w_uk[l].reshape(MLA_KV_RANK, mla_heads * MLA_NOPE_DIM).astype(BF16),
        w_uv=w_uv[l].reshape(MLA_KV_RANK, mla_heads * MLA_V_DIM).astype(BF16),
        w_out=w_out,
        ln_g=ln_g[l], ln_b=ln_b[l],
    )


def kernel(x_prompt, x_sample, cache_diff_k, cache_diff_v, cache_mla_latent, cache_mla_krope, state_ssm, state_conv,
           c_prompt, c_sample, w_mod, b_mod, w_in, conv_w, conv_b, dt_bias, a_log, d_skip, ssd_norm_w, lambda_q1,
           lambda_k1, lambda_q2, lambda_k2, diff_norm_w, mla_q_norm_w, mla_kv_norm_w, w_uq, w_uk, w_uv, w_out,
           ln_g, ln_b):
    depth = w_in.shape[0]
    bp, _, d = x_prompt.shape
    bs = x_sample.shape[0]
    cols = _Cols(d)
    heads = cols.heads

    rows = -(-(bp + bs) // 8) * 8
    c_all = jnp.concatenate([c_prompt, c_sample, jnp.zeros((rows - bp - bs, d), F32)], axis=0)
    mod = _mod_call(c_all, w_mod, b_mod)

    def mods(l, lo, n):
        part = lambda k: mod[l, lo:lo + n, k * d:(k + 1) * d].reshape(n, 1, d)
        return part(0), part(1), part(2)

    pdiff = cache_diff_k.shape[2]
    kc = jnp.transpose(cache_diff_k, (0, 1, 3, 4, 5, 2))
    vc = cache_diff_v.reshape(depth, bs, pdiff * cache_diff_v.shape[3], DIFF_V_DIM)
    krc = jnp.transpose(cache_mla_krope, (0, 1, 3, 2))
    zero_state = jnp.zeros((bp, heads, SSD_HEAD_DIM, SSD_STATE), F32)
    zero_conv = jnp.zeros((bp, SSD_CONV - 1, cols.conv_dim), F32)

    w_in_p = _permute_w_in(w_in, cols)
    hp, hs = x_prompt, x_sample
    mp, ms = mods(0, 0, bp), mods(0, bp, bs)
    up = _modulate_call(hp, mp[1], mp[0])
    us = _modulate_call(hs, ms[1], ms[0])

    def state_buffers(nb, L):
        return tuple(jnp.zeros((depth, nb, L, w), F32) for w in (cols.diff_w, cols.diff_w, MLA_KV_RANK, MLA_ROPE_DIM))

    bufs_p, bufs_s = state_buffers(bp, hp.shape[1]), state_buffers(bs, hs.shape[1])
    rec_p, rec_s = [], []
    for l in range(depth):
        wl = _layer_weights(l, cols, w_in_p, conv_w, conv_b, dt_bias, a_log, d_skip, ssd_norm_w, lambda_q1, lambda_k1,
                            lambda_q2, lambda_k2, diff_norm_w, mla_q_norm_w, mla_kv_norm_w, w_uq, w_uk, w_uv, w_out,
                            ln_g, ln_b)
        nmp = mods(l + 1, 0, bp) if l + 1 < depth else None
        nms = mods(l + 1, bp, bs) if l + 1 < depth else None
        hp, up, bufs_p, rp = _layer(hp, up, mp, nmp, wl, cols, l, depth,
                                    (None, None, None, None, zero_state, zero_conv), bufs_p)
        hs, us, bufs_s, rs = _layer(hs, us, ms, nms, wl, cols, l, depth,
                                    (kc, vc, cache_mla_latent, krc, state_ssm[l], state_conv[l]), bufs_s)
        rec_p.append(rp)
        rec_s.append(rs)
        mp, ms = nmp, nms

    def states(bufs, rec):
        k32, v32, lat32, kr32 = bufs
        nb, L = k32.shape[1], k32.shape[2]
        return (k32.reshape(depth, nb, L, cols.diff_w // (2 * DIFF_QK_DIM), 2, DIFF_QK_DIM),
                v32.reshape(depth, nb, L, cols.diff_w // DIFF_V_DIM, DIFF_V_DIM), lat32, kr32,
                jnp.stack([r[0] for r in rec]), jnp.stack([r[1] for r in rec]))

    return (hp, hs) + states(bufs_p, rec_p) + states(bufs_s, rec_s)
```

```python
import functools
import math

import numpy as np
import jax
import jax.numpy as jnp
from jax import lax
from jax.experimental import pallas as pl
from jax.experimental.pallas import tpu as pltpu

F32 = jnp.float32
BF16 = jnp.bfloat16

CHUNK = 64
ROPE_THETA = 500000.0
NEG_INF = -1e30
SSD_HEAD_DIM = 64
SSD_GROUPS = 4
SSD_STATE = 128
SSD_CONV = 4
DIFF_QK_DIM = 64
DIFF_V_DIM = 128
DIFF_ROT = DIFF_QK_DIM // 4
DIFF_SCALE = DIFF_QK_DIM ** -0.5
MLA_V_DIM = 128
MLA_NOPE_DIM = 128
MLA_ROPE_DIM = 64
MLA_Q_RANK = 768
MLA_KV_RANK = 256
MLA_SCALE = (MLA_NOPE_DIM + MLA_ROPE_DIM) ** -0.5
MLA_QK_PAD = MLA_KV_RANK + 128
MLA_QH_DIM = MLA_NOPE_DIM + 128
MLA_HEAD_GROUP = 2
LOG2E = math.log2(math.e)

LANES = 128
VMEM_LIMIT = 56 * 1024 * 1024


def _cparams(sem):
    return pltpu.CompilerParams(dimension_semantics=sem, vmem_limit_bytes=VMEM_LIMIT)


def _silu(x):
    hx = 0.5 * x
    return hx + hx * jnp.tanh(hx)


def _nt_dot(a, b):
    return lax.dot_general(a, b, (((1,), (1,)), ((), ())), preferred_element_type=F32)


def _split2(v):
    hi = v.astype(BF16)
    lo = (v - hi.astype(F32)).astype(BF16)
    return hi, lo


def _split3(v):
    hi = v.astype(BF16)
    r = v - hi.astype(F32)
    mid = r.astype(BF16)
    lo = (r - mid.astype(F32)).astype(BF16)
    return hi, mid, lo


def _mod_kernel(c_ref, w_ref, b_ref, o_ref):
    a = _silu(c_ref[...]).astype(BF16)
    o_ref[0] = jnp.dot(a, w_ref[0].astype(BF16), preferred_element_type=F32) + b_ref[0]


def _mod_call(c_all, w_mod, b_mod):
    depth, d, n3 = w_mod.shape
    rows = c_all.shape[0]
    tn = 512
    return pl.pallas_call(
        _mod_kernel,
        grid=(depth, n3 // tn),
        in_specs=[pl.BlockSpec((rows, d), lambda l, j: (0, 0)),
                  pl.BlockSpec((1, d, tn), lambda l, j: (l, 0, j)),
                  pl.BlockSpec((1, 1, tn), lambda l, j: (l, 0, j))],
        out_specs=pl.BlockSpec((1, rows, tn), lambda l, j: (l, 0, j)),
        out_shape=jax.ShapeDtypeStruct((depth, rows, n3), F32),
        compiler_params=_cparams(("arbitrary", "arbitrary")),
        name="adaln_mod",
    )(c_all, w_mod, b_mod.reshape(depth, 1, n3))


def _modulate_kernel(x_ref, sc_ref, sh_ref, u_ref):
    u_ref[...] = (x_ref[...] * (1.0 + sc_ref[...]) + sh_ref[...]).astype(BF16)


def _seq_tiles(nb, L, rows):
    if L >= rows:
        return 1, rows
    return min(nb, rows // L), L


def _modulate_call(x, scale, shift):
    nb, L, d = x.shape
    bt, lt = _seq_tiles(nb, L, 256)
    return pl.pallas_call(
        _modulate_kernel,
        grid=(nb // bt, L // lt),
        in_specs=[pl.BlockSpec((bt, lt, d), lambda i, j: (i, j, 0)),
                  pl.BlockSpec((bt, 1, d), lambda i, j: (i, 0, 0)),
                  pl.BlockSpec((bt, 1, d), lambda i, j: (i, 0, 0))],
        out_specs=pl.BlockSpec((bt, lt, d), lambda i, j: (i, j, 0)),
        out_shape=jax.ShapeDtypeStruct((nb, L, d), BF16),
        compiler_params=_cparams(("arbitrary", "arbitrary")),
        name="modulate",
    )(x, scale, shift)


class _Cols:
    def __init__(self, d_model):
        self.ssd_w = d_model // 2
        self.diff_w = d_model // 4
        self.mla_w = d_model // 4
        self.heads = self.ssd_w // SSD_HEAD_DIM
        self.conv_dim = self.ssd_w + 2 * SSD_GROUPS * SSD_STATE
        self.in_sizes = (self.ssd_w, self.conv_dim, self.heads, self.diff_w, self.diff_w, self.diff_w,
                         self.diff_w, MLA_Q_RANK, MLA_KV_RANK, MLA_ROPE_DIM, self.mla_w)
        o = 0
        self.xbc = o; o += self.conv_dim
        self.dq = o; o += self.diff_w
        self.z = o; o += self.ssd_w
        self.dk = o; o += self.diff_w
        self.dv = o; o += self.diff_w
        self.dgate = o; o += self.diff_w
        self.mgate = o; o += self.mla_w
        self.cq = o; o += MLA_Q_RANK + MLA_KV_RANK
        self.krdt = o; o += LANES
        self.used = o
        self.tn = 1280
        self.total = -(-o // self.tn) * self.tn


def _w_in_blocks(cols):
    offs = np.concatenate([[0], np.cumsum(cols.in_sizes)])
    order = (1, 3, 0, 4, 5, 6, 10)
    rows = []
    for i in order:
        rows += list(range(int(offs[i]), int(offs[i + 1]), LANES))
    rows += list(range(int(offs[7]), int(offs[9]), LANES))
    rows.append(-1)
    rows += [-2] * ((cols.total - cols.used) // LANES)
    return np.asarray(rows, np.int32)


def _wprep_kernel(tab_ref, w_ref, sp_ref, o_ref):
    src = tab_ref[pl.program_id(1)]

    @pl.when(src >= 0)
    def _():
        o_ref[0] = w_ref[0].astype(BF16)

    @pl.when(src == -1)
    def _():
        o_ref[0] = sp_ref[0].astype(BF16)

    @pl.when(src == -2)
    def _():
        o_ref[0] = jnp.zeros(o_ref.shape[1:], BF16)


def _permute_w_in(w_in, cols):
    depth, d, _ = w_in.shape
    wt = jnp.transpose(w_in, (0, 2, 1))
    offs = np.concatenate([[0], np.cumsum(cols.in_sizes)])
    pad = LANES - MLA_ROPE_DIM - cols.heads
    special = jnp.concatenate([wt[:, offs[9]:offs[10]], wt[:, offs[2]:offs[3]], jnp.zeros((depth, pad, d), F32)], axis=1)
    tab = jnp.asarray(_w_in_blocks(cols))
    grid_spec = pltpu.PrefetchScalarGridSpec(
        num_scalar_prefetch=1,
        grid=(depth, int(tab.shape[0])),
        in_specs=[pl.BlockSpec((pl.Element(1), pl.Element(LANES), pl.Element(d)),
                               lambda l, i, tab: (l, pl.multiple_of(jnp.maximum(tab[i], 0), 8), 0)),
                  pl.BlockSpec((1, LANES, d), lambda l, i, tab: (l, 0, 0))],
        out_specs=pl.BlockSpec((1, LANES, d), lambda l, i, tab: (l, i, 0)))
    return pl.pallas_call(
        _wprep_kernel, grid_spec=grid_spec,
        out_shape=jax.ShapeDtypeStruct((depth, cols.total, d), BF16),
        compiler_params=_cparams(("arbitrary", "arbitrary")),
        name="w_in_prep",
    )(tab, wt, special)


def _matmul_nt_kernel(x_ref, w_ref, o_ref):
    o_ref[...] = _nt_dot(x_ref[...], w_ref[...])


def _inproj_call(u2d, wp, layer, tn):
    m, k = u2d.shape
    n = wp.shape[1]
    tm = min(m, 512)
    return pl.pallas_call(
        _matmul_nt_kernel,
        grid=(n // tn, m // tm),
        in_specs=[pl.BlockSpec((tm, k), lambda j, i: (i, 0)),
                  pl.BlockSpec((None, tn, k), lambda j, i: (layer, j, 0))],
        out_specs=pl.BlockSpec((tm, tn), lambda j, i: (i, j)),
        out_shape=jax.ShapeDtypeStruct((m, n), F32),
        compiler_params=_cparams(("arbitrary", "arbitrary")),
        name="in_proj",
    )(u2d, wp)


def _ssd_kernel(xbc_ref, z_ref, dtb_ref, conv0_ref, st0_ref, cw_ref, cb_ref, dtbias_ref, alog_ref,
                dskip_ref, nw_ref, e_ref, et_ref, y_ref, st_ref, convo_ref, ext_ref, *, T, nchunks, heads):
    c = pl.program_id(1)
    ssd_w = heads * SSD_HEAD_DIM
    gw = ssd_w // SSD_GROUPS
    hpg = heads // SSD_GROUPS
    conv_dim = ext_ref.shape[1]
    dt_lo = MLA_ROPE_DIM

    @pl.when(c == 0)
    def _():
        ext_ref[0:8, :] = jnp.zeros((8, conv_dim), F32)
        ext_ref[8 - (SSD_CONV - 1):8, :] = conv0_ref[0]
        st_ref[0] = st0_ref[0]

    ext_ref[8:8 + T, :] = xbc_ref[...]
    acc = cb_ref[...]
    for j in range(SSD_CONV):
        lo = 8 - (SSD_CONV - 1) + j
        acc = acc + ext_ref[lo:lo + T, :] * cw_ref[j:j + 1, :]
    ext_ref[0:8, :] = ext_ref[T:T + 8, :]

    @pl.when(c == nchunks - 1)
    def _():
        convo_ref[0] = ext_ref[8 - (SSD_CONV - 1):8, :]

    xact = _silu(acc)
    xs = xact[:, :ssd_w]
    bm = xact[:, ssd_w:ssd_w + SSD_GROUPS * SSD_STATE]
    cm = xact[:, ssd_w + SSD_GROUPS * SSD_STATE:]

    lane = lax.broadcasted_iota(jnp.int32, (1, LANES), 1)
    is_dt = (lane >= dt_lo) & (lane < dt_lo + heads)
    xdt = dtb_ref[...] + dtbias_ref[...]
    dt = jnp.where(is_dt, jnp.maximum(xdt, 0.0) + jnp.log1p(jnp.exp(-jnp.abs(xdt))), 0.0)
    a_neg = jnp.where(is_dt, -jnp.exp(alog_ref[...]), 0.0)
    da = dt * a_neg

    row = lax.broadcasted_iota(jnp.int32, (T, T), 0)
    col = lax.broadcasted_iota(jnp.int32, (T, T), 1)
    causal = row >= col
    tril = causal.astype(BF16)
    eye = (lax.broadcasted_iota(jnp.int32, (LANES, LANES), 0)
           == lax.broadcasted_iota(jnp.int32, (LANES, LANES), 1)).astype(BF16)

    da3 = _split3(da)
    a_cs = sum(jnp.dot(tril, p, preferred_element_type=F32) for p in da3)
    a3 = _split3(a_cs)
    a_cs_t = sum(_nt_dot(eye, p) for p in a3)
    a_last = a_cs[T - 1:T, :]
    ea = jnp.exp(a_cs)
    te = jnp.exp(a_last - a_cs)

    e_mat = e_ref[...]

    def expand(v):
        hi, lo = _split2(v)
        return jnp.dot(hi, e_mat, preferred_element_type=F32) + jnp.dot(lo, e_mat, preferred_element_type=F32)

    dt_x = expand(dt)
    ea_x = expand(ea)
    te_x = expand(te)
    cd_col = jnp.exp(a_cs_t[:, T - 1:T])
    cd_b = jnp.broadcast_to(cd_col, (LANES, SSD_STATE))
    cdh, cdl = _split2(cd_b)
    et_mat = et_ref[...]
    cd_full = (jnp.dot(et_mat, cdh, preferred_element_type=F32)
               + jnp.dot(et_mat, cdl, preferred_element_type=F32))

    xd = xs * dt_x
    xde = (xd * te_x).astype(BF16)
    lane_p = lax.broadcasted_iota(jnp.int32, (T, LANES), 1)
    lower_half = lane_p < SSD_HEAD_DIM

    for g in range(SSD_GROUPS):
        cg = cm[:, g * SSD_STATE:(g + 1) * SSD_STATE].astype(BF16)
        bg = bm[:, g * SSD_STATE:(g + 1) * SSD_STATE].astype(BF16)
        cbm = _nt_dot(cg, bg)
        st_g = st_ref[0, g * hpg:(g + 1) * hpg].reshape(gw, SSD_STATE)
        y_off = _nt_dot(cg, st_g.astype(BF16)) * ea_x[:, g * gw:(g + 1) * gw]
        pieces = []
        for q in range(hpg // 2):
            c0 = g * gw + q * LANES
            xd_pair = xd[:, c0:c0 + LANES].astype(BF16)
            ys = []
            for h in (g * hpg + 2 * q, g * hpg + 2 * q + 1):
                seg = a_cs[:, dt_lo + h:dt_lo + h + 1] - a_cs_t[dt_lo + h:dt_lo + h + 1, :]
                decay = jnp.exp(jnp.where(causal, seg, -jnp.inf))
                ys.append(jnp.dot((cbm * decay).astype(BF16), xd_pair, preferred_element_type=F32))
            pieces.append(jnp.where(lower_half, ys[0], ys[1]))
        y_g = jnp.concatenate(pieces, axis=1) + y_off
        upd = lax.dot_general(xde[:, g * gw:(g + 1) * gw], bg, (((0,), (0,)), ((), ())),
                              preferred_element_type=F32)
        st_new = st_g * cd_full[g * gw:(g + 1) * gw, :] + upd
        st_ref[0, g * hpg:(g + 1) * hpg] = st_new.reshape(hpg, SSD_HEAD_DIM, SSD_STATE)
        y_g = y_g + dskip_ref[:, g * gw:(g + 1) * gw] * xs[:, g * gw:(g + 1) * gw]
        yg = y_g * _silu(z_ref[:, g * gw:(g + 1) * gw])
        ms = jnp.mean(yg * yg, axis=-1, keepdims=True)
        y_ref[:, g * gw:(g + 1) * gw] = (yg * lax.rsqrt(ms + 1e-6) * nw_ref[:, g * gw:(g + 1) * gw]).astype(BF16)


def _ssd_call(h2d, conv0, st0, conv_w, conv_b, dtbias_p, alog_p, dskip_x, norm_w, e_mat, et_mat, cols, nb, L, T):
    nchunks = L // T
    heads = cols.heads
    ssd_w = cols.ssd_w
    cd = cols.conv_dim
    kern = functools.partial(_ssd_kernel, T=T, nchunks=nchunks, heads=heads)
    row = lambda b, c: b * nchunks + c
    const2 = lambda b, c: (0, 0)
    return pl.pallas_call(
        kern,
        grid=(nb, nchunks),
        in_specs=[pl.BlockSpec((T, cd), lambda b, c: (row(b, c), cols.xbc // cd)),
                  pl.BlockSpec((T, ssd_w), lambda b, c: (row(b, c), cols.z // ssd_w)),
                  pl.BlockSpec((T, LANES), lambda b, c: (row(b, c), cols.krdt // LANES)),
                  pl.BlockSpec((1, SSD_CONV - 1, cd), lambda b, c: (b, 0, 0)),
                  pl.BlockSpec((1, heads, SSD_HEAD_DIM, SSD_STATE), lambda b, c: (b, 0, 0, 0)),
                  pl.BlockSpec((SSD_CONV, cd), const2),
                  pl.BlockSpec((1, cd), const2),
                  pl.BlockSpec((1, LANES), const2),
                  pl.BlockSpec((1, LANES), const2),
                  pl.BlockSpec((1, ssd_w), const2),
                  pl.BlockSpec((1, ssd_w), const2),
                  pl.BlockSpec((LANES, ssd_w), const2),
                  pl.BlockSpec((ssd_w, LANES), const2)],
        out_specs=[pl.BlockSpec((T, ssd_w), lambda b, c: (row(b, c), 0)),
                   pl.BlockSpec((1, heads, SSD_HEAD_DIM, SSD_STATE), lambda b, c: (b, 0, 0, 0)),
                   pl.BlockSpec((1, SSD_CONV - 1, cd), lambda b, c: (b, 0, 0))],
        out_shape=[jax.ShapeDtypeStruct((nb * L, ssd_w), BF16),
                   jax.ShapeDtypeStruct((nb, heads, SSD_HEAD_DIM, SSD_STATE), F32),
                   jax.ShapeDtypeStruct((nb, SSD_CONV - 1, cd), F32)],
        scratch_shapes=[pltpu.VMEM((T + 8, cd), F32)],
        compiler_params=_cparams(("arbitrary", "arbitrary")),
        name="ssd_scan",
    )(h2d, h2d, h2d, conv0, st0, conv_w, conv_b, dtbias_p, alog_p, dskip_x, norm_w, e_mat, et_mat)


def _rope_tables(pos, rot_dim, period, width):
    half = rot_dim // 2
    inv = ROPE_THETA ** (-jnp.arange(half, dtype=F32) * (2.0 / rot_dim))
    ang = pos.astype(F32)[:, None] * inv[None, :]
    cos, sin = jnp.cos(ang), jnp.sin(ang)
    lane = np.arange(LANES)
    inner = lane % period
    idx = jnp.asarray(inner % half)
    first = jnp.asarray((inner < half) & (lane < width))
    second = jnp.asarray((inner >= half) & (inner < rot_dim) & (lane < width))
    keep = jnp.asarray((inner >= rot_dim) & (lane < width))
    cos_l, sin_l = cos[:, idx], sin[:, idx]
    cos_t = jnp.where(first | second, cos_l, jnp.where(keep, 1.0, 0.0))
    sin_a = jnp.where(first, -sin_l, 0.0)
    sin_b = jnp.where(second, sin_l, 0.0)
    return cos_t.astype(F32), sin_a.astype(F32), sin_b.astype(F32)


def _rope_tile(x, cos_t, sin_a, sin_b, half):
    return (x * cos_t + pltpu.roll(x, LANES - half, 1) * sin_a + pltpu.roll(x, half, 1) * sin_b)


def _rmsnorm(x, w):
    return x * lax.rsqrt(jnp.mean(x * x, axis=-1, keepdims=True) + 1e-6) * w


def _prep_kernel(dq_ref, dk_ref, dv_ref, cq_ref, kr_ref, cd_ref, sad_ref, sbd_ref, cm_ref, sam_ref, sbm_ref,
                 kvw_ref, *rest, per_head, n_alias):
    if per_head:
        wuk_ref, rest = rest[0], rest[1:]
    rest = rest[n_alias:]
    if per_head:
        k32_ref, v32_ref, lat_ref, kro_ref, qd_ref, kb_ref, vb_ref, latb_ref, kt_ref = rest
    else:
        k32_ref, v32_ref, lat_ref, kro_ref, qd_ref, kb_ref, vb_ref, kcat_ref = rest
    cos_d, sa_d, sb_d = cd_ref[...], sad_ref[...], sbd_ref[...]
    width = dq_ref.shape[1]
    for c in range(width // LANES):
        sl = slice(c * LANES, (c + 1) * LANES)
        q = _rope_tile(dq_ref[:, sl], cos_d, sa_d, sb_d, DIFF_ROT // 2)
        qd_ref[0, :, sl] = (q * (DIFF_SCALE * LOG2E)).astype(BF16)
        k = _rope_tile(dk_ref[:, sl], cos_d, sa_d, sb_d, DIFF_ROT // 2)
        k32_ref[0, :, sl] = k
        kb_ref[0, :, sl] = k.astype(BF16)
    v = dv_ref[...]
    v32_ref[0] = v
    vb_ref[0] = v.astype(BF16)
    lat = _rmsnorm(cq_ref[:, MLA_Q_RANK:MLA_Q_RANK + MLA_KV_RANK], kvw_ref[...])
    lat_ref[0] = lat
    kr = _rope_tile(kr_ref[...], cm_ref[...], sam_ref[...], sbm_ref[...], MLA_ROPE_DIM // 2)
    kro_ref[0] = kr[:, :MLA_ROPE_DIM]
    lat_b = lat.astype(BF16)
    if per_head:
        latb_ref[0] = lat_b
        k_nope = jnp.dot(lat_b, wuk_ref[...], preferred_element_type=F32)
        for h in range(kt_ref.shape[1]):
            k_h = jnp.concatenate([k_nope[:, h * MLA_NOPE_DIM:(h + 1) * MLA_NOPE_DIM], kr], axis=1)
            kt_ref[0, h, 0] = k_h.T.astype(BF16)
    else:
        kcat_ref[0, :, :MLA_KV_RANK] = lat_b
        kcat_ref[0, :, MLA_KV_RANK:] = kr.astype(BF16)


def _prep_call(h2d, tabs_d, tabs_m, kv_w, cols, nb, L, tm, layer, state_bufs, wuk_flat=None):
    nt = L // tm
    dw = cols.diff_w
    per_head = wuk_flat is not None
    row = lambda b, i: b * nt + i
    hspec = lambda width, off: pl.BlockSpec((tm, width), lambda b, i: (row(b, i), off // width))
    tspec = pl.BlockSpec((tm, LANES), lambda b, i: (i, 0))
    ospec = lambda width: pl.BlockSpec((1, tm, width), lambda b, i: (b, i, 0))
    sspec = lambda width: pl.BlockSpec((None, 1, tm, width), lambda b, i: (layer, b, i, 0))
    in_specs = [hspec(dw, cols.dq), hspec(dw, cols.dk), hspec(dw, cols.dv),
                hspec(MLA_Q_RANK + MLA_KV_RANK, cols.cq), hspec(LANES, cols.krdt),
                tspec, tspec, tspec, tspec, tspec, tspec,
                pl.BlockSpec((1, MLA_KV_RANK), lambda b, i: (0, 0))]
    args = [h2d, h2d, h2d, h2d, h2d, *tabs_d, *tabs_m, kv_w]
    if per_head:
        in_specs.append(pl.BlockSpec(wuk_flat.shape, lambda b, i: (0, 0)))
        args.append(wuk_flat)
    aliases = {len(args) + n: n for n in range(len(state_bufs))}
    in_specs += [pl.BlockSpec(memory_space=pl.ANY)] * len(state_bufs)
    args += list(state_bufs)
    out_specs = [sspec(dw), sspec(dw), sspec(MLA_KV_RANK), sspec(MLA_ROPE_DIM), ospec(dw), ospec(dw), ospec(dw)]
    out_shape = [jax.ShapeDtypeStruct(b.shape, b.dtype) for b in state_bufs]
    out_shape += [jax.ShapeDtypeStruct((nb, L, dw), BF16)] * 3
    if per_head:
        heads = wuk_flat.shape[1] // MLA_NOPE_DIM
        out_specs += [ospec(MLA_KV_RANK),
                      pl.BlockSpec((1, heads, 1, MLA_QH_DIM, tm), lambda b, i: (b, 0, i, 0, 0))]
        out_shape += [jax.ShapeDtypeStruct((nb, L, MLA_KV_RANK), BF16),
                      jax.ShapeDtypeStruct((nb, heads, nt, MLA_QH_DIM, tm), BF16)]
    else:
        out_specs.append(ospec(MLA_QK_PAD))
        out_shape.append(jax.ShapeDtypeStruct((nb, L, MLA_QK_PAD), BF16))
    return pl.pallas_call(
        functools.partial(_prep_kernel, per_head=per_head, n_alias=len(state_bufs)),
        grid=(nb, nt),
        in_specs=in_specs, out_specs=out_specs, out_shape=out_shape,
        input_output_aliases=aliases,
        compiler_params=_cparams(("arbitrary", "arbitrary")),
        name="attn_prep",
    )(*args)


def _mlaq_kernel(cq_ref, qw_ref, wuq_ref, wuk_ref, cm_ref, sam_ref, sbm_ref, o_ref, *, heads, absorb):
    cqn = _rmsnorm(cq_ref[:, :MLA_Q_RANK], qw_ref[...]).astype(BF16)
    qm = jnp.dot(cqn, wuq_ref[...], preferred_element_type=F32)
    cos_m, sa_m, sb_m = cm_ref[...], sam_ref[...], sbm_ref[...]
    nope_w = heads * MLA_NOPE_DIM
    scale = MLA_SCALE * LOG2E
    for h in range(heads):
        nope = qm[:, h * MLA_NOPE_DIM:(h + 1) * MLA_NOPE_DIM]
        qr = _rope_tile(qm[:, nope_w + h * LANES:nope_w + (h + 1) * LANES], cos_m, sa_m, sb_m, MLA_ROPE_DIM // 2)
        if absorb:
            ql = jnp.dot(nope.astype(BF16), wuk_ref[h], preferred_element_type=F32)
        else:
            ql = nope
        width = ql.shape[1]
        o_ref[0, h, :, :width] = (ql * scale).astype(BF16)
        o_ref[0, h, :, width:] = (qr * scale).astype(BF16)


def _mlaq_call(h2d, q_w, wuq_p, wuk_t, tabs_m, cols, nb, L, heads, absorb):
    tm = min(L, 256)
    nt = L // tm
    width = MLA_Q_RANK + MLA_KV_RANK
    qdim = MLA_QK_PAD if absorb else MLA_QH_DIM
    tspec = pl.BlockSpec((tm, LANES), lambda b, i: (i, 0))
    return pl.pallas_call(
        functools.partial(_mlaq_kernel, heads=heads, absorb=absorb),
        grid=(nb, nt),
        in_specs=[pl.BlockSpec((tm, width), lambda b, i: (b * nt + i, cols.cq // width)),
                  pl.BlockSpec((1, MLA_Q_RANK), lambda b, i: (0, 0)),
                  pl.BlockSpec(wuq_p.shape, lambda b, i: (0, 0)),
                  pl.BlockSpec(wuk_t.shape, lambda b, i: (0, 0, 0)),
                  tspec, tspec, tspec],
        out_specs=pl.BlockSpec((1, heads, tm, qdim), lambda b, i: (b, 0, i, 0)),
        out_shape=jax.ShapeDtypeStruct((nb, heads, L, qdim), BF16),
        compiler_params=_cparams(("arbitrary", "arbitrary")),
        name="mla_q",
    )(h2d, q_w, wuq_p, wuk_t, *tabs_m)


def _softmax_init(m_ref, l_ref, acc_ref):
    m_ref[...] = jnp.full(m_ref.shape, -jnp.inf, F32)
    l_ref[...] = jnp.zeros(l_ref.shape, F32)
    acc_ref[...] = jnp.zeros(acc_ref.shape, F32)


def _softmax_update(s, v, m_ref, l_ref, acc_ref):
    m_prev = m_ref[...]
    m_new = jnp.maximum(m_prev, jnp.max(s, axis=-1, keepdims=True))
    alpha = jnp.exp2(m_prev - m_new)
    p = jnp.exp2(s - m_new)
    l_ref[...] = alpha * l_ref[...] + jnp.sum(p, axis=-1, keepdims=True)
    acc_ref[...] = alpha * acc_ref[...] + jnp.dot(p.astype(BF16), v, preferred_element_type=F32)
    m_ref[...] = m_new


def _diag_mask(rows, tq, tk, q0):
    q_tok = q0 + (lax.broadcasted_iota(jnp.int32, (rows, tk), 0) & (tq - 1))
    k_tok = lax.broadcasted_iota(jnp.int32, (rows, tk), 1)
    return (k_tok // CHUNK) <= (q_tok // CHUNK)


def _causal_sweep(n_full, qk, upd, finish, sa_ref, sb_ref):
    qk(0, sa_ref)

    def pair(jj, carry):
        j = 2 * jj
        qk(j + 1, sb_ref)
        upd(j, sa_ref)
        qk(j + 2, sa_ref)
        upd(j + 1, sb_ref)
        return carry

    lax.fori_loop(0, n_full // 2, pair, 0)

    @pl.when(n_full % 2 == 1)
    def _():
        qk(n_full, sb_ref)
        upd(n_full - 1, sa_ref)
        finish(sb_ref)

    @pl.when(n_full % 2 == 0)
    def _():
        finish(sa_ref)


def _softmax_init_wide(m_ref, acc_ref):
    m_ref[...] = jnp.full(m_ref.shape, -jnp.inf, F32)
    acc_ref[...] = jnp.zeros(acc_ref.shape, F32)


def _with_ones(v):
    return jnp.concatenate([v, jnp.ones((v.shape[0], LANES), BF16)], axis=1)


def _softmax_update_wide(s, pv_fn, m_ref, acc_ref):
    tk = s.shape[1]
    w = min(tk, LANES)
    m_prev = m_ref[...]
    m_new = jnp.maximum(m_prev, jnp.max(s, axis=-1, keepdims=True))
    alpha = jnp.exp2(m_prev - m_new)
    p = jnp.concatenate([jnp.exp2(s[:, c:c + w] - m_new[:, :w]).astype(BF16) for c in range(0, tk, w)], axis=1)
    acc = acc_ref[...]
    acc_ref[...] = jnp.concatenate([alpha] * (acc.shape[1] // LANES), axis=1) * acc + pv_fn(p)
    m_ref[...] = m_new


def _diff_lambda(lam_ref, lam_init):
    s1 = jnp.sum(lam_ref[0:1, :] * lam_ref[1:2, :], axis=-1, keepdims=True)
    s2 = jnp.sum(lam_ref[2:3, :] * lam_ref[3:4, :], axis=-1, keepdims=True)
    return jnp.exp(s1) - jnp.exp(s2) + lam_init


def _diff_finish(o1, o2, lam, nw, gate, lam_init):
    o = o1 - lam * o2
    o = _rmsnorm(o, nw) * (1.0 - lam_init)
    return (o * _silu(gate)).astype(BF16)


def _stack_streams(q):
    lane = lax.broadcasted_iota(jnp.int32, q.shape, 1)
    zero = jnp.zeros_like(q)
    return jnp.concatenate([jnp.where(lane < DIFF_QK_DIM, q, zero), jnp.where(lane >= DIFF_QK_DIM, q, zero)], axis=0)


def _diff_prompt_kernel(q_ref, k_ref, v_ref, lam_ref, nw_ref, g_ref, o_ref,
                        qz_ref, sa_ref, sb_ref, m_ref, acc_ref, *, t, lam_init):
    qi = pl.program_id(2)
    qz_ref[...] = _stack_streams(q_ref[0])
    _softmax_init_wide(m_ref, acc_ref)

    def blk(ref, j):
        return ref[0, pl.ds(pl.multiple_of(j * t, t), t), :]

    def qk(j, s_ref):
        s_ref[...] = _nt_dot(qz_ref[...], blk(k_ref, j))

    def pv_fn(j):
        return lambda p: jnp.dot(p, _with_ones(blk(v_ref, j)), preferred_element_type=F32)

    def upd(j, s_ref):
        _softmax_update_wide(s_ref[...], pv_fn(j), m_ref, acc_ref)

    def finish(s_ref):
        s = jnp.where(_diag_mask(2 * t, t, t, 0), s_ref[...], NEG_INF)
        _softmax_update_wide(s, pv_fn(qi), m_ref, acc_ref)
        o = acc_ref[:, :DIFF_V_DIM] / acc_ref[:, DIFF_V_DIM:]
        lam = _diff_lambda(lam_ref, lam_init)
        o_ref[0] = _diff_finish(o[:t], o[t:], lam, nw_ref[...], g_ref[...], lam_init)

    _causal_sweep(qi, qk, upd, finish, sa_ref, sb_ref)


def _diff_prompt_call(qd, kb, vb, lam_p, norm_w, h2d, cols, lam_init, t):
    nb, L, dw = qd.shape
    heads = dw // DIFF_V_DIM
    nq = L // t
    kern = functools.partial(_diff_prompt_kernel, t=t, lam_init=lam_init)
    seq = pl.BlockSpec((1, L, LANES), lambda b, h, i: (b, 0, h))
    tile = pl.BlockSpec((1, t, LANES), lambda b, h, i: (b, i, h))
    return pl.pallas_call(
        kern,
        grid=(nb, heads, nq),
        in_specs=[tile, seq, seq,
                  pl.BlockSpec((4, DIFF_QK_DIM), lambda b, h, i: (0, 0)),
                  pl.BlockSpec((1, DIFF_V_DIM), lambda b, h, i: (0, 0)),
                  pl.BlockSpec((t, LANES), lambda b, h, i: (b * nq + i, cols.dgate // LANES + h))],
        out_specs=tile,
        out_shape=jax.ShapeDtypeStruct((nb, L, dw), BF16),
        scratch_shapes=[pltpu.VMEM((2 * t, LANES), BF16),
                        pltpu.VMEM((2 * t, t), F32),
                        pltpu.VMEM((2 * t, t), F32),
                        pltpu.VMEM((2 * t, LANES), F32),
                        pltpu.VMEM((2 * t, DIFF_V_DIM + LANES), F32)],
        compiler_params=_cparams(("arbitrary", "arbitrary", "arbitrary")),
        name="diff_attn_prompt",
    )(qd, kb, vb, lam_p, norm_w, h2d)


def _diff_sample_kernel(q_ref, kc_ref, vc_ref, kn_ref, vn_ref, lam_ref, nw_ref, g_ref, o_ref,
                        qz_ref, m_ref, acc_ref, *, nkc, heads, lam_init):
    j = pl.program_id(1)
    L = q_ref.shape[1]
    rows = 2 * L

    @pl.when(j == 0)
    def _():
        for h in range(heads):
            qz_ref[h] = _stack_streams(q_ref[0, :, h * LANES:(h + 1) * LANES])
        _softmax_init_wide(m_ref, acc_ref)

    def step(scores, values):
        def pv_fn(p):
            return jnp.concatenate([jnp.dot(p[h * rows:(h + 1) * rows], _with_ones(values(h)),
                                            preferred_element_type=F32) for h in range(heads)], axis=0)
        s = jnp.concatenate([scores(h) for h in range(heads)], axis=0)
        _softmax_update_wide(s, pv_fn, m_ref, acc_ref)

    @pl.when(j < nkc)
    def _():
        tk = kc_ref.shape[5]

        def scores(h):
            return jnp.dot(qz_ref[h], kc_ref[0, 0, h].reshape(2 * DIFF_QK_DIM, tk).astype(BF16),
                           preferred_element_type=F32)

        def values(h):
            return vc_ref[0, 0, pl.ds(h, tk, stride=heads), :].astype(BF16)

        step(scores, values)

    @pl.when(j == nkc)
    def _():
        step(lambda h: _nt_dot(qz_ref[h], kn_ref[0, :, h * LANES:(h + 1) * LANES]),
             lambda h: vn_ref[0, :, h * LANES:(h + 1) * LANES])
        lam = _diff_lambda(lam_ref, lam_init)
        o = acc_ref[:, :DIFF_V_DIM] / acc_ref[:, DIFF_V_DIM:]
        for h in range(heads):
            sl = slice(h * LANES, (h + 1) * LANES)
            oh = o[h * rows:(h + 1) * rows]
            o_ref[0, :, sl] = _diff_finish(oh[:L], oh[L:], lam, nw_ref[...], g_ref[:, sl], lam_init)


def _diff_sample_call(qd, kb, vb, k_cache, v_cache, layer, lam_p, norm_w, h2d, cols, lam_init, tk):
    nb, L, dw = qd.shape
    heads = dw // DIFF_V_DIM
    P = k_cache.shape[5]
    nkc = P // tk
    last = nkc - 1
    kern = functools.partial(_diff_sample_kernel, nkc=nkc, heads=heads, lam_init=lam_init)
    kspec = pl.BlockSpec((1, 1, heads, 2, DIFF_QK_DIM, tk),
                         lambda b, j: (layer, b, 0, 0, 0, jnp.minimum(j, last)))
    vspec = pl.BlockSpec((1, 1, tk * heads, DIFF_V_DIM), lambda b, j: (layer, b, jnp.minimum(j, last), 0))
    nspec = pl.BlockSpec((1, L, dw), lambda b, j: (b, 0, 0))
    return pl.pallas_call(
        kern,
        grid=(nb, nkc + 1),
        in_specs=[nspec, kspec, vspec, nspec, nspec,
                  pl.BlockSpec((4, DIFF_QK_DIM), lambda b, j: (0, 0)),
                  pl.BlockSpec((1, DIFF_V_DIM), lambda b, j: (0, 0)),
                  pl.BlockSpec((L, dw), lambda b, j: (b, cols.dgate // dw))],
        out_specs=nspec,
        out_shape=jax.ShapeDtypeStruct((nb, L, dw), BF16),
        scratch_shapes=[pltpu.VMEM((heads, 2 * L, LANES), BF16),
                        pltpu.VMEM((heads * 2 * L, LANES), F32),
                        pltpu.VMEM((heads * 2 * L, DIFF_V_DIM + LANES), F32)],
        compiler_params=_cparams(("arbitrary", "arbitrary")),
        name="diff_attn_sample",
    )(qd, k_cache, v_cache, kb, vb, lam_p, norm_w, h2d)


def _mla_finish(acc_ref, l_ref, wuv_ref, g_ref, o_ref, heads, t):
    o = (acc_ref[...] / l_ref[...]).astype(BF16)
    for h in range(heads):
        sl = slice(h * MLA_V_DIM, (h + 1) * MLA_V_DIM)
        om = jnp.dot(o[h * t:(h + 1) * t], wuv_ref[:, sl], preferred_element_type=F32)
        o_ref[0, :, sl] = (om * _silu(g_ref[:, sl])).astype(BF16)


def _mla_prompt_kernel(q_ref, kt_ref, v_ref, wuv_ref, g_ref, o_ref, sa_ref, sb_ref, m_ref, l_ref, acc_ref, *, t, hg):
    qi = pl.program_id(2)
    _softmax_init(m_ref, l_ref, acc_ref)

    def vblk(j):
        return v_ref[0, pl.ds(pl.multiple_of(j * t, t), t), :]

    def qk(j, s_ref):
        for g in range(hg):
            s_ref[g * t:(g + 1) * t, :] = jnp.dot(q_ref[0, g], kt_ref[0, g, j], preferred_element_type=F32)

    def upd(j, s_ref):
        _softmax_update(s_ref[...], vblk(j), m_ref, l_ref, acc_ref)

    def finish(s_ref):
        s = jnp.where(_diag_mask(hg * t, t, t, 0), s_ref[...], NEG_INF)
        _softmax_update(s, vblk(qi), m_ref, l_ref, acc_ref)
        _mla_finish(acc_ref, l_ref, wuv_ref, g_ref, o_ref, hg, t)

    _causal_sweep(qi, qk, upd, finish, sa_ref, sb_ref)


def _mla_prompt_call(qh, kt, latb, wuv, h2d, cols, t, hg):
    nb, heads, L, qdim = qh.shape
    mw = heads * MLA_V_DIM
    gw = hg * MLA_V_DIM
    nq = L // t
    return pl.pallas_call(
        functools.partial(_mla_prompt_kernel, t=t, hg=hg),
        grid=(nb, heads // hg, nq),
        in_specs=[pl.BlockSpec((1, hg, t, qdim), lambda b, h, i: (b, h, i, 0)),
                  pl.BlockSpec((1, hg, nq, qdim, t), lambda b, h, i: (b, h, 0, 0, 0)),
                  pl.BlockSpec((1, L, MLA_KV_RANK), lambda b, h, i: (b, 0, 0)),
                  pl.BlockSpec((MLA_KV_RANK, gw), lambda b, h, i: (0, h)),
                  pl.BlockSpec((t, gw), lambda b, h, i: (b * nq + i, cols.mgate // gw + h))],
        out_specs=pl.BlockSpec((1, t, gw), lambda b, h, i: (b, i, h)),
        out_shape=jax.ShapeDtypeStruct((nb, L, mw), BF16),
        scratch_shapes=[pltpu.VMEM((hg * t, t), F32),
                        pltpu.VMEM((hg * t, t), F32),
                        pltpu.VMEM((hg * t, 1), F32),
                        pltpu.VMEM((hg * t, 1), F32),
                        pltpu.VMEM((hg * t, MLA_KV_RANK), F32)],
        compiler_params=_cparams(("arbitrary", "arbitrary", "arbitrary")),
        name="mla_attn_prompt",
    )(qh, kt, latb, wuv, h2d)


def _mla_sample_kernel(q_ref, lc_ref, rc_ref, kn_ref, wuv_ref, g_ref, o_ref, m_ref, l_ref, acc_ref,
                       *, nkc, heads):
    j = pl.program_id(1)
    L = q_ref.shape[2]

    @pl.when(j == 0)
    def _():
        _softmax_init(m_ref, l_ref, acc_ref)

    q = q_ref[0].reshape(heads * L, MLA_QK_PAD)

    @pl.when(j < nkc)
    def _():
        lat = lc_ref[0, 0].astype(BF16)
        kr_t = rc_ref[0, 0].astype(BF16)
        s = _nt_dot(q[:, :MLA_KV_RANK], lat) + jnp.dot(q[:, MLA_KV_RANK:MLA_KV_RANK + MLA_ROPE_DIM], kr_t,
                                                       preferred_element_type=F32)
        _softmax_update(s, lat, m_ref, l_ref, acc_ref)

    @pl.when(j == nkc)
    def _():
        kn = kn_ref[0]
        _softmax_update(_nt_dot(q, kn), kn[:, :MLA_KV_RANK], m_ref, l_ref, acc_ref)
        _mla_finish(acc_ref, l_ref, wuv_ref, g_ref, o_ref, heads, L)


def _mla_sample_call(qcat, kcat, lat_cache, kr_cache, layer, wuv, h2d, cols, tk):
    nb, heads, L, _ = qcat.shape
    mw = heads * MLA_V_DIM
    P = lat_cache.shape[2]
    nkc = P // tk
    last = nkc - 1
    kern = functools.partial(_mla_sample_kernel, nkc=nkc, heads=heads)
    return pl.pallas_call(
        kern,
        grid=(nb, nkc + 1),
        in_specs=[pl.BlockSpec((1, heads, L, MLA_QK_PAD), lambda b, j: (b, 0, 0, 0)),
                  pl.BlockSpec((1, 1, tk, MLA_KV_RANK), lambda b, j: (layer, b, jnp.minimum(j, last), 0)),
                  pl.BlockSpec((1, 1, MLA_ROPE_DIM, tk), lambda b, j: (layer, b, 0, jnp.minimum(j, last))),
                  pl.BlockSpec((1, L, MLA_QK_PAD), lambda b, j: (b, 0, 0)),
                  pl.BlockSpec(wuv.shape, lambda b, j: (0, 0)),
                  pl.BlockSpec((L, mw), lambda b, j: (b, cols.mgate // mw))],
        out_specs=pl.BlockSpec((1, L, mw), lambda b, j: (b, 0, 0)),
        out_shape=jax.ShapeDtypeStruct((nb, L, mw), BF16),
        scratch_shapes=[pltpu.VMEM((heads * L, 1), F32),
                        pltpu.VMEM((heads * L, 1), F32),
                        pltpu.VMEM((heads * L, MLA_KV_RANK), F32)],
        compiler_params=_cparams(("arbitrary", "arbitrary")),
        name="mla_attn_sample",
    )(qcat, lat_cache, kr_cache, kcat, wuv, h2d)


def _outproj_kernel(ys_ref, yd_ref, ym_ref, w_ref, x_ref, g_ref, r_ref, *, alpha):
    bt, lt, _ = ys_ref.shape
    flat = lambda ref: ref[...].reshape(bt * lt, ref.shape[2])
    mix = jnp.concatenate([flat(ys_ref), flat(yd_ref), flat(ym_ref)], axis=1)
    acc = jnp.dot(mix, w_ref[...], preferred_element_type=F32)
    r_ref[...] = alpha * x_ref[...] + g_ref[...] * acc.reshape(bt, lt, acc.shape[1])


def _outproj_call(y_ssd, y_diff, y_mla, w_out, x, gate, alpha):
    nb, L, d = x.shape
    bt, lt = _seq_tiles(nb, L, 512)
    tn = 1024
    yspec = lambda width: pl.BlockSpec((bt, lt, width), lambda j, b, i: (b, i, 0))
    return pl.pallas_call(
        functools.partial(_outproj_kernel, alpha=alpha),
        grid=(d // tn, nb // bt, L // lt),
        in_specs=[yspec(y_ssd.shape[2]), yspec(y_diff.shape[2]), yspec(y_mla.shape[2]),
                  pl.BlockSpec((w_out.shape[0], tn), lambda j, b, i: (0, j)),
                  pl.BlockSpec((bt, lt, tn), lambda j, b, i: (b, i, j)),
                  pl.BlockSpec((bt, 1, tn), lambda j, b, i: (b, 0, j))],
        out_specs=pl.BlockSpec((bt, lt, tn), lambda j, b, i: (b, i, j)),
        out_shape=jax.ShapeDtypeStruct((nb, L, d), F32),
        compiler_params=_cparams(("arbitrary", "arbitrary", "arbitrary")),
        name="out_proj",
    )(y_ssd, y_diff, y_mla, w_out, x, gate)


def _layernorm(r, g, b):
    mu = jnp.mean(r, axis=-1, keepdims=True)
    var = jnp.mean(jnp.square(r - mu), axis=-1, keepdims=True)
    return (r - mu) * lax.rsqrt(var + 1e-5) * g + b


def _ln_kernel(r_ref, g_ref, b_ref, x_ref):
    x_ref[...] = _layernorm(r_ref[...], g_ref[...], b_ref[...])


def _ln_mod_kernel(r_ref, g_ref, b_ref, sc_ref, sh_ref, x_ref, u_ref):
    x = _layernorm(r_ref[...], g_ref[...], b_ref[...])
    x_ref[...] = x
    u_ref[...] = (x * (1.0 + sc_ref[...]) + sh_ref[...]).astype(BF16)


def _ln_call(r, g, b, scale=None, shift=None):
    nb, L, d = r.shape
    bt, lt = _seq_tiles(nb, L, 256)
    xspec = pl.BlockSpec((bt, lt, d), lambda i, j: (i, j, 0))
    wspec = pl.BlockSpec((1, 1, d), lambda i, j: (0, 0, 0))
    sspec = pl.BlockSpec((bt, 1, d), lambda i, j: (i, 0, 0))
    g3, b3 = g.reshape(1, 1, d), b.reshape(1, 1, d)
    if scale is None:
        return pl.pallas_call(
            _ln_kernel, grid=(nb // bt, L // lt),
            in_specs=[xspec, wspec, wspec], out_specs=xspec,
            out_shape=jax.ShapeDtypeStruct((nb, L, d), F32),
            compiler_params=_cparams(("arbitrary", "arbitrary")), name="layernorm",
        )(r, g3, b3), None
    return pl.pallas_call(
        _ln_mod_kernel, grid=(nb // bt, L // lt),
        in_specs=[xspec, wspec, wspec, sspec, sspec], out_specs=[xspec, xspec],
        out_shape=[jax.ShapeDtypeStruct((nb, L, d), F32), jax.ShapeDtypeStruct((nb, L, d), BF16)],
        compiler_params=_cparams(("arbitrary", "arbitrary")), name="layernorm_modulate",
    )(r, g3, b3, scale, shift)


def _ssd_chunk(L):
    for t in (128, 64, 32, 16, 8):
        if L % t == 0:
            return t
    raise ValueError(f"sequence length {L} is not a multiple of 8")


def _attn_tile(L, want):
    t = want
    while L % t:
        t //= 2
    return t


def _layer(x, u, mod_l, next_mod, wl, cols, layer_idx, depth, caches, state_bufs):
    nb, L, d = x.shape
    heads = cols.heads
    mla_heads = cols.mla_w // MLA_V_DIM
    _, _, gate = mod_l
    k_cache, v_cache, lat_cache, kr_cache, st0, conv0 = caches
    P = 0 if k_cache is None else lat_cache.shape[2]
    pos = P + jnp.arange(L, dtype=jnp.int32)

    h2d = _inproj_call(u.reshape(nb * L, d), wl["w_in"], layer_idx, cols.tn)

    T = _ssd_chunk(L)
    y_ssd, ssm_new, conv_new = _ssd_call(h2d, conv0, st0, wl["conv_w"], wl["conv_b"], wl["dt_bias"], wl["a_log"],
                                         wl["d_skip"], wl["ssd_norm_w"], wl["e_mat"], wl["et_mat"], cols, nb, L, T)

    tabs_d = _rope_tables(pos, DIFF_ROT, DIFF_QK_DIM, LANES)
    tabs_m = _rope_tables(pos, MLA_ROPE_DIM, MLA_ROPE_DIM, MLA_ROPE_DIM)
    lam_init = 0.8 - 0.6 * math.exp(-0.3 * layer_idx)
    if k_cache is None:
        t = _attn_tile(L, 512)
        k32, v32, lat32, kr32, qd, kb, vb, latb, kt = _prep_call(h2d, tabs_d, tabs_m, wl["kv_norm_w"], cols, nb, L, t,
                                                                 layer_idx, state_bufs, wl["w_uk_flat"])
        qh = _mlaq_call(h2d, wl["q_norm_w"], wl["w_uq"], wl["w_uk"], tabs_m, cols, nb, L, mla_heads, False)
        y_diff = _diff_prompt_call(qd, kb, vb, wl["lam"], wl["diff_norm_w"], h2d, cols, lam_init, t)
        y_mla = _mla_prompt_call(qh, kt, latb, wl["w_uv"], h2d, cols, t, MLA_HEAD_GROUP)
    else:
        k32, v32, lat32, kr32, qd, kb, vb, kcat = _prep_call(h2d, tabs_d, tabs_m, wl["kv_norm_w"], cols, nb, L,
                                                             min(L, 256), layer_idx, state_bufs)
        qcat = _mlaq_call(h2d, wl["q_norm_w"], wl["w_uq"], wl["w_uk"], tabs_m, cols, nb, L, mla_heads, True)
        y_diff = _diff_sample_call(qd, kb, vb, k_cache, v_cache, layer_idx, wl["lam"], wl["diff_norm_w"], h2d,
                                   cols, lam_init, _attn_tile(P, 1024))
        y_mla = _mla_sample_call(qcat, kcat, lat_cache, kr_cache, layer_idx, wl["w_uv"], h2d, cols,
                                 _attn_tile(P, 2048))

    alpha = (2 * depth) ** 0.25
    r = _outproj_call(y_ssd.reshape(nb, L, cols.ssd_w), y_diff, y_mla, wl["w_out"], x, gate, alpha)
    if next_mod is None:
        x_new, u_new = _ln_call(r, wl["ln_g"], wl["ln_b"])
    else:
        x_new, u_new = _ln_call(r, wl["ln_g"], wl["ln_b"], next_mod[1], next_mod[0])
    return x_new, u_new, (k32, v32, lat32, kr32), (ssm_new, conv_new)


def _layer_weights(l, cols, w_in, conv_w, conv_b, dt_bias, a_log, d_skip, ssd_norm_w, lambda_q1, lambda_k1,
                   lambda_q2, lambda_k2, diff_norm_w, mla_q_norm_w, mla_kv_norm_w, w_uq, w_uk, w_uv, w_out,
                   ln_g, ln_b):
    heads = cols.heads
    mla_heads = cols.mla_w // MLA_V_DIM
    lane_pad = lambda v: jnp.pad(v, (MLA_ROPE_DIM, LANES - MLA_ROPE_DIM - heads)).reshape(1, LANES)
    qk = MLA_NOPE_DIM + MLA_ROPE_DIM
    wq = w_uq[l].reshape(MLA_Q_RANK, mla_heads, qk)
    wq_nope = wq[:, :, :MLA_NOPE_DIM].reshape(MLA_Q_RANK, mla_heads * MLA_NOPE_DIM)
    wq_rope = jnp.pad(wq[:, :, MLA_NOPE_DIM:], ((0, 0), (0, 0), (0, LANES - MLA_ROPE_DIM)))
    wq_p = jnp.concatenate([wq_nope, wq_rope.reshape(MLA_Q_RANK, mla_heads * LANES)], axis=1).astype(BF16)
    eh = np.zeros((LANES, cols.ssd_w), np.float32)
    for h in range(heads):
        eh[MLA_ROPE_DIM + h, h * SSD_HEAD_DIM:(h + 1) * SSD_HEAD_DIM] = 1.0
    return dict(
        w_in=w_in,
        conv_w=conv_w[l], conv_b=conv_b[l].reshape(1, -1),
        dt_bias=lane_pad(dt_bias[l]), a_log=lane_pad(a_log[l]),
        d_skip=jnp.repeat(d_skip[l], SSD_HEAD_DIM).reshape(1, -1),
        ssd_norm_w=ssd_norm_w[l].reshape(1, -1),
        e_mat=jnp.asarray(eh, BF16), et_mat=jnp.asarray(eh.T, BF16),
        lam=jnp.stack([lambda_q1[l], lambda_k1[l], lambda_q2[l], lambda_k2[l]]),
        diff_norm_w=diff_norm_w[l].reshape(1, -1),
        q_norm_w=mla_q_norm_w[l].reshape(1, -1), kv_norm_w=mla_kv_norm_w[l].reshape(1, -1),
        w_uq=wq_p,
        w_uk=jnp.transpose(w_uk[l], (1, 2, 0)).astype(BF16),
        w_uk_flat=w_uk[l].reshape(MLA_KV_RANK, mla_heads * MLA_NOPE_DIM).astype(BF16),
        w_uv=w_uv[l].reshape(MLA_KV_RANK, mla_heads * MLA_V_DIM).astype(BF16),
        w_out=w_out[l].astype(BF16),
        ln_g=ln_g[l], ln_b=ln_b[l],
    )


def kernel(x_prompt, x_sample, cache_diff_k, cache_diff_v, cache_mla_latent, cache_mla_krope, state_ssm, state_conv,
           c_prompt, c_sample, w_mod, b_mod, w_in, conv_w, conv_b, dt_bias, a_log, d_skip, ssd_norm_w, lambda_q1,
           lambda_k1, lambda_q2, lambda_k2, diff_norm_w, mla_q_norm_w, mla_kv_norm_w, w_uq, w_uk, w_uv, w_out,
           ln_g, ln_b):
    depth = w_in.shape[0]
    bp, _, d = x_prompt.shape
    bs = x_sample.shape[0]
    cols = _Cols(d)
    heads = cols.heads

    rows = -(-(bp + bs) // 8) * 8
    c_all = jnp.concatenate([c_prompt, c_sample, jnp.zeros((rows - bp - bs, d), F32)], axis=0)
    mod = _mod_call(c_all, w_mod, b_mod)

    def mods(l, lo, n):
        part = lambda k: mod[l, lo:lo + n, k * d:(k + 1) * d].reshape(n, 1, d)
        return part(0), part(1), part(2)

    pdiff = cache_diff_k.shape[2]
    kc = jnp.transpose(cache_diff_k, (0, 1, 3, 4, 5, 2))
    vc = cache_diff_v.reshape(depth, bs, pdiff * cache_diff_v.shape[3], DIFF_V_DIM)
    krc = jnp.transpose(cache_mla_krope, (0, 1, 3, 2))
    zero_state = jnp.zeros((bp, heads, SSD_HEAD_DIM, SSD_STATE), F32)
    zero_conv = jnp.zeros((bp, SSD_CONV - 1, cols.conv_dim), F32)

    w_in_p = _permute_w_in(w_in, cols)
    hp, hs = x_prompt, x_sample
    mp, ms = mods(0, 0, bp), mods(0, bp, bs)
    up = _modulate_call(hp, mp[1], mp[0])
    us = _modulate_call(hs, ms[1], ms[0])

    def state_buffers(nb, L):
        return tuple(jnp.zeros((depth, nb, L, w), F32) for w in (cols.diff_w, cols.diff_w, MLA_KV_RANK, MLA_ROPE_DIM))

    bufs_p, bufs_s = state_buffers(bp, hp.shape[1]), state_buffers(bs, hs.shape[1])
    rec_p, rec_s = [], []
    for l in range(depth):
        wl = _layer_weights(l, cols, w_in_p, conv_w, conv_b, dt_bias, a_log, d_skip, ssd_norm_w, lambda_q1, lambda_k1,
                            lambda_q2, lambda_k2, diff_norm_w, mla_q_norm_w, mla_kv_norm_w, w_uq, w_uk, w_uv, w_out,
                            ln_g, ln_b)
        nmp = mods(l + 1, 0, bp) if l + 1 < depth else None
        nms = mods(l + 1, bp, bs) if l + 1 < depth else None
        hp, up, bufs_p, rp = _layer(hp, up, mp, nmp, wl, cols, l, depth,
                                    (None, None, None, None, zero_state, zero_conv), bufs_p)
        hs, us, bufs_s, rs = _layer(hs, us, ms, nms, wl, cols, l, depth,
                                    (kc, vc, cache_mla_latent, krc, state_ssm[l], state_conv[l]), bufs_s)
        rec_p.append(rp)
        rec_s.append(rs)
        mp, ms = nmp, nms

    def states(bufs, rec):
        k32, v32, lat32, kr32 = bufs
        nb, L = k32.shape[1], k32.shape[2]
        return (k32.reshape(depth, nb, L, cols.diff_w // (2 * DIFF_QK_DIM), 2, DIFF_QK_DIM),
                v32.reshape(depth, nb, L, cols.diff_w // DIFF_V_DIM, DIFF_V_DIM), lat32, kr32,
                jnp.stack([r[0] for r in rec]), jnp.stack([r[1] for r in rec]))

    return (hp, hs) + states(bufs_p, rec_p) + states(bufs_s, rec_s)
```

```python
import functools
import math

import numpy as np
import jax
import jax.numpy as jnp
from jax import lax
from jax.experimental import pallas as pl
from jax.experimental.pallas import tpu as pltpu

F32 = jnp.float32
BF16 = jnp.bfloat16

CHUNK = 64
ROPE_THETA = 500000.0
NEG_INF = -1e30
SSD_HEAD_DIM = 64
SSD_GROUPS = 4
SSD_STATE = 128
SSD_CONV = 4
DIFF_QK_DIM = 64
DIFF_V_DIM = 128
DIFF_ROT = DIFF_QK_DIM // 4
DIFF_SCALE = DIFF_QK_DIM ** -0.5
MLA_V_DIM = 128
MLA_NOPE_DIM = 128
MLA_ROPE_DIM = 64
MLA_Q_RANK = 768
MLA_KV_RANK = 256
MLA_SCALE = (MLA_NOPE_DIM + MLA_ROPE_DIM) ** -0.5
MLA_QK_PAD = MLA_KV_RANK + 128
MLA_QH_DIM = MLA_NOPE_DIM + 128
MLA_HEAD_GROUP = 2
LOG2E = math.log2(math.e)

LANES = 128
VMEM_LIMIT = 56 * 1024 * 1024


def _cparams(sem):
    return pltpu.CompilerParams(dimension_semantics=sem, vmem_limit_bytes=VMEM_LIMIT)


def _silu(x):
    hx = 0.5 * x
    return hx + hx * jnp.tanh(hx)


def _nt_dot(a, b):
    return lax.dot_general(a, b, (((1,), (1,)), ((), ())), preferred_element_type=F32)


def _split2(v):
    hi = v.astype(BF16)
    lo = (v - hi.astype(F32)).astype(BF16)
    return hi, lo


def _split3(v):
    hi = v.astype(BF16)
    r = v - hi.astype(F32)
    mid = r.astype(BF16)
    lo = (r - mid.astype(F32)).astype(BF16)
    return hi, mid, lo


def _mod_kernel(c_ref, w_ref, b_ref, o_ref):
    a = _silu(c_ref[...]).astype(BF16)
    o_ref[0] = jnp.dot(a, w_ref[0].astype(BF16), preferred_element_type=F32) + b_ref[0]


def _mod_call(c_all, w_mod, b_mod):
    depth, d, n3 = w_mod.shape
    rows = c_all.shape[0]
    tn = 512
    return pl.pallas_call(
        _mod_kernel,
        grid=(depth, n3 // tn),
        in_specs=[pl.BlockSpec((rows, d), lambda l, j: (0, 0)),
                  pl.BlockSpec((1, d, tn), lambda l, j: (l, 0, j)),
                  pl.BlockSpec((1, 1, tn), lambda l, j: (l, 0, j))],
        out_specs=pl.BlockSpec((1, rows, tn), lambda l, j: (l, 0, j)),
        out_shape=jax.ShapeDtypeStruct((depth, rows, n3), F32),
        compiler_params=_cparams(("arbitrary", "arbitrary")),
        name="adaln_mod",
    )(c_all, w_mod, b_mod.reshape(depth, 1, n3))


def _modulate_kernel(x_ref, sc_ref, sh_ref, u_ref):
    u_ref[...] = (x_ref[...] * (1.0 + sc_ref[...]) + sh_ref[...]).astype(BF16)


def _seq_tiles(nb, L, rows):
    if L >= rows:
        return 1, rows
    return min(nb, rows // L), L


def _modulate_call(x, scale, shift):
    nb, L, d = x.shape
    bt, lt = _seq_tiles(nb, L, 256)
    return pl.pallas_call(
        _modulate_kernel,
        grid=(nb // bt, L // lt),
        in_specs=[pl.BlockSpec((bt, lt, d), lambda i, j: (i, j, 0)),
                  pl.BlockSpec((bt, 1, d), lambda i, j: (i, 0, 0)),
                  pl.BlockSpec((bt, 1, d), lambda i, j: (i, 0, 0))],
        out_specs=pl.BlockSpec((bt, lt, d), lambda i, j: (i, j, 0)),
        out_shape=jax.ShapeDtypeStruct((nb, L, d), BF16),
        compiler_params=_cparams(("arbitrary", "arbitrary")),
        name="modulate",
    )(x, scale, shift)


class _Cols:
    def __init__(self, d_model):
        self.ssd_w = d_model // 2
        self.diff_w = d_model // 4
        self.mla_w = d_model // 4
        self.heads = self.ssd_w // SSD_HEAD_DIM
        self.conv_dim = self.ssd_w + 2 * SSD_GROUPS * SSD_STATE
        self.in_sizes = (self.ssd_w, self.conv_dim, self.heads, self.diff_w, self.diff_w, self.diff_w,
                         self.diff_w, MLA_Q_RANK, MLA_KV_RANK, MLA_ROPE_DIM, self.mla_w)
        o = 0
        self.xbc = o; o += self.conv_dim
        self.dq = o; o += self.diff_w
        self.z = o; o += self.ssd_w
        self.dk = o; o += self.diff_w
        self.dv = o; o += self.diff_w
        self.dgate = o; o += self.diff_w
        self.mgate = o; o += self.mla_w
        self.cq = o; o += MLA_Q_RANK + MLA_KV_RANK
        self.krdt = o; o += LANES
        self.used = o
        self.tn = 1280
        self.total = -(-o // self.tn) * self.tn


def _w_in_blocks(cols):
    offs = np.concatenate([[0], np.cumsum(cols.in_sizes)])
    order = (1, 3, 0, 4, 5, 6, 10)
    rows = []
    for i in order:
        rows += list(range(int(offs[i]), int(offs[i + 1]), LANES))
    rows += list(range(int(offs[7]), int(offs[9]), LANES))
    rows.append(-1)
    rows += [-2] * ((cols.total - cols.used) // LANES)
    return np.asarray(rows, np.int32)


def _wprep_kernel(tab_ref, w_ref, sp_ref, o_ref):
    src = tab_ref[pl.program_id(1)]

    @pl.when(src >= 0)
    def _():
        o_ref[0] = w_ref[0].astype(BF16)

    @pl.when(src == -1)
    def _():
        o_ref[0] = sp_ref[0].astype(BF16)

    @pl.when(src == -2)
    def _():
        o_ref[0] = jnp.zeros(o_ref.shape[1:], BF16)


def _permute_w_in(w_in, cols):
    depth, d, _ = w_in.shape
    wt = jnp.transpose(w_in, (0, 2, 1))
    offs = np.concatenate([[0], np.cumsum(cols.in_sizes)])
    pad = LANES - MLA_ROPE_DIM - cols.heads
    special = jnp.concatenate([wt[:, offs[9]:offs[10]], wt[:, offs[2]:offs[3]], jnp.zeros((depth, pad, d), F32)], axis=1)
    tab = jnp.asarray(_w_in_blocks(cols))
    grid_spec = pltpu.PrefetchScalarGridSpec(
        num_scalar_prefetch=1,
        grid=(depth, int(tab.shape[0])),
        in_specs=[pl.BlockSpec((pl.Element(1), pl.Element(LANES), pl.Element(d)),
                               lambda l, i, tab: (l, pl.multiple_of(jnp.maximum(tab[i], 0), 8), 0)),
                  pl.BlockSpec((1, LANES, d), lambda l, i, tab: (l, 0, 0))],
        out_specs=pl.BlockSpec((1, LANES, d), lambda l, i, tab: (l, i, 0)))
    return pl.pallas_call(
        _wprep_kernel, grid_spec=grid_spec,
        out_shape=jax.ShapeDtypeStruct((depth, cols.total, d), BF16),
        compiler_params=_cparams(("arbitrary", "arbitrary")),
        name="w_in_prep",
    )(tab, wt, special)


def _matmul_nt_kernel(x_ref, w_ref, o_ref):
    o_ref[...] = _nt_dot(x_ref[...], w_ref[...])


def _inproj_call(u2d, wp, layer, tn):
    m, k = u2d.shape
    n = wp.shape[1]
    tm = min(m, 1024)
    return pl.pallas_call(
        _matmul_nt_kernel,
        grid=(n // tn, m // tm),
        in_specs=[pl.BlockSpec((tm, k), lambda j, i: (i, 0)),
                  pl.BlockSpec((None, tn, k), lambda j, i: (layer, j, 0))],
        out_specs=pl.BlockSpec((tm, tn), lambda j, i: (i, j)),
        out_shape=jax.ShapeDtypeStruct((m, n), F32),
        compiler_params=_cparams(("arbitrary", "arbitrary")),
        name="in_proj",
    )(u2d, wp)


def _ssd_kernel(xbc_ref, z_ref, dtb_ref, conv0_ref, st0_ref, cw_ref, cb_ref, dtbias_ref, alog_ref,
                dskip_ref, nw_ref, e_ref, et_ref, y_ref, st_ref, convo_ref, ext_ref, *, T, nchunks, heads):
    c = pl.program_id(1)
    ssd_w = heads * SSD_HEAD_DIM
    gw = ssd_w // SSD_GROUPS
    hpg = heads // SSD_GROUPS
    conv_dim = ext_ref.shape[1]
    dt_lo = MLA_ROPE_DIM

    @pl.when(c == 0)
    def _():
        ext_ref[0:8, :] = jnp.zeros((8, conv_dim), F32)
        ext_ref[8 - (SSD_CONV - 1):8, :] = conv0_ref[0]
        st_ref[0] = st0_ref[0]

    ext_ref[8:8 + T, :] = xbc_ref[...]
    acc = cb_ref[...]
    for j in range(SSD_CONV):
        lo = 8 - (SSD_CONV - 1) + j
        acc = acc + ext_ref[lo:lo + T, :] * cw_ref[j:j + 1, :]
    ext_ref[0:8, :] = ext_ref[T:T + 8, :]

    @pl.when(c == nchunks - 1)
    def _():
        convo_ref[0] = ext_ref[8 - (SSD_CONV - 1):8, :]

    xact = _silu(acc)
    xs = xact[:, :ssd_w]
    bm = xact[:, ssd_w:ssd_w + SSD_GROUPS * SSD_STATE]
    cm = xact[:, ssd_w + SSD_GROUPS * SSD_STATE:]

    lane = lax.broadcasted_iota(jnp.int32, (1, LANES), 1)
    is_dt = (lane >= dt_lo) & (lane < dt_lo + heads)
    xdt = dtb_ref[...] + dtbias_ref[...]
    dt = jnp.where(is_dt, jnp.maximum(xdt, 0.0) + jnp.log1p(jnp.exp(-jnp.abs(xdt))), 0.0)
    a_neg = jnp.where(is_dt, -jnp.exp(alog_ref[...]), 0.0)
    da = dt * a_neg

    row = lax.broadcasted_iota(jnp.int32, (T, T), 0)
    col = lax.broadcasted_iota(jnp.int32, (T, T), 1)
    causal = row >= col
    tril = causal.astype(BF16)
    eye = (lax.broadcasted_iota(jnp.int32, (LANES, LANES), 0)
           == lax.broadcasted_iota(jnp.int32, (LANES, LANES), 1)).astype(BF16)

    da3 = _split3(da)
    a_cs = sum(jnp.dot(tril, p, preferred_element_type=F32) for p in da3)
    a3 = _split3(a_cs)
    a_cs_t = sum(_nt_dot(eye, p) for p in a3)
    a_last = a_cs[T - 1:T, :]
    ea = jnp.exp(a_cs)
    te = jnp.exp(a_last - a_cs)

    e_mat = e_ref[...]

    def expand(v):
        hi, lo = _split2(v)
        return jnp.dot(hi, e_mat, preferred_element_type=F32) + jnp.dot(lo, e_mat, preferred_element_type=F32)

    dt_x = expand(dt)
    ea_x = expand(ea)
    te_x = expand(te)
    cd_col = jnp.exp(a_cs_t[:, T - 1:T])
    cd_b = jnp.broadcast_to(cd_col, (LANES, SSD_STATE))
    cdh, cdl = _split2(cd_b)
    et_mat = et_ref[...]
    cd_full = (jnp.dot(et_mat, cdh, preferred_element_type=F32)
               + jnp.dot(et_mat, cdl, preferred_element_type=F32))

    xd = xs * dt_x
    xde = (xd * te_x).astype(BF16)
    lane_p = lax.broadcasted_iota(jnp.int32, (T, LANES), 1)
    lower_half = lane_p < SSD_HEAD_DIM

    for g in range(SSD_GROUPS):
        cg = cm[:, g * SSD_STATE:(g + 1) * SSD_STATE].astype(BF16)
        bg = bm[:, g * SSD_STATE:(g + 1) * SSD_STATE].astype(BF16)
        cbm = _nt_dot(cg, bg)
        st_g = st_ref[0, g * hpg:(g + 1) * hpg].reshape(gw, SSD_STATE)
        y_off = _nt_dot(cg, st_g.astype(BF16)) * ea_x[:, g * gw:(g + 1) * gw]
        pieces = []
        for q in range(hpg // 2):
            c0 = g * gw + q * LANES
            xd_pair = xd[:, c0:c0 + LANES].astype(BF16)
            ys = []
            for h in (g * hpg + 2 * q, g * hpg + 2 * q + 1):
                seg = a_cs[:, dt_lo + h:dt_lo + h + 1] - a_cs_t[dt_lo + h:dt_lo + h + 1, :]
                decay = jnp.exp(jnp.where(causal, seg, -jnp.inf))
                ys.append(jnp.dot((cbm * decay).astype(BF16), xd_pair, preferred_element_type=F32))
            pieces.append(jnp.where(lower_half, ys[0], ys[1]))
        y_g = jnp.concatenate(pieces, axis=1) + y_off
        upd = lax.dot_general(xde[:, g * gw:(g + 1) * gw], bg, (((0,), (0,)), ((), ())),
                              preferred_element_type=F32)
        st_new = st_g * cd_full[g * gw:(g + 1) * gw, :] + upd
        st_ref[0, g * hpg:(g + 1) * hpg] = st_new.reshape(hpg, SSD_HEAD_DIM, SSD_STATE)
        y_g = y_g + dskip_ref[:, g * gw:(g + 1) * gw] * xs[:, g * gw:(g + 1) * gw]
        yg = y_g * _silu(z_ref[:, g * gw:(g + 1) * gw])
        ms = jnp.mean(yg * yg, axis=-1, keepdims=True)
        y_ref[:, g * gw:(g + 1) * gw] = (yg * lax.rsqrt(ms + 1e-6) * nw_ref[:, g * gw:(g + 1) * gw]).astype(BF16)


def _ssd_call(h2d, conv0, st0, conv_w, conv_b, dtbias_p, alog_p, dskip_x, norm_w, e_mat, et_mat, cols, nb, L, T):
    nchunks = L // T
    heads = cols.heads
    ssd_w = cols.ssd_w
    cd = cols.conv_dim
    kern = functools.partial(_ssd_kernel, T=T, nchunks=nchunks, heads=heads)
    row = lambda b, c: b * nchunks + c
    const2 = lambda b, c: (0, 0)
    return pl.pallas_call(
        kern,
        grid=(nb, nchunks),
        in_specs=[pl.BlockSpec((T, cd), lambda b, c: (row(b, c), cols.xbc // cd)),
                  pl.BlockSpec((T, ssd_w), lambda b, c: (row(b, c), cols.z // ssd_w)),
                  pl.BlockSpec((T, LANES), lambda b, c: (row(b, c), cols.krdt // LANES)),
                  pl.BlockSpec((1, SSD_CONV - 1, cd), lambda b, c: (b, 0, 0)),
                  pl.BlockSpec((1, heads, SSD_HEAD_DIM, SSD_STATE), lambda b, c: (b, 0, 0, 0)),
                  pl.BlockSpec((SSD_CONV, cd), const2),
                  pl.BlockSpec((1, cd), const2),
                  pl.BlockSpec((1, LANES), const2),
                  pl.BlockSpec((1, LANES), const2),
                  pl.BlockSpec((1, ssd_w), const2),
                  pl.BlockSpec((1, ssd_w), const2),
                  pl.BlockSpec((LANES, ssd_w), const2),
                  pl.BlockSpec((ssd_w, LANES), const2)],
        out_specs=[pl.BlockSpec((T, ssd_w), lambda b, c: (row(b, c), 0)),
                   pl.BlockSpec((1, heads, SSD_HEAD_DIM, SSD_STATE), lambda b, c: (b, 0, 0, 0)),
                   pl.BlockSpec((1, SSD_CONV - 1, cd), lambda b, c: (b, 0, 0))],
        out_shape=[jax.ShapeDtypeStruct((nb * L, ssd_w), BF16),
                   jax.ShapeDtypeStruct((nb, heads, SSD_HEAD_DIM, SSD_STATE), F32),
                   jax.ShapeDtypeStruct((nb, SSD_CONV - 1, cd), F32)],
        scratch_shapes=[pltpu.VMEM((T + 8, cd), F32)],
        compiler_params=_cparams(("arbitrary", "arbitrary")),
        name="ssd_scan",
    )(h2d, h2d, h2d, conv0, st0, conv_w, conv_b, dtbias_p, alog_p, dskip_x, norm_w, e_mat, et_mat)


def _rope_tables(pos, rot_dim, period, width):
    half = rot_dim // 2
    inv = ROPE_THETA ** (-jnp.arange(half, dtype=F32) * (2.0 / rot_dim))
    ang = pos.astype(F32)[:, None] * inv[None, :]
    cos, sin = jnp.cos(ang), jnp.sin(ang)
    lane = np.arange(LANES)
    inner = lane % period
    idx = jnp.asarray(inner % half)
    first = jnp.asarray((inner < half) & (lane < width))
    second = jnp.asarray((inner >= half) & (inner < rot_dim) & (lane < width))
    keep = jnp.asarray((inner >= rot_dim) & (lane < width))
    cos_l, sin_l = cos[:, idx], sin[:, idx]
    cos_t = jnp.where(first | second, cos_l, jnp.where(keep, 1.0, 0.0))
    sin_a = jnp.where(first, -sin_l, 0.0)
    sin_b = jnp.where(second, sin_l, 0.0)
    return cos_t.astype(F32), sin_a.astype(F32), sin_b.astype(F32)


def _rope_tile(x, cos_t, sin_a, sin_b, half):
    return (x * cos_t + pltpu.roll(x, LANES - half, 1) * sin_a + pltpu.roll(x, half, 1) * sin_b)


def _rmsnorm(x, w):
    return x * lax.rsqrt(jnp.mean(x * x, axis=-1, keepdims=True) + 1e-6) * w


def _prep_kernel(dq_ref, dk_ref, dv_ref, cq_ref, kr_ref, cd_ref, sad_ref, sbd_ref, cm_ref, sam_ref, sbm_ref,
                 kvw_ref, *rest, per_head, n_alias):
    if per_head:
        wuk_ref, rest = rest[0], rest[1:]
    rest = rest[n_alias:]
    if per_head:
        k32_ref, v32_ref, lat_ref, kro_ref, qd_ref, kb_ref, vb_ref, latb_ref, kt_ref = rest
    else:
        k32_ref, v32_ref, lat_ref, kro_ref, qd_ref, kb_ref, vb_ref, kcat_ref = rest
    cos_d, sa_d, sb_d = cd_ref[...], sad_ref[...], sbd_ref[...]
    width = dq_ref.shape[1]
    for c in range(width // LANES):
        sl = slice(c * LANES, (c + 1) * LANES)
        q = _rope_tile(dq_ref[:, sl], cos_d, sa_d, sb_d, DIFF_ROT // 2)
        qd_ref[0, :, sl] = (q * (DIFF_SCALE * LOG2E)).astype(BF16)
        k = _rope_tile(dk_ref[:, sl], cos_d, sa_d, sb_d, DIFF_ROT // 2)
        k32_ref[0, :, sl] = k
        kb_ref[0, :, sl] = k.astype(BF16)
    v = dv_ref[...]
    v32_ref[0] = v
    vb_ref[0] = v.astype(BF16)
    lat = _rmsnorm(cq_ref[:, MLA_Q_RANK:MLA_Q_RANK + MLA_KV_RANK], kvw_ref[...])
    lat_ref[0] = lat
    kr = _rope_tile(kr_ref[...], cm_ref[...], sam_ref[...], sbm_ref[...], MLA_ROPE_DIM // 2)
    kro_ref[0] = kr[:, :MLA_ROPE_DIM]
    lat_b = lat.astype(BF16)
    if per_head:
        latb_ref[0] = lat_b
        k_nope = jnp.dot(lat_b, wuk_ref[...], preferred_element_type=F32)
        for h in range(kt_ref.shape[1]):
            k_h = jnp.concatenate([k_nope[:, h * MLA_NOPE_DIM:(h + 1) * MLA_NOPE_DIM], kr], axis=1)
            kt_ref[0, h, 0] = k_h.T.astype(BF16)
    else:
        kcat_ref[0, :, :MLA_KV_RANK] = lat_b
        kcat_ref[0, :, MLA_KV_RANK:] = kr.astype(BF16)


def _prep_call(h2d, tabs_d, tabs_m, kv_w, cols, nb, L, tm, layer, state_bufs, wuk_flat=None):
    nt = L // tm
    dw = cols.diff_w
    per_head = wuk_flat is not None
    row = lambda b, i: b * nt + i
    hspec = lambda width, off: pl.BlockSpec((tm, width), lambda b, i: (row(b, i), off // width))
    tspec = pl.BlockSpec((tm, LANES), lambda b, i: (i, 0))
    ospec = lambda width: pl.BlockSpec((1, tm, width), lambda b, i: (b, i, 0))
    sspec = lambda width: pl.BlockSpec((None, 1, tm, width), lambda b, i: (layer, b, i, 0))
    in_specs = [hspec(dw, cols.dq), hspec(dw, cols.dk), hspec(dw, cols.dv),
                hspec(MLA_Q_RANK + MLA_KV_RANK, cols.cq), hspec(LANES, cols.krdt),
                tspec, tspec, tspec, tspec, tspec, tspec,
                pl.BlockSpec((1, MLA_KV_RANK), lambda b, i: (0, 0))]
    args = [h2d, h2d, h2d, h2d, h2d, *tabs_d, *tabs_m, kv_w]
    if per_head:
        in_specs.append(pl.BlockSpec(wuk_flat.shape, lambda b, i: (0, 0)))
        args.append(wuk_flat)
    aliases = {len(args) + n: n for n in range(len(state_bufs))}
    in_specs += [pl.BlockSpec(memory_space=pl.ANY)] * len(state_bufs)
    args += list(state_bufs)
    out_specs = [sspec(dw), sspec(dw), sspec(MLA_KV_RANK), sspec(MLA_ROPE_DIM), ospec(dw), ospec(dw), ospec(dw)]
    out_shape = [jax.ShapeDtypeStruct(b.shape, b.dtype) for b in state_bufs]
    out_shape += [jax.ShapeDtypeStruct((nb, L, dw), BF16)] * 3
    if per_head:
        heads = wuk_flat.shape[1] // MLA_NOPE_DIM
        out_specs += [ospec(MLA_KV_RANK),
                      pl.BlockSpec((1, heads, 1, MLA_QH_DIM, tm), lambda b, i: (b, 0, i, 0, 0))]
        out_shape += [jax.ShapeDtypeStruct((nb, L, MLA_KV_RANK), BF16),
                      jax.ShapeDtypeStruct((nb, heads, nt, MLA_QH_DIM, tm), BF16)]
    else:
        out_specs.append(ospec(MLA_QK_PAD))
        out_shape.append(jax.ShapeDtypeStruct((nb, L, MLA_QK_PAD), BF16))
    return pl.pallas_call(
        functools.partial(_prep_kernel, per_head=per_head, n_alias=len(state_bufs)),
        grid=(nb, nt),
        in_specs=in_specs, out_specs=out_specs, out_shape=out_shape,
        input_output_aliases=aliases,
        compiler_params=_cparams(("arbitrary", "arbitrary")),
        name="attn_prep",
    )(*args)


def _mlaq_kernel(cq_ref, qw_ref, wuq_ref, wuk_ref, cm_ref, sam_ref, sbm_ref, o_ref, *, heads, absorb):
    cqn = _rmsnorm(cq_ref[:, :MLA_Q_RANK], qw_ref[...]).astype(BF16)
    qm = jnp.dot(cqn, wuq_ref[...], preferred_element_type=F32)
    cos_m, sa_m, sb_m = cm_ref[...], sam_ref[...], sbm_ref[...]
    nope_w = heads * MLA_NOPE_DIM
    scale = MLA_SCALE * LOG2E
    for h in range(heads):
        nope = qm[:, h * MLA_NOPE_DIM:(h + 1) * MLA_NOPE_DIM]
        qr = _rope_tile(qm[:, nope_w + h * LANES:nope_w + (h + 1) * LANES], cos_m, sa_m, sb_m, MLA_ROPE_DIM // 2)
        if absorb:
            ql = jnp.dot(nope.astype(BF16), wuk_ref[h], preferred_element_type=F32)
        else:
            ql = nope
        width = ql.shape[1]
        o_ref[0, h, :, :width] = (ql * scale).astype(BF16)
        o_ref[0, h, :, width:] = (qr * scale).astype(BF16)


def _mlaq_call(h2d, q_w, wuq_p, wuk_t, tabs_m, cols, nb, L, heads, absorb):
    tm = min(L, 256)
    nt = L // tm
    width = MLA_Q_RANK + MLA_KV_RANK
    qdim = MLA_QK_PAD if absorb else MLA_QH_DIM
    tspec = pl.BlockSpec((tm, LANES), lambda b, i: (i, 0))
    return pl.pallas_call(
        functools.partial(_mlaq_kernel, heads=heads, absorb=absorb),
        grid=(nb, nt),
        in_specs=[pl.BlockSpec((tm, width), lambda b, i: (b * nt + i, cols.cq // width)),
                  pl.BlockSpec((1, MLA_Q_RANK), lambda b, i: (0, 0)),
                  pl.BlockSpec(wuq_p.shape, lambda b, i: (0, 0)),
                  pl.BlockSpec(wuk_t.shape, lambda b, i: (0, 0, 0)),
                  tspec, tspec, tspec],
        out_specs=pl.BlockSpec((1, heads, tm, qdim), lambda b, i: (b, 0, i, 0)),
        out_shape=jax.ShapeDtypeStruct((nb, heads, L, qdim), BF16),
        compiler_params=_cparams(("arbitrary", "arbitrary")),
        name="mla_q",
    )(h2d, q_w, wuq_p, wuk_t, *tabs_m)


def _softmax_init(m_ref, l_ref, acc_ref):
    m_ref[...] = jnp.full(m_ref.shape, -jnp.inf, F32)
    l_ref[...] = jnp.zeros(l_ref.shape, F32)
    acc_ref[...] = jnp.zeros(acc_ref.shape, F32)


def _softmax_update(s, v, m_ref, l_ref, acc_ref):
    m_prev = m_ref[...]
    m_new = jnp.maximum(m_prev, jnp.max(s, axis=-1, keepdims=True))
    alpha = jnp.exp2(m_prev - m_new)
    p = jnp.exp2(s - m_new)
    l_ref[...] = alpha * l_ref[...] + jnp.sum(p, axis=-1, keepdims=True)
    acc_ref[...] = alpha * acc_ref[...] + jnp.dot(p.astype(BF16), v, preferred_element_type=F32)
    m_ref[...] = m_new


def _diag_mask(rows, tq, tk, q0):
    q_tok = q0 + (lax.broadcasted_iota(jnp.int32, (rows, tk), 0) & (tq - 1))
    k_tok = lax.broadcasted_iota(jnp.int32, (rows, tk), 1)
    return (k_tok // CHUNK) <= (q_tok // CHUNK)


def _causal_sweep(n_full, qk, upd, finish, sa_ref, sb_ref):
    qk(0, sa_ref)

    def pair(jj, carry):
        j = 2 * jj
        qk(j + 1, sb_ref)
        upd(j, sa_ref)
        qk(j + 2, sa_ref)
        upd(j + 1, sb_ref)
        return carry

    lax.fori_loop(0, n_full // 2, pair, 0)

    @pl.when(n_full % 2 == 1)
    def _():
        qk(n_full, sb_ref)
        upd(n_full - 1, sa_ref)
        finish(sb_ref)

    @pl.when(n_full % 2 == 0)
    def _():
        finish(sa_ref)


def _softmax_init_wide(m_ref, acc_ref):
    m_ref[...] = jnp.full(m_ref.shape, -jnp.inf, F32)
    acc_ref[...] = jnp.zeros(acc_ref.shape, F32)


def _with_ones(v):
    return jnp.concatenate([v, jnp.ones((v.shape[0], LANES), BF16)], axis=1)


def _softmax_update_wide(s, pv_fn, m_ref, acc_ref):
    tk = s.shape[1]
    w = min(tk, LANES)
    m_prev = m_ref[...]
    m_new = jnp.maximum(m_prev, jnp.max(s, axis=-1, keepdims=True))
    alpha = jnp.exp2(m_prev - m_new)
    p = jnp.concatenate([jnp.exp2(s[:, c:c + w] - m_new[:, :w]).astype(BF16) for c in range(0, tk, w)], axis=1)
    acc = acc_ref[...]
    acc_ref[...] = jnp.concatenate([alpha] * (acc.shape[1] // LANES), axis=1) * acc + pv_fn(p)
    m_ref[...] = m_new


def _softmax_update_rep(s, v, m_ref, l_ref, acc_ref):
    tk = s.shape[1]
    m_prev = m_ref[...]
    m_new = jnp.maximum(m_prev, jnp.max(s, axis=-1, keepdims=True))
    alpha = jnp.exp2(m_prev - m_new)
    ps = [jnp.exp2(s[:, c:c + LANES] - m_new) for c in range(0, tk, LANES)]
    l_ref[...] = alpha * l_ref[...] + sum(ps[1:], ps[0])
    p = jnp.concatenate([x.astype(BF16) for x in ps], axis=1)
    acc = acc_ref[...]
    acc_ref[...] = (jnp.concatenate([alpha] * (acc.shape[1] // LANES), axis=1) * acc
                    + jnp.dot(p, v, preferred_element_type=F32))
    m_ref[...] = m_new


def _diff_lambda(lam_ref, lam_init):
    s1 = jnp.sum(lam_ref[0:1, :] * lam_ref[1:2, :], axis=-1, keepdims=True)
    s2 = jnp.sum(lam_ref[2:3, :] * lam_ref[3:4, :], axis=-1, keepdims=True)
    return jnp.exp(s1) - jnp.exp(s2) + lam_init


def _diff_finish(o1, o2, lam, nw, gate, lam_init):
    o = o1 - lam * o2
    o = _rmsnorm(o, nw) * (1.0 - lam_init)
    return (o * _silu(gate)).astype(BF16)


def _stack_streams(q):
    lane = lax.broadcasted_iota(jnp.int32, q.shape, 1)
    zero = jnp.zeros_like(q)
    return jnp.concatenate([jnp.where(lane < DIFF_QK_DIM, q, zero), jnp.where(lane >= DIFF_QK_DIM, q, zero)], axis=0)


def _diff_prompt_kernel(q_ref, k_ref, v_ref, lam_ref, nw_ref, g_ref, o_ref,
                        qz_ref, sa_ref, sb_ref, m_ref, acc_ref, *, t, lam_init):
    qi = pl.program_id(2)
    qz_ref[...] = _stack_streams(q_ref[0])
    _softmax_init_wide(m_ref, acc_ref)

    def blk(ref, j):
        return ref[0, pl.ds(pl.multiple_of(j * t, t), t), :]

    def qk(j, s_ref):
        s_ref[...] = _nt_dot(qz_ref[...], blk(k_ref, j))

    def pv_fn(j):
        return lambda p: jnp.dot(p, _with_ones(blk(v_ref, j)), preferred_element_type=F32)

    def upd(j, s_ref):
        _softmax_update_wide(s_ref[...], pv_fn(j), m_ref, acc_ref)

    def finish(s_ref):
        s = jnp.where(_diag_mask(2 * t, t, t, 0), s_ref[...], NEG_INF)
        _softmax_update_wide(s, pv_fn(qi), m_ref, acc_ref)
        o = acc_ref[:, :DIFF_V_DIM] / acc_ref[:, DIFF_V_DIM:]
        lam = _diff_lambda(lam_ref, lam_init)
        o_ref[0] = _diff_finish(o[:t], o[t:], lam, nw_ref[...], g_ref[...], lam_init)

    _causal_sweep(qi, qk, upd, finish, sa_ref, sb_ref)


def _diff_prompt_call(qd, kb, vb, lam_p, norm_w, h2d, cols, lam_init, t):
    nb, L, dw = qd.shape
    heads = dw // DIFF_V_DIM
    nq = L // t
    kern = functools.partial(_diff_prompt_kernel, t=t, lam_init=lam_init)
    seq = pl.BlockSpec((1, L, LANES), lambda b, h, i: (b, 0, h))
    tile = pl.BlockSpec((1, t, LANES), lambda b, h, i: (b, i, h))
    return pl.pallas_call(
        kern,
        grid=(nb, heads, nq),
        in_specs=[tile, seq, seq,
                  pl.BlockSpec((4, DIFF_QK_DIM), lambda b, h, i: (0, 0)),
                  pl.BlockSpec((1, DIFF_V_DIM), lambda b, h, i: (0, 0)),
                  pl.BlockSpec((t, LANES), lambda b, h, i: (b * nq + i, cols.dgate // LANES + h))],
        out_specs=tile,
        out_shape=jax.ShapeDtypeStruct((nb, L, dw), BF16),
        scratch_shapes=[pltpu.VMEM((2 * t, LANES), BF16),
                        pltpu.VMEM((2 * t, t), F32),
                        pltpu.VMEM((2 * t, t), F32),
                        pltpu.VMEM((2 * t, LANES), F32),
                        pltpu.VMEM((2 * t, DIFF_V_DIM + LANES), F32)],
        compiler_params=_cparams(("arbitrary", "arbitrary", "arbitrary")),
        name="diff_attn_prompt",
    )(qd, kb, vb, lam_p, norm_w, h2d)


def _diff_sample_kernel(q_ref, kc_ref, vc_ref, kn_ref, vn_ref, lam_ref, nw_ref, g_ref, o_ref,
                        qz_ref, m_ref, acc_ref, *, nkc, heads, lam_init):
    j = pl.program_id(1)
    L = q_ref.shape[1]
    rows = 2 * L

    @pl.when(j == 0)
    def _():
        for h in range(heads):
            qz_ref[h] = _stack_streams(q_ref[0, :, h * LANES:(h + 1) * LANES])
        _softmax_init_wide(m_ref, acc_ref)

    def step(scores, values):
        def pv_fn(p):
            return jnp.concatenate([jnp.dot(p[h * rows:(h + 1) * rows], _with_ones(values(h)),
                                            preferred_element_type=F32) for h in range(heads)], axis=0)
        s = jnp.concatenate([scores(h) for h in range(heads)], axis=0)
        _softmax_update_wide(s, pv_fn, m_ref, acc_ref)

    @pl.when(j < nkc)
    def _():
        tk = kc_ref.shape[5]

        def scores(h):
            return jnp.dot(qz_ref[h], kc_ref[0, 0, h].reshape(2 * DIFF_QK_DIM, tk).astype(BF16),
                           preferred_element_type=F32)

        def values(h):
            return vc_ref[0, 0, pl.ds(h, tk, stride=heads), :].astype(BF16)

        step(scores, values)

    @pl.when(j == nkc)
    def _():
        step(lambda h: _nt_dot(qz_ref[h], kn_ref[0, :, h * LANES:(h + 1) * LANES]),
             lambda h: vn_ref[0, :, h * LANES:(h + 1) * LANES])
        lam = _diff_lambda(lam_ref, lam_init)
        o = acc_ref[:, :DIFF_V_DIM] / acc_ref[:, DIFF_V_DIM:]
        for h in range(heads):
            sl = slice(h * LANES, (h + 1) * LANES)
            oh = o[h * rows:(h + 1) * rows]
            o_ref[0, :, sl] = _diff_finish(oh[:L], oh[L:], lam, nw_ref[...], g_ref[:, sl], lam_init)


def _diff_sample_call(qd, kb, vb, k_cache, v_cache, layer, lam_p, norm_w, h2d, cols, lam_init, tk):
    nb, L, dw = qd.shape
    heads = dw // DIFF_V_DIM
    P = k_cache.shape[5]
    nkc = P // tk
    last = nkc - 1
    kern = functools.partial(_diff_sample_kernel, nkc=nkc, heads=heads, lam_init=lam_init)
    kspec = pl.BlockSpec((1, 1, heads, 2, DIFF_QK_DIM, tk),
                         lambda b, j: (layer, b, 0, 0, 0, jnp.minimum(j, last)))
    vspec = pl.BlockSpec((1, 1, tk * heads, DIFF_V_DIM), lambda b, j: (layer, b, jnp.minimum(j, last), 0))
    nspec = pl.BlockSpec((1, L, dw), lambda b, j: (b, 0, 0))
    return pl.pallas_call(
        kern,
        grid=(nb, nkc + 1),
        in_specs=[nspec, kspec, vspec, nspec, nspec,
                  pl.BlockSpec((4, DIFF_QK_DIM), lambda b, j: (0, 0)),
                  pl.BlockSpec((1, DIFF_V_DIM), lambda b, j: (0, 0)),
                  pl.BlockSpec((L, dw), lambda b, j: (b, cols.dgate // dw))],
        out_specs=nspec,
        out_shape=jax.ShapeDtypeStruct((nb, L, dw), BF16),
        scratch_shapes=[pltpu.VMEM((heads, 2 * L, LANES), BF16),
                        pltpu.VMEM((heads * 2 * L, LANES), F32),
                        pltpu.VMEM((heads * 2 * L, DIFF_V_DIM + LANES), F32)],
        compiler_params=_cparams(("arbitrary", "arbitrary")),
        name="diff_attn_sample",
    )(qd, k_cache, v_cache, kb, vb, lam_p, norm_w, h2d)


def _mla_finish(acc_ref, l, wuv_ref, g_ref, o_ref, heads, t):
    o = (acc_ref[...] / l).astype(BF16)
    for h in range(heads):
        sl = slice(h * MLA_V_DIM, (h + 1) * MLA_V_DIM)
        om = jnp.dot(o[h * t:(h + 1) * t], wuv_ref[:, sl], preferred_element_type=F32)
        o_ref[0, :, sl] = (om * _silu(g_ref[:, sl])).astype(BF16)


def _mla_prompt_kernel(q_ref, kt_ref, v_ref, wuv_ref, g_ref, o_ref, sa_ref, sb_ref, m_ref, l_ref, acc_ref, *, t, hg):
    qi = pl.program_id(2)
    _softmax_init(m_ref, l_ref, acc_ref)

    def vblk(j):
        return v_ref[0, pl.ds(pl.multiple_of(j * t, t), t), :]

    def qk(j, s_ref):
        for g in range(hg):
            s_ref[g * t:(g + 1) * t, :] = jnp.dot(q_ref[0, g], kt_ref[0, g, j], preferred_element_type=F32)

    def upd(j, s_ref):
        _softmax_update_rep(s_ref[...], vblk(j), m_ref, l_ref, acc_ref)

    def finish(s_ref):
        s = jnp.where(_diag_mask(hg * t, t, t, 0), s_ref[...], NEG_INF)
        _softmax_update_rep(s, vblk(qi), m_ref, l_ref, acc_ref)
        _mla_finish(acc_ref, jnp.sum(l_ref[...], axis=-1, keepdims=True), wuv_ref, g_ref, o_ref, hg, t)

    _causal_sweep(qi, qk, upd, finish, sa_ref, sb_ref)


def _mla_prompt_call(qh, kt, latb, wuv, h2d, cols, t, hg):
    nb, heads, L, qdim = qh.shape
    mw = heads * MLA_V_DIM
    gw = hg * MLA_V_DIM
    nq = L // t
    return pl.pallas_call(
        functools.partial(_mla_prompt_kernel, t=t, hg=hg),
        grid=(nb, heads // hg, nq),
        in_specs=[pl.BlockSpec((1, hg, t, qdim), lambda b, h, i: (b, h, i, 0)),
                  pl.BlockSpec((1, hg, nq, qdim, t), lambda b, h, i: (b, h, 0, 0, 0)),
                  pl.BlockSpec((1, L, MLA_KV_RANK), lambda b, h, i: (b, 0, 0)),
                  pl.BlockSpec((MLA_KV_RANK, gw), lambda b, h, i: (0, h)),
                  pl.BlockSpec((t, gw), lambda b, h, i: (b * nq + i, cols.mgate // gw + h))],
        out_specs=pl.BlockSpec((1, t, gw), lambda b, h, i: (b, i, h)),
        out_shape=jax.ShapeDtypeStruct((nb, L, mw), BF16),
        scratch_shapes=[pltpu.VMEM((hg * t, t), F32),
                        pltpu.VMEM((hg * t, t), F32),
                        pltpu.VMEM((hg * t, LANES), F32),
                        pltpu.VMEM((hg * t, LANES), F32),
                        pltpu.VMEM((hg * t, MLA_KV_RANK), F32)],
        compiler_params=_cparams(("arbitrary", "arbitrary", "arbitrary")),
        name="mla_attn_prompt",
    )(qh, kt, latb, wuv, h2d)


def _mla_sample_kernel(q_ref, lc_ref, rc_ref, kn_ref, wuv_ref, g_ref, o_ref, m_ref, l_ref, acc_ref,
                       *, nkc, heads):
    j = pl.program_id(1)
    L = q_ref.shape[2]

    @pl.when(j == 0)
    def _():
        _softmax_init(m_ref, l_ref, acc_ref)

    q = q_ref[0].reshape(heads * L, MLA_QK_PAD)

    @pl.when(j < nkc)
    def _():
        lat = lc_ref[0, 0].astype(BF16)
        kr_t = rc_ref[0, 0].astype(BF16)
        s = _nt_dot(q[:, :MLA_KV_RANK], lat) + jnp.dot(q[:, MLA_KV_RANK:MLA_KV_RANK + MLA_ROPE_DIM], kr_t,
                                                       preferred_element_type=F32)
        _softmax_update(s, lat, m_ref, l_ref, acc_ref)

    @pl.when(j == nkc)
    def _():
        kn = kn_ref[0]
        _softmax_update(_nt_dot(q, kn), kn[:, :MLA_KV_RANK], m_ref, l_ref, acc_ref)
        _mla_finish(acc_ref, l_ref[...], wuv_ref, g_ref, o_ref, heads, L)


def _mla_sample_call(qcat, kcat, lat_cache, kr_cache, layer, wuv, h2d, cols, tk):
    nb, heads, L, _ = qcat.shape
    mw = heads * MLA_V_DIM
    P = lat_cache.shape[2]
    nkc = P // tk
    last = nkc - 1
    kern = functools.partial(_mla_sample_kernel, nkc=nkc, heads=heads)
    return pl.pallas_call(
        kern,
        grid=(nb, nkc + 1),
        in_specs=[pl.BlockSpec((1, heads, L, MLA_QK_PAD), lambda b, j: (b, 0, 0, 0)),
                  pl.BlockSpec((1, 1, tk, MLA_KV_RANK), lambda b, j: (layer, b, jnp.minimum(j, last), 0)),
                  pl.BlockSpec((1, 1, MLA_ROPE_DIM, tk), lambda b, j: (layer, b, 0, jnp.minimum(j, last))),
                  pl.BlockSpec((1, L, MLA_QK_PAD), lambda b, j: (b, 0, 0)),
                  pl.BlockSpec(wuv.shape, lambda b, j: (0, 0)),
                  pl.BlockSpec((L, mw), lambda b, j: (b, cols.mgate // mw))],
        out_specs=pl.BlockSpec((1, L, mw), lambda b, j: (b, 0, 0)),
        out_shape=jax.ShapeDtypeStruct((nb, L, mw), BF16),
        scratch_shapes=[pltpu.VMEM((heads * L, 1), F32),
                        pltpu.VMEM((heads * L, 1), F32),
                        pltpu.VMEM((heads * L, MLA_KV_RANK), F32)],
        compiler_params=_cparams(("arbitrary", "arbitrary")),
        name="mla_attn_sample",
    )(qcat, lat_cache, kr_cache, kcat, wuv, h2d)


def _outproj_kernel(ys_ref, yd_ref, ym_ref, w_ref, x_ref, g_ref, r_ref, *, alpha):
    bt, lt, _ = ys_ref.shape
    flat = lambda ref: ref[...].reshape(bt * lt, ref.shape[2])
    mix = jnp.concatenate([flat(ys_ref), flat(yd_ref), flat(ym_ref)], axis=1)
    acc = jnp.dot(mix, w_ref[...], preferred_element_type=F32)
    r_ref[...] = alpha * x_ref[...] + g_ref[...] * acc.reshape(bt, lt, acc.shape[1])


def _outproj_call(y_ssd, y_diff, y_mla, w_out, x, gate, alpha):
    nb, L, d = x.shape
    bt, lt = _seq_tiles(nb, L, 512)
    tn = 1024
    yspec = lambda width: pl.BlockSpec((bt, lt, width), lambda j, b, i: (b, i, 0))
    return pl.pallas_call(
        functools.partial(_outproj_kernel, alpha=alpha),
        grid=(d // tn, nb // bt, L // lt),
        in_specs=[yspec(y_ssd.shape[2]), yspec(y_diff.shape[2]), yspec(y_mla.shape[2]),
                  pl.BlockSpec((w_out.shape[0], tn), lambda j, b, i: (0, j)),
                  pl.BlockSpec((bt, lt, tn), lambda j, b, i: (b, i, j)),
                  pl.BlockSpec((bt, 1, tn), lambda j, b, i: (b, 0, j))],
        out_specs=pl.BlockSpec((bt, lt, tn), lambda j, b, i: (b, i, j)),
        out_shape=jax.ShapeDtypeStruct((nb, L, d), F32),
        compiler_params=_cparams(("arbitrary", "arbitrary", "arbitrary")),
        name="out_proj",
    )(y_ssd, y_diff, y_mla, w_out, x, gate)


def _layernorm(r, g, b):
    mu = jnp.mean(r, axis=-1, keepdims=True)
    var = jnp.mean(jnp.square(r - mu), axis=-1, keepdims=True)
    return (r - mu) * lax.rsqrt(var + 1e-5) * g + b


def _ln_kernel(r_ref, g_ref, b_ref, x_ref):
    x_ref[...] = _layernorm(r_ref[...], g_ref[...], b_ref[...])


def _ln_mod_kernel(r_ref, g_ref, b_ref, sc_ref, sh_ref, x_ref, u_ref):
    x = _layernorm(r_ref[...], g_ref[...], b_ref[...])
    x_ref[...] = x
    u_ref[...] = (x * (1.0 + sc_ref[...]) + sh_ref[...]).astype(BF16)


def _ln_call(r, g, b, scale=None, shift=None):
    nb, L, d = r.shape
    bt, lt = _seq_tiles(nb, L, 256)
    xspec = pl.BlockSpec((bt, lt, d), lambda i, j: (i, j, 0))
    wspec = pl.BlockSpec((1, 1, d), lambda i, j: (0, 0, 0))
    sspec = pl.BlockSpec((bt, 1, d), lambda i, j: (i, 0, 0))
    g3, b3 = g.reshape(1, 1, d), b.reshape(1, 1, d)
    if scale is None:
        return pl.pallas_call(
            _ln_kernel, grid=(nb // bt, L // lt),
            in_specs=[xspec, wspec, wspec], out_specs=xspec,
            out_shape=jax.ShapeDtypeStruct((nb, L, d), F32),
            compiler_params=_cparams(("arbitrary", "arbitrary")), name="layernorm",
        )(r, g3, b3), None
    return pl.pallas_call(
        _ln_mod_kernel, grid=(nb // bt, L // lt),
        in_specs=[xspec, wspec, wspec, sspec, sspec], out_specs=[xspec, xspec],
        out_shape=[jax.ShapeDtypeStruct((nb, L, d), F32), jax.ShapeDtypeStruct((nb, L, d), BF16)],
        compiler_params=_cparams(("arbitrary", "arbitrary")), name="layernorm_modulate",
    )(r, g3, b3, scale, shift)


def _ssd_chunk(L):
    for t in (128, 64, 32, 16, 8):
        if L % t == 0:
            return t
    raise ValueError(f"sequence length {L} is not a multiple of 8")


def _attn_tile(L, want):
    t = want
    while L % t:
        t //= 2
    return t


def _layer(x, u, mod_l, next_mod, wl, cols, layer_idx, depth, caches, state_bufs):
    nb, L, d = x.shape
    heads = cols.heads
    mla_heads = cols.mla_w // MLA_V_DIM
    _, _, gate = mod_l
    k_cache, v_cache, lat_cache, kr_cache, st0, conv0 = caches
    P = 0 if k_cache is None else lat_cache.shape[2]
    pos = P + jnp.arange(L, dtype=jnp.int32)

    h2d = _inproj_call(u.reshape(nb * L, d), wl["w_in"], layer_idx, cols.tn)

    T = _ssd_chunk(L)
    y_ssd, ssm_new, conv_new = _ssd_call(h2d, conv0, st0, wl["conv_w"], wl["conv_b"], wl["dt_bias"], wl["a_log"],
                                         wl["d_skip"], wl["ssd_norm_w"], wl["e_mat"], wl["et_mat"], cols, nb, L, T)

    tabs_d = _rope_tables(pos, DIFF_ROT, DIFF_QK_DIM, LANES)
    tabs_m = _rope_tables(pos, MLA_ROPE_DIM, MLA_ROPE_DIM, MLA_ROPE_DIM)
    lam_init = 0.8 - 0.6 * math.exp(-0.3 * layer_idx)
    if k_cache is None:
        t = _attn_tile(L, 512)
        k32, v32, lat32, kr32, qd, kb, vb, latb, kt = _prep_call(h2d, tabs_d, tabs_m, wl["kv_norm_w"], cols, nb, L, t,
                                                                 layer_idx, state_bufs, wl["w_uk_flat"])
        qh = _mlaq_call(h2d, wl["q_norm_w"], wl["w_uq"], wl["w_uk"], tabs_m, cols, nb, L, mla_heads, False)
        y_diff = _diff_prompt_call(qd, kb, vb, wl["lam"], wl["diff_norm_w"], h2d, cols, lam_init, t)
        y_mla = _mla_prompt_call(qh, kt, latb, wl["w_uv"], h2d, cols, t, MLA_HEAD_GROUP)
    else:
        k32, v32, lat32, kr32, qd, kb, vb, kcat = _prep_call(h2d, tabs_d, tabs_m, wl["kv_norm_w"], cols, nb, L,
                                                             min(L, 256), layer_idx, state_bufs)
        qcat = _mlaq_call(h2d, wl["q_norm_w"], wl["w_uq"], wl["w_uk"], tabs_m, cols, nb, L, mla_heads, True)
        y_diff = _diff_sample_call(qd, kb, vb, k_cache, v_cache, layer_idx, wl["lam"], wl["diff_norm_w"], h2d,
                                   cols, lam_init, _attn_tile(P, 1024))
        y_mla = _mla_sample_call(qcat, kcat, lat_cache, kr_cache, layer_idx, wl["w_uv"], h2d, cols,
                                 _attn_tile(P, 2048))

    alpha = (2 * depth) ** 0.25
    r = _outproj_call(y_ssd.reshape(nb, L, cols.ssd_w), y_diff, y_mla, wl["w_out"], x, gate, alpha)
    if next_mod is None:
        x_new, u_new = _ln_call(r, wl["ln_g"], wl["ln_b"])
    else:
        x_new, u_new = _ln_call(r, wl["ln_g"], wl["ln_b"], next_mod[1], next_mod[0])
    return x_new, u_new, (k32, v32, lat32, kr32), (ssm_new, conv_new)


def _layer_weights(l, cols, w_in, conv_w, conv_b, dt_bias, a_log, d_skip, ssd_norm_w, lambda_q1, lambda_k1,
                   lambda_q2, lambda_k2, diff_norm_w, mla_q_norm_w, mla_kv_norm_w, w_uq, w_uk, w_uv, w_out,
                   ln_g, ln_b):
    heads = cols.heads
    mla_heads = cols.mla_w // MLA_V_DIM
    lane_pad = lambda v: jnp.pad(v, (MLA_ROPE_DIM, LANES - MLA_ROPE_DIM - heads)).reshape(1, LANES)
    qk = MLA_NOPE_DIM + MLA_ROPE_DIM
    wq = w_uq[l].reshape(MLA_Q_RANK, mla_heads, qk)
    wq_nope = wq[:, :, :MLA_NOPE_DIM].reshape(MLA_Q_RANK, mla_heads * MLA_NOPE_DIM)
    wq_rope = jnp.pad(wq[:, :, MLA_NOPE_DIM:], ((0, 0), (0, 0), (0, LANES - MLA_ROPE_DIM)))
    wq_p = jnp.concatenate([wq_nope, wq_rope.reshape(MLA_Q_RANK, mla_heads * LANES)], axis=1).astype(BF16)
    eh = np.zeros((LANES, cols.ssd_w), np.float32)
    for h in range(heads):
        eh[MLA_ROPE_DIM + h, h * SSD_HEAD_DIM:(h + 1) * SSD_HEAD_DIM] = 1.0
    return dict(
        w_in=w_in,
        conv_w=conv_w[l], conv_b=conv_b[l].reshape(1, -1),
        dt_bias=lane_pad(dt_bias[l]), a_log=lane_pad(a_log[l]),
        d_skip=jnp.repeat(d_skip[l], SSD_HEAD_DIM).reshape(1, -1),
        ssd_norm_w=ssd_norm_w[l].reshape(1, -1),
        e_mat=jnp.asarray(eh, BF16), et_mat=jnp.asarray(eh.T, BF16),
        lam=jnp.stack([lambda_q1[l], lambda_k1[l], lambda_q2[l], lambda_k2[l]]),
        diff_norm_w=diff_norm_w[l].reshape(1, -1),
        q_norm_w=mla_q_norm_w[l].reshape(1, -1), kv_norm_w=mla_kv_norm_w[l].reshape(1, -1),
        w_uq=wq_p,
        w_uk=jnp.transpose(w_uk[l], (1, 2, 0)).astype(BF16),
        w_uk_flat=w_uk[l].reshape(MLA_KV_RANK, mla_heads * MLA_NOPE_DIM).astype(BF16),
        w_uv=w_uv[l].reshape(MLA_KV_RANK, mla_heads * MLA_V_DIM).astype(BF16),
        w_out=w_out[l].astype(BF16),
        ln_g=ln_g[l], ln_b=ln_b[l],
    )


def kernel(x_prompt, x_sample, cache_diff_k, cache_diff_v, cache_mla_latent, cache_mla_krope, state_ssm, state_conv,
           c_prompt, c_sample, w_mod, b_mod, w_in, conv_w, conv_b, dt_bias, a_log, d_skip, ssd_norm_w, lambda_q1,
           lambda_k1, lambda_q2, lambda_k2, diff_norm_w, mla_q_norm_w, mla_kv_norm_w, w_uq, w_uk, w_uv, w_out,
           ln_g, ln_b):
    depth = w_in.shape[0]
    bp, _, d = x_prompt.shape
    bs = x_sample.shape[0]
    cols = _Cols(d)
    heads = cols.heads

    rows = -(-(bp + bs) // 8) * 8
    c_all = jnp.concatenate([c_prompt, c_sample, jnp.zeros((rows - bp - bs, d), F32)], axis=0)
    mod = _mod_call(c_all, w_mod, b_mod)

    def mods(l, lo, n):
        part = lambda k: mod[l, lo:lo + n, k * d:(k + 1) * d].reshape(n, 1, d)
        return part(0), part(1), part(2)

    pdiff = cache_diff_k.shape[2]
    kc = jnp.transpose(cache_diff_k, (0, 1, 3, 4, 5, 2))
    vc = cache_diff_v.reshape(depth, bs, pdiff * cache_diff_v.shape[3], DIFF_V_DIM)
    krc = jnp.transpose(cache_mla_krope, (0, 1, 3, 2))
    zero_state = jnp.zeros((bp, heads, SSD_HEAD_DIM, SSD_STATE), F32)
    zero_conv = jnp.zeros((bp, SSD_CONV - 1, cols.conv_dim), F32)

    w_in_p = _permute_w_in(w_in, cols)
    hp, hs = x_prompt, x_sample
    mp, ms = mods(0, 0, bp), mods(0, bp, bs)
    up = _modulate_call(hp, mp[1], mp[0])
    us = _modulate_call(hs, ms[1], ms[0])

    def state_buffers(nb, L):
        return tuple(jnp.zeros((depth, nb, L, w), F32) for w in (cols.diff_w, cols.diff_w, MLA_KV_RANK, MLA_ROPE_DIM))

    bufs_p, bufs_s = state_buffers(bp, hp.shape[1]), state_buffers(bs, hs.shape[1])
    rec_p, rec_s = [], []
    for l in range(depth):
        wl = _layer_weights(l, cols, w_in_p, conv_w, conv_b, dt_bias, a_log, d_skip, ssd_norm_w, lambda_q1, lambda_k1,
                            lambda_q2, lambda_k2, diff_norm_w, mla_q_norm_w, mla_kv_norm_w, w_uq, w_uk, w_uv, w_out,
                            ln_g, ln_b)
        nmp = mods(l + 1, 0, bp) if l + 1 < depth else None
        nms = mods(l + 1, bp, bs) if l + 1 < depth else None
        hp, up, bufs_p, rp = _layer(hp, up, mp, nmp, wl, cols, l, depth,
                                    (None, None, None, None, zero_state, zero_conv), bufs_p)
        hs, us, bufs_s, rs = _layer(hs, us, ms, nms, wl, cols, l, depth,
                                    (kc, vc, cache_mla_latent, krc, state_ssm[l], state_conv[l]), bufs_s)
        rec_p.append(rp)
        rec_s.append(rs)
        mp, ms = nmp, nms

    def states(bufs, rec):
        k32, v32, lat32, kr32 = bufs
        nb, L = k32.shape[1], k32.shape[2]
        return (k32.reshape(depth, nb, L, cols.diff_w // (2 * DIFF_QK_DIM), 2, DIFF_QK_DIM),
                v32.reshape(depth, nb, L, cols.diff_w // DIFF_V_DIM, DIFF_V_DIM), lat32, kr32,
                jnp.stack([r[0] for r in rec]), jnp.stack([r[1] for r in rec]))

    return (hp, hs) + states(bufs_p, rec_p) + states(bufs_s, rec_s)
```

```python
import functools
import math

import numpy as np
import jax
import jax.numpy as jnp
from jax import lax
from jax.experimental import pallas as pl
from jax.experimental.pallas import tpu as pltpu

F32 = jnp.float32
BF16 = jnp.bfloat16

CHUNK = 64
ROPE_THETA = 500000.0
NEG_INF = -1e30
SSD_HEAD_DIM = 64
SSD_GROUPS = 4
SSD_STATE = 128
SSD_CONV = 4
DIFF_QK_DIM = 64
DIFF_V_DIM = 128
DIFF_ROT = DIFF_QK_DIM // 4
DIFF_SCALE = DIFF_QK_DIM ** -0.5
MLA_V_DIM = 128
MLA_NOPE_DIM = 128
MLA_ROPE_DIM = 64
MLA_Q_RANK = 768
MLA_KV_RANK = 256
MLA_SCALE = (MLA_NOPE_DIM + MLA_ROPE_DIM) ** -0.5
MLA_QK_PAD = MLA_KV_RANK + 128
MLA_QH_DIM = MLA_NOPE_DIM + 128
MLA_HEAD_GROUP = 2
LOG2E = math.log2(math.e)

LANES = 128
VMEM_LIMIT = 56 * 1024 * 1024


def _cparams(sem):
    return pltpu.CompilerParams(dimension_semantics=sem, vmem_limit_bytes=VMEM_LIMIT)


def _silu(x):
    hx = 0.5 * x
    return hx + hx * jnp.tanh(hx)


def _nt_dot(a, b):
    return lax.dot_general(a, b, (((1,), (1,)), ((), ())), preferred_element_type=F32)


def _split2(v):
    hi = v.astype(BF16)
    lo = (v - hi.astype(F32)).astype(BF16)
    return hi, lo


def _split3(v):
    hi = v.astype(BF16)
    r = v - hi.astype(F32)
    mid = r.astype(BF16)
    lo = (r - mid.astype(F32)).astype(BF16)
    return hi, mid, lo


def _mod_kernel(c_ref, w_ref, b_ref, o_ref):
    a = _silu(c_ref[...]).astype(BF16)
    o_ref[0] = jnp.dot(a, w_ref[0].astype(BF16), preferred_element_type=F32) + b_ref[0]


def _mod_call(c_all, w_mod, b_mod):
    depth, d, n3 = w_mod.shape
    rows = c_all.shape[0]
    tn = 512
    return pl.pallas_call(
        _mod_kernel,
        grid=(depth, n3 // tn),
        in_specs=[pl.BlockSpec((rows, d), lambda l, j: (0, 0)),
                  pl.BlockSpec((1, d, tn), lambda l, j: (l, 0, j)),
                  pl.BlockSpec((1, 1, tn), lambda l, j: (l, 0, j))],
        out_specs=pl.BlockSpec((1, rows, tn), lambda l, j: (l, 0, j)),
        out_shape=jax.ShapeDtypeStruct((depth, rows, n3), F32),
        compiler_params=_cparams(("arbitrary", "arbitrary")),
        name="adaln_mod",
    )(c_all, w_mod, b_mod.reshape(depth, 1, n3))


def _modulate_kernel(x_ref, sc_ref, sh_ref, u_ref):
    u_ref[...] = (x_ref[...] * (1.0 + sc_ref[...]) + sh_ref[...]).astype(BF16)


def _seq_tiles(nb, L, rows):
    if L >= rows:
        return 1, rows
    return min(nb, rows // L), L


def _modulate_call(x, scale, shift):
    nb, L, d = x.shape
    bt, lt = _seq_tiles(nb, L, 256)
    return pl.pallas_call(
        _modulate_kernel,
        grid=(nb // bt, L // lt),
        in_specs=[pl.BlockSpec((bt, lt, d), lambda i, j: (i, j, 0)),
                  pl.BlockSpec((bt, 1, d), lambda i, j: (i, 0, 0)),
                  pl.BlockSpec((bt, 1, d), lambda i, j: (i, 0, 0))],
        out_specs=pl.BlockSpec((bt, lt, d), lambda i, j: (i, j, 0)),
        out_shape=jax.ShapeDtypeStruct((nb, L, d), BF16),
        compiler_params=_cparams(("arbitrary", "arbitrary")),
        name="modulate",
    )(x, scale, shift)


class _Cols:
    def __init__(self, d_model):
        self.ssd_w = d_model // 2
        self.diff_w = d_model // 4
        self.mla_w = d_model // 4
        self.heads = self.ssd_w // SSD_HEAD_DIM
        self.conv_dim = self.ssd_w + 2 * SSD_GROUPS * SSD_STATE
        self.in_sizes = (self.ssd_w, self.conv_dim, self.heads, self.diff_w, self.diff_w, self.diff_w,
                         self.diff_w, MLA_Q_RANK, MLA_KV_RANK, MLA_ROPE_DIM, self.mla_w)
        o = 0
        self.xbc = o; o += self.conv_dim
        self.dq = o; o += self.diff_w
        self.z = o; o += self.ssd_w
        self.dk = o; o += self.diff_w
        self.dv = o; o += self.diff_w
        self.dgate = o; o += self.diff_w
        self.mgate = o; o += self.mla_w
        self.cq = o; o += MLA_Q_RANK + MLA_KV_RANK
        self.krdt = o; o += LANES
        self.used = o
        self.tn = 1280
        self.total = -(-o // self.tn) * self.tn


def _w_in_blocks(cols):
    offs = np.concatenate([[0], np.cumsum(cols.in_sizes)])
    order = (1, 3, 0, 4, 5, 6, 10)
    rows = []
    for i in order:
        rows += list(range(int(offs[i]), int(offs[i + 1]), LANES))
    rows += list(range(int(offs[7]), int(offs[9]), LANES))
    rows.append(-1)
    rows += [-2] * ((cols.total - cols.used) // LANES)
    return np.asarray(rows, np.int32)


def _wprep_kernel(tab_ref, w_ref, sp_ref, o_ref):
    src = tab_ref[pl.program_id(1)]

    @pl.when(src >= 0)
    def _():
        o_ref[0] = w_ref[0].astype(BF16)

    @pl.when(src == -1)
    def _():
        o_ref[0] = sp_ref[0].astype(BF16)

    @pl.when(src == -2)
    def _():
        o_ref[0] = jnp.zeros(o_ref.shape[1:], BF16)


def _permute_w_in(w_in, cols):
    depth, d, _ = w_in.shape
    wt = jnp.transpose(w_in, (0, 2, 1))
    offs = np.concatenate([[0], np.cumsum(cols.in_sizes)])
    pad = LANES - MLA_ROPE_DIM - cols.heads
    special = jnp.concatenate([wt[:, offs[9]:offs[10]], wt[:, offs[2]:offs[3]], jnp.zeros((depth, pad, d), F32)], axis=1)
    tab = jnp.asarray(_w_in_blocks(cols))
    grid_spec = pltpu.PrefetchScalarGridSpec(
        num_scalar_prefetch=1,
        grid=(depth, int(tab.shape[0])),
        in_specs=[pl.BlockSpec((pl.Element(1), pl.Element(LANES), pl.Element(d)),
                               lambda l, i, tab: (l, pl.multiple_of(jnp.maximum(tab[i], 0), 8), 0)),
                  pl.BlockSpec((1, LANES, d), lambda l, i, tab: (l, 0, 0))],
        out_specs=pl.BlockSpec((1, LANES, d), lambda l, i, tab: (l, i, 0)))
    return pl.pallas_call(
        _wprep_kernel, grid_spec=grid_spec,
        out_shape=jax.ShapeDtypeStruct((depth, cols.total, d), BF16),
        compiler_params=_cparams(("arbitrary", "arbitrary")),
        name="w_in_prep",
    )(tab, wt, special)


def _matmul_nt_kernel(x_ref, w_ref, o_ref):
    o_ref[...] = _nt_dot(x_ref[...], w_ref[...])


def _inproj_call(u2d, wp, layer, tn):
    m, k = u2d.shape
    n = wp.shape[1]
    tm = min(m, 1024)
    return pl.pallas_call(
        _matmul_nt_kernel,
        grid=(n // tn, m // tm),
        in_specs=[pl.BlockSpec((tm, k), lambda j, i: (i, 0)),
                  pl.BlockSpec((None, tn, k), lambda j, i: (layer, j, 0))],
        out_specs=pl.BlockSpec((tm, tn), lambda j, i: (i, j)),
        out_shape=jax.ShapeDtypeStruct((m, n), F32),
        compiler_params=_cparams(("arbitrary", "arbitrary")),
        name="in_proj",
    )(u2d, wp)


def _ssd_kernel(xbc_ref, z_ref, dtb_ref, conv0_ref, st0_ref, cw_ref, cb_ref, dtbias_ref, alog_ref,
                dskip_ref, nw_ref, e_ref, et_ref, y_ref, st_ref, convo_ref, ext_ref, *, T, nchunks, heads):
    c = pl.program_id(1)
    ssd_w = heads * SSD_HEAD_DIM
    gw = ssd_w // SSD_GROUPS
    hpg = heads // SSD_GROUPS
    conv_dim = ext_ref.shape[1]
    dt_lo = MLA_ROPE_DIM

    @pl.when(c == 0)
    def _():
        ext_ref[0:8, :] = jnp.zeros((8, conv_dim), F32)
        ext_ref[8 - (SSD_CONV - 1):8, :] = conv0_ref[0]
        st_ref[0] = st0_ref[0]

    ext_ref[8:8 + T, :] = xbc_ref[...]
    acc = cb_ref[...]
    for j in range(SSD_CONV):
        lo = 8 - (SSD_CONV - 1) + j
        acc = acc + ext_ref[lo:lo + T, :] * cw_ref[j:j + 1, :]
    ext_ref[0:8, :] = ext_ref[T:T + 8, :]

    @pl.when(c == nchunks - 1)
    def _():
        convo_ref[0] = ext_ref[8 - (SSD_CONV - 1):8, :]

    xact = _silu(acc)
    xs = xact[:, :ssd_w]
    bm = xact[:, ssd_w:ssd_w + SSD_GROUPS * SSD_STATE]
    cm = xact[:, ssd_w + SSD_GROUPS * SSD_STATE:]

    lane = lax.broadcasted_iota(jnp.int32, (1, LANES), 1)
    is_dt = (lane >= dt_lo) & (lane < dt_lo + heads)
    xdt = dtb_ref[...] + dtbias_ref[...]
    dt = jnp.where(is_dt, jnp.maximum(xdt, 0.0) + jnp.log1p(jnp.exp(-jnp.abs(xdt))), 0.0)
    a_neg = jnp.where(is_dt, -jnp.exp(alog_ref[...]), 0.0)
    da = dt * a_neg

    row = lax.broadcasted_iota(jnp.int32, (T, T), 0)
    col = lax.broadcasted_iota(jnp.int32, (T, T), 1)
    causal = row >= col
    tril = causal.astype(BF16)
    eye = (lax.broadcasted_iota(jnp.int32, (LANES, LANES), 0)
           == lax.broadcasted_iota(jnp.int32, (LANES, LANES), 1)).astype(BF16)

    da3 = _split3(da)
    a_cs = sum(jnp.dot(tril, p, preferred_element_type=F32) for p in da3)
    a3 = _split3(a_cs)
    a_cs_t = sum(_nt_dot(eye, p) for p in a3)
    a_last = a_cs[T - 1:T, :]
    ea = jnp.exp(a_cs)
    te = jnp.exp(a_last - a_cs)

    e_mat = e_ref[...]

    def expand(v):
        hi, lo = _split2(v)
        return jnp.dot(hi, e_mat, preferred_element_type=F32) + jnp.dot(lo, e_mat, preferred_element_type=F32)

    dt_x = expand(dt)
    ea_x = expand(ea)
    te_x = expand(te)
    cd_col = jnp.exp(a_cs_t[:, T - 1:T])
    cd_b = jnp.broadcast_to(cd_col, (LANES, SSD_STATE))
    cdh, cdl = _split2(cd_b)
    et_mat = et_ref[...]
    cd_full = (jnp.dot(et_mat, cdh, preferred_element_type=F32)
               + jnp.dot(et_mat, cdl, preferred_element_type=F32))

    xd = xs * dt_x
    xde = (xd * te_x).astype(BF16)
    lane_p = lax.broadcasted_iota(jnp.int32, (T, LANES), 1)
    lower_half = lane_p < SSD_HEAD_DIM

    for g in range(SSD_GROUPS):
        cg = cm[:, g * SSD_STATE:(g + 1) * SSD_STATE].astype(BF16)
        bg = bm[:, g * SSD_STATE:(g + 1) * SSD_STATE].astype(BF16)
        cbm = _nt_dot(cg, bg)
        st_g = st_ref[0, g * hpg:(g + 1) * hpg].reshape(gw, SSD_STATE)
        y_off = _nt_dot(cg, st_g.astype(BF16)) * ea_x[:, g * gw:(g + 1) * gw]
        pieces = []
        for q in range(hpg // 2):
            c0 = g * gw + q * LANES
            xd_pair = xd[:, c0:c0 + LANES].astype(BF16)
            ys = []
            for h in (g * hpg + 2 * q, g * hpg + 2 * q + 1):
                seg = a_cs[:, dt_lo + h:dt_lo + h + 1] - a_cs_t[dt_lo + h:dt_lo + h + 1, :]
                decay = jnp.exp(jnp.where(causal, seg, -jnp.inf))
                ys.append(jnp.dot((cbm * decay).astype(BF16), xd_pair, preferred_element_type=F32))
            pieces.append(jnp.where(lower_half, ys[0], ys[1]))
        y_g = jnp.concatenate(pieces, axis=1) + y_off
        upd = lax.dot_general(xde[:, g * gw:(g + 1) * gw], bg, (((0,), (0,)), ((), ())),
                              preferred_element_type=F32)
        st_new = st_g * cd_full[g * gw:(g + 1) * gw, :] + upd
        st_ref[0, g * hpg:(g + 1) * hpg] = st_new.reshape(hpg, SSD_HEAD_DIM, SSD_STATE)
        y_g = y_g + dskip_ref[:, g * gw:(g + 1) * gw] * xs[:, g * gw:(g + 1) * gw]
        yg = y_g * _silu(z_ref[:, g * gw:(g + 1) * gw])
        ms = jnp.mean(yg * yg, axis=-1, keepdims=True)
        y_ref[:, g * gw:(g + 1) * gw] = (yg * lax.rsqrt(ms + 1e-6) * nw_ref[:, g * gw:(g + 1) * gw]).astype(BF16)


def _ssd_call(h2d, conv0, st0, conv_w, conv_b, dtbias_p, alog_p, dskip_x, norm_w, e_mat, et_mat, cols, nb, L, T):
    nchunks = L // T
    heads = cols.heads
    ssd_w = cols.ssd_w
    cd = cols.conv_dim
    kern = functools.partial(_ssd_kernel, T=T, nchunks=nchunks, heads=heads)
    row = lambda b, c: b * nchunks + c
    const2 = lambda b, c: (0, 0)
    return pl.pallas_call(
        kern,
        grid=(nb, nchunks),
        in_specs=[pl.BlockSpec((T, cd), lambda b, c: (row(b, c), cols.xbc // cd)),
                  pl.BlockSpec((T, ssd_w), lambda b, c: (row(b, c), cols.z // ssd_w)),
                  pl.BlockSpec((T, LANES), lambda b, c: (row(b, c), cols.krdt // LANES)),
                  pl.BlockSpec((1, SSD_CONV - 1, cd), lambda b, c: (b, 0, 0)),
                  pl.BlockSpec((1, heads, SSD_HEAD_DIM, SSD_STATE), lambda b, c: (b, 0, 0, 0)),
                  pl.BlockSpec((SSD_CONV, cd), const2),
                  pl.BlockSpec((1, cd), const2),
                  pl.BlockSpec((1, LANES), const2),
                  pl.BlockSpec((1, LANES), const2),
                  pl.BlockSpec((1, ssd_w), const2),
                  pl.BlockSpec((1, ssd_w), const2),
                  pl.BlockSpec((LANES, ssd_w), const2),
                  pl.BlockSpec((ssd_w, LANES), const2)],
        out_specs=[pl.BlockSpec((T, ssd_w), lambda b, c: (row(b, c), 0)),
                   pl.BlockSpec((1, heads, SSD_HEAD_DIM, SSD_STATE), lambda b, c: (b, 0, 0, 0)),
                   pl.BlockSpec((1, SSD_CONV - 1, cd), lambda b, c: (b, 0, 0))],
        out_shape=[jax.ShapeDtypeStruct((nb * L, ssd_w), BF16),
                   jax.ShapeDtypeStruct((nb, heads, SSD_HEAD_DIM, SSD_STATE), F32),
                   jax.ShapeDtypeStruct((nb, SSD_CONV - 1, cd), F32)],
        scratch_shapes=[pltpu.VMEM((T + 8, cd), F32)],
        compiler_params=_cparams(("arbitrary", "arbitrary")),
        name="ssd_scan",
    )(h2d, h2d, h2d, conv0, st0, conv_w, conv_b, dtbias_p, alog_p, dskip_x, norm_w, e_mat, et_mat)


def _rope_tables(pos, rot_dim, period, width):
    half = rot_dim // 2
    inv = ROPE_THETA ** (-jnp.arange(half, dtype=F32) * (2.0 / rot_dim))
    ang = pos.astype(F32)[:, None] * inv[None, :]
    cos, sin = jnp.cos(ang), jnp.sin(ang)
    lane = np.arange(LANES)
    inner = lane % period
    idx = jnp.asarray(inner % half)
    first = jnp.asarray((inner < half) & (lane < width))
    second = jnp.asarray((inner >= half) & (inner < rot_dim) & (lane < width))
    keep = jnp.asarray((inner >= rot_dim) & (lane < width))
    cos_l, sin_l = cos[:, idx], sin[:, idx]
    cos_t = jnp.where(first | second, cos_l, jnp.where(keep, 1.0, 0.0))
    sin_a = jnp.where(first, -sin_l, 0.0)
    sin_b = jnp.where(second, sin_l, 0.0)
    return cos_t.astype(F32), sin_a.astype(F32), sin_b.astype(F32)


def _rope_tile(x, cos_t, sin_a, sin_b, half):
    return (x * cos_t + pltpu.roll(x, LANES - half, 1) * sin_a + pltpu.roll(x, half, 1) * sin_b)


def _rmsnorm(x, w):
    return x * lax.rsqrt(jnp.mean(x * x, axis=-1, keepdims=True) + 1e-6) * w


def _prep_kernel(dq_ref, dk_ref, dv_ref, cq_ref, kr_ref, cd_ref, sad_ref, sbd_ref, cm_ref, sam_ref, sbm_ref,
                 kvw_ref, *rest, per_head, n_alias):
    if per_head:
        wuk_ref, rest = rest[0], rest[1:]
    rest = rest[n_alias:]
    if per_head:
        k32_ref, v32_ref, lat_ref, kro_ref, qd_ref, kb_ref, vb_ref, latb_ref, kt_ref = rest
    else:
        k32_ref, v32_ref, lat_ref, kro_ref, qd_ref, kb_ref, vb_ref, kcat_ref = rest
    cos_d, sa_d, sb_d = cd_ref[...], sad_ref[...], sbd_ref[...]
    width = dq_ref.shape[1]
    for c in range(width // LANES):
        sl = slice(c * LANES, (c + 1) * LANES)
        q = _rope_tile(dq_ref[:, sl], cos_d, sa_d, sb_d, DIFF_ROT // 2)
        qd_ref[0, :, sl] = (q * (DIFF_SCALE * LOG2E)).astype(BF16)
        k = _rope_tile(dk_ref[:, sl], cos_d, sa_d, sb_d, DIFF_ROT // 2)
        k32_ref[0, :, sl] = k
        kb_ref[0, :, sl] = k.astype(BF16)
    v = dv_ref[...]
    v32_ref[0] = v
    vb_ref[0] = v.astype(BF16)
    lat = _rmsnorm(cq_ref[:, MLA_Q_RANK:MLA_Q_RANK + MLA_KV_RANK], kvw_ref[...])
    lat_ref[0] = lat
    kr = _rope_tile(kr_ref[...], cm_ref[...], sam_ref[...], sbm_ref[...], MLA_ROPE_DIM // 2)
    kro_ref[0] = kr[:, :MLA_ROPE_DIM]
    lat_b = lat.astype(BF16)
    if per_head:
        latb_ref[0] = lat_b
        k_nope = jnp.dot(lat_b, wuk_ref[...], preferred_element_type=F32)
        for h in range(kt_ref.shape[1]):
            k_h = jnp.concatenate([k_nope[:, h * MLA_NOPE_DIM:(h + 1) * MLA_NOPE_DIM], kr], axis=1)
            kt_ref[0, h, 0] = k_h.T.astype(BF16)
    else:
        kcat_ref[0, :, :MLA_KV_RANK] = lat_b
        kcat_ref[0, :, MLA_KV_RANK:] = kr.astype(BF16)


def _prep_call(h2d, tabs_d, tabs_m, kv_w, cols, nb, L, tm, layer, state_bufs, wuk_flat=None):
    nt = L // tm
    dw = cols.diff_w
    per_head = wuk_flat is not None
    row = lambda b, i: b * nt + i
    hspec = lambda width, off: pl.BlockSpec((tm, width), lambda b, i: (row(b, i), off // width))
    tspec = pl.BlockSpec((tm, LANES), lambda b, i: (i, 0))
    ospec = lambda width: pl.BlockSpec((1, tm, width), lambda b, i: (b, i, 0))
    sspec = lambda width: pl.BlockSpec((None, 1, tm, width), lambda b, i: (layer, b, i, 0))
    in_specs = [hspec(dw, cols.dq), hspec(dw, cols.dk), hspec(dw, cols.dv),
                hspec(MLA_Q_RANK + MLA_KV_RANK, cols.cq), hspec(LANES, cols.krdt),
                tspec, tspec, tspec, tspec, tspec, tspec,
                pl.BlockSpec((1, MLA_KV_RANK), lambda b, i: (0, 0))]
    args = [h2d, h2d, h2d, h2d, h2d, *tabs_d, *tabs_m, kv_w]
    if per_head:
        in_specs.append(pl.BlockSpec(wuk_flat.shape, lambda b, i: (0, 0)))
        args.append(wuk_flat)
    aliases = {len(args) + n: n for n in range(len(state_bufs))}
    in_specs += [pl.BlockSpec(memory_space=pl.ANY)] * len(state_bufs)
    args += list(state_bufs)
    out_specs = [sspec(dw), sspec(dw), sspec(MLA_KV_RANK), sspec(MLA_ROPE_DIM), ospec(dw), ospec(dw), ospec(dw)]
    out_shape = [jax.ShapeDtypeStruct(b.shape, b.dtype) for b in state_bufs]
    out_shape += [jax.ShapeDtypeStruct((nb, L, dw), BF16)] * 3
    if per_head:
        heads = wuk_flat.shape[1] // MLA_NOPE_DIM
        out_specs += [ospec(MLA_KV_RANK),
                      pl.BlockSpec((1, heads, 1, MLA_QH_DIM, tm), lambda b, i: (b, 0, i, 0, 0))]
        out_shape += [jax.ShapeDtypeStruct((nb, L, MLA_KV_RANK), BF16),
                      jax.ShapeDtypeStruct((nb, heads, nt, MLA_QH_DIM, tm), BF16)]
    else:
        out_specs.append(ospec(MLA_QK_PAD))
        out_shape.append(jax.ShapeDtypeStruct((nb, L, MLA_QK_PAD), BF16))
    return pl.pallas_call(
        functools.partial(_prep_kernel, per_head=per_head, n_alias=len(state_bufs)),
        grid=(nb, nt),
        in_specs=in_specs, out_specs=out_specs, out_shape=out_shape,
        input_output_aliases=aliases,
        compiler_params=_cparams(("arbitrary", "arbitrary")),
        name="attn_prep",
    )(*args)


def _mlaq_kernel(cq_ref, qw_ref, wuq_ref, wuk_ref, cm_ref, sam_ref, sbm_ref, o_ref, *, heads, absorb):
    cqn = _rmsnorm(cq_ref[:, :MLA_Q_RANK], qw_ref[...]).astype(BF16)
    qm = jnp.dot(cqn, wuq_ref[...], preferred_element_type=F32)
    cos_m, sa_m, sb_m = cm_ref[...], sam_ref[...], sbm_ref[...]
    nope_w = heads * MLA_NOPE_DIM
    scale = MLA_SCALE * LOG2E
    for h in range(heads):
        nope = qm[:, h * MLA_NOPE_DIM:(h + 1) * MLA_NOPE_DIM]
        qr = _rope_tile(qm[:, nope_w + h * LANES:nope_w + (h + 1) * LANES], cos_m, sa_m, sb_m, MLA_ROPE_DIM // 2)
        if absorb:
            ql = jnp.dot(nope.astype(BF16), wuk_ref[h], preferred_element_type=F32)
        else:
            ql = nope
        width = ql.shape[1]
        o_ref[0, h, :, :width] = (ql * scale).astype(BF16)
        o_ref[0, h, :, width:] = (qr * scale).astype(BF16)


def _mlaq_call(h2d, q_w, wuq_p, wuk_t, tabs_m, cols, nb, L, heads, absorb):
    tm = min(L, 512)
    nt = L // tm
    width = MLA_Q_RANK + MLA_KV_RANK
    qdim = MLA_QK_PAD if absorb else MLA_QH_DIM
    tspec = pl.BlockSpec((tm, LANES), lambda b, i: (i, 0))
    return pl.pallas_call(
        functools.partial(_mlaq_kernel, heads=heads, absorb=absorb),
        grid=(nb, nt),
        in_specs=[pl.BlockSpec((tm, width), lambda b, i: (b * nt + i, cols.cq // width)),
                  pl.BlockSpec((1, MLA_Q_RANK), lambda b, i: (0, 0)),
                  pl.BlockSpec(wuq_p.shape, lambda b, i: (0, 0)),
                  pl.BlockSpec(wuk_t.shape, lambda b, i: (0, 0, 0)),
                  tspec, tspec, tspec],
        out_specs=pl.BlockSpec((1, heads, tm, qdim), lambda b, i: (b, 0, i, 0)),
        out_shape=jax.ShapeDtypeStruct((nb, heads, L, qdim), BF16),
        compiler_params=_cparams(("arbitrary", "arbitrary")),
        name="mla_q",
    )(h2d, q_w, wuq_p, wuk_t, *tabs_m)


def _softmax_init(m_ref, l_ref, acc_ref):
    m_ref[...] = jnp.full(m_ref.shape, -jnp.inf, F32)
    l_ref[...] = jnp.zeros(l_ref.shape, F32)
    acc_ref[...] = jnp.zeros(acc_ref.shape, F32)


def _softmax_update(s, v, m_ref, l_ref, acc_ref):
    m_prev = m_ref[...]
    m_new = jnp.maximum(m_prev, jnp.max(s, axis=-1, keepdims=True))
    alpha = jnp.exp2(m_prev - m_new)
    p = jnp.exp2(s - m_new)
    l_ref[...] = alpha * l_ref[...] + jnp.sum(p, axis=-1, keepdims=True)
    acc_ref[...] = alpha * acc_ref[...] + jnp.dot(p.astype(BF16), v, preferred_element_type=F32)
    m_ref[...] = m_new


def _diag_mask(rows, tq, tk, q0):
    q_tok = q0 + (lax.broadcasted_iota(jnp.int32, (rows, tk), 0) & (tq - 1))
    k_tok = lax.broadcasted_iota(jnp.int32, (rows, tk), 1)
    return (k_tok // CHUNK) <= (q_tok // CHUNK)


def _causal_sweep(n_full, qk, upd, finish, sa_ref, sb_ref):
    qk(0, sa_ref)

    def pair(jj, carry):
        j = 2 * jj
        qk(j + 1, sb_ref)
        upd(j, sa_ref)
        qk(j + 2, sa_ref)
        upd(j + 1, sb_ref)
        return carry

    lax.fori_loop(0, n_full // 2, pair, 0)

    @pl.when(n_full % 2 == 1)
    def _():
        qk(n_full, sb_ref)
        upd(n_full - 1, sa_ref)
        finish(sb_ref)

    @pl.when(n_full % 2 == 0)
    def _():
        finish(sa_ref)


def _softmax_init_wide(m_ref, acc_ref):
    m_ref[...] = jnp.full(m_ref.shape, -jnp.inf, F32)
    acc_ref[...] = jnp.zeros(acc_ref.shape, F32)


def _with_ones(v):
    return jnp.concatenate([v, jnp.ones((v.shape[0], LANES), BF16)], axis=1)


def _softmax_update_wide(s, pv_fn, m_ref, acc_ref):
    tk = s.shape[1]
    w = min(tk, LANES)
    m_prev = m_ref[...]
    m_new = jnp.maximum(m_prev, jnp.max(s, axis=-1, keepdims=True))
    alpha = jnp.exp2(m_prev - m_new)
    p = jnp.concatenate([jnp.exp2(s[:, c:c + w] - m_new[:, :w]).astype(BF16) for c in range(0, tk, w)], axis=1)
    acc = acc_ref[...]
    acc_ref[...] = jnp.concatenate([alpha] * (acc.shape[1] // LANES), axis=1) * acc + pv_fn(p)
    m_ref[...] = m_new


def _softmax_update_rep(s, v, m_ref, l_ref, acc_ref):
    tk = s.shape[1]
    m_prev = m_ref[...]
    m_new = jnp.maximum(m_prev, jnp.max(s, axis=-1, keepdims=True))
    alpha = jnp.exp2(m_prev - m_new)
    ps = [jnp.exp2(s[:, c:c + LANES] - m_new) for c in range(0, tk, LANES)]
    l_ref[...] = alpha * l_ref[...] + sum(ps[1:], ps[0])
    p = jnp.concatenate([x.astype(BF16) for x in ps], axis=1)
    acc = acc_ref[...]
    acc_ref[...] = (jnp.concatenate([alpha] * (acc.shape[1] // LANES), axis=1) * acc
                    + jnp.dot(p, v, preferred_element_type=F32))
    m_ref[...] = m_new


def _diff_lambda(lam_ref, lam_init):
    s1 = jnp.sum(lam_ref[0:1, :] * lam_ref[1:2, :], axis=-1, keepdims=True)
    s2 = jnp.sum(lam_ref[2:3, :] * lam_ref[3:4, :], axis=-1, keepdims=True)
    return jnp.exp(s1) - jnp.exp(s2) + lam_init


def _diff_finish(o1, o2, lam, nw, gate, lam_init):
    o = o1 - lam * o2
    o = _rmsnorm(o, nw) * (1.0 - lam_init)
    return (o * _silu(gate)).astype(BF16)


def _stack_streams(q):
    lane = lax.broadcasted_iota(jnp.int32, q.shape, 1)
    zero = jnp.zeros_like(q)
    return jnp.concatenate([jnp.where(lane < DIFF_QK_DIM, q, zero), jnp.where(lane >= DIFF_QK_DIM, q, zero)], axis=0)


def _diff_prompt_kernel(q_ref, k_ref, v_ref, lam_ref, nw_ref, g_ref, o_ref,
                        qz_ref, sa_ref, sb_ref, m_ref, acc_ref, *, t, lam_init):
    qi = pl.program_id(2)
    qz_ref[...] = _stack_streams(q_ref[0])
    _softmax_init_wide(m_ref, acc_ref)

    def blk(ref, j):
        return ref[0, pl.ds(pl.multiple_of(j * t, t), t), :]

    def qk(j, s_ref):
        s_ref[...] = _nt_dot(qz_ref[...], blk(k_ref, j))

    def pv_fn(j):
        return lambda p: jnp.dot(p, _with_ones(blk(v_ref, j)), preferred_element_type=F32)

    def upd(j, s_ref):
        _softmax_update_wide(s_ref[...], pv_fn(j), m_ref, acc_ref)

    def finish(s_ref):
        s = jnp.where(_diag_mask(2 * t, t, t, 0), s_ref[...], NEG_INF)
        _softmax_update_wide(s, pv_fn(qi), m_ref, acc_ref)
        o = acc_ref[:, :DIFF_V_DIM] / acc_ref[:, DIFF_V_DIM:]
        lam = _diff_lambda(lam_ref, lam_init)
        o_ref[0] = _diff_finish(o[:t], o[t:], lam, nw_ref[...], g_ref[...], lam_init)

    _causal_sweep(qi, qk, upd, finish, sa_ref, sb_ref)


def _diff_prompt_call(qd, kb, vb, lam_p, norm_w, h2d, cols, lam_init, t):
    nb, L, dw = qd.shape
    heads = dw // DIFF_V_DIM
    nq = L // t
    kern = functools.partial(_diff_prompt_kernel, t=t, lam_init=lam_init)
    seq = pl.BlockSpec((1, L, LANES), lambda b, h, i: (b, 0, h))
    tile = pl.BlockSpec((1, t, LANES), lambda b, h, i: (b, i, h))
    return pl.pallas_call(
        kern,
        grid=(nb, heads, nq),
        in_specs=[tile, seq, seq,
                  pl.BlockSpec((4, DIFF_QK_DIM), lambda b, h, i: (0, 0)),
                  pl.BlockSpec((1, DIFF_V_DIM), lambda b, h, i: (0, 0)),
                  pl.BlockSpec((t, LANES), lambda b, h, i: (b * nq + i, cols.dgate // LANES + h))],
        out_specs=tile,
        out_shape=jax.ShapeDtypeStruct((nb, L, dw), BF16),
        scratch_shapes=[pltpu.VMEM((2 * t, LANES), BF16),
                        pltpu.VMEM((2 * t, t), F32),
                        pltpu.VMEM((2 * t, t), F32),
                        pltpu.VMEM((2 * t, LANES), F32),
                        pltpu.VMEM((2 * t, DIFF_V_DIM + LANES), F32)],
        compiler_params=_cparams(("arbitrary", "arbitrary", "arbitrary")),
        name="diff_attn_prompt",
    )(qd, kb, vb, lam_p, norm_w, h2d)


def _diff_sample_kernel(q_ref, kc_ref, vc_ref, kn_ref, vn_ref, lam_ref, nw_ref, g_ref, o_ref,
                        qz_ref, m_ref, acc_ref, *, nkc, heads, lam_init):
    j = pl.program_id(1)
    L = q_ref.shape[1]
    rows = 2 * L

    @pl.when(j == 0)
    def _():
        for h in range(heads):
            qz_ref[h] = _stack_streams(q_ref[0, :, h * LANES:(h + 1) * LANES])
        _softmax_init_wide(m_ref, acc_ref)

    def step(scores, values):
        def pv_fn(p):
            return jnp.concatenate([jnp.dot(p[h * rows:(h + 1) * rows], _with_ones(values(h)),
                                            preferred_element_type=F32) for h in range(heads)], axis=0)
        s = jnp.concatenate([scores(h) for h in range(heads)], axis=0)
        _softmax_update_wide(s, pv_fn, m_ref, acc_ref)

    @pl.when(j < nkc)
    def _():
        tk = kc_ref.shape[5]

        def scores(h):
            return jnp.dot(qz_ref[h], kc_ref[0, 0, h].reshape(2 * DIFF_QK_DIM, tk).astype(BF16),
                           preferred_element_type=F32)

        def values(h):
            return vc_ref[0, 0, pl.ds(h, tk, stride=heads), :].astype(BF16)

        step(scores, values)

    @pl.when(j == nkc)
    def _():
        step(lambda h: _nt_dot(qz_ref[h], kn_ref[0, :, h * LANES:(h + 1) * LANES]),
             lambda h: vn_ref[0, :, h * LANES:(h + 1) * LANES])
        lam = _diff_lambda(lam_ref, lam_init)
        o = acc_ref[:, :DIFF_V_DIM] / acc_ref[:, DIFF_V_DIM:]
        for h in range(heads):
            sl = slice(h * LANES, (h + 1) * LANES)
            oh = o[h * rows:(h + 1) * rows]
            o_ref[0, :, sl] = _diff_finish(oh[:L], oh[L:], lam, nw_ref[...], g_ref[:, sl], lam_init)


def _diff_sample_call(qd, kb, vb, k_cache, v_cache, layer, lam_p, norm_w, h2d, cols, lam_init, tk):
    nb, L, dw = qd.shape
    heads = dw // DIFF_V_DIM
    P = k_cache.shape[5]
    nkc = P // tk
    last = nkc - 1
    kern = functools.partial(_diff_sample_kernel, nkc=nkc, heads=heads, lam_init=lam_init)
    kspec = pl.BlockSpec((1, 1, heads, 2, DIFF_QK_DIM, tk),
                         lambda b, j: (layer, b, 0, 0, 0, jnp.minimum(j, last)))
    vspec = pl.BlockSpec((1, 1, tk * heads, DIFF_V_DIM), lambda b, j: (layer, b, jnp.minimum(j, last), 0))
    nspec = pl.BlockSpec((1, L, dw), lambda b, j: (b, 0, 0))
    return pl.pallas_call(
        kern,
        grid=(nb, nkc + 1),
        in_specs=[nspec, kspec, vspec, nspec, nspec,
                  pl.BlockSpec((4, DIFF_QK_DIM), lambda b, j: (0, 0)),
                  pl.BlockSpec((1, DIFF_V_DIM), lambda b, j: (0, 0)),
                  pl.BlockSpec((L, dw), lambda b, j: (b, cols.dgate // dw))],
        out_specs=nspec,
        out_shape=jax.ShapeDtypeStruct((nb, L, dw), BF16),
        scratch_shapes=[pltpu.VMEM((heads, 2 * L, LANES), BF16),
                        pltpu.VMEM((heads * 2 * L, LANES), F32),
                        pltpu.VMEM((heads * 2 * L, DIFF_V_DIM + LANES), F32)],
        compiler_params=_cparams(("arbitrary", "arbitrary")),
        name="diff_attn_sample",
    )(qd, k_cache, v_cache, kb, vb, lam_p, norm_w, h2d)


def _mla_finish(acc_ref, l, wuv_ref, g_ref, o_ref, heads, t):
    o = (acc_ref[...] / l).astype(BF16)
    for h in range(heads):
        sl = slice(h * MLA_V_DIM, (h + 1) * MLA_V_DIM)
        om = jnp.dot(o[h * t:(h + 1) * t], wuv_ref[:, sl], preferred_element_type=F32)
        o_ref[0, :, sl] = (om * _silu(g_ref[:, sl])).astype(BF16)


def _mla_prompt_kernel(q_ref, kt_ref, v_ref, wuv_ref, g_ref, o_ref, sa_ref, sb_ref, m_ref, l_ref, acc_ref, *, t, hg):
    qi = pl.program_id(2)
    _softmax_init(m_ref, l_ref, acc_ref)

    def vblk(j):
        return v_ref[0, pl.ds(pl.multiple_of(j * t, t), t), :]

    def qk(j, s_ref):
        for g in range(hg):
            s_ref[g * t:(g + 1) * t, :] = jnp.dot(q_ref[0, g], kt_ref[0, g, j], preferred_element_type=F32)

    def upd(j, s_ref):
        _softmax_update_rep(s_ref[...], vblk(j), m_ref, l_ref, acc_ref)

    def finish(s_ref):
        s = jnp.where(_diag_mask(hg * t, t, t, 0), s_ref[...], NEG_INF)
        _softmax_update_rep(s, vblk(qi), m_ref, l_ref, acc_ref)
        _mla_finish(acc_ref, jnp.sum(l_ref[...], axis=-1, keepdims=True), wuv_ref, g_ref, o_ref, hg, t)

    _causal_sweep(qi, qk, upd, finish, sa_ref, sb_ref)


def _mla_prompt_call(qh, kt, latb, wuv, h2d, cols, t, hg):
    nb, heads, L, qdim = qh.shape
    mw = heads * MLA_V_DIM
    gw = hg * MLA_V_DIM
    nq = L // t
    return pl.pallas_call(
        functools.partial(_mla_prompt_kernel, t=t, hg=hg),
        grid=(nb, heads // hg, nq),
        in_specs=[pl.BlockSpec((1, hg, t, qdim), lambda b, h, i: (b, h, i, 0)),
                  pl.BlockSpec((1, hg, nq, qdim, t), lambda b, h, i: (b, h, 0, 0, 0)),
                  pl.BlockSpec((1, L, MLA_KV_RANK), lambda b, h, i: (b, 0, 0)),
                  pl.BlockSpec((MLA_KV_RANK, gw), lambda b, h, i: (0, h)),
                  pl.BlockSpec((t, gw), lambda b, h, i: (b * nq + i, cols.mgate // gw + h))],
        out_specs=pl.BlockSpec((1, t, gw), lambda b, h, i: (b, i, h)),
        out_shape=jax.ShapeDtypeStruct((nb, L, mw), BF16),
        scratch_shapes=[pltpu.VMEM((hg * t, t), F32),
                        pltpu.VMEM((hg * t, t), F32),
                        pltpu.VMEM((hg * t, LANES), F32),
                        pltpu.VMEM((hg * t, LANES), F32),
                        pltpu.VMEM((hg * t, MLA_KV_RANK), F32)],
        compiler_params=_cparams(("arbitrary", "arbitrary", "arbitrary")),
        name="mla_attn_prompt",
    )(qh, kt, latb, wuv, h2d)


def _mla_sample_kernel(q_ref, lc_ref, rc_ref, kn_ref, wuv_ref, g_ref, o_ref, m_ref, l_ref, acc_ref,
                       *, nkc, heads):
    j = pl.program_id(1)
    L = q_ref.shape[2]

    @pl.when(j == 0)
    def _():
        _softmax_init(m_ref, l_ref, acc_ref)

    q = q_ref[0].reshape(heads * L, MLA_QK_PAD)

    @pl.when(j < nkc)
    def _():
        lat = lc_ref[0, 0].astype(BF16)
        kr_t = rc_ref[0, 0].astype(BF16)
        s = _nt_dot(q[:, :MLA_KV_RANK], lat) + jnp.dot(q[:, MLA_KV_RANK:MLA_KV_RANK + MLA_ROPE_DIM], kr_t,
                                                       preferred_element_type=F32)
        _softmax_update(s, lat, m_ref, l_ref, acc_ref)

    @pl.when(j == nkc)
    def _():
        kn = kn_ref[0]
        _softmax_update(_nt_dot(q, kn), kn[:, :MLA_KV_RANK], m_ref, l_ref, acc_ref)
        _mla_finish(acc_ref, l_ref[...], wuv_ref, g_ref, o_ref, heads, L)


def _mla_sample_call(qcat, kcat, lat_cache, kr_cache, layer, wuv, h2d, cols, tk):
    nb, heads, L, _ = qcat.shape
    mw = heads * MLA_V_DIM
    P = lat_cache.shape[2]
    nkc = P // tk
    last = nkc - 1
    kern = functools.partial(_mla_sample_kernel, nkc=nkc, heads=heads)
    return pl.pallas_call(
        kern,
        grid=(nb, nkc + 1),
        in_specs=[pl.BlockSpec((1, heads, L, MLA_QK_PAD), lambda b, j: (b, 0, 0, 0)),
                  pl.BlockSpec((1, 1, tk, MLA_KV_RANK), lambda b, j: (layer, b, jnp.minimum(j, last), 0)),
                  pl.BlockSpec((1, 1, MLA_ROPE_DIM, tk), lambda b, j: (layer, b, 0, jnp.minimum(j, last))),
                  pl.BlockSpec((1, L, MLA_QK_PAD), lambda b, j: (b, 0, 0)),
                  pl.BlockSpec(wuv.shape, lambda b, j: (0, 0)),
                  pl.BlockSpec((L, mw), lambda b, j: (b, cols.mgate // mw))],
        out_specs=pl.BlockSpec((1, L, mw), lambda b, j: (b, 0, 0)),
        out_shape=jax.ShapeDtypeStruct((nb, L, mw), BF16),
        scratch_shapes=[pltpu.VMEM((heads * L, 1), F32),
                        pltpu.VMEM((heads * L, 1), F32),
                        pltpu.VMEM((heads * L, MLA_KV_RANK), F32)],
        compiler_params=_cparams(("arbitrary", "arbitrary")),
        name="mla_attn_sample",
    )(qcat, lat_cache, kr_cache, kcat, wuv, h2d)


def _outproj_ln_kernel(ys_ref, yd_ref, ym_ref, w_ref, x_ref, g_ref, lg_ref, lb_ref, *rest, alpha, nj, with_mod):
    if with_mod:
        sc_ref, sh_ref, xo_ref, uo_ref, r_scr, mu_scr, rs_scr = rest
    else:
        xo_ref, r_scr, mu_scr, rs_scr = rest
    bt, lt, tn = x_ref.shape
    rows = bt * lt
    j = pl.program_id(2)

    @pl.when(j < nj)
    def _():
        acc, k0 = None, 0
        for ref in (ys_ref, yd_ref, ym_ref):
            k1 = k0 + ref.shape[2]
            part = jnp.dot(ref[...].reshape(rows, ref.shape[2]), w_ref[k0:k1, :], preferred_element_type=F32)
            acc = part if acc is None else acc + part
            k0 = k1
        r = alpha * x_ref[...] + g_ref[...] * acc.reshape(bt, lt, tn)
        r_scr[j] = r.reshape(rows, tn)

    @pl.when(j == nj - 1)
    def _():
        d = nj * tn
        mu = sum(jnp.sum(r_scr[jj], axis=-1, keepdims=True) for jj in range(nj)) / d
        var = sum(jnp.sum(jnp.square(r_scr[jj] - mu), axis=-1, keepdims=True) for jj in range(nj)) / d
        mu_scr[...] = mu
        rs_scr[...] = lax.rsqrt(var + 1e-5)

    @pl.when(j >= nj)
    def _():
        xn = (r_scr[j - nj] - mu_scr[...]) * rs_scr[...]
        xn = xn.reshape(bt, lt, tn) * lg_ref[...] + lb_ref[...]
        xo_ref[...] = xn
        if with_mod:
            uo_ref[...] = (xn * (1.0 + sc_ref[...]) + sh_ref[...]).astype(BF16)


def _outproj_ln_call(y_ssd, y_diff, y_mla, w_out, x, gate, alpha, g, b, scale=None, shift=None):
    nb, L, d = x.shape
    bt, lt = _seq_tiles(nb, L, 512)
    tn = 512
    nj = d // tn
    with_mod = scale is not None
    ph1 = lambda j: jnp.minimum(j, nj - 1)
    ph2 = lambda j: jnp.maximum(j - nj, 0)
    yspec = lambda width: pl.BlockSpec((bt, lt, width), lambda bi, i, j: (bi, i, 0), pipeline_mode=pl.Buffered(1))
    ospec = pl.BlockSpec((bt, lt, tn), lambda bi, i, j: (bi, i, ph2(j)))
    in_specs = [yspec(y_ssd.shape[2]), yspec(y_diff.shape[2]), yspec(y_mla.shape[2]),
                pl.BlockSpec((w_out.shape[0], tn), lambda bi, i, j: (0, ph1(j))),
                pl.BlockSpec((bt, lt, tn), lambda bi, i, j: (bi, i, ph1(j))),
                pl.BlockSpec((bt, 1, tn), lambda bi, i, j: (bi, 0, ph1(j))),
                pl.BlockSpec((1, 1, tn), lambda bi, i, j: (0, 0, ph2(j))),
                pl.BlockSpec((1, 1, tn), lambda bi, i, j: (0, 0, ph2(j)))]
    args = [y_ssd, y_diff, y_mla, w_out, x, gate, g.reshape(1, 1, d), b.reshape(1, 1, d)]
    out_specs, out_shape = [ospec], [jax.ShapeDtypeStruct((nb, L, d), F32)]
    if with_mod:
        in_specs += [pl.BlockSpec((bt, 1, tn), lambda bi, i, j: (bi, 0, ph2(j)))] * 2
        args += [scale, shift]
        out_specs.append(ospec)
        out_shape.append(jax.ShapeDtypeStruct((nb, L, d), BF16))
    res = pl.pallas_call(
        functools.partial(_outproj_ln_kernel, alpha=alpha, nj=nj, with_mod=with_mod),
        grid=(nb // bt, L // lt, 2 * nj),
        in_specs=in_specs, out_specs=out_specs, out_shape=out_shape,
        scratch_shapes=[pltpu.VMEM((nj, bt * lt, tn), F32),
                        pltpu.VMEM((bt * lt, 1), F32),
                        pltpu.VMEM((bt * lt, 1), F32)],
        compiler_params=_cparams(("arbitrary", "arbitrary", "arbitrary")),
        name="out_proj_ln",
    )(*args)
    return (res[0], res[1]) if with_mod else (res[0], None)


def _ssd_chunk(L):
    for t in (128, 64, 32, 16, 8):
        if L % t == 0:
            return t
    raise ValueError(f"sequence length {L} is not a multiple of 8")


def _attn_tile(L, want):
    t = want
    while L % t:
        t //= 2
    return t


def _layer(x, u, mod_l, next_mod, wl, cols, layer_idx, depth, caches, state_bufs):
    nb, L, d = x.shape
    heads = cols.heads
    mla_heads = cols.mla_w // MLA_V_DIM
    _, _, gate = mod_l
    k_cache, v_cache, lat_cache, kr_cache, st0, conv0 = caches
    P = 0 if k_cache is None else lat_cache.shape[2]
    pos = P + jnp.arange(L, dtype=jnp.int32)

    h2d = _inproj_call(u.reshape(nb * L, d), wl["w_in"], layer_idx, cols.tn)

    T = _ssd_chunk(L)
    y_ssd, ssm_new, conv_new = _ssd_call(h2d, conv0, st0, wl["conv_w"], wl["conv_b"], wl["dt_bias"], wl["a_log"],
                                         wl["d_skip"], wl["ssd_norm_w"], wl["e_mat"], wl["et_mat"], cols, nb, L, T)

    tabs_d = _rope_tables(pos, DIFF_ROT, DIFF_QK_DIM, LANES)
    tabs_m = _rope_tables(pos, MLA_ROPE_DIM, MLA_ROPE_DIM, MLA_ROPE_DIM)
    lam_init = 0.8 - 0.6 * math.exp(-0.3 * layer_idx)
    if k_cache is None:
        t = _attn_tile(L, 512)
        k32, v32, lat32, kr32, qd, kb, vb, latb, kt = _prep_call(h2d, tabs_d, tabs_m, wl["kv_norm_w"], cols, nb, L, t,
                                                                 layer_idx, state_bufs, wl["w_uk_flat"])
        qh = _mlaq_call(h2d, wl["q_norm_w"], wl["w_uq"], wl["w_uk"], tabs_m, cols, nb, L, mla_heads, False)
        y_diff = _diff_prompt_call(qd, kb, vb, wl["lam"], wl["diff_norm_w"], h2d, cols, lam_init, t)
        y_mla = _mla_prompt_call(qh, kt, latb, wl["w_uv"], h2d, cols, t, MLA_HEAD_GROUP)
    else:
        k32, v32, lat32, kr32, qd, kb, vb, kcat = _prep_call(h2d, tabs_d, tabs_m, wl["kv_norm_w"], cols, nb, L,
                                                             min(L, 256), layer_idx, state_bufs)
        qcat = _mlaq_call(h2d, wl["q_norm_w"], wl["w_uq"], wl["w_uk"], tabs_m, cols, nb, L, mla_heads, True)
        y_diff = _diff_sample_call(qd, kb, vb, k_cache, v_cache, layer_idx, wl["lam"], wl["diff_norm_w"], h2d,
                                   cols, lam_init, _attn_tile(P, 2048))
        y_mla = _mla_sample_call(qcat, kcat, lat_cache, kr_cache, layer_idx, wl["w_uv"], h2d, cols,
                                 _attn_tile(P, 2048))

    alpha = (2 * depth) ** 0.25
    ys = y_ssd.reshape(nb, L, cols.ssd_w)
    if next_mod is None:
        x_new, u_new = _outproj_ln_call(ys, y_diff, y_mla, wl["w_out"], x, gate, alpha, wl["ln_g"], wl["ln_b"])
    else:
        x_new, u_new = _outproj_ln_call(ys, y_diff, y_mla, wl["w_out"], x, gate, alpha, wl["ln_g"], wl["ln_b"],
                                        next_mod[1], next_mod[0])
    return x_new, u_new, (k32, v32, lat32, kr32), (ssm_new, conv_new)


def _layer_weights(l, cols, w_in, conv_w, conv_b, dt_bias, a_log, d_skip, ssd_norm_w, lambda_q1, lambda_k1,
                   lambda_q2, lambda_k2, diff_norm_w, mla_q_norm_w, mla_kv_norm_w, w_uq, w_uk, w_uv, w_out,
                   ln_g, ln_b):
    heads = cols.heads
    mla_heads = cols.mla_w // MLA_V_DIM
    lane_pad = lambda v: jnp.pad(v, (MLA_ROPE_DIM, LANES - MLA_ROPE_DIM - heads)).reshape(1, LANES)
    qk = MLA_NOPE_DIM + MLA_ROPE_DIM
    wq = w_uq[l].reshape(MLA_Q_RANK, mla_heads, qk)
    wq_nope = wq[:, :, :MLA_NOPE_DIM].reshape(MLA_Q_RANK, mla_heads * MLA_NOPE_DIM)
    wq_rope = jnp.pad(wq[:, :, MLA_NOPE_DIM:], ((0, 0), (0, 0), (0, LANES - MLA_ROPE_DIM)))
    wq_p = jnp.concatenate([wq_nope, wq_rope.reshape(MLA_Q_RANK, mla_heads * LANES)], axis=1).astype(BF16)
    eh = np.zeros((LANES, cols.ssd_w), np.float32)
    for h in range(heads):
        eh[MLA_ROPE_DIM + h, h * SSD_HEAD_DIM:(h + 1) * SSD_HEAD_DIM] = 1.0
    return dict(
        w_in=w_in,
        conv_w=conv_w[l], conv_b=conv_b[l].reshape(1, -1),
        dt_bias=lane_pad(dt_bias[l]), a_log=lane_pad(a_log[l]),
        d_skip=jnp.repeat(d_skip[l], SSD_HEAD_DIM).reshape(1, -1),
        ssd_norm_w=ssd_norm_w[l].reshape(1, -1),
        e_mat=jnp.asarray(eh, BF16), et_mat=jnp.asarray(eh.T, BF16),
        lam=jnp.stack([lambda_q1[l], lambda_k1[l], lambda_q2[l], lambda_k2[l]]),
        diff_norm_w=diff_norm_w[l].reshape(1, -1),
        q_norm_w=mla_q_norm_w[l].reshape(1, -1), kv_norm_w=mla_kv_norm_w[l].reshape(1, -1),
        w_uq=wq_p,
        w_uk=jnp.transpose(w_uk[l], (1, 2, 0)).astype(BF16),
        w_uk_flat=w_uk[l].reshape(MLA_KV_RANK, mla_heads * MLA_NOPE_DIM).astype(BF16),
        w_uv=w_uv[l].reshape(MLA_KV_RANK, mla_heads * MLA_V_DIM).astype(BF16),
        w_out=w_out[l].astype(BF16),
        ln_g=ln_g[l], ln_b=ln_b[l],
    )


def kernel(x_prompt, x_sample, cache_diff_k, cache_diff_v, cache_mla_latent, cache_mla_krope, state_ssm, state_conv,
           c_prompt, c_sample, w_mod, b_mod, w_in, conv_w, conv_b, dt_bias, a_log, d_skip, ssd_norm_w, lambda_q1,
           lambda_k1, lambda_q2, lambda_k2, diff_norm_w, mla_q_norm_w, mla_kv_norm_w, w_uq, w_uk, w_uv, w_out,
           ln_g, ln_b):
    depth = w_in.shape[0]
    bp, _, d = x_prompt.shape
    bs = x_sample.shape[0]
    cols = _Cols(d)
    heads = cols.heads

    rows = -(-(bp + bs) // 8) * 8
    c_all = jnp.concatenate([c_prompt, c_sample, jnp.zeros((rows - bp - bs, d), F32)], axis=0)
    mod = _mod_call(c_all, w_mod, b_mod)

    def mods(l, lo, n):
        part = lambda k: mod[l, lo:lo + n, k * d:(k + 1) * d].reshape(n, 1, d)
        return part(0), part(1), part(2)

    pdiff = cache_diff_k.shape[2]
    kc = jnp.transpose(cache_diff_k, (0, 1, 3, 4, 5, 2))
    vc = cache_diff_v.reshape(depth, bs, pdiff * cache_diff_v.shape[3], DIFF_V_DIM)
    krc = jnp.transpose(cache_mla_krope, (0, 1, 3, 2))
    zero_state = jnp.zeros((bp, heads, SSD_HEAD_DIM, SSD_STATE), F32)
    zero_conv = jnp.zeros((bp, SSD_CONV - 1, cols.conv_dim), F32)

    w_in_p = _permute_w_in(w_in, cols)
    hp, hs = x_prompt, x_sample
    mp, ms = mods(0, 0, bp), mods(0, bp, bs)
    up = _modulate_call(hp, mp[1], mp[0])
    us = _modulate_call(hs, ms[1], ms[0])

    def state_buffers(nb, L):
        return tuple(jnp.zeros((depth, nb, L, w), F32) for w in (cols.diff_w, cols.diff_w, MLA_KV_RANK, MLA_ROPE_DIM))

    bufs_p, bufs_s = state_buffers(bp, hp.shape[1]), state_buffers(bs, hs.shape[1])
    rec_p, rec_s = [], []
    for l in range(depth):
        wl = _layer_weights(l, cols, w_in_p, conv_w, conv_b, dt_bias, a_log, d_skip, ssd_norm_w, lambda_q1, lambda_k1,
                            lambda_q2, lambda_k2, diff_norm_w, mla_q_norm_w, mla_kv_norm_w, w_uq, w_uk, w_uv, w_out,
                            ln_g, ln_b)
        nmp = mods(l + 1, 0, bp) if l + 1 < depth else None
        nms = mods(l + 1, bp, bs) if l + 1 < depth else None
        hp, up, bufs_p, rp = _layer(hp, up, mp, nmp, wl, cols, l, depth,
                                    (None, None, None, None, zero_state, zero_conv), bufs_p)
        hs, us, bufs_s, rs = _layer(hs, us, ms, nms, wl, cols, l, depth,
                                    (kc, vc, cache_mla_latent, krc, state_ssm[l], state_conv[l]), bufs_s)
        rec_p.append(rp)
        rec_s.append(rs)
        mp, ms = nmp, nms

    def states(bufs, rec):
        k32, v32, lat32, kr32 = bufs
        nb, L = k32.shape[1], k32.shape[2]
        return (k32.reshape(depth, nb, L, cols.diff_w // (2 * DIFF_QK_DIM), 2, DIFF_QK_DIM),
                v32.reshape(depth, nb, L, cols.diff_w // DIFF_V_DIM, DIFF_V_DIM), lat32, kr32,
                jnp.stack([r[0] for r in rec]), jnp.stack([r[1] for r in rec]))

    return (hp, hs) + states(bufs_p, rec_p) + states(bufs_s, rec_s)
```

```python
import functools
import math

import numpy as np
import jax
import jax.numpy as jnp
from jax import lax
from jax.experimental import pallas as pl
from jax.experimental.pallas import tpu as pltpu

F32 = jnp.float32
BF16 = jnp.bfloat16

CHUNK = 64
ROPE_THETA = 500000.0
NEG_INF = -1e30
SSD_HEAD_DIM = 64
SSD_GROUPS = 4
SSD_STATE = 128
SSD_CONV = 4
DIFF_QK_DIM = 64
DIFF_V_DIM = 128
DIFF_ROT = DIFF_QK_DIM // 4
DIFF_SCALE = DIFF_QK_DIM ** -0.5
MLA_V_DIM = 128
MLA_NOPE_DIM = 128
MLA_ROPE_DIM = 64
MLA_Q_RANK = 768
MLA_KV_RANK = 256
MLA_SCALE = (MLA_NOPE_DIM + MLA_ROPE_DIM) ** -0.5
MLA_QK_PAD = MLA_KV_RANK + 128
MLA_QH_DIM = MLA_NOPE_DIM + 128
MLA_HEAD_GROUP = 2
LOG2E = math.log2(math.e)

LANES = 128
VMEM_LIMIT = 56 * 1024 * 1024


def _cparams(sem):
    return pltpu.CompilerParams(dimension_semantics=sem, vmem_limit_bytes=VMEM_LIMIT)


def _silu(x):
    hx = 0.5 * x
    return hx + hx * jnp.tanh(hx)


def _nt_dot(a, b):
    return lax.dot_general(a, b, (((1,), (1,)), ((), ())), preferred_element_type=F32)


def _split2(v):
    hi = v.astype(BF16)
    lo = (v - hi.astype(F32)).astype(BF16)
    return hi, lo


def _split3(v):
    hi = v.astype(BF16)
    r = v - hi.astype(F32)
    mid = r.astype(BF16)
    lo = (r - mid.astype(F32)).astype(BF16)
    return hi, mid, lo


def _mod_kernel(c_ref, w_ref, b_ref, o_ref):
    a = _silu(c_ref[...]).astype(BF16)
    o_ref[0] = jnp.dot(a, w_ref[0].astype(BF16), preferred_element_type=F32) + b_ref[0]


def _mod_call(c_all, w_mod, b_mod):
    depth, d, n3 = w_mod.shape
    rows = c_all.shape[0]
    tn = 512
    return pl.pallas_call(
        _mod_kernel,
        grid=(depth, n3 // tn),
        in_specs=[pl.BlockSpec((rows, d), lambda l, j: (0, 0)),
                  pl.BlockSpec((1, d, tn), lambda l, j: (l, 0, j)),
                  pl.BlockSpec((1, 1, tn), lambda l, j: (l, 0, j))],
        out_specs=pl.BlockSpec((1, rows, tn), lambda l, j: (l, 0, j)),
        out_shape=jax.ShapeDtypeStruct((depth, rows, n3), F32),
        compiler_params=_cparams(("arbitrary", "arbitrary")),
        name="adaln_mod",
    )(c_all, w_mod, b_mod.reshape(depth, 1, n3))


def _modulate_kernel(x_ref, sc_ref, sh_ref, u_ref):
    u_ref[...] = (x_ref[...] * (1.0 + sc_ref[...]) + sh_ref[...]).astype(BF16)


def _seq_tiles(nb, L, rows):
    if L >= rows:
        return 1, rows
    return min(nb, rows // L), L


def _modulate_call(x, scale, shift):
    nb, L, d = x.shape
    bt, lt = _seq_tiles(nb, L, 256)
    return pl.pallas_call(
        _modulate_kernel,
        grid=(nb // bt, L // lt),
        in_specs=[pl.BlockSpec((bt, lt, d), lambda i, j: (i, j, 0)),
                  pl.BlockSpec((bt, 1, d), lambda i, j: (i, 0, 0)),
                  pl.BlockSpec((bt, 1, d), lambda i, j: (i, 0, 0))],
        out_specs=pl.BlockSpec((bt, lt, d), lambda i, j: (i, j, 0)),
        out_shape=jax.ShapeDtypeStruct((nb, L, d), BF16),
        compiler_params=_cparams(("arbitrary", "arbitrary")),
        name="modulate",
    )(x, scale, shift)


class _Cols:
    def __init__(self, d_model):
        self.ssd_w = d_model // 2
        self.diff_w = d_model // 4
        self.mla_w = d_model // 4
        self.heads = self.ssd_w // SSD_HEAD_DIM
        self.conv_dim = self.ssd_w + 2 * SSD_GROUPS * SSD_STATE
        self.in_sizes = (self.ssd_w, self.conv_dim, self.heads, self.diff_w, self.diff_w, self.diff_w,
                         self.diff_w, MLA_Q_RANK, MLA_KV_RANK, MLA_ROPE_DIM, self.mla_w)
        o = 0
        self.xbc = o; o += self.conv_dim
        self.dq = o; o += self.diff_w
        self.z = o; o += self.ssd_w
        self.dk = o; o += self.diff_w
        self.dv = o; o += self.diff_w
        self.dgate = o; o += self.diff_w
        self.mgate = o; o += self.mla_w
        self.cq = o; o += MLA_Q_RANK + MLA_KV_RANK
        self.krdt = o; o += LANES
        self.used = o
        self.tn = 1280
        self.total = -(-o // self.tn) * self.tn


def _w_in_blocks(cols):
    offs = np.concatenate([[0], np.cumsum(cols.in_sizes)])
    order = (1, 3, 0, 4, 5, 6, 10)
    rows = []
    for i in order:
        rows += list(range(int(offs[i]), int(offs[i + 1]), LANES))
    rows += list(range(int(offs[7]), int(offs[9]), LANES))
    rows.append(-1)
    rows += [-2] * ((cols.total - cols.used) // LANES)
    return np.asarray(rows, np.int32)


def _wprep_kernel(tab_ref, w_ref, sp_ref, o_ref):
    src = tab_ref[pl.program_id(1)]

    @pl.when(src >= 0)
    def _():
        o_ref[0] = w_ref[0].astype(BF16)

    @pl.when(src == -1)
    def _():
        o_ref[0] = sp_ref[0].astype(BF16)

    @pl.when(src == -2)
    def _():
        o_ref[0] = jnp.zeros(o_ref.shape[1:], BF16)


def _permute_w_in(w_in, cols):
    depth, d, _ = w_in.shape
    wt = jnp.transpose(w_in, (0, 2, 1))
    offs = np.concatenate([[0], np.cumsum(cols.in_sizes)])
    pad = LANES - MLA_ROPE_DIM - cols.heads
    special = jnp.concatenate([wt[:, offs[9]:offs[10]], wt[:, offs[2]:offs[3]], jnp.zeros((depth, pad, d), F32)], axis=1)
    tab = jnp.asarray(_w_in_blocks(cols))
    grid_spec = pltpu.PrefetchScalarGridSpec(
        num_scalar_prefetch=1,
        grid=(depth, int(tab.shape[0])),
        in_specs=[pl.BlockSpec((pl.Element(1), pl.Element(LANES), pl.Element(d)),
                               lambda l, i, tab: (l, pl.multiple_of(jnp.maximum(tab[i], 0), 8), 0)),
                  pl.BlockSpec((1, LANES, d), lambda l, i, tab: (l, 0, 0))],
        out_specs=pl.BlockSpec((1, LANES, d), lambda l, i, tab: (l, i, 0)))
    return pl.pallas_call(
        _wprep_kernel, grid_spec=grid_spec,
        out_shape=jax.ShapeDtypeStruct((depth, cols.total, d), BF16),
        compiler_params=_cparams(("arbitrary", "arbitrary")),
        name="w_in_prep",
    )(tab, wt, special)


def _matmul_nt_kernel(x_ref, w_ref, o_ref):
    o_ref[...] = _nt_dot(x_ref[...], w_ref[...])


def _inproj_call(u2d, wp, layer, tn):
    m, k = u2d.shape
    n = wp.shape[1]
    tm = min(m, 1024)
    return pl.pallas_call(
        _matmul_nt_kernel,
        grid=(n // tn, m // tm),
        in_specs=[pl.BlockSpec((tm, k), lambda j, i: (i, 0)),
                  pl.BlockSpec((None, tn, k), lambda j, i: (layer, j, 0))],
        out_specs=pl.BlockSpec((tm, tn), lambda j, i: (i, j)),
        out_shape=jax.ShapeDtypeStruct((m, n), F32),
        compiler_params=_cparams(("arbitrary", "arbitrary")),
        name="in_proj",
    )(u2d, wp)


def _ssd_kernel(xbc_ref, z_ref, dtb_ref, conv0_ref, st0_ref, cw_ref, cb_ref, dtbias_ref, alog_ref,
                dskip_ref, nw_ref, e_ref, et_ref, y_ref, st_ref, convo_ref, ext_ref, *, T, nchunks, heads):
    c = pl.program_id(1)
    ssd_w = heads * SSD_HEAD_DIM
    gw = ssd_w // SSD_GROUPS
    hpg = heads // SSD_GROUPS
    conv_dim = ext_ref.shape[1]
    dt_lo = MLA_ROPE_DIM

    @pl.when(c == 0)
    def _():
        ext_ref[0:8, :] = jnp.zeros((8, conv_dim), F32)
        ext_ref[8 - (SSD_CONV - 1):8, :] = conv0_ref[0]
        st_ref[0] = st0_ref[0]

    ext_ref[8:8 + T, :] = xbc_ref[...]
    acc = cb_ref[...]
    for j in range(SSD_CONV):
        lo = 8 - (SSD_CONV - 1) + j
        acc = acc + ext_ref[lo:lo + T, :] * cw_ref[j:j + 1, :]
    ext_ref[0:8, :] = ext_ref[T:T + 8, :]

    @pl.when(c == nchunks - 1)
    def _():
        convo_ref[0] = ext_ref[8 - (SSD_CONV - 1):8, :]

    xact = _silu(acc)
    xs = xact[:, :ssd_w]
    bm = xact[:, ssd_w:ssd_w + SSD_GROUPS * SSD_STATE]
    cm = xact[:, ssd_w + SSD_GROUPS * SSD_STATE:]

    lane = lax.broadcasted_iota(jnp.int32, (1, LANES), 1)
    is_dt = (lane >= dt_lo) & (lane < dt_lo + heads)
    xdt = dtb_ref[...] + dtbias_ref[...]
    dt = jnp.where(is_dt, jnp.maximum(xdt, 0.0) + jnp.log1p(jnp.exp(-jnp.abs(xdt))), 0.0)
    a_neg = jnp.where(is_dt, -jnp.exp(alog_ref[...]), 0.0)
    da = dt * a_neg

    row = lax.broadcasted_iota(jnp.int32, (T, T), 0)
    col = lax.broadcasted_iota(jnp.int32, (T, T), 1)
    causal = row >= col
    tril = causal.astype(BF16)
    eye = (lax.broadcasted_iota(jnp.int32, (LANES, LANES), 0)
           == lax.broadcasted_iota(jnp.int32, (LANES, LANES), 1)).astype(BF16)

    da3 = _split3(da)
    a_cs = sum(jnp.dot(tril, p, preferred_element_type=F32) for p in da3)
    a3 = _split3(a_cs)
    a_cs_t = sum(_nt_dot(eye, p) for p in a3)
    a_last = a_cs[T - 1:T, :]
    ea = jnp.exp(a_cs)
    te = jnp.exp(a_last - a_cs)

    e_mat = e_ref[...]

    def expand(v):
        hi, lo = _split2(v)
        return jnp.dot(hi, e_mat, preferred_element_type=F32) + jnp.dot(lo, e_mat, preferred_element_type=F32)

    dt_x = expand(dt)
    ea_x = expand(ea)
    te_x = expand(te)
    cd_col = jnp.exp(a_cs_t[:, T - 1:T])
    cd_b = jnp.broadcast_to(cd_col, (LANES, SSD_STATE))
    cdh, cdl = _split2(cd_b)
    et_mat = et_ref[...]
    cd_full = (jnp.dot(et_mat, cdh, preferred_element_type=F32)
               + jnp.dot(et_mat, cdl, preferred_element_type=F32))

    xd = xs * dt_x
    xde = (xd * te_x).astype(BF16)
    lane_p = lax.broadcasted_iota(jnp.int32, (T, LANES), 1)
    lower_half = lane_p < SSD_HEAD_DIM

    for g in range(SSD_GROUPS):
        cg = cm[:, g * SSD_STATE:(g + 1) * SSD_STATE].astype(BF16)
        bg = bm[:, g * SSD_STATE:(g + 1) * SSD_STATE].astype(BF16)
        cbm = _nt_dot(cg, bg)
        st_g = st_ref[0, g * hpg:(g + 1) * hpg].reshape(gw, SSD_STATE)
        y_off = _nt_dot(cg, st_g.astype(BF16)) * ea_x[:, g * gw:(g + 1) * gw]
        pieces = []
        for q in range(hpg // 2):
            c0 = g * gw + q * LANES
            xd_pair = xd[:, c0:c0 + LANES].astype(BF16)
            ys = []
            for h in (g * hpg + 2 * q, g * hpg + 2 * q + 1):
                seg = a_cs[:, dt_lo + h:dt_lo + h + 1] - a_cs_t[dt_lo + h:dt_lo + h + 1, :]
                decay = jnp.exp(jnp.where(causal, seg, -jnp.inf))
                ys.append(jnp.dot((cbm * decay).astype(BF16), xd_pair, preferred_element_type=F32))
            pieces.append(jnp.where(lower_half, ys[0], ys[1]))
        y_g = jnp.concatenate(pieces, axis=1) + y_off
        upd = lax.dot_general(xde[:, g * gw:(g + 1) * gw], bg, (((0,), (0,)), ((), ())),
                              preferred_element_type=F32)
        st_new = st_g * cd_full[g * gw:(g + 1) * gw, :] + upd
        st_ref[0, g * hpg:(g + 1) * hpg] = st_new.reshape(hpg, SSD_HEAD_DIM, SSD_STATE)
        y_g = y_g + dskip_ref[:, g * gw:(g + 1) * gw] * xs[:, g * gw:(g + 1) * gw]
        yg = y_g * _silu(z_ref[:, g * gw:(g + 1) * gw])
        ms = jnp.mean(yg * yg, axis=-1, keepdims=True)
        y_ref[:, g * gw:(g + 1) * gw] = (yg * lax.rsqrt(ms + 1e-6) * nw_ref[:, g * gw:(g + 1) * gw]).astype(BF16)


def _ssd_call(h2d, conv0, st0, conv_w, conv_b, dtbias_p, alog_p, dskip_x, norm_w, e_mat, et_mat, cols, nb, L, T):
    nchunks = L // T
    heads = cols.heads
    ssd_w = cols.ssd_w
    cd = cols.conv_dim
    kern = functools.partial(_ssd_kernel, T=T, nchunks=nchunks, heads=heads)
    row = lambda b, c: b * nchunks + c
    const2 = lambda b, c: (0, 0)
    return pl.pallas_call(
        kern,
        grid=(nb, nchunks),
        in_specs=[pl.BlockSpec((T, cd), lambda b, c: (row(b, c), cols.xbc // cd)),
                  pl.BlockSpec((T, ssd_w), lambda b, c: (row(b, c), cols.z // ssd_w)),
                  pl.BlockSpec((T, LANES), lambda b, c: (row(b, c), cols.krdt // LANES)),
                  pl.BlockSpec((1, SSD_CONV - 1, cd), lambda b, c: (b, 0, 0)),
                  pl.BlockSpec((1, heads, SSD_HEAD_DIM, SSD_STATE), lambda b, c: (b, 0, 0, 0)),
                  pl.BlockSpec((SSD_CONV, cd), const2),
                  pl.BlockSpec((1, cd), const2),
                  pl.BlockSpec((1, LANES), const2),
                  pl.BlockSpec((1, LANES), const2),
                  pl.BlockSpec((1, ssd_w), const2),
                  pl.BlockSpec((1, ssd_w), const2),
                  pl.BlockSpec((LANES, ssd_w), const2),
                  pl.BlockSpec((ssd_w, LANES), const2)],
        out_specs=[pl.BlockSpec((T, ssd_w), lambda b, c: (row(b, c), 0)),
                   pl.BlockSpec((1, heads, SSD_HEAD_DIM, SSD_STATE), lambda b, c: (b, 0, 0, 0)),
                   pl.BlockSpec((1, SSD_CONV - 1, cd), lambda b, c: (b, 0, 0))],
        out_shape=[jax.ShapeDtypeStruct((nb * L, ssd_w), BF16),
                   jax.ShapeDtypeStruct((nb, heads, SSD_HEAD_DIM, SSD_STATE), F32),
                   jax.ShapeDtypeStruct((nb, SSD_CONV - 1, cd), F32)],
        scratch_shapes=[pltpu.VMEM((T + 8, cd), F32)],
        compiler_params=_cparams(("arbitrary", "arbitrary")),
        name="ssd_scan",
    )(h2d, h2d, h2d, conv0, st0, conv_w, conv_b, dtbias_p, alog_p, dskip_x, norm_w, e_mat, et_mat)


def _rope_tables(pos, rot_dim, period, width):
    half = rot_dim // 2
    inv = ROPE_THETA ** (-jnp.arange(half, dtype=F32) * (2.0 / rot_dim))
    ang = pos.astype(F32)[:, None] * inv[None, :]
    cos, sin = jnp.cos(ang), jnp.sin(ang)
    lane = np.arange(LANES)
    inner = lane % period
    idx = jnp.asarray(inner % half)
    first = jnp.asarray((inner < half) & (lane < width))
    second = jnp.asarray((inner >= half) & (inner < rot_dim) & (lane < width))
    keep = jnp.asarray((inner >= rot_dim) & (lane < width))
    cos_l, sin_l = cos[:, idx], sin[:, idx]
    cos_t = jnp.where(first | second, cos_l, jnp.where(keep, 1.0, 0.0))
    sin_a = jnp.where(first, -sin_l, 0.0)
    sin_b = jnp.where(second, sin_l, 0.0)
    return cos_t.astype(F32), sin_a.astype(F32), sin_b.astype(F32)


def _rope_tile(x, cos_t, sin_a, sin_b, half):
    return (x * cos_t + pltpu.roll(x, LANES - half, 1) * sin_a + pltpu.roll(x, half, 1) * sin_b)


def _rmsnorm(x, w):
    return x * lax.rsqrt(jnp.mean(x * x, axis=-1, keepdims=True) + 1e-6) * w


def _prep_kernel(dq_ref, dk_ref, dv_ref, cq_ref, kr_ref, cd_ref, sad_ref, sbd_ref, cm_ref, sam_ref, sbm_ref,
                 kvw_ref, *rest, per_head, n_alias):
    if per_head:
        wuk_ref, rest = rest[0], rest[1:]
    rest = rest[n_alias:]
    if per_head:
        k32_ref, v32_ref, lat_ref, kro_ref, qd_ref, kb_ref, vb_ref, latb_ref, kt_ref = rest
    else:
        k32_ref, v32_ref, lat_ref, kro_ref, qd_ref, kb_ref, vb_ref, kcat_ref = rest
    cos_d, sa_d, sb_d = cd_ref[...], sad_ref[...], sbd_ref[...]
    width = dq_ref.shape[1]
    for c in range(width // LANES):
        sl = slice(c * LANES, (c + 1) * LANES)
        q = _rope_tile(dq_ref[:, sl], cos_d, sa_d, sb_d, DIFF_ROT // 2)
        qd_ref[0, :, sl] = (q * (DIFF_SCALE * LOG2E)).astype(BF16)
        k = _rope_tile(dk_ref[:, sl], cos_d, sa_d, sb_d, DIFF_ROT // 2)
        k32_ref[0, :, sl] = k
        kb_ref[0, :, sl] = k.astype(BF16)
    v = dv_ref[...]
    v32_ref[0] = v
    vb_ref[0] = v.astype(BF16)
    lat = _rmsnorm(cq_ref[:, MLA_Q_RANK:MLA_Q_RANK + MLA_KV_RANK], kvw_ref[...])
    lat_ref[0] = lat
    kr = _rope_tile(kr_ref[...], cm_ref[...], sam_ref[...], sbm_ref[...], MLA_ROPE_DIM // 2)
    kro_ref[0] = kr[:, :MLA_ROPE_DIM]
    lat_b = lat.astype(BF16)
    if per_head:
        latb_ref[0] = lat_b
        k_nope = jnp.dot(lat_b, wuk_ref[...], preferred_element_type=F32)
        for h in range(kt_ref.shape[1]):
            k_h = jnp.concatenate([k_nope[:, h * MLA_NOPE_DIM:(h + 1) * MLA_NOPE_DIM], kr], axis=1)
            kt_ref[0, h, 0] = k_h.T.astype(BF16)
    else:
        kcat_ref[0, :, :MLA_KV_RANK] = lat_b
        kcat_ref[0, :, MLA_KV_RANK:] = kr.astype(BF16)


def _prep_call(h2d, tabs_d, tabs_m, kv_w, cols, nb, L, tm, layer, state_bufs, wuk_flat=None):
    nt = L // tm
    dw = cols.diff_w
    per_head = wuk_flat is not None
    row = lambda b, i: b * nt + i
    hspec = lambda width, off: pl.BlockSpec((tm, width), lambda b, i: (row(b, i), off // width))
    tspec = pl.BlockSpec((tm, LANES), lambda b, i: (i, 0))
    ospec = lambda width: pl.BlockSpec((1, tm, width), lambda b, i: (b, i, 0))
    sspec = lambda width: pl.BlockSpec((None, 1, tm, width), lambda b, i: (layer, b, i, 0))
    in_specs = [hspec(dw, cols.dq), hspec(dw, cols.dk), hspec(dw, cols.dv),
                hspec(MLA_Q_RANK + MLA_KV_RANK, cols.cq), hspec(LANES, cols.krdt),
                tspec, tspec, tspec, tspec, tspec, tspec,
                pl.BlockSpec((1, MLA_KV_RANK), lambda b, i: (0, 0))]
    args = [h2d, h2d, h2d, h2d, h2d, *tabs_d, *tabs_m, kv_w]
    if per_head:
        in_specs.append(pl.BlockSpec(wuk_flat.shape, lambda b, i: (0, 0)))
        args.append(wuk_flat)
    aliases = {len(args) + n: n for n in range(len(state_bufs))}
    in_specs += [pl.BlockSpec(memory_space=pl.ANY)] * len(state_bufs)
    args += list(state_bufs)
    out_specs = [sspec(dw), sspec(dw), sspec(MLA_KV_RANK), sspec(MLA_ROPE_DIM), ospec(dw), ospec(dw), ospec(dw)]
    out_shape = [jax.ShapeDtypeStruct(b.shape, b.dtype) for b in state_bufs]
    out_shape += [jax.ShapeDtypeStruct((nb, L, dw), BF16)] * 3
    if per_head:
        heads = wuk_flat.shape[1] // MLA_NOPE_DIM
        out_specs += [ospec(MLA_KV_RANK),
                      pl.BlockSpec((1, heads, 1, MLA_QH_DIM, tm), lambda b, i: (b, 0, i, 0, 0))]
        out_shape += [jax.ShapeDtypeStruct((nb, L, MLA_KV_RANK), BF16),
                      jax.ShapeDtypeStruct((nb, heads, nt, MLA_QH_DIM, tm), BF16)]
    else:
        out_specs.append(ospec(MLA_QK_PAD))
        out_shape.append(jax.ShapeDtypeStruct((nb, L, MLA_QK_PAD), BF16))
    return pl.pallas_call(
        functools.partial(_prep_kernel, per_head=per_head, n_alias=len(state_bufs)),
        grid=(nb, nt),
        in_specs=in_specs, out_specs=out_specs, out_shape=out_shape,
        input_output_aliases=aliases,
        compiler_params=_cparams(("arbitrary", "arbitrary")),
        name="attn_prep",
    )(*args)


def _mlaq_kernel(cq_ref, qw_ref, wuq_ref, wuk_ref, cm_ref, sam_ref, sbm_ref, o_ref, *, heads, absorb):
    cqn = _rmsnorm(cq_ref[:, :MLA_Q_RANK], qw_ref[...]).astype(BF16)
    qm = jnp.dot(cqn, wuq_ref[...], preferred_element_type=F32)
    cos_m, sa_m, sb_m = cm_ref[...], sam_ref[...], sbm_ref[...]
    nope_w = heads * MLA_NOPE_DIM
    scale = MLA_SCALE * LOG2E
    for h in range(heads):
        nope = qm[:, h * MLA_NOPE_DIM:(h + 1) * MLA_NOPE_DIM]
        qr = _rope_tile(qm[:, nope_w + h * LANES:nope_w + (h + 1) * LANES], cos_m, sa_m, sb_m, MLA_ROPE_DIM // 2)
        if absorb:
            ql = jnp.dot(nope.astype(BF16), wuk_ref[h], preferred_element_type=F32)
        else:
            ql = nope
        width = ql.shape[1]
        o_ref[0, h, :, :width] = (ql * scale).astype(BF16)
        o_ref[0, h, :, width:] = (qr * scale).astype(BF16)


def _mlaq_call(h2d, q_w, wuq_p, wuk_t, tabs_m, cols, nb, L, heads, absorb):
    tm = min(L, 512)
    nt = L // tm
    width = MLA_Q_RANK + MLA_KV_RANK
    qdim = MLA_QK_PAD if absorb else MLA_QH_DIM
    tspec = pl.BlockSpec((tm, LANES), lambda b, i: (i, 0))
    return pl.pallas_call(
        functools.partial(_mlaq_kernel, heads=heads, absorb=absorb),
        grid=(nb, nt),
        in_specs=[pl.BlockSpec((tm, width), lambda b, i: (b * nt + i, cols.cq // width)),
                  pl.BlockSpec((1, MLA_Q_RANK), lambda b, i: (0, 0)),
                  pl.BlockSpec(wuq_p.shape, lambda b, i: (0, 0)),
                  pl.BlockSpec(wuk_t.shape, lambda b, i: (0, 0, 0)),
                  tspec, tspec, tspec],
        out_specs=pl.BlockSpec((1, heads, tm, qdim), lambda b, i: (b, 0, i, 0)),
        out_shape=jax.ShapeDtypeStruct((nb, heads, L, qdim), BF16),
        compiler_params=_cparams(("arbitrary", "arbitrary")),
        name="mla_q",
    )(h2d, q_w, wuq_p, wuk_t, *tabs_m)


def _softmax_init(m_ref, l_ref, acc_ref):
    m_ref[...] = jnp.full(m_ref.shape, -jnp.inf, F32)
    l_ref[...] = jnp.zeros(l_ref.shape, F32)
    acc_ref[...] = jnp.zeros(acc_ref.shape, F32)


def _softmax_update(s, v, m_ref, l_ref, acc_ref):
    m_prev = m_ref[...]
    m_new = jnp.maximum(m_prev, jnp.max(s, axis=-1, keepdims=True))
    alpha = jnp.exp2(m_prev - m_new)
    p = jnp.exp2(s - m_new)
    l_ref[...] = alpha * l_ref[...] + jnp.sum(p, axis=-1, keepdims=True)
    acc_ref[...] = alpha * acc_ref[...] + jnp.dot(p.astype(BF16), v, preferred_element_type=F32)
    m_ref[...] = m_new


def _diag_mask(rows, tq, tk, q0):
    q_tok = q0 + (lax.broadcasted_iota(jnp.int32, (rows, tk), 0) & (tq - 1))
    k_tok = lax.broadcasted_iota(jnp.int32, (rows, tk), 1)
    return (k_tok // CHUNK) <= (q_tok // CHUNK)


def _causal_sweep(n_full, qk, upd, finish, sa_ref, sb_ref):
    qk(0, sa_ref)

    def pair(jj, carry):
        j = 2 * jj
        qk(j + 1, sb_ref)
        upd(j, sa_ref)
        qk(j + 2, sa_ref)
        upd(j + 1, sb_ref)
        return carry

    lax.fori_loop(0, n_full // 2, pair, 0)

    @pl.when(n_full % 2 == 1)
    def _():
        qk(n_full, sb_ref)
        upd(n_full - 1, sa_ref)
        finish(sb_ref)

    @pl.when(n_full % 2 == 0)
    def _():
        finish(sa_ref)


def _softmax_init_wide(m_ref, acc_ref):
    m_ref[...] = jnp.full(m_ref.shape, -jnp.inf, F32)
    acc_ref[...] = jnp.zeros(acc_ref.shape, F32)


def _with_ones(v):
    return jnp.concatenate([v, jnp.ones((v.shape[0], LANES), BF16)], axis=1)


def _softmax_update_wide(s, pv_fn, m_ref, acc_ref):
    tk = s.shape[1]
    w = min(tk, LANES)
    m_prev = m_ref[...]
    m_new = jnp.maximum(m_prev, jnp.max(s, axis=-1, keepdims=True))
    alpha = jnp.exp2(m_prev - m_new)
    p = jnp.concatenate([jnp.exp2(s[:, c:c + w] - m_new[:, :w]).astype(BF16) for c in range(0, tk, w)], axis=1)
    acc = acc_ref[...]
    acc_ref[...] = jnp.concatenate([alpha] * (acc.shape[1] // LANES), axis=1) * acc + pv_fn(p)
    m_ref[...] = m_new


def _softmax_update_rep(s, v, m_ref, l_ref, acc_ref):
    tk = s.shape[1]
    m_prev = m_ref[...]
    m_new = jnp.maximum(m_prev, jnp.max(s, axis=-1, keepdims=True))
    alpha = jnp.exp2(m_prev - m_new)
    ps = [jnp.exp2(s[:, c:c + LANES] - m_new) for c in range(0, tk, LANES)]
    l_ref[...] = alpha * l_ref[...] + sum(ps[1:], ps[0])
    p = jnp.concatenate([x.astype(BF16) for x in ps], axis=1)
    acc = acc_ref[...]
    acc_ref[...] = (jnp.concatenate([alpha] * (acc.shape[1] // LANES), axis=1) * acc
                    + jnp.dot(p, v, preferred_element_type=F32))
    m_ref[...] = m_new


def _diff_lambda(lam_ref, lam_init):
    s1 = jnp.sum(lam_ref[0:1, :] * lam_ref[1:2, :], axis=-1, keepdims=True)
    s2 = jnp.sum(lam_ref[2:3, :] * lam_ref[3:4, :], axis=-1, keepdims=True)
    return jnp.exp(s1) - jnp.exp(s2) + lam_init


def _diff_finish(o1, o2, lam, nw, gate, lam_init):
    o = o1 - lam * o2
    o = _rmsnorm(o, nw) * (1.0 - lam_init)
    return (o * _silu(gate)).astype(BF16)


def _stack_streams(q):
    lane = lax.broadcasted_iota(jnp.int32, q.shape, 1)
    zero = jnp.zeros_like(q)
    return jnp.concatenate([jnp.where(lane < DIFF_QK_DIM, q, zero), jnp.where(lane >= DIFF_QK_DIM, q, zero)], axis=0)


def _diff_prompt_kernel(q_ref, k_ref, v_ref, lam_ref, nw_ref, g_ref, o_ref,
                        qz_ref, sa_ref, sb_ref, m_ref, acc_ref, *, t, lam_init):
    qi = pl.program_id(2)
    qz_ref[...] = _stack_streams(q_ref[0])
    _softmax_init_wide(m_ref, acc_ref)

    def blk(ref, j):
        return ref[0, pl.ds(pl.multiple_of(j * t, t), t), :]

    def qk(j, s_ref):
        s_ref[...] = _nt_dot(qz_ref[...], blk(k_ref, j))

    def pv_fn(j):
        return lambda p: jnp.dot(p, _with_ones(blk(v_ref, j)), preferred_element_type=F32)

    def upd(j, s_ref):
        _softmax_update_wide(s_ref[...], pv_fn(j), m_ref, acc_ref)

    def finish(s_ref):
        s = jnp.where(_diag_mask(2 * t, t, t, 0), s_ref[...], NEG_INF)
        _softmax_update_wide(s, pv_fn(qi), m_ref, acc_ref)
        o = acc_ref[:, :DIFF_V_DIM] / acc_ref[:, DIFF_V_DIM:]
        lam = _diff_lambda(lam_ref, lam_init)
        o_ref[0] = _diff_finish(o[:t], o[t:], lam, nw_ref[...], g_ref[...], lam_init)

    _causal_sweep(qi, qk, upd, finish, sa_ref, sb_ref)


def _diff_prompt_call(qd, kb, vb, lam_p, norm_w, h2d, cols, lam_init, t):
    nb, L, dw = qd.shape
    heads = dw // DIFF_V_DIM
    nq = L // t
    kern = functools.partial(_diff_prompt_kernel, t=t, lam_init=lam_init)
    seq = pl.BlockSpec((1, L, LANES), lambda b, h, i: (b, 0, h))
    tile = pl.BlockSpec((1, t, LANES), lambda b, h, i: (b, i, h))
    return pl.pallas_call(
        kern,
        grid=(nb, heads, nq),
        in_specs=[tile, seq, seq,
                  pl.BlockSpec((4, DIFF_QK_DIM), lambda b, h, i: (0, 0)),
                  pl.BlockSpec((1, DIFF_V_DIM), lambda b, h, i: (0, 0)),
                  pl.BlockSpec((t, LANES), lambda b, h, i: (b * nq + i, cols.dgate // LANES + h))],
        out_specs=tile,
        out_shape=jax.ShapeDtypeStruct((nb, L, dw), BF16),
        scratch_shapes=[pltpu.VMEM((2 * t, LANES), BF16),
                        pltpu.VMEM((2 * t, t), F32),
                        pltpu.VMEM((2 * t, t), F32),
                        pltpu.VMEM((2 * t, LANES), F32),
                        pltpu.VMEM((2 * t, DIFF_V_DIM + LANES), F32)],
        compiler_params=_cparams(("arbitrary", "arbitrary", "arbitrary")),
        name="diff_attn_prompt",
    )(qd, kb, vb, lam_p, norm_w, h2d)


def _diff_sample_kernel(q_ref, kc_ref, vc_ref, kn_ref, vn_ref, lam_ref, nw_ref, g_ref, o_ref,
                        qz_ref, m_ref, acc_ref, *, nkc, heads, lam_init):
    j = pl.program_id(1)
    L = q_ref.shape[1]
    rows = 2 * L

    @pl.when(j == 0)
    def _():
        for h in range(heads):
            qz_ref[h] = _stack_streams(q_ref[0, :, h * LANES:(h + 1) * LANES])
        _softmax_init_wide(m_ref, acc_ref)

    def step(scores, values):
        def pv_fn(p):
            return jnp.concatenate([jnp.dot(p[h * rows:(h + 1) * rows], _with_ones(values(h)),
                                            preferred_element_type=F32) for h in range(heads)], axis=0)
        s = jnp.concatenate([scores(h) for h in range(heads)], axis=0)
        _softmax_update_wide(s, pv_fn, m_ref, acc_ref)

    @pl.when(j < nkc)
    def _():
        tk = kc_ref.shape[5]

        def scores(h):
            return jnp.dot(qz_ref[h], kc_ref[0, 0, h].reshape(2 * DIFF_QK_DIM, tk).astype(BF16),
                           preferred_element_type=F32)

        def values(h):
            return vc_ref[0, 0, pl.ds(h, tk, stride=heads), :].astype(BF16)

        step(scores, values)

    @pl.when(j == nkc)
    def _():
        step(lambda h: _nt_dot(qz_ref[h], kn_ref[0, :, h * LANES:(h + 1) * LANES]),
             lambda h: vn_ref[0, :, h * LANES:(h + 1) * LANES])
        lam = _diff_lambda(lam_ref, lam_init)
        o = acc_ref[:, :DIFF_V_DIM] / acc_ref[:, DIFF_V_DIM:]
        for h in range(heads):
            sl = slice(h * LANES, (h + 1) * LANES)
            oh = o[h * rows:(h + 1) * rows]
            o_ref[0, :, sl] = _diff_finish(oh[:L], oh[L:], lam, nw_ref[...], g_ref[:, sl], lam_init)


def _diff_sample_call(qd, kb, vb, k_cache, v_cache, layer, lam_p, norm_w, h2d, cols, lam_init, tk):
    nb, L, dw = qd.shape
    heads = dw // DIFF_V_DIM
    P = k_cache.shape[5]
    nkc = P // tk
    last = nkc - 1
    kern = functools.partial(_diff_sample_kernel, nkc=nkc, heads=heads, lam_init=lam_init)
    kspec = pl.BlockSpec((1, 1, heads, 2, DIFF_QK_DIM, tk),
                         lambda b, j: (layer, b, 0, 0, 0, jnp.minimum(j, last)))
    vspec = pl.BlockSpec((1, 1, tk * heads, DIFF_V_DIM), lambda b, j: (layer, b, jnp.minimum(j, last), 0))
    nspec = pl.BlockSpec((1, L, dw), lambda b, j: (b, 0, 0))
    return pl.pallas_call(
        kern,
        grid=(nb, nkc + 1),
        in_specs=[nspec, kspec, vspec, nspec, nspec,
                  pl.BlockSpec((4, DIFF_QK_DIM), lambda b, j: (0, 0)),
                  pl.BlockSpec((1, DIFF_V_DIM), lambda b, j: (0, 0)),
                  pl.BlockSpec((L, dw), lambda b, j: (b, cols.dgate // dw))],
        out_specs=nspec,
        out_shape=jax.ShapeDtypeStruct((nb, L, dw), BF16),
        scratch_shapes=[pltpu.VMEM((heads, 2 * L, LANES), BF16),
                        pltpu.VMEM((heads * 2 * L, LANES), F32),
                        pltpu.VMEM((heads * 2 * L, DIFF_V_DIM + LANES), F32)],
        compiler_params=_cparams(("arbitrary", "arbitrary")),
        name="diff_attn_sample",
    )(qd, k_cache, v_cache, kb, vb, lam_p, norm_w, h2d)


def _mla_finish(acc_ref, l, wuv_ref, g_ref, o_ref, heads, t):
    o = (acc_ref[...] / l).astype(BF16)
    for h in range(heads):
        sl = slice(h * MLA_V_DIM, (h + 1) * MLA_V_DIM)
        om = jnp.dot(o[h * t:(h + 1) * t], wuv_ref[:, sl], preferred_element_type=F32)
        o_ref[0, :, sl] = (om * _silu(g_ref[:, sl])).astype(BF16)


def _mla_prompt_kernel(q_ref, kt_ref, v_ref, wuv_ref, g_ref, o_ref, sa_ref, sb_ref, m_ref, l_ref, acc_ref, *, t, hg):
    qi = pl.program_id(2)
    _softmax_init(m_ref, l_ref, acc_ref)

    def vblk(j):
        return v_ref[0, pl.ds(pl.multiple_of(j * t, t), t), :]

    def qk(j, s_ref):
        for g in range(hg):
            s_ref[g * t:(g + 1) * t, :] = jnp.dot(q_ref[0, g], kt_ref[0, g, j], preferred_element_type=F32)

    def upd(j, s_ref):
        _softmax_update_rep(s_ref[...], vblk(j), m_ref, l_ref, acc_ref)

    def finish(s_ref):
        s = jnp.where(_diag_mask(hg * t, t, t, 0), s_ref[...], NEG_INF)
        _softmax_update_rep(s, vblk(qi), m_ref, l_ref, acc_ref)
        _mla_finish(acc_ref, jnp.sum(l_ref[...], axis=-1, keepdims=True), wuv_ref, g_ref, o_ref, hg, t)

    _causal_sweep(qi, qk, upd, finish, sa_ref, sb_ref)


def _mla_prompt_call(qh, kt, latb, wuv, h2d, cols, t, hg):
    nb, heads, L, qdim = qh.shape
    mw = heads * MLA_V_DIM
    gw = hg * MLA_V_DIM
    nq = L // t
    return pl.pallas_call(
        functools.partial(_mla_prompt_kernel, t=t, hg=hg),
        grid=(nb, heads // hg, nq),
        in_specs=[pl.BlockSpec((1, hg, t, qdim), lambda b, h, i: (b, h, i, 0)),
                  pl.BlockSpec((1, hg, nq, qdim, t), lambda b, h, i: (b, h, 0, 0, 0)),
                  pl.BlockSpec((1, L, MLA_KV_RANK), lambda b, h, i: (b, 0, 0)),
                  pl.BlockSpec((MLA_KV_RANK, gw), lambda b, h, i: (0, h)),
                  pl.BlockSpec((t, gw), lambda b, h, i: (b * nq + i, cols.mgate // gw + h))],
        out_specs=pl.BlockSpec((1, t, gw), lambda b, h, i: (b, i, h)),
        out_shape=jax.ShapeDtypeStruct((nb, L, mw), BF16),
        scratch_shapes=[pltpu.VMEM((hg * t, t), F32),
                        pltpu.VMEM((hg * t, t), F32),
                        pltpu.VMEM((hg * t, LANES), F32),
                        pltpu.VMEM((hg * t, LANES), F32),
                        pltpu.VMEM((hg * t, MLA_KV_RANK), F32)],
        compiler_params=_cparams(("arbitrary", "arbitrary", "arbitrary")),
        name="mla_attn_prompt",
    )(qh, kt, latb, wuv, h2d)


def _mla_sample_kernel(q_ref, lc_ref, rc_ref, kn_ref, wuv_ref, g_ref, o_ref, m_ref, l_ref, acc_ref,
                       *, nkc, heads):
    j = pl.program_id(1)
    L = q_ref.shape[2]

    @pl.when(j == 0)
    def _():
        _softmax_init(m_ref, l_ref, acc_ref)

    q = q_ref[0].reshape(heads * L, MLA_QK_PAD)

    @pl.when(j < nkc)
    def _():
        lat = lc_ref[0, 0].astype(BF16)
        kr_t = rc_ref[0, 0].astype(BF16)
        s = _nt_dot(q[:, :MLA_KV_RANK], lat) + jnp.dot(q[:, MLA_KV_RANK:MLA_KV_RANK + MLA_ROPE_DIM], kr_t,
                                                       preferred_element_type=F32)
        _softmax_update(s, lat, m_ref, l_ref, acc_ref)

    @pl.when(j == nkc)
    def _():
        kn = kn_ref[0]
        _softmax_update(_nt_dot(q, kn), kn[:, :MLA_KV_RANK], m_ref, l_ref, acc_ref)
        _mla_finish(acc_ref, l_ref[...], wuv_ref, g_ref, o_ref, heads, L)


def _mla_sample_call(qcat, kcat, lat_cache, kr_cache, layer, wuv, h2d, cols, tk):
    nb, heads, L, _ = qcat.shape
    mw = heads * MLA_V_DIM
    P = lat_cache.shape[2]
    nkc = P // tk
    last = nkc - 1
    kern = functools.partial(_mla_sample_kernel, nkc=nkc, heads=heads)
    return pl.pallas_call(
        kern,
        grid=(nb, nkc + 1),
        in_specs=[pl.BlockSpec((1, heads, L, MLA_QK_PAD), lambda b, j: (b, 0, 0, 0)),
                  pl.BlockSpec((1, 1, tk, MLA_KV_RANK), lambda b, j: (layer, b, jnp.minimum(j, last), 0)),
                  pl.BlockSpec((1, 1, MLA_ROPE_DIM, tk), lambda b, j: (layer, b, 0, jnp.minimum(j, last))),
                  pl.BlockSpec((1, L, MLA_QK_PAD), lambda b, j: (b, 0, 0)),
                  pl.BlockSpec(wuv.shape, lambda b, j: (0, 0)),
                  pl.BlockSpec((L, mw), lambda b, j: (b, cols.mgate // mw))],
        out_specs=pl.BlockSpec((1, L, mw), lambda b, j: (b, 0, 0)),
        out_shape=jax.ShapeDtypeStruct((nb, L, mw), BF16),
        scratch_shapes=[pltpu.VMEM((heads * L, 1), F32),
                        pltpu.VMEM((heads * L, 1), F32),
                        pltpu.VMEM((heads * L, MLA_KV_RANK), F32)],
        compiler_params=_cparams(("arbitrary", "arbitrary")),
        name="mla_attn_sample",
    )(qcat, lat_cache, kr_cache, kcat, wuv, h2d)


def _outproj_kernel(ys_ref, yd_ref, ym_ref, w_ref, x_ref, g_ref, r_ref, *, alpha):
    bt, lt, _ = ys_ref.shape
    flat = lambda ref: ref[...].reshape(bt * lt, ref.shape[2])
    mix = jnp.concatenate([flat(ys_ref), flat(yd_ref), flat(ym_ref)], axis=1)
    acc = jnp.dot(mix, w_ref[...], preferred_element_type=F32)
    r_ref[...] = alpha * x_ref[...] + g_ref[...] * acc.reshape(bt, lt, acc.shape[1])


def _outproj_call(y_ssd, y_diff, y_mla, w_out, x, gate, alpha):
    nb, L, d = x.shape
    bt, lt = _seq_tiles(nb, L, 512)
    tn = 1024
    yspec = lambda width: pl.BlockSpec((bt, lt, width), lambda j, b, i: (b, i, 0))
    return pl.pallas_call(
        functools.partial(_outproj_kernel, alpha=alpha),
        grid=(d // tn, nb // bt, L // lt),
        in_specs=[yspec(y_ssd.shape[2]), yspec(y_diff.shape[2]), yspec(y_mla.shape[2]),
                  pl.BlockSpec((w_out.shape[0], tn), lambda j, b, i: (0, j)),
                  pl.BlockSpec((bt, lt, tn), lambda j, b, i: (b, i, j)),
                  pl.BlockSpec((bt, 1, tn), lambda j, b, i: (b, 0, j))],
        out_specs=pl.BlockSpec((bt, lt, tn), lambda j, b, i: (b, i, j)),
        out_shape=jax.ShapeDtypeStruct((nb, L, d), F32),
        compiler_params=_cparams(("arbitrary", "arbitrary", "arbitrary")),
        name="out_proj",
    )(y_ssd, y_diff, y_mla, w_out, x, gate)


def _layernorm(r, g, b):
    mu = jnp.mean(r, axis=-1, keepdims=True)
    var = jnp.mean(jnp.square(r - mu), axis=-1, keepdims=True)
    return (r - mu) * lax.rsqrt(var + 1e-5) * g + b


def _ln_kernel(r_ref, g_ref, b_ref, x_ref):
    x_ref[...] = _layernorm(r_ref[...], g_ref[...], b_ref[...])


def _ln_mod_kernel(r_ref, g_ref, b_ref, sc_ref, sh_ref, x_ref, u_ref):
    x = _layernorm(r_ref[...], g_ref[...], b_ref[...])
    x_ref[...] = x
    u_ref[...] = (x * (1.0 + sc_ref[...]) + sh_ref[...]).astype(BF16)


def _ln_call(r, g, b, scale=None, shift=None):
    nb, L, d = r.shape
    bt, lt = _seq_tiles(nb, L, 256)
    xspec = pl.BlockSpec((bt, lt, d), lambda i, j: (i, j, 0))
    wspec = pl.BlockSpec((1, 1, d), lambda i, j: (0, 0, 0))
    sspec = pl.BlockSpec((bt, 1, d), lambda i, j: (i, 0, 0))
    g3, b3 = g.reshape(1, 1, d), b.reshape(1, 1, d)
    if scale is None:
        return pl.pallas_call(
            _ln_kernel, grid=(nb // bt, L // lt),
            in_specs=[xspec, wspec, wspec], out_specs=xspec,
            out_shape=jax.ShapeDtypeStruct((nb, L, d), F32),
            compiler_params=_cparams(("arbitrary", "arbitrary")), name="layernorm",
        )(r, g3, b3), None
    return pl.pallas_call(
        _ln_mod_kernel, grid=(nb // bt, L // lt),
        in_specs=[xspec, wspec, wspec, sspec, sspec], out_specs=[xspec, xspec],
        out_shape=[jax.ShapeDtypeStruct((nb, L, d), F32), jax.ShapeDtypeStruct((nb, L, d), BF16)],
        compiler_params=_cparams(("arbitrary", "arbitrary")), name="layernorm_modulate",
    )(r, g3, b3, scale, shift)


def _ssd_chunk(L):
    for t in (128, 64, 32, 16, 8):
        if L % t == 0:
            return t
    raise ValueError(f"sequence length {L} is not a multiple of 8")


def _attn_tile(L, want):
    t = want
    while L % t:
        t //= 2
    return t


def _layer(x, u, mod_l, next_mod, wl, cols, layer_idx, depth, caches, state_bufs):
    nb, L, d = x.shape
    heads = cols.heads
    mla_heads = cols.mla_w // MLA_V_DIM
    _, _, gate = mod_l
    k_cache, v_cache, lat_cache, kr_cache, st0, conv0 = caches
    P = 0 if k_cache is None else lat_cache.shape[2]
    pos = P + jnp.arange(L, dtype=jnp.int32)

    h2d = _inproj_call(u.reshape(nb * L, d), wl["w_in"], layer_idx, cols.tn)

    T = _ssd_chunk(L)
    y_ssd, ssm_new, conv_new = _ssd_call(h2d, conv0, st0, wl["conv_w"], wl["conv_b"], wl["dt_bias"], wl["a_log"],
                                         wl["d_skip"], wl["ssd_norm_w"], wl["e_mat"], wl["et_mat"], cols, nb, L, T)

    tabs_d = _rope_tables(pos, DIFF_ROT, DIFF_QK_DIM, LANES)
    tabs_m = _rope_tables(pos, MLA_ROPE_DIM, MLA_ROPE_DIM, MLA_ROPE_DIM)
    lam_init = 0.8 - 0.6 * math.exp(-0.3 * layer_idx)
    if k_cache is None:
        t = _attn_tile(L, 512)
        k32, v32, lat32, kr32, qd, kb, vb, latb, kt = _prep_call(h2d, tabs_d, tabs_m, wl["kv_norm_w"], cols, nb, L, t,
                                                                 layer_idx, state_bufs, wl["w_uk_flat"])
        qh = _mlaq_call(h2d, wl["q_norm_w"], wl["w_uq"], wl["w_uk"], tabs_m, cols, nb, L, mla_heads, False)
        y_diff = _diff_prompt_call(qd, kb, vb, wl["lam"], wl["diff_norm_w"], h2d, cols, lam_init, t)
        y_mla = _mla_prompt_call(qh, kt, latb, wl["w_uv"], h2d, cols, t, MLA_HEAD_GROUP)
    else:
        k32, v32, lat32, kr32, qd, kb, vb, kcat = _prep_call(h2d, tabs_d, tabs_m, wl["kv_norm_w"], cols, nb, L,
                                                             min(L, 256), layer_idx, state_bufs)
        qcat = _mlaq_call(h2d, wl["q_norm_w"], wl["w_uq"], wl["w_uk"], tabs_m, cols, nb, L, mla_heads, True)
        y_diff = _diff_sample_call(qd, kb, vb, k_cache, v_cache, layer_idx, wl["lam"], wl["diff_norm_w"], h2d,
                                   cols, lam_init, _attn_tile(P, 1024))
        y_mla = _mla_sample_call(qcat, kcat, lat_cache, kr_cache, layer_idx, wl["w_uv"], h2d, cols,
                                 _attn_tile(P, 2048))

    alpha = (2 * depth) ** 0.25
    r = _outproj_call(y_ssd.reshape(nb, L, cols.ssd_w), y_diff, y_mla, wl["w_out"], x, gate, alpha)
    if next_mod is None:
        x_new, u_new = _ln_call(r, wl["ln_g"], wl["ln_b"])
    else:
        x_new, u_new = _ln_call(r, wl["ln_g"], wl["ln_b"], next_mod[1], next_mod[0])
    return x_new, u_new, (k32, v32, lat32, kr32), (ssm_new, conv_new)


def _layer_weights(l, cols, w_in, conv_w, conv_b, dt_bias, a_log, d_skip, ssd_norm_w, lambda_q1, lambda_k1,
                   lambda_q2, lambda_k2, diff_norm_w, mla_q_norm_w, mla_kv_norm_w, w_uq, w_uk, w_uv, w_out,
                   ln_g, ln_b):
    heads = cols.heads
    mla_heads = cols.mla_w // MLA_V_DIM
    lane_pad = lambda v: jnp.pad(v, (MLA_ROPE_DIM, LANES - MLA_ROPE_DIM - heads)).reshape(1, LANES)
    qk = MLA_NOPE_DIM + MLA_ROPE_DIM
    wq = w_uq[l].reshape(MLA_Q_RANK, mla_heads, qk)
    wq_nope = wq[:, :, :MLA_NOPE_DIM].reshape(MLA_Q_RANK, mla_heads * MLA_NOPE_DIM)
    wq_rope = jnp.pad(wq[:, :, MLA_NOPE_DIM:], ((0, 0), (0, 0), (0, LANES - MLA_ROPE_DIM)))
    wq_p = jnp.concatenate([wq_nope, wq_rope.reshape(MLA_Q_RANK, mla_heads * LANES)], axis=1).astype(BF16)
    eh = np.zeros((LANES, cols.ssd_w), np.float32)
    for h in range(heads):
        eh[MLA_ROPE_DIM + h, h * SSD_HEAD_DIM:(h + 1) * SSD_HEAD_DIM] = 1.0
    return dict(
        w_in=w_in,
        conv_w=conv_w[l], conv_b=conv_b[l].reshape(1, -1),
        dt_bias=lane_pad(dt_bias[l]), a_log=lane_pad(a_log[l]),
        d_skip=jnp.repeat(d_skip[l], SSD_HEAD_DIM).reshape(1, -1),
        ssd_norm_w=ssd_norm_w[l].reshape(1, -1),
        e_mat=jnp.asarray(eh, BF16), et_mat=jnp.asarray(eh.T, BF16),
        lam=jnp.stack([lambda_q1[l], lambda_k1[l], lambda_q2[l], lambda_k2[l]]),
        diff_norm_w=diff_norm_w[l].reshape(1, -1),
        q_norm_w=mla_q_norm_w[l].reshape(1, -1), kv_norm_w=mla_kv_norm_w[l].reshape(1, -1),
        w_uq=wq_p,
        w_uk=jnp.transpose(w_uk[l], (1, 2, 0)).astype(BF16),
        w_uk_flat=w_uk[l].reshape(MLA_KV_RANK, mla_heads * MLA_NOPE_DIM).astype(BF16),
        w_uv=w_uv[l].reshape(MLA_KV_RANK, mla_heads * MLA_V_DIM).astype(BF16),
        w_out=w_out[l].astype(BF16),
        ln_g=ln_g[l], ln_b=ln_b[l],
    )


def kernel(x_prompt, x_sample, cache_diff_k, cache_diff_v, cache_mla_latent, cache_mla_krope, state_ssm, state_conv,
           c_prompt, c_sample, w_mod, b_mod, w_in, conv_w, conv_b, dt_bias, a_log, d_skip, ssd_norm_w, lambda_q1,
           lambda_k1, lambda_q2, lambda_k2, diff_norm_w, mla_q_norm_w, mla_kv_norm_w, w_uq, w_uk, w_uv, w_out,
           ln_g, ln_b):
    depth = w_in.shape[0]
    bp, _, d = x_prompt.shape
    bs = x_sample.shape[0]
    cols = _Cols(d)
    heads = cols.heads

    rows = -(-(bp + bs) // 8) * 8
    c_all = jnp.concatenate([c_prompt, c_sample, jnp.zeros((rows - bp - bs, d), F32)], axis=0)
    mod = _mod_call(c_all, w_mod, b_mod)

    def mods(l, lo, n):
        part = lambda k: mod[l, lo:lo + n, k * d:(k + 1) * d].reshape(n, 1, d)
        return part(0), part(1), part(2)

    pdiff = cache_diff_k.shape[2]
    kc = jnp.transpose(cache_diff_k, (0, 1, 3, 4, 5, 2))
    vc = cache_diff_v.reshape(depth, bs, pdiff * cache_diff_v.shape[3], DIFF_V_DIM)
    krc = jnp.transpose(cache_mla_krope, (0, 1, 3, 2))
    zero_state = jnp.zeros((bp, heads, SSD_HEAD_DIM, SSD_STATE), F32)
    zero_conv = jnp.zeros((bp, SSD_CONV - 1, cols.conv_dim), F32)

    w_in_p = _permute_w_in(w_in, cols)
    hp, hs = x_prompt, x_sample
    mp, ms = mods(0, 0, bp), mods(0, bp, bs)
    up = _modulate_call(hp, mp[1], mp[0])
    us = _modulate_call(hs, ms[1], ms[0])

    def state_buffers(nb, L):
        return tuple(jnp.zeros((depth, nb, L, w), F32) for w in (cols.diff_w, cols.diff_w, MLA_KV_RANK, MLA_ROPE_DIM))

    bufs_p, bufs_s = state_buffers(bp, hp.shape[1]), state_buffers(bs, hs.shape[1])
    rec_p, rec_s = [], []
    for l in range(depth):
        wl = _layer_weights(l, cols, w_in_p, conv_w, conv_b, dt_bias, a_log, d_skip, ssd_norm_w, lambda_q1, lambda_k1,
                            lambda_q2, lambda_k2, diff_norm_w, mla_q_norm_w, mla_kv_norm_w, w_uq, w_uk, w_uv, w_out,
                            ln_g, ln_b)
        nmp = mods(l + 1, 0, bp) if l + 1 < depth else None
        nms = mods(l + 1, bp, bs) if l + 1 < depth else None
        hp, up, bufs_p, rp = _layer(hp, up, mp, nmp, wl, cols, l, depth,
                                    (None, None, None, None, zero_state, zero_conv), bufs_p)
        hs, us, bufs_s, rs = _layer(hs, us, ms, nms, wl, cols, l, depth,
                                    (kc, vc, cache_mla_latent, krc, state_ssm[l], state_conv[l]), bufs_s)
        rec_p.append(rp)
        rec_s.append(rs)
        mp, ms = nmp, nms

    def states(bufs, rec):
        k32, v32, lat32, kr32 = bufs
        nb, L = k32.shape[1], k32.shape[2]
        return (k32.reshape(depth, nb, L, cols.diff_w // (2 * DIFF_QK_DIM), 2, DIFF_QK_DIM),
                v32.reshape(depth, nb, L, cols.diff_w // DIFF_V_DIM, DIFF_V_DIM), lat32, kr32,
                jnp.stack([r[0] for r in rec]), jnp.stack([r[1] for r in rec]))

    return (hp, hs) + states(bufs_p, rec_p) + states(bufs_s, rec_s)
```

```python
import functools
import math

import numpy as np
import jax
import jax.numpy as jnp
from jax import lax
from jax.experimental import pallas as pl
from jax.experimental.pallas import tpu as pltpu

F32 = jnp.float32
BF16 = jnp.bfloat16

CHUNK = 64
ROPE_THETA = 500000.0
NEG_INF = -1e30
SSD_HEAD_DIM = 64
SSD_GROUPS = 4
SSD_STATE = 128
SSD_CONV = 4
DIFF_QK_DIM = 64
DIFF_V_DIM = 128
DIFF_ROT = DIFF_QK_DIM // 4
DIFF_SCALE = DIFF_QK_DIM ** -0.5
MLA_V_DIM = 128
MLA_NOPE_DIM = 128
MLA_ROPE_DIM = 64
MLA_Q_RANK = 768
MLA_KV_RANK = 256
MLA_SCALE = (MLA_NOPE_DIM + MLA_ROPE_DIM) ** -0.5
MLA_QK_PAD = MLA_KV_RANK + 128
MLA_QH_DIM = MLA_NOPE_DIM + 128
MLA_HEAD_GROUP = 2
LOG2E = math.log2(math.e)

LANES = 128
VMEM_LIMIT = 56 * 1024 * 1024


def _cparams(sem):
    return pltpu.CompilerParams(dimension_semantics=sem, vmem_limit_bytes=VMEM_LIMIT)


def _silu(x):
    hx = 0.5 * x
    return hx + hx * jnp.tanh(hx)


def _nt_dot(a, b):
    return lax.dot_general(a, b, (((1,), (1,)), ((), ())), preferred_element_type=F32)


def _split2(v):
    hi = v.astype(BF16)
    lo = (v - hi.astype(F32)).astype(BF16)
    return hi, lo


def _split3(v):
    hi = v.astype(BF16)
    r = v - hi.astype(F32)
    mid = r.astype(BF16)
    lo = (r - mid.astype(F32)).astype(BF16)
    return hi, mid, lo


def _mod_kernel(c_ref, w_ref, b_ref, o_ref):
    a = _silu(c_ref[...]).astype(BF16)
    o_ref[0] = jnp.dot(a, w_ref[0].astype(BF16), preferred_element_type=F32) + b_ref[0]


def _mod_call(c_all, w_mod, b_mod):
    depth, d, n3 = w_mod.shape
    rows = c_all.shape[0]
    tn = 512
    return pl.pallas_call(
        _mod_kernel,
        grid=(depth, n3 // tn),
        in_specs=[pl.BlockSpec((rows, d), lambda l, j: (0, 0)),
                  pl.BlockSpec((1, d, tn), lambda l, j: (l, 0, j)),
                  pl.BlockSpec((1, 1, tn), lambda l, j: (l, 0, j))],
        out_specs=pl.BlockSpec((1, rows, tn), lambda l, j: (l, 0, j)),
        out_shape=jax.ShapeDtypeStruct((depth, rows, n3), F32),
        compiler_params=_cparams(("arbitrary", "arbitrary")),
        name="adaln_mod",
    )(c_all, w_mod, b_mod.reshape(depth, 1, n3))


def _modulate_kernel(x_ref, sc_ref, sh_ref, u_ref):
    u_ref[...] = (x_ref[...] * (1.0 + sc_ref[...]) + sh_ref[...]).astype(BF16)


def _seq_tiles(nb, L, rows):
    if L >= rows:
        return 1, rows
    return min(nb, rows // L), L


def _modulate_call(x, scale, shift):
    nb, L, d = x.shape
    bt, lt = _seq_tiles(nb, L, 256)
    return pl.pallas_call(
        _modulate_kernel,
        grid=(nb // bt, L // lt),
        in_specs=[pl.BlockSpec((bt, lt, d), lambda i, j: (i, j, 0)),
                  pl.BlockSpec((bt, 1, d), lambda i, j: (i, 0, 0)),
                  pl.BlockSpec((bt, 1, d), lambda i, j: (i, 0, 0))],
        out_specs=pl.BlockSpec((bt, lt, d), lambda i, j: (i, j, 0)),
        out_shape=jax.ShapeDtypeStruct((nb, L, d), BF16),
        compiler_params=_cparams(("arbitrary", "arbitrary")),
        name="modulate",
    )(x, scale, shift)


class _Cols:
    def __init__(self, d_model):
        self.ssd_w = d_model // 2
        self.diff_w = d_model // 4
        self.mla_w = d_model // 4
        self.heads = self.ssd_w // SSD_HEAD_DIM
        self.conv_dim = self.ssd_w + 2 * SSD_GROUPS * SSD_STATE
        self.in_sizes = (self.ssd_w, self.conv_dim, self.heads, self.diff_w, self.diff_w, self.diff_w,
                         self.diff_w, MLA_Q_RANK, MLA_KV_RANK, MLA_ROPE_DIM, self.mla_w)
        o = 0
        self.xbc = o; o += self.conv_dim
        self.dq = o; o += self.diff_w
        self.z = o; o += self.ssd_w
        self.dk = o; o += self.diff_w
        self.dv = o; o += self.diff_w
        self.dgate = o; o += self.diff_w
        self.mgate = o; o += self.mla_w
        self.cq = o; o += MLA_Q_RANK + MLA_KV_RANK
        self.krdt = o; o += LANES
        self.used = o
        self.tn = 1280
        self.total = -(-o // self.tn) * self.tn


def _w_in_blocks(cols):
    offs = np.concatenate([[0], np.cumsum(cols.in_sizes)])
    order = (1, 3, 0, 4, 5, 6, 10)
    rows = []
    for i in order:
        rows += list(range(int(offs[i]), int(offs[i + 1]), LANES))
    rows += list(range(int(offs[7]), int(offs[9]), LANES))
    rows.append(-1)
    rows += [-2] * ((cols.total - cols.used) // LANES)
    return np.asarray(rows, np.int32)


def _wprep_kernel(tab_ref, w_ref, sp_ref, o_ref):
    src = tab_ref[pl.program_id(1)]

    @pl.when(src >= 0)
    def _():
        o_ref[0] = w_ref[0].astype(BF16)

    @pl.when(src == -1)
    def _():
        o_ref[0] = sp_ref[0].astype(BF16)

    @pl.when(src == -2)
    def _():
        o_ref[0] = jnp.zeros(o_ref.shape[1:], BF16)


def _permute_w_in(w_in, cols):
    depth, d, _ = w_in.shape
    wt = jnp.transpose(w_in, (0, 2, 1))
    offs = np.concatenate([[0], np.cumsum(cols.in_sizes)])
    pad = LANES - MLA_ROPE_DIM - cols.heads
    special = jnp.concatenate([wt[:, offs[9]:offs[10]], wt[:, offs[2]:offs[3]], jnp.zeros((depth, pad, d), F32)], axis=1)
    tab = jnp.asarray(_w_in_blocks(cols))
    grid_spec = pltpu.PrefetchScalarGridSpec(
        num_scalar_prefetch=1,
        grid=(depth, int(tab.shape[0])),
        in_specs=[pl.BlockSpec((pl.Element(1), pl.Element(LANES), pl.Element(d)),
                               lambda l, i, tab: (l, pl.multiple_of(jnp.maximum(tab[i], 0), 8), 0)),
                  pl.BlockSpec((1, LANES, d), lambda l, i, tab: (l, 0, 0))],
        out_specs=pl.BlockSpec((1, LANES, d), lambda l, i, tab: (l, i, 0)))
    return pl.pallas_call(
        _wprep_kernel, grid_spec=grid_spec,
        out_shape=jax.ShapeDtypeStruct((depth, cols.total, d), BF16),
        compiler_params=_cparams(("arbitrary", "arbitrary")),
        name="w_in_prep",
    )(tab, wt, special)


def _matmul_nt_kernel(x_ref, w_ref, o_ref):
    o_ref[...] = _nt_dot(x_ref[...], w_ref[...])


def _inproj_call(u2d, wp, layer, tn):
    m, k = u2d.shape
    n = wp.shape[1]
    tm = min(m, 1024)
    return pl.pallas_call(
        _matmul_nt_kernel,
        grid=(n // tn, m // tm),
        in_specs=[pl.BlockSpec((tm, k), lambda j, i: (i, 0)),
                  pl.BlockSpec((None, tn, k), lambda j, i: (layer, j, 0))],
        out_specs=pl.BlockSpec((tm, tn), lambda j, i: (i, j)),
        out_shape=jax.ShapeDtypeStruct((m, n), F32),
        compiler_params=_cparams(("arbitrary", "arbitrary")),
        name="in_proj",
    )(u2d, wp)


def _ssd_kernel(xbc_ref, z_ref, dtb_ref, conv0_ref, st0_ref, cw_ref, cb_ref, dtbias_ref, alog_ref,
                dskip_ref, nw_ref, e_ref, et_ref, y_ref, st_ref, convo_ref, ext_ref, *, T, nchunks, heads):
    c = pl.program_id(1)
    ssd_w = heads * SSD_HEAD_DIM
    gw = ssd_w // SSD_GROUPS
    hpg = heads // SSD_GROUPS
    conv_dim = ext_ref.shape[1]
    dt_lo = MLA_ROPE_DIM

    @pl.when(c == 0)
    def _():
        ext_ref[0:8, :] = jnp.zeros((8, conv_dim), F32)
        ext_ref[8 - (SSD_CONV - 1):8, :] = conv0_ref[0]
        st_ref[0] = st0_ref[0]

    ext_ref[8:8 + T, :] = xbc_ref[...]

    def conv(c0, width):
        acc = cb_ref[:, c0:c0 + width]
        for j in range(SSD_CONV):
            lo = 8 - (SSD_CONV - 1) + j
            acc = acc + ext_ref[lo:lo + T, c0:c0 + width] * cw_ref[j:j + 1, c0:c0 + width]
        return _silu(acc)

    lane = lax.broadcasted_iota(jnp.int32, (1, LANES), 1)
    is_dt = (lane >= dt_lo) & (lane < dt_lo + heads)
    xdt = dtb_ref[...] + dtbias_ref[...]
    dt = jnp.where(is_dt, jnp.maximum(xdt, 0.0) + jnp.log1p(jnp.exp(-jnp.abs(xdt))), 0.0)
    a_neg = jnp.where(is_dt, -jnp.exp(alog_ref[...]), 0.0)
    da = dt * a_neg

    row = lax.broadcasted_iota(jnp.int32, (T, T), 0)
    col = lax.broadcasted_iota(jnp.int32, (T, T), 1)
    causal = row >= col
    tril = causal.astype(BF16)
    eye = (lax.broadcasted_iota(jnp.int32, (LANES, LANES), 0)
           == lax.broadcasted_iota(jnp.int32, (LANES, LANES), 1)).astype(BF16)

    da3 = _split3(da)
    a_cs = sum(jnp.dot(tril, p, preferred_element_type=F32) for p in da3)
    a3 = _split3(a_cs)
    a_cs_t = sum(_nt_dot(eye, p) for p in a3)
    a_last = a_cs[T - 1:T, :]
    dt2 = _split2(dt)
    ea2 = _split2(jnp.exp(a_cs))
    te2 = _split2(jnp.exp(a_last - a_cs))
    cd_col = jnp.exp(a_cs_t[:, T - 1:T])
    cd2 = _split2(jnp.broadcast_to(cd_col, (LANES, SSD_STATE)))
    lane_p = lax.broadcasted_iota(jnp.int32, (T, LANES), 1)
    lower_half = lane_p < SSD_HEAD_DIM
    bc0 = ssd_w

    for g in range(SSD_GROUPS):
        gs = slice(g * gw, (g + 1) * gw)
        e_g = e_ref[:, gs]

        def expand(v2):
            return (jnp.dot(v2[0], e_g, preferred_element_type=F32) + jnp.dot(v2[1], e_g, preferred_element_type=F32))

        xs = conv(g * gw, gw)
        bg = conv(bc0 + g * SSD_STATE, SSD_STATE).astype(BF16)
        cg = conv(bc0 + SSD_GROUPS * SSD_STATE + g * SSD_STATE, SSD_STATE).astype(BF16)
        xd = xs * expand(dt2)
        xde = (xd * expand(te2)).astype(BF16)
        cbm = _nt_dot(cg, bg)
        st_g = st_ref[0, g * hpg:(g + 1) * hpg].reshape(gw, SSD_STATE)
        y_off = _nt_dot(cg, st_g.astype(BF16)) * expand(ea2)
        pieces = []
        for q in range(hpg // 2):
            xd_pair = xd[:, q * LANES:(q + 1) * LANES].astype(BF16)
            ys = []
            for h in (g * hpg + 2 * q, g * hpg + 2 * q + 1):
                seg = a_cs[:, dt_lo + h:dt_lo + h + 1] - a_cs_t[dt_lo + h:dt_lo + h + 1, :]
                decay = jnp.exp(jnp.where(causal, seg, -jnp.inf))
                ys.append(jnp.dot((cbm * decay).astype(BF16), xd_pair, preferred_element_type=F32))
            pieces.append(jnp.where(lower_half, ys[0], ys[1]))
        y_g = jnp.concatenate(pieces, axis=1) + y_off
        upd = lax.dot_general(xde, bg, (((0,), (0,)), ((), ())), preferred_element_type=F32)
        et_g = et_ref[gs, :]
        cd_g = (jnp.dot(et_g, cd2[0], preferred_element_type=F32)
                + jnp.dot(et_g, cd2[1], preferred_element_type=F32))
        st_ref[0, g * hpg:(g + 1) * hpg] = (st_g * cd_g + upd).reshape(hpg, SSD_HEAD_DIM, SSD_STATE)
        y_g = y_g + dskip_ref[:, gs] * xs
        yg = y_g * _silu(z_ref[:, gs])
        ms = jnp.mean(yg * yg, axis=-1, keepdims=True)
        y_ref[:, gs] = (yg * lax.rsqrt(ms + 1e-6) * nw_ref[:, gs]).astype(BF16)

    ext_ref[0:8, :] = ext_ref[T:T + 8, :]

    @pl.when(c == nchunks - 1)
    def _():
        convo_ref[0] = ext_ref[8 - (SSD_CONV - 1):8, :]


def _ssd_call(h2d, conv0, st0, conv_w, conv_b, dtbias_p, alog_p, dskip_x, norm_w, e_mat, et_mat, cols, nb, L, T):
    nchunks = L // T
    heads = cols.heads
    ssd_w = cols.ssd_w
    cd = cols.conv_dim
    kern = functools.partial(_ssd_kernel, T=T, nchunks=nchunks, heads=heads)
    row = lambda b, c: b * nchunks + c
    const2 = lambda b, c: (0, 0)
    return pl.pallas_call(
        kern,
        grid=(nb, nchunks),
        in_specs=[pl.BlockSpec((T, cd), lambda b, c: (row(b, c), cols.xbc // cd)),
                  pl.BlockSpec((T, ssd_w), lambda b, c: (row(b, c), cols.z // ssd_w)),
                  pl.BlockSpec((T, LANES), lambda b, c: (row(b, c), cols.krdt // LANES)),
                  pl.BlockSpec((1, SSD_CONV - 1, cd), lambda b, c: (b, 0, 0)),
                  pl.BlockSpec((1, heads, SSD_HEAD_DIM, SSD_STATE), lambda b, c: (b, 0, 0, 0)),
                  pl.BlockSpec((SSD_CONV, cd), const2),
                  pl.BlockSpec((1, cd), const2),
                  pl.BlockSpec((1, LANES), const2),
                  pl.BlockSpec((1, LANES), const2),
                  pl.BlockSpec((1, ssd_w), const2),
                  pl.BlockSpec((1, ssd_w), const2),
                  pl.BlockSpec((LANES, ssd_w), const2),
                  pl.BlockSpec((ssd_w, LANES), const2)],
        out_specs=[pl.BlockSpec((T, ssd_w), lambda b, c: (row(b, c), 0)),
                   pl.BlockSpec((1, heads, SSD_HEAD_DIM, SSD_STATE), lambda b, c: (b, 0, 0, 0)),
                   pl.BlockSpec((1, SSD_CONV - 1, cd), lambda b, c: (b, 0, 0))],
        out_shape=[jax.ShapeDtypeStruct((nb * L, ssd_w), BF16),
                   jax.ShapeDtypeStruct((nb, heads, SSD_HEAD_DIM, SSD_STATE), F32),
                   jax.ShapeDtypeStruct((nb, SSD_CONV - 1, cd), F32)],
        scratch_shapes=[pltpu.VMEM((T + 8, cd), F32)],
        compiler_params=_cparams(("arbitrary", "arbitrary")),
        name="ssd_scan",
    )(h2d, h2d, h2d, conv0, st0, conv_w, conv_b, dtbias_p, alog_p, dskip_x, norm_w, e_mat, et_mat)


def _rope_tables(pos, rot_dim, period, width):
    half = rot_dim // 2
    inv = ROPE_THETA ** (-jnp.arange(half, dtype=F32) * (2.0 / rot_dim))
    ang = pos.astype(F32)[:, None] * inv[None, :]
    cos, sin = jnp.cos(ang), jnp.sin(ang)
    lane = np.arange(LANES)
    inner = lane % period
    idx = jnp.asarray(inner % half)
    first = jnp.asarray((inner < half) & (lane < width))
    second = jnp.asarray((inner >= half) & (inner < rot_dim) & (lane < width))
    keep = jnp.asarray((inner >= rot_dim) & (lane < width))
    cos_l, sin_l = cos[:, idx], sin[:, idx]
    cos_t = jnp.where(first | second, cos_l, jnp.where(keep, 1.0, 0.0))
    sin_a = jnp.where(first, -sin_l, 0.0)
    sin_b = jnp.where(second, sin_l, 0.0)
    return cos_t.astype(F32), sin_a.astype(F32), sin_b.astype(F32)


def _rope_tile(x, cos_t, sin_a, sin_b, half):
    return (x * cos_t + pltpu.roll(x, LANES - half, 1) * sin_a + pltpu.roll(x, half, 1) * sin_b)


def _rmsnorm(x, w):
    return x * lax.rsqrt(jnp.mean(x * x, axis=-1, keepdims=True) + 1e-6) * w


def _prep_kernel(dq_ref, dk_ref, dv_ref, cq_ref, kr_ref, cd_ref, sad_ref, sbd_ref, cm_ref, sam_ref, sbm_ref,
                 kvw_ref, *rest, per_head, n_alias):
    if per_head:
        wuk_ref, rest = rest[0], rest[1:]
    rest = rest[n_alias:]
    if per_head:
        k32_ref, v32_ref, lat_ref, kro_ref, qd_ref, kb_ref, vb_ref, latb_ref, kt_ref = rest
    else:
        k32_ref, v32_ref, lat_ref, kro_ref, qd_ref, kb_ref, vb_ref, kcat_ref = rest
    cos_d, sa_d, sb_d = cd_ref[...], sad_ref[...], sbd_ref[...]
    width = dq_ref.shape[1]
    for c in range(width // LANES):
        sl = slice(c * LANES, (c + 1) * LANES)
        q = _rope_tile(dq_ref[:, sl], cos_d, sa_d, sb_d, DIFF_ROT // 2)
        qd_ref[0, :, sl] = (q * (DIFF_SCALE * LOG2E)).astype(BF16)
        k = _rope_tile(dk_ref[:, sl], cos_d, sa_d, sb_d, DIFF_ROT // 2)
        k32_ref[0, :, sl] = k
        kb_ref[0, :, sl] = k.astype(BF16)
    v = dv_ref[...]
    v32_ref[0] = v
    vb_ref[0] = v.astype(BF16)
    lat = _rmsnorm(cq_ref[:, MLA_Q_RANK:MLA_Q_RANK + MLA_KV_RANK], kvw_ref[...])
    lat_ref[0] = lat
    kr = _rope_tile(kr_ref[...], cm_ref[...], sam_ref[...], sbm_ref[...], MLA_ROPE_DIM // 2)
    kro_ref[0] = kr[:, :MLA_ROPE_DIM]
    lat_b = lat.astype(BF16)
    if per_head:
        latb_ref[0] = lat_b
        k_nope = jnp.dot(lat_b, wuk_ref[...], preferred_element_type=F32)
        for h in range(kt_ref.shape[1]):
            k_h = jnp.concatenate([k_nope[:, h * MLA_NOPE_DIM:(h + 1) * MLA_NOPE_DIM], kr], axis=1)
            kt_ref[0, h, 0] = k_h.T.astype(BF16)
    else:
        kcat_ref[0, :, :MLA_KV_RANK] = lat_b
        kcat_ref[0, :, MLA_KV_RANK:] = kr.astype(BF16)


def _prep_call(h2d, tabs_d, tabs_m, kv_w, cols, nb, L, tm, layer, state_bufs, wuk_flat=None):
    nt = L // tm
    dw = cols.diff_w
    per_head = wuk_flat is not None
    row = lambda b, i: b * nt + i
    hspec = lambda width, off: pl.BlockSpec((tm, width), lambda b, i: (row(b, i), off // width))
    tspec = pl.BlockSpec((tm, LANES), lambda b, i: (i, 0))
    ospec = lambda width: pl.BlockSpec((1, tm, width), lambda b, i: (b, i, 0))
    sspec = lambda width: pl.BlockSpec((None, 1, tm, width), lambda b, i: (layer, b, i, 0))
    in_specs = [hspec(dw, cols.dq), hspec(dw, cols.dk), hspec(dw, cols.dv),
                hspec(MLA_Q_RANK + MLA_KV_RANK, cols.cq), hspec(LANES, cols.krdt),
                tspec, tspec, tspec, tspec, tspec, tspec,
                pl.BlockSpec((1, MLA_KV_RANK), lambda b, i: (0, 0))]
    args = [h2d, h2d, h2d, h2d, h2d, *tabs_d, *tabs_m, kv_w]
    if per_head:
        in_specs.append(pl.BlockSpec(wuk_flat.shape, lambda b, i: (0, 0)))
        args.append(wuk_flat)
    aliases = {len(args) + n: n for n in range(len(state_bufs))}
    in_specs += [pl.BlockSpec(memory_space=pl.ANY)] * len(state_bufs)
    args += list(state_bufs)
    out_specs = [sspec(dw), sspec(dw), sspec(MLA_KV_RANK), sspec(MLA_ROPE_DIM), ospec(dw), ospec(dw), ospec(dw)]
    out_shape = [jax.ShapeDtypeStruct(b.shape, b.dtype) for b in state_bufs]
    out_shape += [jax.ShapeDtypeStruct((nb, L, dw), BF16)] * 3
    if per_head:
        heads = wuk_flat.shape[1] // MLA_NOPE_DIM
        out_specs += [ospec(MLA_KV_RANK),
                      pl.BlockSpec((1, heads, 1, MLA_QH_DIM, tm), lambda b, i: (b, 0, i, 0, 0))]
        out_shape += [jax.ShapeDtypeStruct((nb, L, MLA_KV_RANK), BF16),
                      jax.ShapeDtypeStruct((nb, heads, nt, MLA_QH_DIM, tm), BF16)]
    else:
        out_specs.append(ospec(MLA_QK_PAD))
        out_shape.append(jax.ShapeDtypeStruct((nb, L, MLA_QK_PAD), BF16))
    return pl.pallas_call(
        functools.partial(_prep_kernel, per_head=per_head, n_alias=len(state_bufs)),
        grid=(nb, nt),
        in_specs=in_specs, out_specs=out_specs, out_shape=out_shape,
        input_output_aliases=aliases,
        compiler_params=_cparams(("arbitrary", "arbitrary")),
        name="attn_prep",
    )(*args)


def _mlaq_kernel(cq_ref, qw_ref, wuq_ref, wuk_ref, cm_ref, sam_ref, sbm_ref, o_ref, *, heads, absorb):
    cqn = _rmsnorm(cq_ref[:, :MLA_Q_RANK], qw_ref[...]).astype(BF16)
    qm = jnp.dot(cqn, wuq_ref[...], preferred_element_type=F32)
    cos_m, sa_m, sb_m = cm_ref[...], sam_ref[...], sbm_ref[...]
    nope_w = heads * MLA_NOPE_DIM
    scale = MLA_SCALE * LOG2E
    for h in range(heads):
        nope = qm[:, h * MLA_NOPE_DIM:(h + 1) * MLA_NOPE_DIM]
        qr = _rope_tile(qm[:, nope_w + h * LANES:nope_w + (h + 1) * LANES], cos_m, sa_m, sb_m, MLA_ROPE_DIM // 2)
        if absorb:
            ql = jnp.dot(nope.astype(BF16), wuk_ref[h], preferred_element_type=F32)
        else:
            ql = nope
        width = ql.shape[1]
        o_ref[0, h, :, :width] = (ql * scale).astype(BF16)
        o_ref[0, h, :, width:] = (qr * scale).astype(BF16)


def _mlaq_call(h2d, q_w, wuq_p, wuk_t, tabs_m, cols, nb, L, heads, absorb):
    tm = min(L, 512)
    nt = L // tm
    width = MLA_Q_RANK + MLA_KV_RANK
    qdim = MLA_QK_PAD if absorb else MLA_QH_DIM
    tspec = pl.BlockSpec((tm, LANES), lambda b, i: (i, 0))
    return pl.pallas_call(
        functools.partial(_mlaq_kernel, heads=heads, absorb=absorb),
        grid=(nb, nt),
        in_specs=[pl.BlockSpec((tm, width), lambda b, i: (b * nt + i, cols.cq // width)),
                  pl.BlockSpec((1, MLA_Q_RANK), lambda b, i: (0, 0)),
                  pl.BlockSpec(wuq_p.shape, lambda b, i: (0, 0)),
                  pl.BlockSpec(wuk_t.shape, lambda b, i: (0, 0, 0)),
                  tspec, tspec, tspec],
        out_specs=pl.BlockSpec((1, heads, tm, qdim), lambda b, i: (b, 0, i, 0)),
        out_shape=jax.ShapeDtypeStruct((nb, heads, L, qdim), BF16),
        compiler_params=_cparams(("arbitrary", "arbitrary")),
        name="mla_q",
    )(h2d, q_w, wuq_p, wuk_t, *tabs_m)


def _softmax_init(m_ref, l_ref, acc_ref):
    m_ref[...] = jnp.full(m_ref.shape, -jnp.inf, F32)
    l_ref[...] = jnp.zeros(l_ref.shape, F32)
    acc_ref[...] = jnp.zeros(acc_ref.shape, F32)


def _softmax_update(s, v, m_ref, l_ref, acc_ref):
    m_prev = m_ref[...]
    m_new = jnp.maximum(m_prev, jnp.max(s, axis=-1, keepdims=True))
    alpha = jnp.exp2(m_prev - m_new)
    p = jnp.exp2(s - m_new)
    l_ref[...] = alpha * l_ref[...] + jnp.sum(p, axis=-1, keepdims=True)
    acc_ref[...] = alpha * acc_ref[...] + jnp.dot(p.astype(BF16), v, preferred_element_type=F32)
    m_ref[...] = m_new


def _diag_mask(rows, tq, tk, q0):
    q_tok = q0 + (lax.broadcasted_iota(jnp.int32, (rows, tk), 0) & (tq - 1))
    k_tok = lax.broadcasted_iota(jnp.int32, (rows, tk), 1)
    return (k_tok // CHUNK) <= (q_tok // CHUNK)


def _causal_sweep(n_full, qk, upd, finish, sa_ref, sb_ref):
    qk(0, sa_ref)

    def pair(jj, carry):
        j = 2 * jj
        qk(j + 1, sb_ref)
        upd(j, sa_ref)
        qk(j + 2, sa_ref)
        upd(j + 1, sb_ref)
        return carry

    lax.fori_loop(0, n_full // 2, pair, 0)

    @pl.when(n_full % 2 == 1)
    def _():
        qk(n_full, sb_ref)
        upd(n_full - 1, sa_ref)
        finish(sb_ref)

    @pl.when(n_full % 2 == 0)
    def _():
        finish(sa_ref)


def _softmax_init_wide(m_ref, acc_ref):
    m_ref[...] = jnp.full(m_ref.shape, -jnp.inf, F32)
    acc_ref[...] = jnp.zeros(acc_ref.shape, F32)


def _with_ones(v):
    return jnp.concatenate([v, jnp.ones((v.shape[0], LANES), BF16)], axis=1)


def _softmax_update_wide(s, pv_fn, m_ref, acc_ref):
    tk = s.shape[1]
    w = min(tk, LANES)
    m_prev = m_ref[...]
    m_new = jnp.maximum(m_prev, jnp.max(s, axis=-1, keepdims=True))
    alpha = jnp.exp2(m_prev - m_new)
    p = jnp.concatenate([jnp.exp2(s[:, c:c + w] - m_new[:, :w]).astype(BF16) for c in range(0, tk, w)], axis=1)
    acc = acc_ref[...]
    acc_ref[...] = jnp.concatenate([alpha] * (acc.shape[1] // LANES), axis=1) * acc + pv_fn(p)
    m_ref[...] = m_new


def _softmax_update_rep(s, v, m_ref, l_ref, acc_ref):
    tk = s.shape[1]
    m_prev = m_ref[...]
    m_new = jnp.maximum(m_prev, jnp.max(s, axis=-1, keepdims=True))
    alpha = jnp.exp2(m_prev - m_new)
    ps = [jnp.exp2(s[:, c:c + LANES] - m_new) for c in range(0, tk, LANES)]
    l_ref[...] = alpha * l_ref[...] + sum(ps[1:], ps[0])
    p = jnp.concatenate([x.astype(BF16) for x in ps], axis=1)
    acc = acc_ref[...]
    acc_ref[...] = (jnp.concatenate([alpha] * (acc.shape[1] // LANES), axis=1) * acc
                    + jnp.dot(p, v, preferred_element_type=F32))
    m_ref[...] = m_new


def _diff_lambda(lam_ref, lam_init):
    s1 = jnp.sum(lam_ref[0:1, :] * lam_ref[1:2, :], axis=-1, keepdims=True)
    s2 = jnp.sum(lam_ref[2:3, :] * lam_ref[3:4, :], axis=-1, keepdims=True)
    return jnp.exp(s1) - jnp.exp(s2) + lam_init


def _diff_finish(o1, o2, lam, nw, gate, lam_init):
    o = o1 - lam * o2
    o = _rmsnorm(o, nw) * (1.0 - lam_init)
    return (o * _silu(gate)).astype(BF16)


def _stack_streams(q):
    lane = lax.broadcasted_iota(jnp.int32, q.shape, 1)
    zero = jnp.zeros_like(q)
    return jnp.concatenate([jnp.where(lane < DIFF_QK_DIM, q, zero), jnp.where(lane >= DIFF_QK_DIM, q, zero)], axis=0)


def _diff_prompt_kernel(q_ref, k_ref, v_ref, lam_ref, nw_ref, g_ref, o_ref,
                        qz_ref, sa_ref, sb_ref, m_ref, acc_ref, *, t, lam_init):
    qi = pl.program_id(2)
    qz_ref[...] = _stack_streams(q_ref[0])
    _softmax_init_wide(m_ref, acc_ref)

    def blk(ref, j):
        return ref[0, pl.ds(pl.multiple_of(j * t, t), t), :]

    def qk(j, s_ref):
        s_ref[...] = _nt_dot(qz_ref[...], blk(k_ref, j))

    def pv_fn(j):
        return lambda p: jnp.dot(p, _with_ones(blk(v_ref, j)), preferred_element_type=F32)

    def upd(j, s_ref):
        _softmax_update_wide(s_ref[...], pv_fn(j), m_ref, acc_ref)

    def finish(s_ref):
        s = jnp.where(_diag_mask(2 * t, t, t, 0), s_ref[...], NEG_INF)
        _softmax_update_wide(s, pv_fn(qi), m_ref, acc_ref)
        o = acc_ref[:, :DIFF_V_DIM] / acc_ref[:, DIFF_V_DIM:]
        lam = _diff_lambda(lam_ref, lam_init)
        o_ref[0] = _diff_finish(o[:t], o[t:], lam, nw_ref[...], g_ref[...], lam_init)

    _causal_sweep(qi, qk, upd, finish, sa_ref, sb_ref)


def _diff_prompt_call(qd, kb, vb, lam_p, norm_w, h2d, cols, lam_init, t):
    nb, L, dw = qd.shape
    heads = dw // DIFF_V_DIM
    nq = L // t
    kern = functools.partial(_diff_prompt_kernel, t=t, lam_init=lam_init)
    seq = pl.BlockSpec((1, L, LANES), lambda b, h, i: (b, 0, h))
    tile = pl.BlockSpec((1, t, LANES), lambda b, h, i: (b, i, h))
    return pl.pallas_call(
        kern,
        grid=(nb, heads, nq),
        in_specs=[tile, seq, seq,
                  pl.BlockSpec((4, DIFF_QK_DIM), lambda b, h, i: (0, 0)),
                  pl.BlockSpec((1, DIFF_V_DIM), lambda b, h, i: (0, 0)),
                  pl.BlockSpec((t, LANES), lambda b, h, i: (b * nq + i, cols.dgate // LANES + h))],
        out_specs=tile,
        out_shape=jax.ShapeDtypeStruct((nb, L, dw), BF16),
        scratch_shapes=[pltpu.VMEM((2 * t, LANES), BF16),
                        pltpu.VMEM((2 * t, t), F32),
                        pltpu.VMEM((2 * t, t), F32),
                        pltpu.VMEM((2 * t, LANES), F32),
                        pltpu.VMEM((2 * t, DIFF_V_DIM + LANES), F32)],
        compiler_params=_cparams(("arbitrary", "arbitrary", "arbitrary")),
        name="diff_attn_prompt",
    )(qd, kb, vb, lam_p, norm_w, h2d)


def _diff_sample_kernel(q_ref, kc_ref, vc_ref, kn_ref, vn_ref, lam_ref, nw_ref, g_ref, o_ref,
                        qz_ref, m_ref, acc_ref, *, nkc, heads, lam_init):
    j = pl.program_id(1)
    L = q_ref.shape[1]
    rows = 2 * L

    @pl.when(j == 0)
    def _():
        for h in range(heads):
            qz_ref[h] = _stack_streams(q_ref[0, :, h * LANES:(h + 1) * LANES])
        _softmax_init_wide(m_ref, acc_ref)

    def step(scores, values):
        def pv_fn(p):
            return jnp.concatenate([jnp.dot(p[h * rows:(h + 1) * rows], _with_ones(values(h)),
                                            preferred_element_type=F32) for h in range(heads)], axis=0)
        s = jnp.concatenate([scores(h) for h in range(heads)], axis=0)
        _softmax_update_wide(s, pv_fn, m_ref, acc_ref)

    @pl.when(j < nkc)
    def _():
        tk = kc_ref.shape[5]

        def scores(h):
            return jnp.dot(qz_ref[h], kc_ref[0, 0, h].reshape(2 * DIFF_QK_DIM, tk).astype(BF16),
                           preferred_element_type=F32)

        def values(h):
            return vc_ref[0, 0, pl.ds(h, tk, stride=heads), :].astype(BF16)

        step(scores, values)

    @pl.when(j == nkc)
    def _():
        step(lambda h: _nt_dot(qz_ref[h], kn_ref[0, :, h * LANES:(h + 1) * LANES]),
             lambda h: vn_ref[0, :, h * LANES:(h + 1) * LANES])
        lam = _diff_lambda(lam_ref, lam_init)
        o = acc_ref[:, :DIFF_V_DIM] / acc_ref[:, DIFF_V_DIM:]
        for h in range(heads):
            sl = slice(h * LANES, (h + 1) * LANES)
            oh = o[h * rows:(h + 1) * rows]
            o_ref[0, :, sl] = _diff_finish(oh[:L], oh[L:], lam, nw_ref[...], g_ref[:, sl], lam_init)


def _diff_sample_call(qd, kb, vb, k_cache, v_cache, layer, lam_p, norm_w, h2d, cols, lam_init, tk):
    nb, L, dw = qd.shape
    heads = dw // DIFF_V_DIM
    P = k_cache.shape[5]
    nkc = P // tk
    last = nkc - 1
    kern = functools.partial(_diff_sample_kernel, nkc=nkc, heads=heads, lam_init=lam_init)
    kspec = pl.BlockSpec((1, 1, heads, 2, DIFF_QK_DIM, tk),
                         lambda b, j: (layer, b, 0, 0, 0, jnp.minimum(j, last)))
    vspec = pl.BlockSpec((1, 1, tk * heads, DIFF_V_DIM), lambda b, j: (layer, b, jnp.minimum(j, last), 0))
    nspec = pl.BlockSpec((1, L, dw), lambda b, j: (b, 0, 0))
    return pl.pallas_call(
        kern,
        grid=(nb, nkc + 1),
        in_specs=[nspec, kspec, vspec, nspec, nspec,
                  pl.BlockSpec((4, DIFF_QK_DIM), lambda b, j: (0, 0)),
                  pl.BlockSpec((1, DIFF_V_DIM), lambda b, j: (0, 0)),
                  pl.BlockSpec((L, dw), lambda b, j: (b, cols.dgate // dw))],
        out_specs=nspec,
        out_shape=jax.ShapeDtypeStruct((nb, L, dw), BF16),
        scratch_shapes=[pltpu.VMEM((heads, 2 * L, LANES), BF16),
                        pltpu.VMEM((heads * 2 * L, LANES), F32),
                        pltpu.VMEM((heads * 2 * L, DIFF_V_DIM + LANES), F32)],
        compiler_params=_cparams(("arbitrary", "arbitrary")),
        name="diff_attn_sample",
    )(qd, k_cache, v_cache, kb, vb, lam_p, norm_w, h2d)


def _mla_finish(acc_ref, l, wuv_ref, g_ref, o_ref, heads, t):
    o = (acc_ref[...] / l).astype(BF16)
    for h in range(heads):
        sl = slice(h * MLA_V_DIM, (h + 1) * MLA_V_DIM)
        om = jnp.dot(o[h * t:(h + 1) * t], wuv_ref[:, sl], preferred_element_type=F32)
        o_ref[0, :, sl] = (om * _silu(g_ref[:, sl])).astype(BF16)


def _mla_prompt_kernel(q_ref, kt_ref, v_ref, wuv_ref, g_ref, o_ref, sa_ref, sb_ref, m_ref, l_ref, acc_ref, *, t, hg):
    qi = pl.program_id(2)
    _softmax_init(m_ref, l_ref, acc_ref)

    def vblk(j):
        return v_ref[0, pl.ds(pl.multiple_of(j * t, t), t), :]

    def qk(j, s_ref):
        for g in range(hg):
            s_ref[g * t:(g + 1) * t, :] = jnp.dot(q_ref[0, g], kt_ref[0, g, j], preferred_element_type=F32)

    def upd(j, s_ref):
        _softmax_update_rep(s_ref[...], vblk(j), m_ref, l_ref, acc_ref)

    def finish(s_ref):
        s = jnp.where(_diag_mask(hg * t, t, t, 0), s_ref[...], NEG_INF)
        _softmax_update_rep(s, vblk(qi), m_ref, l_ref, acc_ref)
        _mla_finish(acc_ref, jnp.sum(l_ref[...], axis=-1, keepdims=True), wuv_ref, g_ref, o_ref, hg, t)

    _causal_sweep(qi, qk, upd, finish, sa_ref, sb_ref)


def _mla_prompt_call(qh, kt, latb, wuv, h2d, cols, t, hg):
    nb, heads, L, qdim = qh.shape
    mw = heads * MLA_V_DIM
    gw = hg * MLA_V_DIM
    nq = L // t
    return pl.pallas_call(
        functools.partial(_mla_prompt_kernel, t=t, hg=hg),
        grid=(nb, heads // hg, nq),
        in_specs=[pl.BlockSpec((1, hg, t, qdim), lambda b, h, i: (b, h, i, 0)),
                  pl.BlockSpec((1, hg, nq, qdim, t), lambda b, h, i: (b, h, 0, 0, 0)),
                  pl.BlockSpec((1, L, MLA_KV_RANK), lambda b, h, i: (b, 0, 0)),
                  pl.BlockSpec((MLA_KV_RANK, gw), lambda b, h, i: (0, h)),
                  pl.BlockSpec((t, gw), lambda b, h, i: (b * nq + i, cols.mgate // gw + h))],
        out_specs=pl.BlockSpec((1, t, gw), lambda b, h, i: (b, i, h)),
        out_shape=jax.ShapeDtypeStruct((nb, L, mw), BF16),
        scratch_shapes=[pltpu.VMEM((hg * t, t), F32),
                        pltpu.VMEM((hg * t, t), F32),
                        pltpu.VMEM((hg * t, LANES), F32),
                        pltpu.VMEM((hg * t, LANES), F32),
                        pltpu.VMEM((hg * t, MLA_KV_RANK), F32)],
        compiler_params=_cparams(("arbitrary", "arbitrary", "arbitrary")),
        name="mla_attn_prompt",
    )(qh, kt, latb, wuv, h2d)


def _mla_sample_kernel(q_ref, lc_ref, rc_ref, kn_ref, wuv_ref, g_ref, o_ref, m_ref, l_ref, acc_ref,
                       *, nkc, heads):
    j = pl.program_id(1)
    L = q_ref.shape[2]

    @pl.when(j == 0)
    def _():
        _softmax_init(m_ref, l_ref, acc_ref)

    q = q_ref[0].reshape(heads * L, MLA_QK_PAD)

    @pl.when(j < nkc)
    def _():
        lat = lc_ref[0, 0].astype(BF16)
        kr_t = rc_ref[0, 0].astype(BF16)
        s = _nt_dot(q[:, :MLA_KV_RANK], lat) + jnp.dot(q[:, MLA_KV_RANK:MLA_KV_RANK + MLA_ROPE_DIM], kr_t,
                                                       preferred_element_type=F32)
        _softmax_update(s, lat, m_ref, l_ref, acc_ref)

    @pl.when(j == nkc)
    def _():
        kn = kn_ref[0]
        _softmax_update(_nt_dot(q, kn), kn[:, :MLA_KV_RANK], m_ref, l_ref, acc_ref)
        _mla_finish(acc_ref, l_ref[...], wuv_ref, g_ref, o_ref, heads, L)


def _mla_sample_call(qcat, kcat, lat_cache, kr_cache, layer, wuv, h2d, cols, tk):
    nb, heads, L, _ = qcat.shape
    mw = heads * MLA_V_DIM
    P = lat_cache.shape[2]
    nkc = P // tk
    last = nkc - 1
    kern = functools.partial(_mla_sample_kernel, nkc=nkc, heads=heads)
    return pl.pallas_call(
        kern,
        grid=(nb, nkc + 1),
        in_specs=[pl.BlockSpec((1, heads, L, MLA_QK_PAD), lambda b, j: (b, 0, 0, 0)),
                  pl.BlockSpec((1, 1, tk, MLA_KV_RANK), lambda b, j: (layer, b, jnp.minimum(j, last), 0)),
                  pl.BlockSpec((1, 1, MLA_ROPE_DIM, tk), lambda b, j: (layer, b, 0, jnp.minimum(j, last))),
                  pl.BlockSpec((1, L, MLA_QK_PAD), lambda b, j: (b, 0, 0)),
                  pl.BlockSpec(wuv.shape, lambda b, j: (0, 0)),
                  pl.BlockSpec((L, mw), lambda b, j: (b, cols.mgate // mw))],
        out_specs=pl.BlockSpec((1, L, mw), lambda b, j: (b, 0, 0)),
        out_shape=jax.ShapeDtypeStruct((nb, L, mw), BF16),
        scratch_shapes=[pltpu.VMEM((heads * L, 1), F32),
                        pltpu.VMEM((heads * L, 1), F32),
                        pltpu.VMEM((heads * L, MLA_KV_RANK), F32)],
        compiler_params=_cparams(("arbitrary", "arbitrary")),
        name="mla_attn_sample",
    )(qcat, lat_cache, kr_cache, kcat, wuv, h2d)


def _outproj_kernel(ys_ref, yd_ref, ym_ref, w_ref, x_ref, g_ref, r_ref, *, alpha):
    bt, lt, _ = ys_ref.shape
    flat = lambda ref: ref[...].reshape(bt * lt, ref.shape[2])
    mix = jnp.concatenate([flat(ys_ref), flat(yd_ref), flat(ym_ref)], axis=1)
    acc = jnp.dot(mix, w_ref[...], preferred_element_type=F32)
    r_ref[...] = alpha * x_ref[...] + g_ref[...] * acc.reshape(bt, lt, acc.shape[1])


def _outproj_call(y_ssd, y_diff, y_mla, w_out, x, gate, alpha):
    nb, L, d = x.shape
    bt, lt = _seq_tiles(nb, L, 512)
    tn = 1024
    yspec = lambda width: pl.BlockSpec((bt, lt, width), lambda j, b, i: (b, i, 0))
    return pl.pallas_call(
        functools.partial(_outproj_kernel, alpha=alpha),
        grid=(d // tn, nb // bt, L // lt),
        in_specs=[yspec(y_ssd.shape[2]), yspec(y_diff.shape[2]), yspec(y_mla.shape[2]),
                  pl.BlockSpec((w_out.shape[0], tn), lambda j, b, i: (0, j)),
                  pl.BlockSpec((bt, lt, tn), lambda j, b, i: (b, i, j)),
                  pl.BlockSpec((bt, 1, tn), lambda j, b, i: (b, 0, j))],
        out_specs=pl.BlockSpec((bt, lt, tn), lambda j, b, i: (b, i, j)),
        out_shape=jax.ShapeDtypeStruct((nb, L, d), F32),
        compiler_params=_cparams(("arbitrary", "arbitrary", "arbitrary")),
        name="out_proj",
    )(y_ssd, y_diff, y_mla, w_out, x, gate)


def _layernorm(r, g, b):
    mu = jnp.mean(r, axis=-1, keepdims=True)
    var = jnp.mean(jnp.square(r - mu), axis=-1, keepdims=True)
    return (r - mu) * lax.rsqrt(var + 1e-5) * g + b


def _ln_kernel(r_ref, g_ref, b_ref, x_ref):
    x_ref[...] = _layernorm(r_ref[...], g_ref[...], b_ref[...])


def _ln_mod_kernel(r_ref, g_ref, b_ref, sc_ref, sh_ref, x_ref, u_ref):
    x = _layernorm(r_ref[...], g_ref[...], b_ref[...])
    x_ref[...] = x
    u_ref[...] = (x * (1.0 + sc_ref[...]) + sh_ref[...]).astype(BF16)


def _ln_call(r, g, b, scale=None, shift=None):
    nb, L, d = r.shape
    bt, lt = _seq_tiles(nb, L, 256)
    xspec = pl.BlockSpec((bt, lt, d), lambda i, j: (i, j, 0))
    wspec = pl.BlockSpec((1, 1, d), lambda i, j: (0, 0, 0))
    sspec = pl.BlockSpec((bt, 1, d), lambda i, j: (i, 0, 0))
    g3, b3 = g.reshape(1, 1, d), b.reshape(1, 1, d)
    if scale is None:
        return pl.pallas_call(
            _ln_kernel, grid=(nb // bt, L // lt),
            in_specs=[xspec, wspec, wspec], out_specs=xspec,
            out_shape=jax.ShapeDtypeStruct((nb, L, d), F32),
            compiler_params=_cparams(("arbitrary", "arbitrary")), name="layernorm",
        )(r, g3, b3), None
    return pl.pallas_call(
        _ln_mod_kernel, grid=(nb // bt, L // lt),
        in_specs=[xspec, wspec, wspec, sspec, sspec], out_specs=[xspec, xspec],
        out_shape=[jax.ShapeDtypeStruct((nb, L, d), F32), jax.ShapeDtypeStruct((nb, L, d), BF16)],
        compiler_params=_cparams(("arbitrary", "arbitrary")), name="layernorm_modulate",
    )(r, g3, b3, scale, shift)


def _ssd_chunk(L):
    for t in (128, 64, 32, 16, 8):
        if L % t == 0:
            return t
    raise ValueError(f"sequence length {L} is not a multiple of 8")


def _attn_tile(L, want):
    t = want
    while L % t:
        t //= 2
    return t


def _layer(x, u, mod_l, next_mod, wl, cols, layer_idx, depth, caches, state_bufs):
    nb, L, d = x.shape
    heads = cols.heads
    mla_heads = cols.mla_w // MLA_V_DIM
    _, _, gate = mod_l
    k_cache, v_cache, lat_cache, kr_cache, st0, conv0 = caches
    P = 0 if k_cache is None else lat_cache.shape[2]
    pos = P + jnp.arange(L, dtype=jnp.int32)

    h2d = _inproj_call(u.reshape(nb * L, d), wl["w_in"], layer_idx, cols.tn)

    T = _ssd_chunk(L)
    y_ssd, ssm_new, conv_new = _ssd_call(h2d, conv0, st0, wl["conv_w"], wl["conv_b"], wl["dt_bias"], wl["a_log"],
                                         wl["d_skip"], wl["ssd_norm_w"], wl["e_mat"], wl["et_mat"], cols, nb, L, T)

    tabs_d = _rope_tables(pos, DIFF_ROT, DIFF_QK_DIM, LANES)
    tabs_m = _rope_tables(pos, MLA_ROPE_DIM, MLA_ROPE_DIM, MLA_ROPE_DIM)
    lam_init = 0.8 - 0.6 * math.exp(-0.3 * layer_idx)
    if k_cache is None:
        t = _attn_tile(L, 512)
        k32, v32, lat32, kr32, qd, kb, vb, latb, kt = _prep_call(h2d, tabs_d, tabs_m, wl["kv_norm_w"], cols, nb, L, t,
                                                                 layer_idx, state_bufs, wl["w_uk_flat"])
        qh = _mlaq_call(h2d, wl["q_norm_w"], wl["w_uq"], wl["w_uk"], tabs_m, cols, nb, L, mla_heads, False)
        y_diff = _diff_prompt_call(qd, kb, vb, wl["lam"], wl["diff_norm_w"], h2d, cols, lam_init, t)
        y_mla = _mla_prompt_call(qh, kt, latb, wl["w_uv"], h2d, cols, t, MLA_HEAD_GROUP)
    else:
        k32, v32, lat32, kr32, qd, kb, vb, kcat = _prep_call(h2d, tabs_d, tabs_m, wl["kv_norm_w"], cols, nb, L,
                                                             min(L, 256), layer_idx, state_bufs)
        qcat = _mlaq_call(h2d, wl["q_norm_w"], wl["w_uq"], wl["w_uk"], tabs_m, cols, nb, L, mla_heads, True)
        y_diff = _diff_sample_call(qd, kb, vb, k_cache, v_cache, layer_idx, wl["lam"], wl["diff_norm_w"], h2d,
                                   cols, lam_init, _attn_tile(P, 1024))
        y_mla = _mla_sample_call(qcat, kcat, lat_cache, kr_cache, layer_idx, wl["w_uv"], h2d, cols,
                                 _attn_tile(P, 2048))

    alpha = (2 * depth) ** 0.25
    r = _outproj_call(y_ssd.reshape(nb, L, cols.ssd_w), y_diff, y_mla, wl["w_out"], x, gate, alpha)
    if next_mod is None:
        x_new, u_new = _ln_call(r, wl["ln_g"], wl["ln_b"])
    else:
        x_new, u_new = _ln_call(r, wl["ln_g"], wl["ln_b"], next_mod[1], next_mod[0])
    return x_new, u_new, (k32, v32, lat32, kr32), (ssm_new, conv_new)


def _layer_weights(l, cols, w_in, conv_w, conv_b, dt_bias, a_log, d_skip, ssd_norm_w, lambda_q1, lambda_k1,
                   lambda_q2, lambda_k2, diff_norm_w, mla_q_norm_w, mla_kv_norm_w, w_uq, w_uk, w_uv, w_out,
                   ln_g, ln_b):
    heads = cols.heads
    mla_heads = cols.mla_w // MLA_V_DIM
    lane_pad = lambda v: jnp.pad(v, (MLA_ROPE_DIM, LANES - MLA_ROPE_DIM - heads)).reshape(1, LANES)
    qk = MLA_NOPE_DIM + MLA_ROPE_DIM
    wq = w_uq[l].reshape(MLA_Q_RANK, mla_heads, qk)
    wq_nope = wq[:, :, :MLA_NOPE_DIM].reshape(MLA_Q_RANK, mla_heads * MLA_NOPE_DIM)
    wq_rope = jnp.pad(wq[:, :, MLA_NOPE_DIM:], ((0, 0), (0, 0), (0, LANES - MLA_ROPE_DIM)))
    wq_p = jnp.concatenate([wq_nope, wq_rope.reshape(MLA_Q_RANK, mla_heads * LANES)], axis=1).astype(BF16)
    eh = np.zeros((LANES, cols.ssd_w), np.float32)
    for h in range(heads):
        eh[MLA_ROPE_DIM + h, h * SSD_HEAD_DIM:(h + 1) * SSD_HEAD_DIM] = 1.0
    return dict(
        w_in=w_in,
        conv_w=conv_w[l], conv_b=conv_b[l].reshape(1, -1),
        dt_bias=lane_pad(dt_bias[l]), a_log=lane_pad(a_log[l]),
        d_skip=jnp.repeat(d_skip[l], SSD_HEAD_DIM).reshape(1, -1),
        ssd_norm_w=ssd_norm_w[l].reshape(1, -1),
        e_mat=jnp.asarray(eh, BF16), et_mat=jnp.asarray(eh.T, BF16),
        lam=jnp.stack([lambda_q1[l], lambda_k1[l], lambda_q2[l], lambda_k2[l]]),
        diff_norm_w=diff_norm_w[l].reshape(1, -1),
        q_norm_w=mla_q_norm_w[l].reshape(1, -1), kv_norm_w=mla_kv_norm_w[l].reshape(1, -1),
        w_uq=wq_p,
        w_uk=jnp.transpose(w_uk[l], (1, 2, 0)).astype(BF16),
        w_uk_flat=w_uk[l].reshape(MLA_KV_RANK, mla_heads * MLA_NOPE_DIM).astype(BF16),
        w_uv=w_uv[l].reshape(MLA_KV_RANK, mla_heads * MLA_V_DIM).astype(BF16),
        w_out=w_out[l].astype(BF16),
        ln_g=ln_g[l], ln_b=ln_b[l],
    )


def kernel(x_prompt, x_sample, cache_diff_k, cache_diff_v, cache_mla_latent, cache_mla_krope, state_ssm, state_conv,
           c_prompt, c_sample, w_mod, b_mod, w_in, conv_w, conv_b, dt_bias, a_log, d_skip, ssd_norm_w, lambda_q1,
           lambda_k1, lambda_q2, lambda_k2, diff_norm_w, mla_q_norm_w, mla_kv_norm_w, w_uq, w_uk, w_uv, w_out,
           ln_g, ln_b):
    depth = w_in.shape[0]
    bp, _, d = x_prompt.shape
    bs = x_sample.shape[0]
    cols = _Cols(d)
    heads = cols.heads

    rows = -(-(bp + bs) // 8) * 8
    c_all = jnp.concatenate([c_prompt, c_sample, jnp.zeros((rows - bp - bs, d), F32)], axis=0)
    mod = _mod_call(c_all, w_mod, b_mod)

    def mods(l, lo, n):
        part = lambda k: mod[l, lo:lo + n, k * d:(k + 1) * d].reshape(n, 1, d)
        return part(0), part(1), part(2)

    pdiff = cache_diff_k.shape[2]
    kc = jnp.transpose(cache_diff_k, (0, 1, 3, 4, 5, 2))
    vc = cache_diff_v.reshape(depth, bs, pdiff * cache_diff_v.shape[3], DIFF_V_DIM)
    krc = jnp.transpose(cache_mla_krope, (0, 1, 3, 2))
    zero_state = jnp.zeros((bp, heads, SSD_HEAD_DIM, SSD_STATE), F32)
    zero_conv = jnp.zeros((bp, SSD_CONV - 1, cols.conv_dim), F32)

    w_in_p = _permute_w_in(w_in, cols)
    hp, hs = x_prompt, x_sample
    mp, ms = mods(0, 0, bp), mods(0, bp, bs)
    up = _modulate_call(hp, mp[1], mp[0])
    us = _modulate_call(hs, ms[1], ms[0])

    def state_buffers(nb, L):
        return tuple(jnp.zeros((depth, nb, L, w), F32) for w in (cols.diff_w, cols.diff_w, MLA_KV_RANK, MLA_ROPE_DIM))

    bufs_p, bufs_s = state_buffers(bp, hp.shape[1]), state_buffers(bs, hs.shape[1])
    rec_p, rec_s = [], []
    for l in range(depth):
        wl = _layer_weights(l, cols, w_in_p, conv_w, conv_b, dt_bias, a_log, d_skip, ssd_norm_w, lambda_q1, lambda_k1,
                            lambda_q2, lambda_k2, diff_norm_w, mla_q_norm_w, mla_kv_norm_w, w_uq, w_uk, w_uv, w_out,
                            ln_g, ln_b)
        nmp = mods(l + 1, 0, bp) if l + 1 < depth else None
        nms = mods(l + 1, bp, bs) if l + 1 < depth else None
        hp, up, bufs_p, rp = _layer(hp, up, mp, nmp, wl, cols, l, depth,
                                    (None, None, None, None, zero_state, zero_conv), bufs_p)
        hs, us, bufs_s, rs = _layer(hs, us, ms, nms, wl, cols, l, depth,
                                    (kc, vc, cache_mla_latent, krc, state_ssm[l], state_conv[l]), bufs_s)
        rec_p.append(rp)
        rec_s.append(rs)
        mp, ms = nmp, nms

    def states(bufs, rec):
        k32, v32, lat32, kr32 = bufs
        nb, L = k32.shape[1], k32.shape[2]
        return (k32.reshape(depth, nb, L, cols.diff_w // (2 * DIFF_QK_DIM), 2, DIFF_QK_DIM),
                v32.reshape(depth, nb, L, cols.diff_w // DIFF_V_DIM, DIFF_V_DIM), lat32, kr32,
                jnp.stack([r[0] for r in rec]), jnp.stack([r[1] for r in rec]))

    return (hp, hs) + states(bufs_p, rec_p) + states(bufs_s, rec_s)
```

```python
import functools
import math

import numpy as np
import jax
import jax.numpy as jnp
from jax import lax
from jax.experimental import pallas as pl
from jax.experimental.pallas import tpu as pltpu

F32 = jnp.float32
BF16 = jnp.bfloat16

CHUNK = 64
ROPE_THETA = 500000.0
NEG_INF = -1e30
SSD_HEAD_DIM = 64
SSD_GROUPS = 4
SSD_STATE = 128
SSD_CONV = 4
DIFF_QK_DIM = 64
DIFF_V_DIM = 128
DIFF_ROT = DIFF_QK_DIM // 4
DIFF_SCALE = DIFF_QK_DIM ** -0.5
MLA_V_DIM = 128
MLA_NOPE_DIM = 128
MLA_ROPE_DIM = 64
MLA_Q_RANK = 768
MLA_KV_RANK = 256
MLA_SCALE = (MLA_NOPE_DIM + MLA_ROPE_DIM) ** -0.5
MLA_QK_PAD = MLA_KV_RANK + 128
MLA_QH_DIM = MLA_NOPE_DIM + 128
MLA_HEAD_GROUP = 2
LOG2E = math.log2(math.e)

LANES = 128
VMEM_LIMIT = 56 * 1024 * 1024


def _cparams(sem):
    return pltpu.CompilerParams(dimension_semantics=sem, vmem_limit_bytes=VMEM_LIMIT)


def _silu(x):
    hx = 0.5 * x
    return hx + hx * jnp.tanh(hx)


def _nt_dot(a, b):
    return lax.dot_general(a, b, (((1,), (1,)), ((), ())), preferred_element_type=F32)


def _split2(v):
    hi = v.astype(BF16)
    lo = (v - hi.astype(F32)).astype(BF16)
    return hi, lo


def _split3(v):
    hi = v.astype(BF16)
    r = v - hi.astype(F32)
    mid = r.astype(BF16)
    lo = (r - mid.astype(F32)).astype(BF16)
    return hi, mid, lo


def _mod_kernel(c_ref, w_ref, b_ref, o_ref):
    a = _silu(c_ref[...]).astype(BF16)
    o_ref[0] = jnp.dot(a, w_ref[0].astype(BF16), preferred_element_type=F32) + b_ref[0]


def _mod_call(c_all, w_mod, b_mod):
    depth, d, n3 = w_mod.shape
    rows = c_all.shape[0]
    tn = 512
    return pl.pallas_call(
        _mod_kernel,
        grid=(depth, n3 // tn),
        in_specs=[pl.BlockSpec((rows, d), lambda l, j: (0, 0)),
                  pl.BlockSpec((1, d, tn), lambda l, j: (l, 0, j)),
                  pl.BlockSpec((1, 1, tn), lambda l, j: (l, 0, j))],
        out_specs=pl.BlockSpec((1, rows, tn), lambda l, j: (l, 0, j)),
        out_shape=jax.ShapeDtypeStruct((depth, rows, n3), F32),
        compiler_params=_cparams(("arbitrary", "arbitrary")),
        name="adaln_mod",
    )(c_all, w_mod, b_mod.reshape(depth, 1, n3))


def _modulate_kernel(x_ref, sc_ref, sh_ref, u_ref):
    u_ref[...] = (x_ref[...] * (1.0 + sc_ref[...]) + sh_ref[...]).astype(BF16)


def _seq_tiles(nb, L, rows):
    if L >= rows:
        return 1, rows
    return min(nb, rows // L), L


def _modulate_call(x, scale, shift):
    nb, L, d = x.shape
    bt, lt = _seq_tiles(nb, L, 256)
    return pl.pallas_call(
        _modulate_kernel,
        grid=(nb // bt, L // lt),
        in_specs=[pl.BlockSpec((bt, lt, d), lambda i, j: (i, j, 0)),
                  pl.BlockSpec((bt, 1, d), lambda i, j: (i, 0, 0)),
                  pl.BlockSpec((bt, 1, d), lambda i, j: (i, 0, 0))],
        out_specs=pl.BlockSpec((bt, lt, d), lambda i, j: (i, j, 0)),
        out_shape=jax.ShapeDtypeStruct((nb, L, d), BF16),
        compiler_params=_cparams(("arbitrary", "arbitrary")),
        name="modulate",
    )(x, scale, shift)


class _Cols:
    def __init__(self, d_model):
        self.ssd_w = d_model // 2
        self.diff_w = d_model // 4
        self.mla_w = d_model // 4
        self.heads = self.ssd_w // SSD_HEAD_DIM
        self.conv_dim = self.ssd_w + 2 * SSD_GROUPS * SSD_STATE
        self.in_sizes = (self.ssd_w, self.conv_dim, self.heads, self.diff_w, self.diff_w, self.diff_w,
                         self.diff_w, MLA_Q_RANK, MLA_KV_RANK, MLA_ROPE_DIM, self.mla_w)
        o = 0
        self.xbc = o; o += self.conv_dim
        self.dq = o; o += self.diff_w
        self.z = o; o += self.ssd_w
        self.dk = o; o += self.diff_w
        self.dv = o; o += self.diff_w
        self.dgate = o; o += self.diff_w
        self.mgate = o; o += self.mla_w
        self.cq = o; o += MLA_Q_RANK + MLA_KV_RANK
        self.krdt = o; o += LANES
        self.used = o
        self.tn = 1280
        self.total = -(-o // self.tn) * self.tn


def _w_in_blocks(cols):
    offs = np.concatenate([[0], np.cumsum(cols.in_sizes)])
    order = (1, 3, 0, 4, 5, 6, 10)
    rows = []
    for i in order:
        rows += list(range(int(offs[i]), int(offs[i + 1]), LANES))
    rows += list(range(int(offs[7]), int(offs[9]), LANES))
    rows.append(-1)
    rows += [-2] * ((cols.total - cols.used) // LANES)
    return np.asarray(rows, np.int32)


def _wprep_kernel(tab_ref, w_ref, sp_ref, o_ref):
    src = tab_ref[pl.program_id(1)]

    @pl.when(src >= 0)
    def _():
        o_ref[0] = w_ref[0].astype(BF16)

    @pl.when(src == -1)
    def _():
        o_ref[0] = sp_ref[0].astype(BF16)

    @pl.when(src == -2)
    def _():
        o_ref[0] = jnp.zeros(o_ref.shape[1:], BF16)


def _permute_w_in(w_in, cols):
    depth, d, _ = w_in.shape
    wt = jnp.transpose(w_in, (0, 2, 1))
    offs = np.concatenate([[0], np.cumsum(cols.in_sizes)])
    pad = LANES - MLA_ROPE_DIM - cols.heads
    special = jnp.concatenate([wt[:, offs[9]:offs[10]], wt[:, offs[2]:offs[3]], jnp.zeros((depth, pad, d), F32)], axis=1)
    tab = jnp.asarray(_w_in_blocks(cols))
    grid_spec = pltpu.PrefetchScalarGridSpec(
        num_scalar_prefetch=1,
        grid=(depth, int(tab.shape[0])),
        in_specs=[pl.BlockSpec((pl.Element(1), pl.Element(LANES), pl.Element(d)),
                               lambda l, i, tab: (l, pl.multiple_of(jnp.maximum(tab[i], 0), 8), 0)),
                  pl.BlockSpec((1, LANES, d), lambda l, i, tab: (l, 0, 0))],
        out_specs=pl.BlockSpec((1, LANES, d), lambda l, i, tab: (l, i, 0)))
    return pl.pallas_call(
        _wprep_kernel, grid_spec=grid_spec,
        out_shape=jax.ShapeDtypeStruct((depth, cols.total, d), BF16),
        compiler_params=_cparams(("arbitrary", "arbitrary")),
        name="w_in_prep",
    )(tab, wt, special)


def _matmul_nt_kernel(x_ref, w_ref, o_ref):
    o_ref[...] = _nt_dot(x_ref[...], w_ref[...])


def _inproj_call(u2d, wp, layer, tn):
    m, k = u2d.shape
    n = wp.shape[1]
    tm = min(m, 1024)
    return pl.pallas_call(
        _matmul_nt_kernel,
        grid=(n // tn, m // tm),
        in_specs=[pl.BlockSpec((tm, k), lambda j, i: (i, 0)),
                  pl.BlockSpec((None, tn, k), lambda j, i: (layer, j, 0))],
        out_specs=pl.BlockSpec((tm, tn), lambda j, i: (i, j)),
        out_shape=jax.ShapeDtypeStruct((m, n), F32),
        compiler_params=_cparams(("arbitrary", "arbitrary")),
        name="in_proj",
    )(u2d, wp)


def _ssd_kernel(xbc_ref, z_ref, dtb_ref, conv0_ref, st0_ref, cw_ref, cb_ref, dtbias_ref, alog_ref,
                dskip_ref, nw_ref, e_ref, et_ref, y_ref, st_ref, convo_ref, ext_ref, *, T, nchunks, heads):
    c = pl.program_id(1)
    ssd_w = heads * SSD_HEAD_DIM
    gw = ssd_w // SSD_GROUPS
    hpg = heads // SSD_GROUPS
    conv_dim = ext_ref.shape[1]
    dt_lo = MLA_ROPE_DIM

    @pl.when(c == 0)
    def _():
        ext_ref[0:8, :] = jnp.zeros((8, conv_dim), F32)
        ext_ref[8 - (SSD_CONV - 1):8, :] = conv0_ref[0]
        st_ref[0] = st0_ref[0]

    ext_ref[8:8 + T, :] = xbc_ref[...]

    def conv(c0, width):
        acc = cb_ref[:, c0:c0 + width]
        for j in range(SSD_CONV):
            lo = 8 - (SSD_CONV - 1) + j
            acc = acc + ext_ref[lo:lo + T, c0:c0 + width] * cw_ref[j:j + 1, c0:c0 + width]
        return _silu(acc)

    lane = lax.broadcasted_iota(jnp.int32, (1, LANES), 1)
    is_dt = (lane >= dt_lo) & (lane < dt_lo + heads)
    xdt = dtb_ref[...] + dtbias_ref[...]
    dt = jnp.where(is_dt, jnp.maximum(xdt, 0.0) + jnp.log1p(jnp.exp(-jnp.abs(xdt))), 0.0)
    a_neg = jnp.where(is_dt, -jnp.exp(alog_ref[...]), 0.0)
    da = dt * a_neg

    row = lax.broadcasted_iota(jnp.int32, (T, T), 0)
    col = lax.broadcasted_iota(jnp.int32, (T, T), 1)
    causal = row >= col
    tril = causal.astype(BF16)
    eye = (lax.broadcasted_iota(jnp.int32, (LANES, LANES), 0)
           == lax.broadcasted_iota(jnp.int32, (LANES, LANES), 1)).astype(BF16)

    da3 = _split3(da)
    a_cs = sum(jnp.dot(tril, p, preferred_element_type=F32) for p in da3)
    a3 = _split3(a_cs)
    a_cs_t = sum(_nt_dot(eye, p) for p in a3)
    a_last = a_cs[T - 1:T, :]
    dt2 = _split2(dt)
    ea2 = _split2(jnp.exp(a_cs))
    te2 = _split2(jnp.exp(a_last - a_cs))
    cd_col = jnp.exp(a_cs_t[:, T - 1:T])
    cd2 = _split2(jnp.broadcast_to(cd_col, (LANES, SSD_STATE)))
    lane_p = lax.broadcasted_iota(jnp.int32, (T, LANES), 1)
    lower_half = lane_p < SSD_HEAD_DIM
    bc0 = ssd_w

    for g in range(SSD_GROUPS):
        gs = slice(g * gw, (g + 1) * gw)
        e_g = e_ref[:, gs]

        def expand(v2):
            return (jnp.dot(v2[0], e_g, preferred_element_type=F32) + jnp.dot(v2[1], e_g, preferred_element_type=F32))

        xs = conv(g * gw, gw)
        bg = conv(bc0 + g * SSD_STATE, SSD_STATE).astype(BF16)
        cg = conv(bc0 + SSD_GROUPS * SSD_STATE + g * SSD_STATE, SSD_STATE).astype(BF16)
        xd = xs * expand(dt2)
        xde = (xd * expand(te2)).astype(BF16)
        cbm = _nt_dot(cg, bg)
        st_g = st_ref[0, g * hpg:(g + 1) * hpg].reshape(gw, SSD_STATE)
        y_off = _nt_dot(cg, st_g.astype(BF16)) * expand(ea2)
        pieces = []
        for q in range(hpg // 2):
            xd_pair = xd[:, q * LANES:(q + 1) * LANES].astype(BF16)
            ys = []
            for h in (g * hpg + 2 * q, g * hpg + 2 * q + 1):
                seg = a_cs[:, dt_lo + h:dt_lo + h + 1] - a_cs_t[dt_lo + h:dt_lo + h + 1, :]
                decay = jnp.exp(jnp.where(causal, seg, -jnp.inf))
                ys.append(jnp.dot((cbm * decay).astype(BF16), xd_pair, preferred_element_type=F32))
            pieces.append(jnp.where(lower_half, ys[0], ys[1]))
        y_g = jnp.concatenate(pieces, axis=1) + y_off
        upd = lax.dot_general(xde, bg, (((0,), (0,)), ((), ())), preferred_element_type=F32)
        et_g = et_ref[gs, :]
        cd_g = (jnp.dot(et_g, cd2[0], preferred_element_type=F32)
                + jnp.dot(et_g, cd2[1], preferred_element_type=F32))
        st_ref[0, g * hpg:(g + 1) * hpg] = (st_g * cd_g + upd).reshape(hpg, SSD_HEAD_DIM, SSD_STATE)
        y_g = y_g + dskip_ref[:, gs] * xs
        yg = y_g * _silu(z_ref[:, gs])
        ms = jnp.mean(yg * yg, axis=-1, keepdims=True)
        y_ref[:, gs] = (yg * lax.rsqrt(ms + 1e-6) * nw_ref[:, gs]).astype(BF16)

    ext_ref[0:8, :] = ext_ref[T:T + 8, :]

    @pl.when(c == nchunks - 1)
    def _():
        convo_ref[0] = ext_ref[8 - (SSD_CONV - 1):8, :]


def _ssd_call(h2d, conv0, st0, conv_w, conv_b, dtbias_p, alog_p, dskip_x, norm_w, e_mat, et_mat, cols, nb, L, T):
    nchunks = L // T
    heads = cols.heads
    ssd_w = cols.ssd_w
    cd = cols.conv_dim
    kern = functools.partial(_ssd_kernel, T=T, nchunks=nchunks, heads=heads)
    row = lambda b, c: b * nchunks + c
    const2 = lambda b, c: (0, 0)
    return pl.pallas_call(
        kern,
        grid=(nb, nchunks),
        in_specs=[pl.BlockSpec((T, cd), lambda b, c: (row(b, c), cols.xbc // cd)),
                  pl.BlockSpec((T, ssd_w), lambda b, c: (row(b, c), cols.z // ssd_w)),
                  pl.BlockSpec((T, LANES), lambda b, c: (row(b, c), cols.krdt // LANES)),
                  pl.BlockSpec((1, SSD_CONV - 1, cd), lambda b, c: (b, 0, 0)),
                  pl.BlockSpec((1, heads, SSD_HEAD_DIM, SSD_STATE), lambda b, c: (b, 0, 0, 0)),
                  pl.BlockSpec((SSD_CONV, cd), const2),
                  pl.BlockSpec((1, cd), const2),
                  pl.BlockSpec((1, LANES), const2),
                  pl.BlockSpec((1, LANES), const2),
                  pl.BlockSpec((1, ssd_w), const2),
                  pl.BlockSpec((1, ssd_w), const2),
                  pl.BlockSpec((LANES, ssd_w), const2),
                  pl.BlockSpec((ssd_w, LANES), const2)],
        out_specs=[pl.BlockSpec((T, ssd_w), lambda b, c: (row(b, c), 0)),
                   pl.BlockSpec((1, heads, SSD_HEAD_DIM, SSD_STATE), lambda b, c: (b, 0, 0, 0)),
                   pl.BlockSpec((1, SSD_CONV - 1, cd), lambda b, c: (b, 0, 0))],
        out_shape=[jax.ShapeDtypeStruct((nb * L, ssd_w), BF16),
                   jax.ShapeDtypeStruct((nb, heads, SSD_HEAD_DIM, SSD_STATE), F32),
                   jax.ShapeDtypeStruct((nb, SSD_CONV - 1, cd), F32)],
        scratch_shapes=[pltpu.VMEM((T + 8, cd), F32)],
        compiler_params=_cparams(("arbitrary", "arbitrary")),
        name="ssd_scan",
    )(h2d, h2d, h2d, conv0, st0, conv_w, conv_b, dtbias_p, alog_p, dskip_x, norm_w, e_mat, et_mat)


def _rope_tables(pos, rot_dim, period, width):
    half = rot_dim // 2
    inv = ROPE_THETA ** (-jnp.arange(half, dtype=F32) * (2.0 / rot_dim))
    ang = pos.astype(F32)[:, None] * inv[None, :]
    cos, sin = jnp.cos(ang), jnp.sin(ang)
    lane = np.arange(LANES)
    inner = lane % period
    idx = jnp.asarray(inner % half)
    first = jnp.asarray((inner < half) & (lane < width))
    second = jnp.asarray((inner >= half) & (inner < rot_dim) & (lane < width))
    keep = jnp.asarray((inner >= rot_dim) & (lane < width))
    cos_l, sin_l = cos[:, idx], sin[:, idx]
    cos_t = jnp.where(first | second, cos_l, jnp.where(keep, 1.0, 0.0))
    sin_a = jnp.where(first, -sin_l, 0.0)
    sin_b = jnp.where(second, sin_l, 0.0)
    return cos_t.astype(F32), sin_a.astype(F32), sin_b.astype(F32)


def _rope_tile(x, cos_t, sin_a, sin_b, half):
    return (x * cos_t + pltpu.roll(x, LANES - half, 1) * sin_a + pltpu.roll(x, half, 1) * sin_b)


def _rmsnorm(x, w):
    return x * lax.rsqrt(jnp.mean(x * x, axis=-1, keepdims=True) + 1e-6) * w


def _prep_kernel(dq_ref, dk_ref, dv_ref, cq_ref, kr_ref, cd_ref, sad_ref, sbd_ref, cm_ref, sam_ref, sbm_ref,
                 kvw_ref, *rest, per_head, n_alias):
    if per_head:
        wuk_ref, rest = rest[0], rest[1:]
    rest = rest[n_alias:]
    if per_head:
        k32_ref, v32_ref, lat_ref, kro_ref, qd_ref, kb_ref, vb_ref, latb_ref, kt_ref = rest
    else:
        k32_ref, v32_ref, lat_ref, kro_ref, qd_ref, kb_ref, vb_ref, kcat_ref = rest
    cos_d, sa_d, sb_d = cd_ref[...], sad_ref[...], sbd_ref[...]
    width = dq_ref.shape[1]
    for c in range(width // LANES):
        sl = slice(c * LANES, (c + 1) * LANES)
        q = _rope_tile(dq_ref[:, sl], cos_d, sa_d, sb_d, DIFF_ROT // 2)
        qd_ref[0, :, sl] = (q * (DIFF_SCALE * LOG2E)).astype(BF16)
        k = _rope_tile(dk_ref[:, sl], cos_d, sa_d, sb_d, DIFF_ROT // 2)
        k32_ref[0, :, sl] = k
        kb_ref[0, :, sl] = k.astype(BF16)
    v = dv_ref[...]
    v32_ref[0] = v
    vb_ref[0] = v.astype(BF16)
    lat = _rmsnorm(cq_ref[:, MLA_Q_RANK:MLA_Q_RANK + MLA_KV_RANK], kvw_ref[...])
    lat_ref[0] = lat
    kr = _rope_tile(kr_ref[...], cm_ref[...], sam_ref[...], sbm_ref[...], MLA_ROPE_DIM // 2)
    kro_ref[0] = kr[:, :MLA_ROPE_DIM]
    lat_b = lat.astype(BF16)
    if per_head:
        latb_ref[0] = lat_b
        k_nope = jnp.dot(lat_b, wuk_ref[...], preferred_element_type=F32)
        for h in range(kt_ref.shape[1]):
            k_h = jnp.concatenate([k_nope[:, h * MLA_NOPE_DIM:(h + 1) * MLA_NOPE_DIM], kr], axis=1)
            kt_ref[0, h, 0] = k_h.T.astype(BF16)
    else:
        kcat_ref[0, :, :MLA_KV_RANK] = lat_b
        kcat_ref[0, :, MLA_KV_RANK:] = kr.astype(BF16)


def _prep_call(h2d, tabs_d, tabs_m, kv_w, cols, nb, L, tm, layer, state_bufs, wuk_flat=None):
    nt = L // tm
    dw = cols.diff_w
    per_head = wuk_flat is not None
    row = lambda b, i: b * nt + i
    hspec = lambda width, off: pl.BlockSpec((tm, width), lambda b, i: (row(b, i), off // width))
    tspec = pl.BlockSpec((tm, LANES), lambda b, i: (i, 0))
    ospec = lambda width: pl.BlockSpec((1, tm, width), lambda b, i: (b, i, 0))
    sspec = lambda width: pl.BlockSpec((None, 1, tm, width), lambda b, i: (layer, b, i, 0))
    in_specs = [hspec(dw, cols.dq), hspec(dw, cols.dk), hspec(dw, cols.dv),
                hspec(MLA_Q_RANK + MLA_KV_RANK, cols.cq), hspec(LANES, cols.krdt),
                tspec, tspec, tspec, tspec, tspec, tspec,
                pl.BlockSpec((1, MLA_KV_RANK), lambda b, i: (0, 0))]
    args = [h2d, h2d, h2d, h2d, h2d, *tabs_d, *tabs_m, kv_w]
    if per_head:
        in_specs.append(pl.BlockSpec(wuk_flat.shape, lambda b, i: (0, 0)))
        args.append(wuk_flat)
    aliases = {len(args) + n: n for n in range(len(state_bufs))}
    in_specs += [pl.BlockSpec(memory_space=pl.ANY)] * len(state_bufs)
    args += list(state_bufs)
    out_specs = [sspec(dw), sspec(dw), sspec(MLA_KV_RANK), sspec(MLA_ROPE_DIM), ospec(dw), ospec(dw), ospec(dw)]
    out_shape = [jax.ShapeDtypeStruct(b.shape, b.dtype) for b in state_bufs]
    out_shape += [jax.ShapeDtypeStruct((nb, L, dw), BF16)] * 3
    if per_head:
        heads = wuk_flat.shape[1] // MLA_NOPE_DIM
        out_specs += [ospec(MLA_KV_RANK),
                      pl.BlockSpec((1, heads, 1, MLA_QH_DIM, tm), lambda b, i: (b, 0, i, 0, 0))]
        out_shape += [jax.ShapeDtypeStruct((nb, L, MLA_KV_RANK), BF16),
                      jax.ShapeDtypeStruct((nb, heads, nt, MLA_QH_DIM, tm), BF16)]
    else:
        out_specs.append(ospec(MLA_QK_PAD))
        out_shape.append(jax.ShapeDtypeStruct((nb, L, MLA_QK_PAD), BF16))
    return pl.pallas_call(
        functools.partial(_prep_kernel, per_head=per_head, n_alias=len(state_bufs)),
        grid=(nb, nt),
        in_specs=in_specs, out_specs=out_specs, out_shape=out_shape,
        input_output_aliases=aliases,
        compiler_params=_cparams(("arbitrary", "arbitrary")),
        name="attn_prep",
    )(*args)


def _mlaq_kernel(cq_ref, qw_ref, wuq_ref, wuk_ref, cm_ref, sam_ref, sbm_ref, o_ref, *, heads, absorb):
    cqn = _rmsnorm(cq_ref[:, :MLA_Q_RANK], qw_ref[...]).astype(BF16)
    qm = jnp.dot(cqn, wuq_ref[...], preferred_element_type=F32)
    cos_m, sa_m, sb_m = cm_ref[...], sam_ref[...], sbm_ref[...]
    nope_w = heads * MLA_NOPE_DIM
    scale = MLA_SCALE * LOG2E
    for h in range(heads):
        nope = qm[:, h * MLA_NOPE_DIM:(h + 1) * MLA_NOPE_DIM]
        qr = _rope_tile(qm[:, nope_w + h * LANES:nope_w + (h + 1) * LANES], cos_m, sa_m, sb_m, MLA_ROPE_DIM // 2)
        if absorb:
            ql = jnp.dot(nope.astype(BF16), wuk_ref[h], preferred_element_type=F32)
        else:
            ql = nope
        width = ql.shape[1]
        o_ref[0, h, :, :width] = (ql * scale).astype(BF16)
        o_ref[0, h, :, width:] = (qr * scale).astype(BF16)


def _mlaq_call(h2d, q_w, wuq_p, wuk_t, tabs_m, cols, nb, L, heads, absorb):
    tm = min(L, 512)
    nt = L // tm
    width = MLA_Q_RANK + MLA_KV_RANK
    qdim = MLA_QK_PAD if absorb else MLA_QH_DIM
    tspec = pl.BlockSpec((tm, LANES), lambda b, i: (i, 0))
    return pl.pallas_call(
        functools.partial(_mlaq_kernel, heads=heads, absorb=absorb),
        grid=(nb, nt),
        in_specs=[pl.BlockSpec((tm, width), lambda b, i: (b * nt + i, cols.cq // width)),
                  pl.BlockSpec((1, MLA_Q_RANK), lambda b, i: (0, 0)),
                  pl.BlockSpec(wuq_p.shape, lambda b, i: (0, 0)),
                  pl.BlockSpec(wuk_t.shape, lambda b, i: (0, 0, 0)),
                  tspec, tspec, tspec],
        out_specs=pl.BlockSpec((1, heads, tm, qdim), lambda b, i: (b, 0, i, 0)),
        out_shape=jax.ShapeDtypeStruct((nb, heads, L, qdim), BF16),
        compiler_params=_cparams(("arbitrary", "arbitrary")),
        name="mla_q",
    )(h2d, q_w, wuq_p, wuk_t, *tabs_m)


def _softmax_init(m_ref, l_ref, acc_ref):
    m_ref[...] = jnp.full(m_ref.shape, -jnp.inf, F32)
    l_ref[...] = jnp.zeros(l_ref.shape, F32)
    acc_ref[...] = jnp.zeros(acc_ref.shape, F32)


def _softmax_update(s, v, m_ref, l_ref, acc_ref):
    m_prev = m_ref[...]
    m_new = jnp.maximum(m_prev, jnp.max(s, axis=-1, keepdims=True))
    alpha = jnp.exp2(m_prev - m_new)
    p = jnp.exp2(s - m_new)
    l_ref[...] = alpha * l_ref[...] + jnp.sum(p, axis=-1, keepdims=True)
    acc_ref[...] = alpha * acc_ref[...] + jnp.dot(p.astype(BF16), v, preferred_element_type=F32)
    m_ref[...] = m_new


def _diag_mask(rows, tq, tk, q0):
    q_tok = q0 + (lax.broadcasted_iota(jnp.int32, (rows, tk), 0) & (tq - 1))
    k_tok = lax.broadcasted_iota(jnp.int32, (rows, tk), 1)
    return (k_tok // CHUNK) <= (q_tok // CHUNK)


def _causal_sweep(n_full, qk, upd, finish, sa_ref, sb_ref):
    qk(0, sa_ref)

    def pair(jj, carry):
        j = 2 * jj
        qk(j + 1, sb_ref)
        upd(j, sa_ref)
        qk(j + 2, sa_ref)
        upd(j + 1, sb_ref)
        return carry

    lax.fori_loop(0, n_full // 2, pair, 0)

    @pl.when(n_full % 2 == 1)
    def _():
        qk(n_full, sb_ref)
        upd(n_full - 1, sa_ref)
        finish(sb_ref)

    @pl.when(n_full % 2 == 0)
    def _():
        finish(sa_ref)


def _paired_sweep(i, init, sweep_a, sweep_b, sa_ref, sb_ref):
    qk_a, upd_a, fin_a = sweep_a
    qk_b, upd_b, fin_b = sweep_b

    def pairs(qk, upd, a_ref, b_ref):
        def body(jj, carry):
            j = 2 * jj
            qk(j + 1, b_ref)
            upd(j, a_ref)
            qk(j + 2, a_ref)
            upd(j + 1, b_ref)
            return carry
        lax.fori_loop(0, i, body, 0)

    qk_a(0, sa_ref)
    init()
    pairs(qk_a, upd_a, sa_ref, sb_ref)
    qk_b(0, sb_ref)
    fin_a(sa_ref)
    init()
    pairs(qk_b, upd_b, sb_ref, sa_ref)
    qk_b(2 * i + 1, sa_ref)
    upd_b(2 * i, sb_ref)
    fin_b(sa_ref)


def _softmax_init_wide(m_ref, acc_ref):
    m_ref[...] = jnp.full(m_ref.shape, -jnp.inf, F32)
    acc_ref[...] = jnp.zeros(acc_ref.shape, F32)


def _with_ones(v):
    return jnp.concatenate([v, jnp.ones((v.shape[0], LANES), BF16)], axis=1)


def _softmax_update_wide(s, pv_fn, m_ref, acc_ref):
    tk = s.shape[1]
    w = min(tk, LANES)
    m_prev = m_ref[...]
    m_new = jnp.maximum(m_prev, jnp.max(s, axis=-1, keepdims=True))
    alpha = jnp.exp2(m_prev - m_new)
    p = jnp.concatenate([jnp.exp2(s[:, c:c + w] - m_new[:, :w]).astype(BF16) for c in range(0, tk, w)], axis=1)
    acc = acc_ref[...]
    acc_ref[...] = jnp.concatenate([alpha] * (acc.shape[1] // LANES), axis=1) * acc + pv_fn(p)
    m_ref[...] = m_new


def _softmax_update_rep(s, v, m_ref, l_ref, acc_ref):
    tk = s.shape[1]
    m_prev = m_ref[...]
    m_new = jnp.maximum(m_prev, jnp.max(s, axis=-1, keepdims=True))
    alpha = jnp.exp2(m_prev - m_new)
    ps = [jnp.exp2(s[:, c:c + LANES] - m_new) for c in range(0, tk, LANES)]
    l_ref[...] = alpha * l_ref[...] + sum(ps[1:], ps[0])
    p = jnp.concatenate([x.astype(BF16) for x in ps], axis=1)
    acc = acc_ref[...]
    acc_ref[...] = (jnp.concatenate([alpha] * (acc.shape[1] // LANES), axis=1) * acc
                    + jnp.dot(p, v, preferred_element_type=F32))
    m_ref[...] = m_new


def _diff_lambda(lam_ref, lam_init):
    s1 = jnp.sum(lam_ref[0:1, :] * lam_ref[1:2, :], axis=-1, keepdims=True)
    s2 = jnp.sum(lam_ref[2:3, :] * lam_ref[3:4, :], axis=-1, keepdims=True)
    return jnp.exp(s1) - jnp.exp(s2) + lam_init


def _diff_finish(o1, o2, lam, nw, gate, lam_init):
    o = o1 - lam * o2
    o = _rmsnorm(o, nw) * (1.0 - lam_init)
    return (o * _silu(gate)).astype(BF16)


def _stack_streams(q):
    lane = lax.broadcasted_iota(jnp.int32, q.shape, 1)
    zero = jnp.zeros_like(q)
    return jnp.concatenate([jnp.where(lane < DIFF_QK_DIM, q, zero), jnp.where(lane >= DIFF_QK_DIM, q, zero)], axis=0)


def _diff_prompt_kernel(q_ref, k_ref, v_ref, lam_ref, nw_ref, g_ref, o_ref,
                        qz_ref, sa_ref, sb_ref, m_ref, acc_ref, *, t, lam_init):
    qi = pl.program_id(2)
    qz_ref[...] = _stack_streams(q_ref[0])
    _softmax_init_wide(m_ref, acc_ref)

    def blk(ref, j):
        return ref[0, pl.ds(pl.multiple_of(j * t, t), t), :]

    def qk(j, s_ref):
        s_ref[...] = _nt_dot(qz_ref[...], blk(k_ref, j))

    def pv_fn(j):
        return lambda p: jnp.dot(p, _with_ones(blk(v_ref, j)), preferred_element_type=F32)

    def upd(j, s_ref):
        _softmax_update_wide(s_ref[...], pv_fn(j), m_ref, acc_ref)

    def finish(s_ref):
        s = jnp.where(_diag_mask(2 * t, t, t, 0), s_ref[...], NEG_INF)
        _softmax_update_wide(s, pv_fn(qi), m_ref, acc_ref)
        o = acc_ref[:, :DIFF_V_DIM] / acc_ref[:, DIFF_V_DIM:]
        lam = _diff_lambda(lam_ref, lam_init)
        o_ref[0] = _diff_finish(o[:t], o[t:], lam, nw_ref[...], g_ref[...], lam_init)

    _causal_sweep(qi, qk, upd, finish, sa_ref, sb_ref)


def _diff_prompt_pair_kernel(q_ref, k_ref, v_ref, lam_ref, nw_ref, g_ref, o_ref,
                             qa_ref, qb_ref, sa_ref, sb_ref, m_ref, acc_ref, *, t, lam_init):
    i = pl.program_id(2)
    qa_ref[...] = _stack_streams(q_ref[0, 0:t, :])
    qb_ref[...] = _stack_streams(q_ref[0, t:2 * t, :])

    def blk(ref, j):
        return ref[0, pl.ds(pl.multiple_of(j * t, t), t), :]

    def pv_fn(j):
        return lambda p: jnp.dot(p, _with_ones(blk(v_ref, j)), preferred_element_type=F32)

    def upd(j, s_ref):
        _softmax_update_wide(s_ref[...], pv_fn(j), m_ref, acc_ref)

    def sweep(qz_ref, n_full, rows):
        def qk(j, s_ref):
            s_ref[...] = _nt_dot(qz_ref[...], blk(k_ref, j))

        def finish(s_ref):
            s = jnp.where(_diag_mask(2 * t, t, t, 0), s_ref[...], NEG_INF)
            _softmax_update_wide(s, pv_fn(n_full), m_ref, acc_ref)
            o = acc_ref[:, :DIFF_V_DIM] / acc_ref[:, DIFF_V_DIM:]
            lam = _diff_lambda(lam_ref, lam_init)
            o_ref[0, rows, :] = _diff_finish(o[:t], o[t:], lam, nw_ref[...], g_ref[rows, :], lam_init)

        return qk, upd, finish

    _paired_sweep(i, lambda: _softmax_init_wide(m_ref, acc_ref),
                  sweep(qa_ref, 2 * i, slice(0, t)), sweep(qb_ref, 2 * i + 1, slice(t, 2 * t)), sa_ref, sb_ref)


def _diff_prompt_call(qd, kb, vb, lam_p, norm_w, h2d, cols, lam_init, t):
    nb, L, dw = qd.shape
    heads = dw // DIFF_V_DIM
    nq = L // t
    per = 2 if nq % 2 == 0 else 1
    body = _diff_prompt_pair_kernel if per == 2 else _diff_prompt_kernel
    kern = functools.partial(body, t=t, lam_init=lam_init)
    nsteps, tq = nq // per, per * t
    seq = pl.BlockSpec((1, L, LANES), lambda b, h, i: (b, 0, h))
    tile = pl.BlockSpec((1, tq, LANES), lambda b, h, i: (b, i, h))
    return pl.pallas_call(
        kern,
        grid=(nb, heads, nsteps),
        in_specs=[tile, seq, seq,
                  pl.BlockSpec((4, DIFF_QK_DIM), lambda b, h, i: (0, 0)),
                  pl.BlockSpec((1, DIFF_V_DIM), lambda b, h, i: (0, 0)),
                  pl.BlockSpec((tq, LANES), lambda b, h, i: (b * nsteps + i, cols.dgate // LANES + h))],
        out_specs=tile,
        out_shape=jax.ShapeDtypeStruct((nb, L, dw), BF16),
        scratch_shapes=[pltpu.VMEM((2 * t, LANES), BF16)] * per + [
                        pltpu.VMEM((2 * t, t), F32),
                        pltpu.VMEM((2 * t, t), F32),
                        pltpu.VMEM((2 * t, LANES), F32),
                        pltpu.VMEM((2 * t, DIFF_V_DIM + LANES), F32)],
        compiler_params=_cparams(("arbitrary", "arbitrary", "arbitrary")),
        name="diff_attn_prompt",
    )(qd, kb, vb, lam_p, norm_w, h2d)


def _diff_sample_kernel(q_ref, kc_ref, vc_ref, kn_ref, vn_ref, lam_ref, nw_ref, g_ref, o_ref,
                        qz_ref, m_ref, acc_ref, *, nkc, heads, lam_init):
    j = pl.program_id(1)
    L = q_ref.shape[1]
    rows = 2 * L

    @pl.when(j == 0)
    def _():
        for h in range(heads):
            qz_ref[h] = _stack_streams(q_ref[0, :, h * LANES:(h + 1) * LANES])
        _softmax_init_wide(m_ref, acc_ref)

    def step(scores, values):
        def pv_fn(p):
            return jnp.concatenate([jnp.dot(p[h * rows:(h + 1) * rows], _with_ones(values(h)),
                                            preferred_element_type=F32) for h in range(heads)], axis=0)
        s = jnp.concatenate([scores(h) for h in range(heads)], axis=0)
        _softmax_update_wide(s, pv_fn, m_ref, acc_ref)

    @pl.when(j < nkc)
    def _():
        tk = kc_ref.shape[5]

        def scores(h):
            return jnp.dot(qz_ref[h], kc_ref[0, 0, h].reshape(2 * DIFF_QK_DIM, tk).astype(BF16),
                           preferred_element_type=F32)

        def values(h):
            return vc_ref[0, 0, pl.ds(h, tk, stride=heads), :].astype(BF16)

        step(scores, values)

    @pl.when(j == nkc)
    def _():
        step(lambda h: _nt_dot(qz_ref[h], kn_ref[0, :, h * LANES:(h + 1) * LANES]),
             lambda h: vn_ref[0, :, h * LANES:(h + 1) * LANES])
        lam = _diff_lambda(lam_ref, lam_init)
        o = acc_ref[:, :DIFF_V_DIM] / acc_ref[:, DIFF_V_DIM:]
        for h in range(heads):
            sl = slice(h * LANES, (h + 1) * LANES)
            oh = o[h * rows:(h + 1) * rows]
            o_ref[0, :, sl] = _diff_finish(oh[:L], oh[L:], lam, nw_ref[...], g_ref[:, sl], lam_init)


def _diff_sample_call(qd, kb, vb, k_cache, v_cache, layer, lam_p, norm_w, h2d, cols, lam_init, tk):
    nb, L, dw = qd.shape
    heads = dw // DIFF_V_DIM
    P = k_cache.shape[5]
    nkc = P // tk
    last = nkc - 1
    kern = functools.partial(_diff_sample_kernel, nkc=nkc, heads=heads, lam_init=lam_init)
    kspec = pl.BlockSpec((1, 1, heads, 2, DIFF_QK_DIM, tk),
                         lambda b, j: (layer, b, 0, 0, 0, jnp.minimum(j, last)))
    vspec = pl.BlockSpec((1, 1, tk * heads, DIFF_V_DIM), lambda b, j: (layer, b, jnp.minimum(j, last), 0))
    nspec = pl.BlockSpec((1, L, dw), lambda b, j: (b, 0, 0))
    return pl.pallas_call(
        kern,
        grid=(nb, nkc + 1),
        in_specs=[nspec, kspec, vspec, nspec, nspec,
                  pl.BlockSpec((4, DIFF_QK_DIM), lambda b, j: (0, 0)),
                  pl.BlockSpec((1, DIFF_V_DIM), lambda b, j: (0, 0)),
                  pl.BlockSpec((L, dw), lambda b, j: (b, cols.dgate // dw))],
        out_specs=nspec,
        out_shape=jax.ShapeDtypeStruct((nb, L, dw), BF16),
        scratch_shapes=[pltpu.VMEM((heads, 2 * L, LANES), BF16),
                        pltpu.VMEM((heads * 2 * L, LANES), F32),
                        pltpu.VMEM((heads * 2 * L, DIFF_V_DIM + LANES), F32)],
        compiler_params=_cparams(("arbitrary", "arbitrary")),
        name="diff_attn_sample",
    )(qd, k_cache, v_cache, kb, vb, lam_p, norm_w, h2d)


def _mla_finish(acc_ref, l, wuv_ref, g_ref, o_ref, heads, t):
    o = (acc_ref[...] / l).astype(BF16)
    for h in range(heads):
        sl = slice(h * MLA_V_DIM, (h + 1) * MLA_V_DIM)
        om = jnp.dot(o[h * t:(h + 1) * t], wuv_ref[:, sl], preferred_element_type=F32)
        o_ref[0, :, sl] = (om * _silu(g_ref[:, sl])).astype(BF16)


def _mla_prompt_kernel(q_ref, kt_ref, v_ref, wuv_ref, g_ref, o_ref, sa_ref, sb_ref, m_ref, l_ref, acc_ref, *, t, hg):
    qi = pl.program_id(2)
    _softmax_init(m_ref, l_ref, acc_ref)

    def vblk(j):
        return v_ref[0, pl.ds(pl.multiple_of(j * t, t), t), :]

    def qk(j, s_ref):
        for g in range(hg):
            s_ref[g * t:(g + 1) * t, :] = jnp.dot(q_ref[0, g], kt_ref[0, g, j], preferred_element_type=F32)

    def upd(j, s_ref):
        _softmax_update_rep(s_ref[...], vblk(j), m_ref, l_ref, acc_ref)

    def finish(s_ref):
        s = jnp.where(_diag_mask(hg * t, t, t, 0), s_ref[...], NEG_INF)
        _softmax_update_rep(s, vblk(qi), m_ref, l_ref, acc_ref)
        _mla_finish(acc_ref, jnp.sum(l_ref[...], axis=-1, keepdims=True), wuv_ref, g_ref, o_ref, hg, t)

    _causal_sweep(qi, qk, upd, finish, sa_ref, sb_ref)


def _mla_prompt_pair_kernel(q_ref, kt_ref, v_ref, wuv_ref, g_ref, o_ref, sa_ref, sb_ref, m_ref, l_ref, acc_ref,
                            *, t, hg):
    i = pl.program_id(2)

    def vblk(j):
        return v_ref[0, pl.ds(pl.multiple_of(j * t, t), t), :]

    def upd(j, s_ref):
        _softmax_update_rep(s_ref[...], vblk(j), m_ref, l_ref, acc_ref)

    def sweep(n_full, rows):
        def qk(j, s_ref):
            for g in range(hg):
                s_ref[g * t:(g + 1) * t, :] = jnp.dot(q_ref[0, g, rows, :], kt_ref[0, g, j], preferred_element_type=F32)

        def finish(s_ref):
            s = jnp.where(_diag_mask(hg * t, t, t, 0), s_ref[...], NEG_INF)
            _softmax_update_rep(s, vblk(n_full), m_ref, l_ref, acc_ref)
            _mla_finish(acc_ref, jnp.sum(l_ref[...], axis=-1, keepdims=True), wuv_ref, g_ref.at[rows, :],
                        o_ref.at[:, rows, :], hg, t)

        return qk, upd, finish

    _paired_sweep(i, lambda: _softmax_init(m_ref, l_ref, acc_ref),
                  sweep(2 * i, slice(0, t)), sweep(2 * i + 1, slice(t, 2 * t)), sa_ref, sb_ref)


def _mla_prompt_call(qh, kt, latb, wuv, h2d, cols, t, hg):
    nb, heads, L, qdim = qh.shape
    mw = heads * MLA_V_DIM
    gw = hg * MLA_V_DIM
    nq = L // t
    per = 2 if nq % 2 == 0 else 1
    body = _mla_prompt_pair_kernel if per == 2 else _mla_prompt_kernel
    nsteps, tq = nq // per, per * t
    return pl.pallas_call(
        functools.partial(body, t=t, hg=hg),
        grid=(nb, heads // hg, nsteps),
        in_specs=[pl.BlockSpec((1, hg, tq, qdim), lambda b, h, i: (b, h, i, 0)),
                  pl.BlockSpec((1, hg, nq, qdim, t), lambda b, h, i: (b, h, 0, 0, 0)),
                  pl.BlockSpec((1, L, MLA_KV_RANK), lambda b, h, i: (b, 0, 0)),
                  pl.BlockSpec((MLA_KV_RANK, gw), lambda b, h, i: (0, h)),
                  pl.BlockSpec((tq, gw), lambda b, h, i: (b * nsteps + i, cols.mgate // gw + h))],
        out_specs=pl.BlockSpec((1, tq, gw), lambda b, h, i: (b, i, h)),
        out_shape=jax.ShapeDtypeStruct((nb, L, mw), BF16),
        scratch_shapes=[pltpu.VMEM((hg * t, t), F32),
                        pltpu.VMEM((hg * t, t), F32),
                        pltpu.VMEM((hg * t, LANES), F32),
                        pltpu.VMEM((hg * t, LANES), F32),
                        pltpu.VMEM((hg * t, MLA_KV_RANK), F32)],
        compiler_params=_cparams(("arbitrary", "arbitrary", "arbitrary")),
        name="mla_attn_prompt",
    )(qh, kt, latb, wuv, h2d)


def _mla_sample_kernel(q_ref, lc_ref, rc_ref, kn_ref, wuv_ref, g_ref, o_ref, m_ref, l_ref, acc_ref,
                       *, nkc, heads):
    j = pl.program_id(1)
    L = q_ref.shape[2]

    @pl.when(j == 0)
    def _():
        _softmax_init(m_ref, l_ref, acc_ref)

    q = q_ref[0].reshape(heads * L, MLA_QK_PAD)

    @pl.when(j < nkc)
    def _():
        lat = lc_ref[0, 0].astype(BF16)
        kr_t = rc_ref[0, 0].astype(BF16)
        s = _nt_dot(q[:, :MLA_KV_RANK], lat) + jnp.dot(q[:, MLA_KV_RANK:MLA_KV_RANK + MLA_ROPE_DIM], kr_t,
                                                       preferred_element_type=F32)
        _softmax_update(s, lat, m_ref, l_ref, acc_ref)

    @pl.when(j == nkc)
    def _():
        kn = kn_ref[0]
        _softmax_update(_nt_dot(q, kn), kn[:, :MLA_KV_RANK], m_ref, l_ref, acc_ref)
        _mla_finish(acc_ref, l_ref[...], wuv_ref, g_ref, o_ref, heads, L)


def _mla_sample_call(qcat, kcat, lat_cache, kr_cache, layer, wuv, h2d, cols, tk):
    nb, heads, L, _ = qcat.shape
    mw = heads * MLA_V_DIM
    P = lat_cache.shape[2]
    nkc = P // tk
    last = nkc - 1
    kern = functools.partial(_mla_sample_kernel, nkc=nkc, heads=heads)
    return pl.pallas_call(
        kern,
        grid=(nb, nkc + 1),
        in_specs=[pl.BlockSpec((1, heads, L, MLA_QK_PAD), lambda b, j: (b, 0, 0, 0)),
                  pl.BlockSpec((1, 1, tk, MLA_KV_RANK), lambda b, j: (layer, b, jnp.minimum(j, last), 0)),
                  pl.BlockSpec((1, 1, MLA_ROPE_DIM, tk), lambda b, j: (layer, b, 0, jnp.minimum(j, last))),
                  pl.BlockSpec((1, L, MLA_QK_PAD), lambda b, j: (b, 0, 0)),
                  pl.BlockSpec(wuv.shape, lambda b, j: (0, 0)),
                  pl.BlockSpec((L, mw), lambda b, j: (b, cols.mgate // mw))],
        out_specs=pl.BlockSpec((1, L, mw), lambda b, j: (b, 0, 0)),
        out_shape=jax.ShapeDtypeStruct((nb, L, mw), BF16),
        scratch_shapes=[pltpu.VMEM((heads * L, 1), F32),
                        pltpu.VMEM((heads * L, 1), F32),
                        pltpu.VMEM((heads * L, MLA_KV_RANK), F32)],
        compiler_params=_cparams(("arbitrary", "arbitrary")),
        name="mla_attn_sample",
    )(qcat, lat_cache, kr_cache, kcat, wuv, h2d)


def _outproj_kernel(ys_ref, yd_ref, ym_ref, w_ref, x_ref, g_ref, r_ref, *, alpha):
    bt, lt, _ = ys_ref.shape
    flat = lambda ref: ref[...].reshape(bt * lt, ref.shape[2])
    mix = jnp.concatenate([flat(ys_ref), flat(yd_ref), flat(ym_ref)], axis=1)
    acc = jnp.dot(mix, w_ref[...], preferred_element_type=F32)
    r_ref[...] = alpha * x_ref[...] + g_ref[...] * acc.reshape(bt, lt, acc.shape[1])


def _outproj_call(y_ssd, y_diff, y_mla, w_out, x, gate, alpha):
    nb, L, d = x.shape
    bt, lt = _seq_tiles(nb, L, 512)
    tn = 1024
    yspec = lambda width: pl.BlockSpec((bt, lt, width), lambda j, b, i: (b, i, 0))
    return pl.pallas_call(
        functools.partial(_outproj_kernel, alpha=alpha),
        grid=(d // tn, nb // bt, L // lt),
        in_specs=[yspec(y_ssd.shape[2]), yspec(y_diff.shape[2]), yspec(y_mla.shape[2]),
                  pl.BlockSpec((w_out.shape[0], tn), lambda j, b, i: (0, j)),
                  pl.BlockSpec((bt, lt, tn), lambda j, b, i: (b, i, j)),
                  pl.BlockSpec((bt, 1, tn), lambda j, b, i: (b, 0, j))],
        out_specs=pl.BlockSpec((bt, lt, tn), lambda j, b, i: (b, i, j)),
        out_shape=jax.ShapeDtypeStruct((nb, L, d), F32),
        compiler_params=_cparams(("arbitrary", "arbitrary", "arbitrary")),
        name="out_proj",
    )(y_ssd, y_diff, y_mla, w_out, x, gate)


def _layernorm(r, g, b):
    mu = jnp.mean(r, axis=-1, keepdims=True)
    var = jnp.mean(jnp.square(r - mu), axis=-1, keepdims=True)
    return (r - mu) * lax.rsqrt(var + 1e-5) * g + b


def _ln_kernel(r_ref, g_ref, b_ref, x_ref):
    x_ref[...] = _layernorm(r_ref[...], g_ref[...], b_ref[...])


def _ln_mod_kernel(r_ref, g_ref, b_ref, sc_ref, sh_ref, x_ref, u_ref):
    x = _layernorm(r_ref[...], g_ref[...], b_ref[...])
    x_ref[...] = x
    u_ref[...] = (x * (1.0 + sc_ref[...]) + sh_ref[...]).astype(BF16)


def _ln_call(r, g, b, scale=None, shift=None):
    nb, L, d = r.shape
    bt, lt = _seq_tiles(nb, L, 256)
    xspec = pl.BlockSpec((bt, lt, d), lambda i, j: (i, j, 0))
    wspec = pl.BlockSpec((1, 1, d), lambda i, j: (0, 0, 0))
    sspec = pl.BlockSpec((bt, 1, d), lambda i, j: (i, 0, 0))
    g3, b3 = g.reshape(1, 1, d), b.reshape(1, 1, d)
    if scale is None:
        return pl.pallas_call(
            _ln_kernel, grid=(nb // bt, L // lt),
            in_specs=[xspec, wspec, wspec], out_specs=xspec,
            out_shape=jax.ShapeDtypeStruct((nb, L, d), F32),
            compiler_params=_cparams(("arbitrary", "arbitrary")), name="layernorm",
        )(r, g3, b3), None
    return pl.pallas_call(
        _ln_mod_kernel, grid=(nb // bt, L // lt),
        in_specs=[xspec, wspec, wspec, sspec, sspec], out_specs=[xspec, xspec],
        out_shape=[jax.ShapeDtypeStruct((nb, L, d), F32), jax.ShapeDtypeStruct((nb, L, d), BF16)],
        compiler_params=_cparams(("arbitrary", "arbitrary")), name="layernorm_modulate",
    )(r, g3, b3, scale, shift)


def _ssd_chunk(L):
    for t in (128, 64, 32, 16, 8):
        if L % t == 0:
            return t
    raise ValueError(f"sequence length {L} is not a multiple of 8")


def _attn_tile(L, want):
    t = want
    while L % t:
        t //= 2
    return t


def _layer(x, u, mod_l, next_mod, wl, cols, layer_idx, depth, caches, state_bufs):
    nb, L, d = x.shape
    heads = cols.heads
    mla_heads = cols.mla_w // MLA_V_DIM
    _, _, gate = mod_l
    k_cache, v_cache, lat_cache, kr_cache, st0, conv0 = caches
    P = 0 if k_cache is None else lat_cache.shape[2]
    pos = P + jnp.arange(L, dtype=jnp.int32)

    h2d = _inproj_call(u.reshape(nb * L, d), wl["w_in"], layer_idx, cols.tn)

    T = _ssd_chunk(L)
    y_ssd, ssm_new, conv_new = _ssd_call(h2d, conv0, st0, wl["conv_w"], wl["conv_b"], wl["dt_bias"], wl["a_log"],
                                         wl["d_skip"], wl["ssd_norm_w"], wl["e_mat"], wl["et_mat"], cols, nb, L, T)

    tabs_d = _rope_tables(pos, DIFF_ROT, DIFF_QK_DIM, LANES)
    tabs_m = _rope_tables(pos, MLA_ROPE_DIM, MLA_ROPE_DIM, MLA_ROPE_DIM)
    lam_init = 0.8 - 0.6 * math.exp(-0.3 * layer_idx)
    if k_cache is None:
        t = _attn_tile(L, 512)
        k32, v32, lat32, kr32, qd, kb, vb, latb, kt = _prep_call(h2d, tabs_d, tabs_m, wl["kv_norm_w"], cols, nb, L, t,
                                                                 layer_idx, state_bufs, wl["w_uk_flat"])
        qh = _mlaq_call(h2d, wl["q_norm_w"], wl["w_uq"], wl["w_uk"], tabs_m, cols, nb, L, mla_heads, False)
        y_diff = _diff_prompt_call(qd, kb, vb, wl["lam"], wl["diff_norm_w"], h2d, cols, lam_init, t)
        y_mla = _mla_prompt_call(qh, kt, latb, wl["w_uv"], h2d, cols, t, MLA_HEAD_GROUP)
    else:
        k32, v32, lat32, kr32, qd, kb, vb, kcat = _prep_call(h2d, tabs_d, tabs_m, wl["kv_norm_w"], cols, nb, L,
                                                             min(L, 256), layer_idx, state_bufs)
        qcat = _mlaq_call(h2d, wl["q_norm_w"], wl["w_uq"], wl["w_uk"], tabs_m, cols, nb, L, mla_heads, True)
        y_diff = _diff_sample_call(qd, kb, vb, k_cache, v_cache, layer_idx, wl["lam"], wl["diff_norm_w"], h2d,
                                   cols, lam_init, _attn_tile(P, 1024))
        y_mla = _mla_sample_call(qcat, kcat, lat_cache, kr_cache, layer_idx, wl["w_uv"], h2d, cols,
                                 _attn_tile(P, 2048))

    alpha = (2 * depth) ** 0.25
    r = _outproj_call(y_ssd.reshape(nb, L, cols.ssd_w), y_diff, y_mla, wl["w_out"], x, gate, alpha)
    if next_mod is None:
        x_new, u_new = _ln_call(r, wl["ln_g"], wl["ln_b"])
    else:
        x_new, u_new = _ln_call(r, wl["ln_g"], wl["ln_b"], next_mod[1], next_mod[0])
    return x_new, u_new, (k32, v32, lat32, kr32), (ssm_new, conv_new)


def _layer_weights(l, cols, w_in, conv_w, conv_b, dt_bias, a_log, d_skip, ssd_norm_w, lambda_q1, lambda_k1,
                   lambda_q2, lambda_k2, diff_norm_w, mla_q_norm_w, mla_kv_norm_w, w_uq, w_uk, w_uv, w_out,
                   ln_g, ln_b):
    heads = cols.heads
    mla_heads = cols.mla_w // MLA_V_DIM
    lane_pad = lambda v: jnp.pad(v, (MLA_ROPE_DIM, LANES - MLA_ROPE_DIM - heads)).reshape(1, LANES)
    qk = MLA_NOPE_DIM + MLA_ROPE_DIM
    wq = w_uq[l].reshape(MLA_Q_RANK, mla_heads, qk)
    wq_nope = wq[:, :, :MLA_NOPE_DIM].reshape(MLA_Q_RANK, mla_heads * MLA_NOPE_DIM)
    wq_rope = jnp.pad(wq[:, :, MLA_NOPE_DIM:], ((0, 0), (0, 0), (0, LANES - MLA_ROPE_DIM)))
    wq_p = jnp.concatenate([wq_nope, wq_rope.reshape(MLA_Q_RANK, mla_heads * LANES)], axis=1).astype(BF16)
    eh = np.zeros((LANES, cols.ssd_w), np.float32)
    for h in range(heads):
        eh[MLA_ROPE_DIM + h, h * SSD_HEAD_DIM:(h + 1) * SSD_HEAD_DIM] = 1.0
    return dict(
        w_in=w_in,
        conv_w=conv_w[l], conv_b=conv_b[l].reshape(1, -1),
        dt_bias=lane_pad(dt_bias[l]), a_log=lane_pad(a_log[l]),
        d_skip=jnp.repeat(d_skip[l], SSD_HEAD_DIM).reshape(1, -1),
        ssd_norm_w=ssd_norm_w[l].reshape(1, -1),
        e_mat=jnp.asarray(eh, BF16), et_mat=jnp.asarray(eh.T, BF16),
        lam=jnp.stack([lambda_q1[l], lambda_k1[l], lambda_q2[l], lambda_k2[l]]),
        diff_norm_w=diff_norm_w[l].reshape(1, -1),
        q_norm_w=mla_q_norm_w[l].reshape(1, -1), kv_norm_w=mla_kv_norm_w[l].reshape(1, -1),
        w_uq=wq_p,
        w_uk=jnp.transpose(w_uk[l], (1, 2, 0)).astype(BF16),
        w_uk_flat=w_uk[l].reshape(MLA_KV_RANK, mla_heads * MLA_NOPE_DIM).astype(BF16),
        w_uv=w_uv[l].reshape(MLA_KV_RANK, mla_heads * MLA_V_DIM).astype(BF16),
        w_out=w_out[l].astype(BF16),
        ln_g=ln_g[l], ln_b=ln_b[l],
    )


def kernel(x_prompt, x_sample, cache_diff_k, cache_diff_v, cache_mla_latent, cache_mla_krope, state_ssm, state_conv,
           c_prompt, c_sample, w_mod, b_mod, w_in, conv_w, conv_b, dt_bias, a_log, d_skip, ssd_norm_w, lambda_q1,
           lambda_k1, lambda_q2, lambda_k2, diff_norm_w, mla_q_norm_w, mla_kv_norm_w, w_uq, w_uk, w_uv, w_out,
           ln_g, ln_b):
    depth = w_in.shape[0]
    bp, _, d = x_prompt.shape
    bs = x_sample.shape[0]
    cols = _Cols(d)
    heads = cols.heads

    rows = -(-(bp + bs) // 8) * 8
    c_all = jnp.concatenate([c_prompt, c_sample, jnp.zeros((rows - bp - bs, d), F32)], axis=0)
    mod = _mod_call(c_all, w_mod, b_mod)

    def mods(l, lo, n):
        part = lambda k: mod[l, lo:lo + n, k * d:(k + 1) * d].reshape(n, 1, d)
        return part(0), part(1), part(2)

    pdiff = cache_diff_k.shape[2]
    kc = jnp.transpose(cache_diff_k, (0, 1, 3, 4, 5, 2))
    vc = cache_diff_v.reshape(depth, bs, pdiff * cache_diff_v.shape[3], DIFF_V_DIM)
    krc = jnp.transpose(cache_mla_krope, (0, 1, 3, 2))
    zero_state = jnp.zeros((bp, heads, SSD_HEAD_DIM, SSD_STATE), F32)
    zero_conv = jnp.zeros((bp, SSD_CONV - 1, cols.conv_dim), F32)

    w_in_p = _permute_w_in(w_in, cols)
    hp, hs = x_prompt, x_sample
    mp, ms = mods(0, 0, bp), mods(0, bp, bs)
    up = _modulate_call(hp, mp[1], mp[0])
    us = _modulate_call(hs, ms[1], ms[0])

    def state_buffers(nb, L):
        return tuple(jnp.zeros((depth, nb, L, w), F32) for w in (cols.diff_w, cols.diff_w, MLA_KV_RANK, MLA_ROPE_DIM))

    bufs_p, bufs_s = state_buffers(bp, hp.shape[1]), state_buffers(bs, hs.shape[1])
    rec_p, rec_s = [], []
    for l in range(depth):
        wl = _layer_weights(l, cols, w_in_p, conv_w, conv_b, dt_bias, a_log, d_skip, ssd_norm_w, lambda_q1, lambda_k1,
                            lambda_q2, lambda_k2, diff_norm_w, mla_q_norm_w, mla_kv_norm_w, w_uq, w_uk, w_uv, w_out,
                            ln_g, ln_b)
        nmp = mods(l + 1, 0, bp) if l + 1 < depth else None
        nms = mods(l + 1, bp, bs) if l + 1 < depth else None
        hp, up, bufs_p, rp = _layer(hp, up, mp, nmp, wl, cols, l, depth,
                                    (None, None, None, None, zero_state, zero_conv), bufs_p)
        hs, us, bufs_s, rs = _layer(hs, us, ms, nms, wl, cols, l, depth,
                                    (kc, vc, cache_mla_latent, krc, state_ssm[l], state_conv[l]), bufs_s)
        rec_p.append(rp)
        rec_s.append(rs)
        mp, ms = nmp, nms

    def states(bufs, rec):
        k32, v32, lat32, kr32 = bufs
        nb, L = k32.shape[1], k32.shape[2]
        return (k32.reshape(depth, nb, L, cols.diff_w // (2 * DIFF_QK_DIM), 2, DIFF_QK_DIM),
                v32.reshape(depth, nb, L, cols.diff_w // DIFF_V_DIM, DIFF_V_DIM), lat32, kr32,
                jnp.stack([r[0] for r in rec]), jnp.stack([r[1] for r in rec]))

    return (hp, hs) + states(bufs_p, rec_p) + states(bufs_s, rec_s)
```
